```python
import math
import jax, jax.numpy as jnp
from jax import lax
import numpy as np

D_MODEL = 1024
BATCH = 8
SEQ = 2048
DEPTH = 1
DEC_BATCH = 128
DEC_SEQ = 8
PAST_LEN = 8192
PAGE_SIZE = 128

MIX_W = D_MODEL
POOL_W = MIX_W // 2
POOL_WINDOWS = (2, 4, 8, 16)
POOL_GROUPS = len(POOL_WINDOWS)
POOL_GC = POOL_W // POOL_GROUPS
POOL_BUF = max(POOL_WINDOWS) - 1
ATTN_W = MIX_W - POOL_W
HEAD_DIM = 64
N_HEADS = ATTN_W // HEAD_DIM
N_KV_HEADS = 2
GQA = N_HEADS // N_KV_HEADS
KV_W = N_KV_HEADS * HEAD_DIM
WINDOW = 128
ATTN_BLOCK = WINDOW
NUM_BUCKETS = 32
MAX_EXACT = NUM_BUCKETS // 2
REL_MAX_DIST = 128
N_EXPERTS = 256
N_EXPERT_GROUPS = 8
TOPK_GROUPS = 4
TOP_K = 8
EXPERT_FF = 256
SHARED_FF = 256
ROUTED_SCALE = 2.5
EXPERT_BLOCK = 128
EPS = 1e-6
NEG_INF = -1e30

kernel_name = 'hybrid_pool_swa_moe_adaln_step'


def rmsnorm(x, g):
    x32 = x.astype(jnp.float32)
    y = x32 * lax.rsqrt(jnp.mean(x32 * x32, axis=-1, keepdims=True) + EPS)
    return y.astype(x.dtype) * g


def swiglu(x, w1, w3, w2):
    return (jax.nn.silu(x @ w1) * (x @ w3)) @ w2


def rel_pos_bias(dist, table):
    n = jnp.maximum(dist, 0)
    nf = jnp.maximum(n, 1).astype(jnp.float32)
    large = MAX_EXACT + (jnp.log(nf / MAX_EXACT) / math.log(REL_MAX_DIST / MAX_EXACT)
                         * (NUM_BUCKETS - MAX_EXACT)).astype(jnp.int32)
    bucket = jnp.where(n < MAX_EXACT, n, jnp.minimum(large, NUM_BUCKETS - 1))
    return jnp.transpose(table[bucket].astype(jnp.float32), (2, 0, 1))


def attend(q, k, v, mask, bias, sinks):
    B, N, Lq, H, HD = q.shape
    Lk = k.shape[2]
    qg = q.reshape(B, N, Lq, N_KV_HEADS, GQA, HD)
    s = jnp.einsum('bnqkgd,bnskd->bnkgqs', qg, k, preferred_element_type=jnp.float32) * (HD ** -0.5)
    s = s + bias.reshape(N_KV_HEADS, GQA, Lq, Lk)
    s = jnp.where(mask[None, :, None, None], s, NEG_INF)
    sink = sinks.astype(jnp.float32).reshape(N_KV_HEADS, GQA, 1, 1)
    m = jnp.maximum(jnp.max(s, axis=-1, keepdims=True), sink)
    p = jnp.exp(s - m)
    denom = jnp.sum(p, axis=-1, keepdims=True) + jnp.exp(sink - m)
    o = jnp.einsum('bnkgqs,bnskd->bnqkgd', (p / denom).astype(v.dtype), v)
    return o.reshape(B, N, Lq, H * HD)


def swa_prompt(q, k, v, rel_bias, sinks):
    B, L = q.shape[:2]
    nb = L // ATTN_BLOCK
    qb = q.reshape(B, nb, ATTN_BLOCK, N_HEADS, HEAD_DIM)
    kb = k.reshape(B, nb, ATTN_BLOCK, N_KV_HEADS, HEAD_DIM)
    vb = v.reshape(B, nb, ATTN_BLOCK, N_KV_HEADS, HEAD_DIM)
    kk = jnp.concatenate([jnp.concatenate([jnp.zeros_like(kb[:, :1]), kb[:, :-1]], 1), kb], 2)
    vv = jnp.concatenate([jnp.concatenate([jnp.zeros_like(vb[:, :1]), vb[:, :-1]], 1), vb], 2)
    kj = jnp.arange(2 * ATTN_BLOCK)[None, :]
    dist = jnp.arange(ATTN_BLOCK)[:, None] + ATTN_BLOCK - kj
    key_pos = jnp.arange(nb)[:, None, None] * ATTN_BLOCK - ATTN_BLOCK + kj[None]
    mask = ((dist >= 0) & (dist < WINDOW))[None] & (key_pos >= 0)
    o = attend(qb, kk, vv, mask, rel_pos_bias(dist, rel_bias), sinks).reshape(B, L, ATTN_W)
    wb = min(WINDOW, L)
    return o, k[:, -wb:], v[:, -wb:]


def swa_sample(q, k, v, k_buf, v_buf, rel_bias, sinks):
    B, L = q.shape[:2]
    W = k_buf.shape[1]
    kk = jnp.concatenate([k_buf.astype(k.dtype), k], 1)
    vv = jnp.concatenate([v_buf.astype(v.dtype), v], 1)
    dist = jnp.arange(L)[:, None] + W - jnp.arange(W + L)[None, :]
    mask = ((dist >= 0) & (dist < WINDOW))[None]
    o = attend(q[:, None], kk[:, None], vv[:, None], mask, rel_pos_bias(dist, rel_bias), sinks)
    return o.reshape(B, L, ATTN_W), kk[:, -W:], vv[:, -W:]


def pool_mix(u, buf, pos0, w_pool, pool_scale):
    B, L, C = u.shape
    xcat = jnp.concatenate([buf.astype(u.dtype), u], 1)
    xp = xcat.astype(jnp.float32)
    cs = jnp.concatenate([jnp.zeros((B, 1, C), jnp.float32), jnp.cumsum(xp, axis=1)], 1)
    pos = pos0 + jnp.arange(L)
    outs = []
    for g, w in enumerate(POOL_WINDOWS):
        sl = slice(g * POOL_GC, (g + 1) * POOL_GC)
        hi = cs[:, POOL_BUF + 1:POOL_BUF + 1 + L, sl]
        lo = cs[:, POOL_BUF + 1 - w:POOL_BUF + 1 - w + L, sl]
        cnt = jnp.minimum(w, pos + 1).astype(jnp.float32)[None, :, None]
        outs.append((hi - lo) / cnt - xp[:, POOL_BUF:, sl])
    d = jnp.stack(outs, axis=2).astype(u.dtype)
    y = jnp.einsum('blgc,gce->blge', d, w_pool).reshape(B, L, C) * pool_scale
    return y, xcat[:, -POOL_BUF:]


def route(h, w_router, router_bias):
    T = h.shape[0]
    s = jax.nn.sigmoid(jnp.einsum('td,de->te', h.astype(jnp.float32), w_router.astype(jnp.float32)))
    sb = s + router_bias.astype(jnp.float32)
    per = N_EXPERTS // N_EXPERT_GROUPS
    g_score = jnp.sum(lax.top_k(sb.reshape(T, N_EXPERT_GROUPS, per), 2)[0], axis=-1)
    _, g_idx = lax.top_k(g_score, TOPK_GROUPS)
    g_mask = jnp.any(g_idx[..., None] == jnp.arange(N_EXPERT_GROUPS), axis=1)
    sb_masked = jnp.where(jnp.repeat(g_mask, per, axis=1), sb, NEG_INF)
    _, idx = lax.top_k(sb_masked, TOP_K)
    w = jnp.take_along_axis(s, idx, axis=-1)
    w = w / jnp.sum(w, axis=-1, keepdims=True) * ROUTED_SCALE
    return idx, w


def moe_routed(h, idx, gate, w1, w3, w2):
    T, D = h.shape
    A = T * TOP_K
    e_flat = idx.reshape(A)
    tok_flat = jnp.arange(A, dtype=jnp.int32) // TOP_K
    g_flat = gate.reshape(A)
    order = jnp.argsort(e_flat)
    e_s, tok_s, g_s = e_flat[order], tok_flat[order], g_flat[order]
    counts = jnp.bincount(e_flat, length=N_EXPERTS)
    start = jnp.cumsum(counts) - counts
    padded = (counts + EXPERT_BLOCK - 1) // EXPERT_BLOCK * EXPERT_BLOCK
    pad_end = jnp.cumsum(padded)
    pad_start = pad_end - padded
    dest = pad_start[e_s] + (jnp.arange(A, dtype=jnp.int32) - start[e_s])
    n_blk = -(-A // EXPERT_BLOCK) + N_EXPERTS
    n_rows = n_blk * EXPERT_BLOCK
    row_tok = jnp.full((n_rows,), T, jnp.int32).at[dest].set(tok_s)
    h_pad = jnp.concatenate([h, jnp.zeros((1, D), h.dtype)], 0)
    xb = h_pad[row_tok].reshape(n_blk, EXPERT_BLOCK, D)
    blk_e = jnp.minimum(jnp.searchsorted(pad_end, jnp.arange(n_blk, dtype=jnp.int32) * EXPERT_BLOCK, side='right'),
                        N_EXPERTS - 1)

    def one_block(args):
        xblk, e = args
        return swiglu(xblk, w1[e], w3[e], w2[e])

    yb = lax.map(one_block, (xb, blk_e)).reshape(n_rows, D)
    y_s = yb[dest] * g_s[:, None].astype(yb.dtype)
    return jax.ops.segment_sum(y_s, tok_s, num_segments=T)


def layer(x, c, pool_buf, k_buf, v_buf, pos0, rel_bias, w_ada, b_ada, g_attn_norm, w_in, g_q, g_k,
          w_pool, pool_scale, w_out, sinks, g_ffn_norm, w_router, router_bias, w1, w3, w2, ws1, ws3, ws2):
    B, L, D = x.shape
    mod = jnp.einsum('bd,de->be', jax.nn.silu(c), w_ada) + b_ada
    sh1, sc1, gt1, sh2, sc2, gt2 = [m[:, None, :] for m in jnp.split(mod, 6, axis=-1)]
    h = rmsnorm(x, g_attn_norm) * (1 + sc1) + sh1
    u = jnp.einsum('bld,de->ble', h, w_in)
    u_pool = u[..., :POOL_W]
    q = u[..., POOL_W:POOL_W + ATTN_W].reshape(B, L, N_HEADS, HEAD_DIM)
    k = u[..., POOL_W + ATTN_W:POOL_W + ATTN_W + KV_W].reshape(B, L, N_KV_HEADS, HEAD_DIM)
    v = u[..., POOL_W + ATTN_W + KV_W:].reshape(B, L, N_KV_HEADS, HEAD_DIM)
    q = rmsnorm(q, g_q)
    k = rmsnorm(k, g_k)
    pool_out, new_pool = pool_mix(u_pool, pool_buf, pos0, w_pool, pool_scale)
    if k_buf is None:
        attn_out, new_k, new_v = swa_prompt(q, k, v, rel_bias, sinks)
    else:
        attn_out, new_k, new_v = swa_sample(q, k, v, k_buf, v_buf, rel_bias, sinks)
    mix = jnp.concatenate([pool_out, attn_out], axis=-1)
    x = x + gt1 * jnp.einsum('ble,ed->bld', mix, w_out)
    h2 = rmsnorm(x, g_ffn_norm) * (1 + sc2) + sh2
    hf = h2.reshape(B * L, D)
    idx, gate = route(hf, w_router, router_bias)
    ffn = moe_routed(hf, idx, gate, w1, w3, w2) + swiglu(hf, ws1, ws3, ws2)
    x = x + gt2 * ffn.reshape(B, L, D)
    return x, new_pool, new_k, new_v


def setup_inputs(seed: int = 0) -> dict:
    key = jax.random.key(seed)
    ks = jax.random.split(key, 32)
    f32 = jnp.float32

    def nrm(k, shape, s):
        return jax.random.normal(k, shape, f32) * s

    wb = min(WINDOW, PAST_LEN)
    return {
        'x_prompt': nrm(ks[0], (BATCH, SEQ, D_MODEL), 1.0),
        'x_sample': nrm(ks[1], (DEC_BATCH, DEC_SEQ, D_MODEL), 1.0),
        'state_pool': nrm(ks[2], (DEPTH, DEC_BATCH, POOL_BUF, POOL_W), 1.0),
        'cache_swa_k': nrm(ks[3], (DEPTH, DEC_BATCH, wb, N_KV_HEADS, HEAD_DIM), 1.0),
        'cache_swa_v': nrm(ks[4], (DEPTH, DEC_BATCH, wb, N_KV_HEADS, HEAD_DIM), 1.0),
        'c_prompt': nrm(ks[5], (BATCH, D_MODEL), 1.0),
        'c_sample': nrm(ks[6], (DEC_BATCH, D_MODEL), 1.0),
        'w_ada': nrm(ks[7], (DEPTH, D_MODEL, 6 * D_MODEL), 0.5 * D_MODEL ** -0.5),
        'b_ada': nrm(ks[8], (DEPTH, 6 * D_MODEL), 0.02),
        'g_attn_norm': 1 + nrm(ks[9], (DEPTH, D_MODEL), 0.1),
        'w_in': nrm(ks[10], (DEPTH, D_MODEL, POOL_W + ATTN_W + 2 * KV_W), D_MODEL ** -0.5),
        'g_q': 1 + nrm(ks[11], (DEPTH, HEAD_DIM), 0.1),
        'g_k': 1 + nrm(ks[12], (DEPTH, HEAD_DIM), 0.1),
        'w_pool': nrm(ks[13], (DEPTH, POOL_GROUPS, POOL_GC, POOL_GC), POOL_GC ** -0.5),
        'pool_scale': 1 + nrm(ks[14], (DEPTH, POOL_W), 0.1),
        'w_out': nrm(ks[15], (DEPTH, MIX_W, D_MODEL), MIX_W ** -0.5),
        'attn_sinks': nrm(ks[16], (DEPTH, N_HEADS), 0.5),
        'rel_bias': nrm(ks[17], (NUM_BUCKETS, N_HEADS), 0.5),
        'g_ffn_norm': 1 + nrm(ks[18], (DEPTH, D_MODEL), 0.1),
        'w_router': nrm(ks[19], (DEPTH, D_MODEL, N_EXPERTS), D_MODEL ** -0.5),
        'router_bias': nrm(ks[20], (DEPTH, N_EXPERTS), 0.01),
        'w1': nrm(ks[21], (DEPTH, N_EXPERTS, D_MODEL, EXPERT_FF), D_MODEL ** -0.5),
        'w3': nrm(ks[22], (DEPTH, N_EXPERTS, D_MODEL, EXPERT_FF), D_MODEL ** -0.5),
        'w2': nrm(ks[23], (DEPTH, N_EXPERTS, EXPERT_FF, D_MODEL), EXPERT_FF ** -0.5),
        'ws1': nrm(ks[24], (DEPTH, D_MODEL, SHARED_FF), D_MODEL ** -0.5),
        'ws3': nrm(ks[25], (DEPTH, D_MODEL, SHARED_FF), D_MODEL ** -0.5),
        'ws2': nrm(ks[26], (DEPTH, SHARED_FF, D_MODEL), SHARED_FF ** -0.5),
    }


def reference(x_prompt, x_sample, state_pool, cache_swa_k, cache_swa_v, c_prompt, c_sample,
              w_ada, b_ada, g_attn_norm, w_in, g_q, g_k, w_pool, pool_scale, w_out, attn_sinks,
              rel_bias, g_ffn_norm, w_router, router_bias, w1, w3, w2, ws1, ws3, ws2):
    yp, ys = x_prompt, x_sample
    pool0 = jnp.zeros((x_prompt.shape[0], POOL_BUF, POOL_W), x_prompt.dtype)
    np_pool, np_k, np_v, ns_pool, ns_k, ns_v = [], [], [], [], [], []
    for l in range(DEPTH):
        lp = [w[l] for w in (w_ada, b_ada, g_attn_norm, w_in, g_q, g_k, w_pool, pool_scale, w_out,
                             attn_sinks, g_ffn_norm, w_router, router_bias, w1, w3, w2, ws1, ws3, ws2)]
        yp, sp, kp, vp = layer(yp, c_prompt, pool0, None, None, 0, rel_bias, *lp)
        ys, ss, k_s, v_s = layer(ys, c_sample, state_pool[l], cache_swa_k[l], cache_swa_v[l], PAST_LEN,
                                 rel_bias, *lp)
        np_pool.append(sp); np_k.append(kp); np_v.append(vp)
        ns_pool.append(ss); ns_k.append(k_s); ns_v.append(v_s)
    return (yp, ys, jnp.stack(np_pool), jnp.stack(np_k), jnp.stack(np_v),
            jnp.stack(ns_pool), jnp.stack(ns_k), jnp.stack(ns_v))
```

```python
import functools
import math

import numpy as np
import jax
import jax.numpy as jnp
from jax import lax
from jax.experimental import pallas as pl
from jax.experimental.pallas import tpu as pltpu
from jax.experimental.pallas import tpu_sc as plsc

f32 = jnp.float32
bf16 = jnp.bfloat16
i32 = jnp.int32

D_MODEL = 1024
PAST_LEN = 8192
POOL_W = 512
POOL_WINDOWS = (2, 4, 8, 16)
POOL_GC = 128
POOL_BUF = 15
ATTN_W = 512
HEAD_DIM = 64
N_HEADS = 8
N_KV_HEADS = 2
GQA = 4
KV_W = 128
WINDOW = 128
NUM_BUCKETS = 32
MAX_EXACT = 16
REL_MAX_DIST = 128
N_EXPERTS = 256
N_EXPERT_GROUPS = 8
GROUP_SIZE = 32
TOPK_GROUPS = 4
TOP_K = 8
EXPERT_FF = 256
ROUTED_SCALE = 2.5
EXPERT_BLOCK = 128
EPS = 1e-6
NEG_INF = -1e30
QKV_W = POOL_W + ATTN_W + 2 * KV_W
QK_W = ATTN_W + KV_W
HIST = 16

SC_WORKERS = 32
SC_LANES = 16


def _dot(a, b):
    return jnp.dot(a, b, preferred_element_type=f32)


def _dot_t(a, b):
    return lax.dot_general(a, b, (((1,), (1,)), ((), ())), preferred_element_type=f32)


def _split_bf16(a):
    hi = a.astype(bf16)
    lo = (a - hi.astype(f32)).astype(bf16)
    return hi, lo


def _mod_norm(x, g, sc, sh):
    ms = jnp.mean(x * x, axis=-1, keepdims=True)
    y = x * lax.rsqrt(ms + EPS)
    return (y * g) * (1.0 + sc) + sh


def _ada_body(c_ref, w_ref, b_ref, o_ref):
    c = c_ref[...]
    a = (c * jax.nn.sigmoid(c)).astype(bf16)
    o_ref[...] = _dot(a, w_ref[...].astype(bf16)) + b_ref[...]


def _ada(c, w_ada, b_ada):
    n = c.shape[0]
    tn = 1024
    return pl.pallas_call(
        _ada_body,
        grid=(6 * D_MODEL // tn,),
        in_specs=[
            pl.BlockSpec((n, D_MODEL), lambda j: (0, 0)),
            pl.BlockSpec((D_MODEL, tn), lambda j: (0, j)),
            pl.BlockSpec((1, tn), lambda j: (0, j)),
        ],
        out_specs=pl.BlockSpec((n, tn), lambda j: (0, j)),
        out_shape=jax.ShapeDtypeStruct((n, 6 * D_MODEL), f32),
        name="ada_mod",
    )(c, w_ada, b_ada.reshape(1, -1))


def _relbias_body(table_ref, bucket_ref, o_ref):
    bucket = bucket_ref[...]
    for h in range(N_HEADS):
        acc = jnp.zeros(bucket.shape, f32)
        for b in range(NUM_BUCKETS):
            acc = jnp.where(bucket == b, table_ref[b, h], acc)
        o_ref[h] = acc


def _rel_buckets(dist):
    n = np.maximum(dist, 0)
    nf = np.maximum(n, 1).astype(np.float64)
    large = MAX_EXACT + (np.log(nf / MAX_EXACT) / math.log(REL_MAX_DIST / MAX_EXACT)
                         * (NUM_BUCKETS - MAX_EXACT)).astype(np.int32)
    return np.where(n < MAX_EXACT, n, np.minimum(large, NUM_BUCKETS - 1)).astype(np.int32)


def _relbias(table, dist):
    lq, lk = dist.shape
    return pl.pallas_call(
        _relbias_body,
        in_specs=[
            pl.BlockSpec(memory_space=pltpu.SMEM),
            pl.BlockSpec((lq, lk), lambda: (0, 0)),
        ],
        out_specs=pl.BlockSpec((N_HEADS, lq, lk), lambda: (0, 0, 0)),
        out_shape=jax.ShapeDtypeStruct((N_HEADS, lq, lk), f32),
        name="rel_bias",
    )(table, jnp.asarray(_rel_buckets(dist)))


def _qkv_from_h(h, w_ref, gqk_ref, bd_ref):
    u = _dot(h.astype(bf16), w_ref[...])
    qk = u[:, POOL_W:POOL_W + QK_W]
    y_hi, y_lo = _split_bf16(qk * qk)
    bd = bd_ref[...]
    ss = _dot(y_hi, bd) + _dot(y_lo, bd)
    qkn = (qk * lax.rsqrt(ss * (1.0 / HEAD_DIM) + EPS)) * gqk_ref[...]
    q = qkn[:, :ATTN_W] * (HEAD_DIM ** -0.5)
    k = qkn[:, ATTN_W:]
    v = u[:, POOL_W + QK_W:]
    return u[:, :POOL_W], q, k, v


def _inproj_prompt_body(x_ref, sh_ref, sc_ref, g_ref, w_ref, gqk_ref, bd_ref, wp_ref, ps_ref,
                        q_ref, k_ref, v_ref, po_ref, np_ref, kc_ref, vc_ref, hist, *, tl, nt):
    j = pl.program_id(1)
    h = _mod_norm(x_ref[0], g_ref[...], sc_ref[0], sh_ref[0])
    up, q, k, v = _qkv_from_h(h, w_ref, gqk_ref, bd_ref)
    q_ref[0] = q.astype(bf16)
    k_ref[0] = k.astype(bf16)
    v_ref[0] = v.astype(bf16)

    @pl.when(j == nt - 1)
    def _():
        kc_ref[0] = k[tl - WINDOW:, :]
        vc_ref[0] = v[tl - WINDOW:, :]

    @pl.when(j == 0)
    def _():
        hist[0:HIST, :] = jnp.zeros((HIST, POOL_W), f32)

    hist[HIST:HIST + tl, :] = up
    pos = j * tl + lax.broadcasted_iota(i32, (tl, 1), 0)
    for g, w in enumerate(POOL_WINDOWS):
        lanes = slice(g * POOL_GC, (g + 1) * POOL_GC)
        cur = up[:, lanes]
        acc = cur
        for s in range(1, w):
            acc = acc + hist[HIST - s:HIST - s + tl, lanes]
        cnt = jnp.minimum(w, pos + 1).astype(f32)
        d = acc / cnt - cur
        yg = _dot(d.astype(bf16), wp_ref[g]) * ps_ref[:, lanes]
        po_ref[0, :, lanes] = yg.astype(bf16)

    @pl.when(j == nt - 1)
    def _():
        np_ref[0] = hist[tl + 1:tl + HIST, :]

    hist[0:HIST, :] = hist[tl:tl + HIST, :]


def _inproj_prompt(x, mod3, g_attn, w_in, gqk, bd, w_pool, pool_scale, tl=512):
    B, L, D = x.shape
    nt = L // tl
    full = lambda shape: pl.BlockSpec(shape, lambda b, j: (0,) * len(shape))
    return pl.pallas_call(
        functools.partial(_inproj_prompt_body, tl=tl, nt=nt),
        grid=(B, nt),
        in_specs=[
            pl.BlockSpec((1, tl, D), lambda b, j: (b, j, 0)),
            pl.BlockSpec((1, 1, D), lambda b, j: (b, 0, 0)),
            pl.BlockSpec((1, 1, D), lambda b, j: (b, 0, 1)),
            full((1, D)),
            full((D, QKV_W)),
            full((1, QK_W)),
            full((QK_W, QK_W)),
            full((4, POOL_GC, POOL_GC)),
            full((1, POOL_W)),
        ],
        out_specs=[
            pl.BlockSpec((1, tl, ATTN_W), lambda b, j: (b, j, 0)),
            pl.BlockSpec((1, tl, KV_W), lambda b, j: (b, j, 0)),
            pl.BlockSpec((1, tl, KV_W), lambda b, j: (b, j, 0)),
            pl.BlockSpec((1, tl, POOL_W), lambda b, j: (b, j, 0)),
            pl.BlockSpec((1, POOL_BUF, POOL_W), lambda b, j: (b, 0, 0)),
            pl.BlockSpec((1, WINDOW, KV_W), lambda b, j: (b, 0, 0)),
            pl.BlockSpec((1, WINDOW, KV_W), lambda b, j: (b, 0, 0)),
        ],
        out_shape=[
            jax.ShapeDtypeStruct((B, L, ATTN_W), bf16),
            jax.ShapeDtypeStruct((B, L, KV_W), bf16),
            jax.ShapeDtypeStruct((B, L, KV_W), bf16),
            jax.ShapeDtypeStruct((B, L, POOL_W), bf16),
            jax.ShapeDtypeStruct((B, POOL_BUF, POOL_W), f32),
            jax.ShapeDtypeStruct((B, WINDOW, KV_W), f32),
            jax.ShapeDtypeStruct((B, WINDOW, KV_W), f32),
        ],
        scratch_shapes=[pltpu.VMEM((HIST + tl, POOL_W), f32)],
        compiler_params=pltpu.CompilerParams(dimension_semantics=("arbitrary", "arbitrary")),
        name="inproj_prompt",
    )(x, mod3, mod3, g_attn, w_in, gqk, bd, w_pool, pool_scale)


def _inproj_sample_body(x_ref, sh_ref, sc_ref, g_ref, w_ref, gqk_ref, bd_ref, wp_ref, ps_ref, st_ref,
                        q_ref, k_ref, v_ref, po_ref, np_ref, ext, *, bt, ls, pos0):
    n = bt * ls
    h3 = _mod_norm(x_ref[...], g_ref[...][None], sc_ref[...], sh_ref[...])
    up, q, k, v = _qkv_from_h(h3.reshape(n, D_MODEL), w_ref, gqk_ref, bd_ref)
    q_ref[...] = q.astype(bf16)
    k_ref[...] = k
    v_ref[...] = v

    ext[:, 1:HIST, :] = st_ref[...]
    ext[:, HIST:HIST + ls, :] = up.reshape(bt, ls, POOL_W)
    pos = pos0 + lax.broadcasted_iota(i32, (1, ls, 1), 1)
    for g, w in enumerate(POOL_WINDOWS):
        lanes = slice(g * POOL_GC, (g + 1) * POOL_GC)
        cur = ext[:, HIST:HIST + ls, lanes]
        acc = cur
        for s in range(1, w):
            acc = acc + ext[:, HIST - s:HIST - s + ls, lanes]
        cnt = jnp.minimum(w, pos + 1).astype(f32)
        d = (acc / cnt - cur).reshape(n, POOL_GC)
        yg = _dot(d.astype(bf16), wp_ref[g]) * ps_ref[:, lanes]
        po_ref[:, lanes] = yg.astype(bf16)
    np_ref[...] = ext[:, ls + 1:ls + HIST, :]


def _inproj_sample(x, mod3, g_attn, w_in, gqk, bd, w_pool, pool_scale, state, pos0, bt=64):
    B, ls, D = x.shape
    n = bt * ls
    full = lambda shape: pl.BlockSpec(shape, lambda i: (0,) * len(shape))
    return pl.pallas_call(
        functools.partial(_inproj_sample_body, bt=bt, ls=ls, pos0=pos0),
        grid=(B // bt,),
        in_specs=[
            pl.BlockSpec((bt, ls, D), lambda i: (i, 0, 0)),
            pl.BlockSpec((bt, 1, D), lambda i: (i, 0, 0)),
            pl.BlockSpec((bt, 1, D), lambda i: (i, 0, 1)),
            full((1, D)),
            full((D, QKV_W)),
            full((1, QK_W)),
            full((QK_W, QK_W)),
            full((4, POOL_GC, POOL_GC)),
            full((1, POOL_W)),
            pl.BlockSpec((bt, POOL_BUF, POOL_W), lambda i: (i, 0, 0)),
        ],
        out_specs=[
            pl.BlockSpec((n, ATTN_W), lambda i: (i, 0)),
            pl.BlockSpec((n, KV_W), lambda i: (i, 0)),
            pl.BlockSpec((n, KV_W), lambda i: (i, 0)),
            pl.BlockSpec((n, POOL_W), lambda i: (i, 0)),
            pl.BlockSpec((bt, POOL_BUF, POOL_W), lambda i: (i, 0, 0)),
        ],
        out_shape=[
            jax.ShapeDtypeStruct((B * ls, ATTN_W), bf16),
            jax.ShapeDtypeStruct((B * ls, KV_W), f32),
            jax.ShapeDtypeStruct((B * ls, KV_W), f32),
            jax.ShapeDtypeStruct((B * ls, POOL_W), bf16),
            jax.ShapeDtypeStruct((B, POOL_BUF, POOL_W), f32),
        ],
        scratch_shapes=[pltpu.VMEM((bt, HIST + ls, POOL_W), f32)],
        name="inproj_sample",
    )(x, mod3, mod3, g_attn, w_in, gqk, bd, w_pool, pool_scale, state)


def _softmax_sink(parts, sink):
    m = sink
    for s in parts:
        m = jnp.maximum(m, jnp.max(s, axis=-1, keepdims=True))
    ps = [jnp.exp(s - m) for s in parts]
    denom = jnp.exp(sink - m)
    for p in ps:
        denom = denom + jnp.sum(p, axis=-1, keepdims=True)
    inv = 1.0 / denom
    return [(p * inv).astype(bf16) for p in ps]


def _attn_prompt_body(sinks_ref, q_ref, kp_ref, kc_ref, vp_ref, vc_ref, bias_ref, o_ref):
    j = pl.program_id(1)
    qi = lax.broadcasted_iota(i32, (WINDOW, WINDOW), 0)
    kj = lax.broadcasted_iota(i32, (WINDOW, WINDOW), 1)
    valid_prev = (kj > qi) & (j > 0)
    valid_cur = kj <= qi
    outs = []
    for h in range(N_HEADS):
        hs = slice(h * HEAD_DIM, (h + 1) * HEAD_DIM)
        ks = slice((h // GQA) * HEAD_DIM, (h // GQA + 1) * HEAD_DIM)
        qh = q_ref[0, :, hs]
        s_prev = jnp.where(valid_prev, _dot_t(qh, kp_ref[0, :, ks]) + bias_ref[h, :, 0:WINDOW], NEG_INF)
        s_cur = jnp.where(valid_cur, _dot_t(qh, kc_ref[0, :, ks]) + bias_ref[h, :, WINDOW:2 * WINDOW], NEG_INF)
        p_prev, p_cur = _softmax_sink([s_prev, s_cur], sinks_ref[h])
        outs.append(_dot(p_prev, vp_ref[0, :, ks]) + _dot(p_cur, vc_ref[0, :, ks]))
    o_ref[0] = jnp.concatenate(outs, axis=-1).astype(bf16)


def _attn_prompt(q, k, v, bias, sinks):
    B, L, _ = q.shape
    nb = L // WINDOW
    cur = lambda b, j: (b, j, 0)
    prev = lambda b, j: (b, jnp.maximum(j - 1, 0), 0)
    return pl.pallas_call(
        _attn_prompt_body,
        grid=(B, nb),
        in_specs=[
            pl.BlockSpec(memory_space=pltpu.SMEM),
            pl.BlockSpec((1, WINDOW, ATTN_W), cur),
            pl.BlockSpec((1, WINDOW, KV_W), prev),
            pl.BlockSpec((1, WINDOW, KV_W), cur),
            pl.BlockSpec((1, WINDOW, KV_W), prev),
            pl.BlockSpec((1, WINDOW, KV_W), cur),
            pl.BlockSpec((N_HEADS, WINDOW, 2 * WINDOW), lambda b, j: (0, 0, 0)),
        ],
        out_specs=pl.BlockSpec((1, WINDOW, ATTN_W), cur),
        out_shape=jax.ShapeDtypeStruct((B, L, ATTN_W), bf16),
        name="attn_prompt",
    )(sinks, q, k, k, v, v, bias)


def _attn_sample_body(q_ref, kb_ref, vb_ref, kn_ref, vn_ref, bb_ref, bn_ref, sink_ref,
                      o_ref, nk_ref, nv_ref, *, bb, ls):
    W = kb_ref.shape[1]
    rows = GQA * ls
    qi = lax.broadcasted_iota(i32, (rows, W), 0) % ls
    kj = lax.broadcasted_iota(i32, (rows, W), 1)
    valid_buf = kj > qi
    qi2 = lax.broadcasted_iota(i32, (rows, ls), 0) % ls
    kj2 = lax.broadcasted_iota(i32, (rows, ls), 1)
    valid_new = kj2 <= qi2

    def one(b, carry):
        qb = q_ref[b]
        kbuf = kb_ref[b]
        vbuf = vb_ref[b]
        knew = kn_ref[b]
        vnew = vn_ref[b]
        outs = []
        for kv in range(N_KV_HEADS):
            ks = slice(kv * HEAD_DIM, (kv + 1) * HEAD_DIM)
            qg = jnp.concatenate(
                [qb[:, (kv * GQA + g) * HEAD_DIM:(kv * GQA + g + 1) * HEAD_DIM] for g in range(GQA)], axis=0)
            s_buf = jnp.where(valid_buf, _dot_t(qg, kbuf[:, ks].astype(bf16)) + bb_ref[kv], NEG_INF)
            s_new = jnp.where(valid_new, _dot_t(qg, knew[:, ks].astype(bf16)) + bn_ref[kv], NEG_INF)
            p_buf, p_new = _softmax_sink([s_buf, s_new], sink_ref[kv])
            o = _dot(p_buf, vbuf[:, ks].astype(bf16)) + _dot(p_new, vnew[:, ks].astype(bf16))
            outs.extend([o[g * ls:(g + 1) * ls] for g in range(GQA)])
        o_ref[b] = jnp.concatenate(outs, axis=-1).astype(bf16)
        nk_ref[b, 0:W - ls, :] = kbuf[ls:, :]
        nk_ref[b, W - ls:W, :] = knew
        nv_ref[b, 0:W - ls, :] = vbuf[ls:, :]
        nv_ref[b, W - ls:W, :] = vnew
        return carry

    lax.fori_loop(0, bb, one, 0)


def _attn_sample(q, k_buf, v_buf, k_new, v_new, bias_buf, bias_new, sink_col, bb=16):
    B, ls, _ = q.shape
    W = k_buf.shape[1]
    rows = GQA * ls
    blk = lambda shape: pl.BlockSpec(shape, lambda i: (i, 0, 0))
    full = lambda shape: pl.BlockSpec(shape, lambda i: (0, 0, 0))
    return pl.pallas_call(
        functools.partial(_attn_sample_body, bb=bb, ls=ls),
        grid=(B // bb,),
        in_specs=[
            blk((bb, ls, ATTN_W)),
            blk((bb, W, KV_W)),
            blk((bb, W, KV_W)),
            blk((bb, ls, KV_W)),
            blk((bb, ls, KV_W)),
            full((N_KV_HEADS, rows, W)),
            full((N_KV_HEADS, rows, ls)),
            full((N_KV_HEADS, rows, 1)),
        ],
        out_specs=[blk((bb, ls, ATTN_W)), blk((bb, W, KV_W)), blk((bb, W, KV_W))],
        out_shape=[
            jax.ShapeDtypeStruct((B, ls, ATTN_W), bf16),
            jax.ShapeDtypeStruct((B, W, KV_W), f32),
            jax.ShapeDtypeStruct((B, W, KV_W), f32),
        ],
        name="attn_sample",
    )(q, k_buf, v_buf, k_new, v_new, bias_buf, bias_new, sink_col)


def _outproj_core(po, at, x, gt, sc, sh, g_ref, wo_ref, wrh_ref, wrl_ref):
    mixo = _dot(po, wo_ref[0:POOL_W, :]) + _dot(at, wo_ref[POOL_W:, :])
    x1 = x + gt * mixo.reshape(x.shape)
    h2 = _mod_norm(x1, g_ref[...].reshape((1,) * (x.ndim - 1) + (D_MODEL,)), sc, sh)
    h_hi, h_lo = _split_bf16(h2.reshape(-1, D_MODEL))
    wh = wrh_ref[...]
    logits = _dot_t(wh, h_hi) + (_dot_t(wh, h_lo) + _dot_t(wrl_ref[...], h_hi))
    return x1, h2, logits


def _outproj_prompt_body(po_ref, at_ref, x_ref, gt_ref, sc_ref, sh_ref, g_ref, wo_ref, wrh_ref, wrl_ref,
                         x1_ref, h2_ref, lg_ref):
    x1, h2, logits = _outproj_core(po_ref[0], at_ref[0], x_ref[0], gt_ref[0], sc_ref[0], sh_ref[0],
                                   g_ref, wo_ref, wrh_ref, wrl_ref)
    x1_ref[...] = x1
    h2_ref[...] = h2
    lg_ref[...] = logits


def _outproj_sample_body(po_ref, at_ref, x_ref, gt_ref, sc_ref, sh_ref, g_ref, wo_ref, wrh_ref, wrl_ref,
                         x1_ref, h2_ref, lg_ref):
    x1, h2, logits = _outproj_core(po_ref[...], at_ref[...], x_ref[...], gt_ref[...], sc_ref[...], sh_ref[...],
                                   g_ref, wo_ref, wrh_ref, wrl_ref)
    x1_ref[...] = x1.reshape(-1, D_MODEL)
    h2_ref[...] = h2.reshape(-1, D_MODEL)
    lg_ref[...] = logits


def _outproj_prompt(po, at, x, mod3, g_ffn, w_out, wr_hi, wr_lo, tm=512):
    B, L, D = x.shape
    nt = L // tm
    n_tok = B * L
    full = lambda shape: pl.BlockSpec(shape, lambda b, j: (0,) * len(shape))
    modspec = lambda c: pl.BlockSpec((1, 1, D), lambda b, j: (b, 0, c))
    return pl.pallas_call(
        _outproj_prompt_body,
        grid=(B, nt),
        in_specs=[
            pl.BlockSpec((1, tm, POOL_W), lambda b, j: (b, j, 0)),
            pl.BlockSpec((1, tm, ATTN_W), lambda b, j: (b, j, 0)),
            pl.BlockSpec((1, tm, D), lambda b, j: (b, j, 0)),
            modspec(2), modspec(4), modspec(3),
            full((1, D)), full((D, D)), full((N_EXPERTS, D)), full((N_EXPERTS, D)),
        ],
        out_specs=[
            pl.BlockSpec((tm, D), lambda b, j: (b * nt + j, 0)),
            pl.BlockSpec((tm, D), lambda b, j: (b * nt + j, 0)),
            pl.BlockSpec((N_EXPERTS, tm), lambda b, j: (0, b * nt + j)),
        ],
        out_shape=[
            jax.ShapeDtypeStruct((n_tok, D), f32),
            jax.ShapeDtypeStruct((n_tok, D), f32),
            jax.ShapeDtypeStruct((N_EXPERTS, n_tok), f32),
        ],
        name="outproj_prompt",
    )(po, at, x, mod3, mod3, mod3, g_ffn, w_out, wr_hi, wr_lo)


def _outproj_sample(po, at, x, mod3, g_ffn, w_out, wr_hi, wr_lo, bt=64):
    B, ls, D = x.shape
    n = bt * ls
    full = lambda shape: pl.BlockSpec(shape, lambda i: (0,) * len(shape))
    modspec = lambda c: pl.BlockSpec((bt, 1, D), lambda i: (i, 0, c))
    return pl.pallas_call(
        _outproj_sample_body,
        grid=(B // bt,),
        in_specs=[
            pl.BlockSpec((n, POOL_W), lambda i: (i, 0)),
            pl.BlockSpec((n, ATTN_W), lambda i: (i, 0)),
            pl.BlockSpec((bt, ls, D), lambda i: (i, 0, 0)),
            modspec(2), modspec(4), modspec(3),
            full((1, D)), full((D, D)), full((N_EXPERTS, D)), full((N_EXPERTS, D)),
        ],
        out_specs=[
            pl.BlockSpec((n, D), lambda i: (i, 0)),
            pl.BlockSpec((n, D), lambda i: (i, 0)),
            pl.BlockSpec((N_EXPERTS, n), lambda i: (0, i)),
        ],
        out_shape=[
            jax.ShapeDtypeStruct((B * ls, D), f32),
            jax.ShapeDtypeStruct((B * ls, D), f32),
            jax.ShapeDtypeStruct((N_EXPERTS, B * ls), f32),
        ],
        name="outproj_sample",
    )(po, at, x, mod3, mod3, mod3, g_ffn, w_out, wr_hi, wr_lo)


def _route_body(lg_ref, rb_ref, tri_ref, idx_ref, rank_ref, gate_ref, cnt_ref, carry, *, tr, nsteps):
    step = pl.program_id(0)

    @pl.when(step == 0)
    def _():
        carry[...] = jnp.zeros(carry.shape, f32)

    s = jax.nn.sigmoid(lg_ref[...])
    sb = s + rb_ref[...]
    e_iota = lax.broadcasted_iota(i32, (N_EXPERTS, tr), 0)
    g_iota = lax.broadcasted_iota(i32, (GROUP_SIZE, tr), 0)

    gscore = []
    for g in range(N_EXPERT_GROUPS):
        v = sb[g * GROUP_SIZE:(g + 1) * GROUP_SIZE]
        m1 = jnp.max(v, axis=0, keepdims=True)
        i1 = jnp.min(jnp.where(v == m1, g_iota, GROUP_SIZE), axis=0, keepdims=True)
        m2 = jnp.max(jnp.where(g_iota == i1, -jnp.inf, v), axis=0, keepdims=True)
        gscore.append(m1 + m2)
    parts = []
    for g in range(N_EXPERT_GROUPS):
        beaten = jnp.zeros((1, tr), i32)
        for g2 in range(N_EXPERT_GROUPS):
            if g2 == g:
                continue
            ahead = gscore[g2] > gscore[g]
            if g2 < g:
                ahead = ahead | (gscore[g2] == gscore[g])
            beaten = beaten + ahead.astype(i32)
        keep = beaten < TOPK_GROUPS
        parts.append(jnp.where(keep, sb[g * GROUP_SIZE:(g + 1) * GROUP_SIZE], NEG_INF))
    cur = jnp.concatenate(parts, axis=0)

    sel = jnp.zeros((N_EXPERTS, tr), f32)
    idxs, svals = [], []
    for _ in range(TOP_K):
        m = jnp.max(cur, axis=0, keepdims=True)
        ik = jnp.min(jnp.where(cur == m, e_iota, N_EXPERTS), axis=0, keepdims=True)
        hit = e_iota == ik
        svals.append(jnp.sum(jnp.where(hit, s, 0.0), axis=0, keepdims=True))
        cur = jnp.where(hit, -jnp.inf, cur)
        sel = jnp.where(hit, 1.0, sel)
        idxs.append(ik)
    ssum = svals[0]
    for sv in svals[1:]:
        ssum = ssum + sv
    gate_ref[...] = jnp.concatenate([sv / ssum * ROUTED_SCALE for sv in svals], axis=0)
    idx_ref[...] = jnp.concatenate(idxs, axis=0)

    before = carry[...] + _dot(sel.astype(bf16), tri_ref[...])
    ranks = [jnp.sum(jnp.where(e_iota == ik, before, 0.0), axis=0, keepdims=True) for ik in idxs]
    rank_ref[...] = jnp.concatenate(ranks, axis=0).astype(i32)
    carry[...] = carry[...] + jnp.sum(sel, axis=1, keepdims=True)

    @pl.when(step == nsteps - 1)
    def _():
        cnt_ref[...] = carry[...]


def _route(logits_t, router_bias, tr=512):
    E, T = logits_t.shape
    nsteps = T // tr
    tri = jnp.asarray(np.triu(np.ones((tr, tr), np.float32), 1), bf16)
    return pl.pallas_call(
        functools.partial(_route_body, tr=tr, nsteps=nsteps),
        grid=(nsteps,),
        in_specs=[
            pl.BlockSpec((E, tr), lambda i: (0, i)),
            pl.BlockSpec((E, 1), lambda i: (0, 0)),
            pl.BlockSpec((tr, tr), lambda i: (0, 0)),
        ],
        out_specs=[
            pl.BlockSpec((TOP_K, tr), lambda i: (0, i)),
            pl.BlockSpec((TOP_K, tr), lambda i: (0, i)),
            pl.BlockSpec((TOP_K, tr), lambda i: (0, i)),
            pl.BlockSpec((E, 1), lambda i: (0, 0)),
        ],
        out_shape=[
            jax.ShapeDtypeStruct((TOP_K, T), i32),
            jax.ShapeDtypeStruct((TOP_K, T), i32),
            jax.ShapeDtypeStruct((TOP_K, T), f32),
            jax.ShapeDtypeStruct((E, 1), f32),
        ],
        scratch_shapes=[pltpu.VMEM((E, 1), f32)],
        compiler_params=pltpu.CompilerParams(dimension_semantics=("arbitrary",)),
        name="route",
    )(logits_t, router_bias.reshape(E, 1), tri)


def _sc_mesh():
    return plsc.VectorSubcoreMesh(core_axis_name="c", subcore_axis_name="s")


def _sc_worker_id():
    return lax.axis_index("s") * 2 + lax.axis_index("c")


def _dispatch(h2_a, h2_b, dest, n_rows, chunk=32):
    ta, D = h2_a.shape
    T = ta + h2_b.shape[0]
    per_worker = T // SC_WORKERS
    nchunk = per_worker // chunk
    assert per_worker * SC_WORKERS == T and nchunk * chunk == per_worker and ta % chunk == 0

    @functools.partial(
        pl.kernel, mesh=_sc_mesh(),
        out_type=jax.ShapeDtypeStruct((n_rows, D), f32),
        scratch_types=[pltpu.VMEM((chunk,), i32), pltpu.VMEM((chunk, D), f32)],
        name="moe_dispatch",
    )
    def body(ha_hbm, hb_hbm, dest_hbm, xs_hbm, idx_v, rows_v):
        base = _sc_worker_id() * per_worker

        @pl.loop(0, nchunk)
        def _(ci):
            t0 = base + ci * chunk

            @pl.when(t0 < ta)
            def _():
                pltpu.sync_copy(ha_hbm.at[pl.ds(t0, chunk)], rows_v)

            @pl.when(t0 >= ta)
            def _():
                pltpu.sync_copy(hb_hbm.at[pl.ds(t0 - ta, chunk)], rows_v)

            for k in range(TOP_K):
                pltpu.sync_copy(dest_hbm.at[k, pl.ds(t0, chunk)], idx_v)
                pltpu.sync_copy(rows_v, xs_hbm.at[idx_v])

    return body(h2_a, h2_b, dest)


def _combine(ys, dest, gate, chunk=8):
    T = dest.shape[1]
    D = ys.shape[1]
    per_worker = T // SC_WORKERS
    nchunk = per_worker // chunk

    @functools.partial(
        pl.kernel, mesh=_sc_mesh(),
        out_type=jax.ShapeDtypeStruct((T, D), f32),
        scratch_types=[
            pltpu.VMEM((TOP_K, chunk), i32),
            pltpu.VMEM((TOP_K * chunk,), f32),
            pltpu.VMEM((TOP_K, chunk, D), f32),
            pltpu.VMEM((chunk, D), f32),
            pltpu.SemaphoreType.DMA,
        ],
        compiler_params=pltpu.CompilerParams(needs_layout_passes=False),
        name="moe_combine",
    )
    def body(ys_hbm, dest_hbm, gate_hbm, out_hbm, idx_v, gate_v, buf, out_v, sem):
        base = _sc_worker_id() * per_worker

        @pl.loop(0, nchunk)
        def _(ci):
            t0 = base + ci * chunk
            for k in range(TOP_K):
                pltpu.sync_copy(dest_hbm.at[k, pl.ds(t0, chunk)], idx_v.at[k])
                pltpu.sync_copy(gate_hbm.at[k, pl.ds(t0, chunk)], gate_v.at[pl.ds(k * chunk, chunk)])
            copies = [pltpu.async_copy(ys_hbm.at[idx_v.at[k]], buf.at[k], sem) for k in range(TOP_K)]
            for cp in copies:
                cp.wait()

            @pl.loop(0, chunk)
            def _(t):
                gs = [plsc.load_gather(gate_v, [jnp.full((SC_LANES,), k * chunk, i32) + t]) for k in range(TOP_K)]

                @pl.loop(0, D // SC_LANES)
                def _(j):
                    sl = pl.ds(j * SC_LANES, SC_LANES)
                    acc = gs[0] * buf[0, t, sl]
                    for k in range(1, TOP_K):
                        acc = acc + gs[k] * buf[k, t, sl]
                    out_v[t, sl] = acc

            pltpu.sync_copy(out_v, out_hbm.at[pl.ds(t0, chunk)])

    return body(ys, dest, gate)


def _gmm_body(blk_e_ref, blk_rows_ref, nv_ref, xs_ref, w1_ref, w3_ref, w2_ref, ys_ref):
    b = pl.program_id(0)

    @pl.when(b < nv_ref[0])
    def _():
        rows = lax.broadcasted_iota(i32, (EXPERT_BLOCK, 1), 0)
        x = jnp.where(rows < blk_rows_ref[b], xs_ref[...], 0.0).astype(bf16)
        a = _dot(x, w1_ref[0].astype(bf16))
        c = _dot(x, w3_ref[0].astype(bf16))
        hmid = (a * jax.nn.sigmoid(a)) * c
        ys_ref[...] = _dot(hmid.astype(bf16), w2_ref[0].astype(bf16))


def _gmm(xs, w1, w3, w2, blk_e, blk_rows, n_valid):
    n_rows, D = xs.shape
    nb = n_rows // EXPERT_BLOCK
    row_map = lambda b, be, br, nv: (jnp.minimum(b, nv[0] - 1), 0)
    w_map = lambda b, be, br, nv: (be[jnp.minimum(b, nv[0] - 1)], 0, 0)
    return pl.pallas_call(
        _gmm_body,
        grid_spec=pltpu.PrefetchScalarGridSpec(
            num_scalar_prefetch=3,
            grid=(nb,),
            in_specs=[
                pl.BlockSpec((EXPERT_BLOCK, D), row_map),
                pl.BlockSpec((1, D, EXPERT_FF), w_map),
                pl.BlockSpec((1, D, EXPERT_FF), w_map),
                pl.BlockSpec((1, EXPERT_FF, D), w_map),
            ],
            out_specs=pl.BlockSpec((EXPERT_BLOCK, D), row_map),
        ),
        out_shape=jax.ShapeDtypeStruct((n_rows, D), f32),
        compiler_params=pltpu.CompilerParams(dimension_semantics=("arbitrary",)),
        name="moe_gmm",
    )(blk_e, blk_rows, n_valid, xs, w1, w3, w2)


def _final_core(x1, h2, comb, gt, ws1_ref, ws3_ref, ws2_ref):
    hb = h2.astype(bf16)
    a = _dot(hb, ws1_ref[...])
    c = _dot(hb, ws3_ref[...])
    shared = _dot(((a * jax.nn.sigmoid(a)) * c).astype(bf16), ws2_ref[...])
    return x1, comb + shared, gt


def _final_prompt_body(x1_ref, h2_ref, cb_ref, gt_ref, ws1_ref, ws3_ref, ws2_ref, y_ref):
    x1, ffn, gt = _final_core(x1_ref[...], h2_ref[...], cb_ref[...], gt_ref[0], ws1_ref, ws3_ref, ws2_ref)
    y_ref[0] = x1 + gt * ffn


def _final_sample_body(x1_ref, h2_ref, cb_ref, gt_ref, ws1_ref, ws3_ref, ws2_ref, y_ref):
    x1, ffn, gt = _final_core(x1_ref[...], h2_ref[...], cb_ref[...], gt_ref[...], ws1_ref, ws3_ref, ws2_ref)
    shp = y_ref.shape
    y_ref[...] = x1.reshape(shp) + gt * ffn.reshape(shp)


def _final_prompt(x1, h2, comb, mod3, ws1, ws3, ws2, B, L, tm=512):
    D = D_MODEL
    nt = L // tm
    full = lambda shape: pl.BlockSpec(shape, lambda b, j: (0,) * len(shape))
    rows = pl.BlockSpec((tm, D), lambda b, j: (b * nt + j, 0))
    return pl.pallas_call(
        _final_prompt_body,
        grid=(B, nt),
        in_specs=[rows, rows, rows, pl.BlockSpec((1, 1, D), lambda b, j: (b, 0, 5)),
                  full((D, EXPERT_FF)), full((D, EXPERT_FF)), full((EXPERT_FF, D))],
        out_specs=pl.BlockSpec((1, tm, D), lambda b, j: (b, j, 0)),
        out_shape=jax.ShapeDtypeStruct((B, L, D), f32),
        name="final_prompt",
    )(x1, h2, comb, mod3, ws1, ws3, ws2)


def _final_sample(x1, h2, comb, mod3, ws1, ws3, ws2, B, ls, row0, bt=64):
    D = D_MODEL
    n = bt * ls
    blk0 = row0 // n
    full = lambda shape: pl.BlockSpec(shape, lambda i: (0,) * len(shape))
    rows = pl.BlockSpec((n, D), lambda i: (i, 0))
    comb_rows = pl.BlockSpec((n, D), lambda i: (blk0 + i, 0))
    return pl.pallas_call(
        _final_sample_body,
        grid=(B // bt,),
        in_specs=[rows, rows, comb_rows, pl.BlockSpec((bt, 1, D), lambda i: (i, 0, 5)),
                  full((D, EXPERT_FF)), full((D, EXPERT_FF)), full((EXPERT_FF, D))],
        out_specs=pl.BlockSpec((bt, ls, D), lambda i: (i, 0, 0)),
        out_shape=jax.ShapeDtypeStruct((B, ls, D), f32),
        name="final_sample",
    )(x1, h2, comb, mod3, ws1, ws3, ws2)


def kernel(x_prompt, x_sample, state_pool, cache_swa_k, cache_swa_v, c_prompt, c_sample, w_ada, b_ada,
           g_attn_norm, w_in, g_q, g_k, w_pool, pool_scale, w_out, attn_sinks, rel_bias, g_ffn_norm,
           w_router, router_bias, w1, w3, w2, ws1, ws3, ws2):
    B, L, D = x_prompt.shape
    BS, LS, _ = x_sample.shape
    depth = w_ada.shape[0]
    assert depth == 1
    W = cache_swa_k.shape[2]
    tp, ts = B * L, BS * LS
    T = tp + ts
    n_rows = (T * TOP_K // EXPERT_BLOCK + N_EXPERTS) * EXPERT_BLOCK
    nb = n_rows // EXPERT_BLOCK

    g_attn = g_attn_norm[0].reshape(1, D)
    g_ffn = g_ffn_norm[0].reshape(1, D)
    w_in_b = w_in[0].astype(bf16)
    w_out_b = w_out[0].astype(bf16)
    w_pool_b = w_pool[0].astype(bf16)
    ps = pool_scale[0].reshape(1, POOL_W)
    gqk = jnp.concatenate([jnp.tile(g_q[0], N_HEADS), jnp.tile(g_k[0], N_KV_HEADS)]).reshape(1, QK_W)
    head_of = np.arange(QK_W) // HEAD_DIM
    bd = jnp.asarray((head_of[:, None] == head_of[None, :]).astype(np.float32), bf16)
    wr_t = w_router[0].T
    wr_hi = wr_t.astype(bf16)
    wr_lo = (wr_t - wr_hi.astype(f32)).astype(bf16)
    ws1_b, ws3_b, ws2_b = ws1[0].astype(bf16), ws3[0].astype(bf16), ws2[0].astype(bf16)
    sinks = attn_sinks[0]

    mod = _ada(jnp.concatenate([c_prompt, c_sample], axis=0), w_ada[0], b_ada[0])
    mod_p = mod[:B].reshape(B, 1, 6 * D)
    mod_s = mod[B:].reshape(BS, 1, 6 * D)

    dist_p = np.arange(WINDOW)[:, None] + WINDOW - np.arange(2 * WINDOW)[None, :]
    bias_p = _relbias(rel_bias, dist_p)
    dist_s = np.arange(LS)[:, None] + W - np.arange(W + LS)[None, :]
    bias_s = _relbias(rel_bias, dist_s)
    bias_s_buf = bias_s[:, :, :W].reshape(N_KV_HEADS, GQA * LS, W)
    bias_s_new = bias_s[:, :, W:].reshape(N_KV_HEADS, GQA * LS, LS)
    sink_col = jnp.repeat(sinks, LS).reshape(N_KV_HEADS, GQA * LS, 1)

    q_p, k_p, v_p, po_p, new_pool_p, kc_p, vc_p = _inproj_prompt(
        x_prompt, mod_p, g_attn, w_in_b, gqk, bd, w_pool_b, ps)
    q_s, k_s, v_s, po_s, new_pool_s = _inproj_sample(
        x_sample, mod_s, g_attn, w_in_b, gqk, bd, w_pool_b, ps, state_pool[0], PAST_LEN)
    at_p = _attn_prompt(q_p, k_p, v_p, bias_p, sinks)
    at_s, nk_s, nv_s = _attn_sample(
        q_s.reshape(BS, LS, ATTN_W), cache_swa_k[0].reshape(BS, W, KV_W), cache_swa_v[0].reshape(BS, W, KV_W),
        k_s.reshape(BS, LS, KV_W), v_s.reshape(BS, LS, KV_W), bias_s_buf, bias_s_new, sink_col)

    x1_p, h2_p, lg_p = _outproj_prompt(po_p, at_p, x_prompt, mod_p, g_ffn, w_out_b, wr_hi, wr_lo)
    x1_s, h2_s, lg_s = _outproj_sample(po_s, at_s.reshape(ts, ATTN_W), x_sample, mod_s, g_ffn, w_out_b,
                                       wr_hi, wr_lo)

    idx, rank, gate, counts = _route(jnp.concatenate([lg_p, lg_s], axis=1), router_bias[0])
    counts = counts.reshape(N_EXPERTS).astype(i32)
    padded = (counts + EXPERT_BLOCK - 1) // EXPERT_BLOCK * EXPERT_BLOCK
    pad_end = jnp.cumsum(padded)
    pad_start = pad_end - padded
    dest = pad_start[idx] + rank
    n_valid = (pad_end[-1] // EXPERT_BLOCK).astype(i32).reshape(1)
    blk_row0 = jnp.arange(nb, dtype=i32) * EXPERT_BLOCK
    blk_e = jnp.minimum(jnp.searchsorted(pad_end, blk_row0, side="right"), N_EXPERTS - 1).astype(i32)
    blk_rows = jnp.clip(counts[blk_e] - (blk_row0 - pad_start[blk_e]), 0, EXPERT_BLOCK).astype(i32)

    xs = _dispatch(h2_p, h2_s, dest, n_rows)
    ys = _gmm(xs, w1[0], w3[0], w2[0], blk_e, blk_rows, n_valid)
    comb = _combine(ys, dest, gate)

    y_p = _final_prompt(x1_p, h2_p, comb, mod_p, ws1_b, ws3_b, ws2_b, B, L)
    y_s = _final_sample(x1_s, h2_s, comb, mod_s, ws1_b, ws3_b, ws2_b, BS, LS, tp)

    return (y_p, y_s, new_pool_p[None], kc_p.reshape(1, B, WINDOW, N_KV_HEADS, HEAD_DIM),
            vc_p.reshape(1, B, WINDOW, N_KV_HEADS, HEAD_DIM), new_pool_s[None],
            nk_s.reshape(1, BS, W, N_KV_HEADS, HEAD_DIM), nv_s.reshape(1, BS, W, N_KV_HEADS, HEAD_DIM))
```

```python
import functools
import math

import numpy as np
import jax
import jax.numpy as jnp
from jax import lax
from jax.experimental import pallas as pl
from jax.experimental.pallas import tpu as pltpu
from jax.experimental.pallas import tpu_sc as plsc

f32 = jnp.float32
bf16 = jnp.bfloat16
i32 = jnp.int32

D_MODEL = 1024
PAST_LEN = 8192
POOL_W = 512
POOL_WINDOWS = (2, 4, 8, 16)
POOL_GC = 128
POOL_BUF = 15
ATTN_W = 512
HEAD_DIM = 64
N_HEADS = 8
N_KV_HEADS = 2
GQA = 4
KV_W = 128
WINDOW = 128
NUM_BUCKETS = 32
MAX_EXACT = 16
REL_MAX_DIST = 128
N_EXPERTS = 256
N_EXPERT_GROUPS = 8
GROUP_SIZE = 32
TOPK_GROUPS = 4
TOP_K = 8
EXPERT_FF = 256
ROUTED_SCALE = 2.5
EXPERT_BLOCK = 128
EPS = 1e-6
NEG_INF = -1e30
QKV_W = POOL_W + ATTN_W + 2 * KV_W
QK_W = ATTN_W + KV_W
HIST = 16

SC_WORKERS = 32
SC_LANES = 16
GATE_ROW = 128


def _dot(a, b):
    return jnp.dot(a, b, preferred_element_type=f32)


def _dot_t(a, b):
    return lax.dot_general(a, b, (((1,), (1,)), ((), ())), preferred_element_type=f32)


def _split_bf16(a):
    hi = a.astype(bf16)
    lo = (a - hi.astype(f32)).astype(bf16)
    return hi, lo


HI_MASK = -65536


def _pack_pairs(a):
    h = a.shape[1] // 2
    hi = lax.bitcast_convert_type(a[:, :h].astype(bf16).astype(f32), i32)
    lo = lax.bitcast_convert_type(a[:, h:].astype(bf16).astype(f32), i32)
    return hi | lax.shift_right_logical(lo, 16)


def _unpack_pairs(w):
    hi = lax.bitcast_convert_type(w & HI_MASK, f32).astype(bf16)
    lo = lax.bitcast_convert_type(lax.shift_left(w, 16), f32).astype(bf16)
    return hi, lo


def _mod_norm(x, g, sc, sh):
    ms = jnp.mean(x * x, axis=-1, keepdims=True)
    y = x * lax.rsqrt(ms + EPS)
    return (y * g) * (1.0 + sc) + sh


def _ada_body(c_ref, w_ref, b_ref, o_ref):
    c = c_ref[...]
    a = (c * jax.nn.sigmoid(c)).astype(bf16)
    o_ref[...] = _dot(a, w_ref[...].astype(bf16)) + b_ref[...]


def _ada(c, w_ada, b_ada):
    n = c.shape[0]
    tn = 1024
    return pl.pallas_call(
        _ada_body,
        grid=(6 * D_MODEL // tn,),
        in_specs=[
            pl.BlockSpec((n, D_MODEL), lambda j: (0, 0)),
            pl.BlockSpec((D_MODEL, tn), lambda j: (0, j)),
            pl.BlockSpec((1, tn), lambda j: (0, j)),
        ],
        out_specs=pl.BlockSpec((n, tn), lambda j: (0, j)),
        out_shape=jax.ShapeDtypeStruct((n, 6 * D_MODEL), f32),
        name="ada_mod",
    )(c, w_ada, b_ada.reshape(1, -1))


def _relbias_body(table_ref, bucket_ref, o_ref):
    bucket = bucket_ref[...]
    for h in range(N_HEADS):
        acc = jnp.zeros(bucket.shape, f32)
        for b in range(NUM_BUCKETS):
            acc = jnp.where(bucket == b, table_ref[b, h], acc)
        o_ref[h] = acc


def _rel_buckets(dist):
    n = np.maximum(dist, 0)
    nf = np.maximum(n, 1).astype(np.float64)
    large = MAX_EXACT + (np.log(nf / MAX_EXACT) / math.log(REL_MAX_DIST / MAX_EXACT)
                         * (NUM_BUCKETS - MAX_EXACT)).astype(np.int32)
    return np.where(n < MAX_EXACT, n, np.minimum(large, NUM_BUCKETS - 1)).astype(np.int32)


def _relbias(table, dist):
    lq, lk = dist.shape
    return pl.pallas_call(
        _relbias_body,
        in_specs=[
            pl.BlockSpec(memory_space=pltpu.SMEM),
            pl.BlockSpec((lq, lk), lambda: (0, 0)),
        ],
        out_specs=pl.BlockSpec((N_HEADS, lq, lk), lambda: (0, 0, 0)),
        out_shape=jax.ShapeDtypeStruct((N_HEADS, lq, lk), f32),
        name="rel_bias",
    )(table, jnp.asarray(_rel_buckets(dist)))


def _qkv_from_h(h, w_ref, gqk_ref, bd_ref):
    u = _dot(h.astype(bf16), w_ref[...])
    qk = u[:, POOL_W:POOL_W + QK_W]
    y_hi, y_lo = _split_bf16(qk * qk)
    bd = bd_ref[...]
    ss = _dot(y_hi, bd) + _dot(y_lo, bd)
    qkn = (qk * lax.rsqrt(ss * (1.0 / HEAD_DIM) + EPS)) * gqk_ref[...]
    q = qkn[:, :ATTN_W] * (HEAD_DIM ** -0.5)
    k = qkn[:, ATTN_W:]
    v = u[:, POOL_W + QK_W:]
    return u[:, :POOL_W], q, k, v


def _inproj_prompt_body(x_ref, sh_ref, sc_ref, g_ref, w_ref, gqk_ref, bd_ref, wp_ref, ps_ref,
                        q_ref, k_ref, v_ref, po_ref, np_ref, kc_ref, vc_ref, hist, *, tl, nt):
    j = pl.program_id(1)
    h = _mod_norm(x_ref[0], g_ref[...], sc_ref[0], sh_ref[0])
    up, q, k, v = _qkv_from_h(h, w_ref, gqk_ref, bd_ref)
    q_ref[0] = q.astype(bf16)
    k_ref[0] = k.astype(bf16)
    v_ref[0] = v.astype(bf16)

    @pl.when(j == nt - 1)
    def _():
        kc_ref[0] = k[tl - WINDOW:, :]
        vc_ref[0] = v[tl - WINDOW:, :]

    @pl.when(j == 0)
    def _():
        hist[0:HIST, :] = jnp.zeros((HIST, POOL_W), f32)

    hist[HIST:HIST + tl, :] = up
    pos = j * tl + lax.broadcasted_iota(i32, (tl, 1), 0)
    for g, w in enumerate(POOL_WINDOWS):
        lanes = slice(g * POOL_GC, (g + 1) * POOL_GC)
        cur = up[:, lanes]
        acc = cur
        for s in range(1, w):
            acc = acc + hist[HIST - s:HIST - s + tl, lanes]
        cnt = jnp.minimum(w, pos + 1).astype(f32)
        d = acc / cnt - cur
        yg = _dot(d.astype(bf16), wp_ref[g]) * ps_ref[:, lanes]
        po_ref[0, :, lanes] = yg.astype(bf16)

    @pl.when(j == nt - 1)
    def _():
        np_ref[0] = hist[tl + 1:tl + HIST, :]

    hist[0:HIST, :] = hist[tl:tl + HIST, :]


def _inproj_prompt(x, mod3, g_attn, w_in, gqk, bd, w_pool, pool_scale, tl=512):
    B, L, D = x.shape
    nt = L // tl
    full = lambda shape: pl.BlockSpec(shape, lambda b, j: (0,) * len(shape))
    return pl.pallas_call(
        functools.partial(_inproj_prompt_body, tl=tl, nt=nt),
        grid=(B, nt),
        in_specs=[
            pl.BlockSpec((1, tl, D), lambda b, j: (b, j, 0)),
            pl.BlockSpec((1, 1, D), lambda b, j: (b, 0, 0)),
            pl.BlockSpec((1, 1, D), lambda b, j: (b, 0, 1)),
            full((1, D)),
            full((D, QKV_W)),
            full((1, QK_W)),
            full((QK_W, QK_W)),
            full((4, POOL_GC, POOL_GC)),
            full((1, POOL_W)),
        ],
        out_specs=[
            pl.BlockSpec((1, tl, ATTN_W), lambda b, j: (b, j, 0)),
            pl.BlockSpec((1, tl, KV_W), lambda b, j: (b, j, 0)),
            pl.BlockSpec((1, tl, KV_W), lambda b, j: (b, j, 0)),
            pl.BlockSpec((1, tl, POOL_W), lambda b, j: (b, j, 0)),
            pl.BlockSpec((1, POOL_BUF, POOL_W), lambda b, j: (b, 0, 0)),
            pl.BlockSpec((1, WINDOW, KV_W), lambda b, j: (b, 0, 0)),
            pl.BlockSpec((1, WINDOW, KV_W), lambda b, j: (b, 0, 0)),
        ],
        out_shape=[
            jax.ShapeDtypeStruct((B, L, ATTN_W), bf16),
            jax.ShapeDtypeStruct((B, L, KV_W), bf16),
            jax.ShapeDtypeStruct((B, L, KV_W), bf16),
            jax.ShapeDtypeStruct((B, L, POOL_W), bf16),
            jax.ShapeDtypeStruct((B, POOL_BUF, POOL_W), f32),
            jax.ShapeDtypeStruct((B, WINDOW, KV_W), f32),
            jax.ShapeDtypeStruct((B, WINDOW, KV_W), f32),
        ],
        scratch_shapes=[pltpu.VMEM((HIST + tl, POOL_W), f32)],
        compiler_params=pltpu.CompilerParams(dimension_semantics=("arbitrary", "arbitrary")),
        name="inproj_prompt",
    )(x, mod3, mod3, g_attn, w_in, gqk, bd, w_pool, pool_scale)


def _inproj_sample_body(x_ref, sh_ref, sc_ref, g_ref, w_ref, gqk_ref, bd_ref, wp_ref, ps_ref, st_ref,
                        q_ref, k_ref, v_ref, po_ref, np_ref, ext, *, bt, ls, pos0):
    n = bt * ls
    h3 = _mod_norm(x_ref[...], g_ref[...][None], sc_ref[...], sh_ref[...])
    up, q, k, v = _qkv_from_h(h3.reshape(n, D_MODEL), w_ref, gqk_ref, bd_ref)
    q_ref[...] = q.astype(bf16)
    k_ref[...] = k
    v_ref[...] = v

    ext[:, 1:HIST, :] = st_ref[...]
    ext[:, HIST:HIST + ls, :] = up.reshape(bt, ls, POOL_W)
    pos = pos0 + lax.broadcasted_iota(i32, (1, ls, 1), 1)
    for g, w in enumerate(POOL_WINDOWS):
        lanes = slice(g * POOL_GC, (g + 1) * POOL_GC)
        cur = ext[:, HIST:HIST + ls, lanes]
        acc = cur
        for s in range(1, w):
            acc = acc + ext[:, HIST - s:HIST - s + ls, lanes]
        cnt = jnp.minimum(w, pos + 1).astype(f32)
        d = (acc / cnt - cur).reshape(n, POOL_GC)
        yg = _dot(d.astype(bf16), wp_ref[g]) * ps_ref[:, lanes]
        po_ref[:, lanes] = yg.astype(bf16)
    np_ref[...] = ext[:, ls + 1:ls + HIST, :]


def _inproj_sample(x, mod3, g_attn, w_in, gqk, bd, w_pool, pool_scale, state, pos0, bt=64):
    B, ls, D = x.shape
    n = bt * ls
    full = lambda shape: pl.BlockSpec(shape, lambda i: (0,) * len(shape))
    return pl.pallas_call(
        functools.partial(_inproj_sample_body, bt=bt, ls=ls, pos0=pos0),
        grid=(B // bt,),
        in_specs=[
            pl.BlockSpec((bt, ls, D), lambda i: (i, 0, 0)),
            pl.BlockSpec((bt, 1, D), lambda i: (i, 0, 0)),
            pl.BlockSpec((bt, 1, D), lambda i: (i, 0, 1)),
            full((1, D)),
            full((D, QKV_W)),
            full((1, QK_W)),
            full((QK_W, QK_W)),
            full((4, POOL_GC, POOL_GC)),
            full((1, POOL_W)),
            pl.BlockSpec((bt, POOL_BUF, POOL_W), lambda i: (i, 0, 0)),
        ],
        out_specs=[
            pl.BlockSpec((n, ATTN_W), lambda i: (i, 0)),
            pl.BlockSpec((n, KV_W), lambda i: (i, 0)),
            pl.BlockSpec((n, KV_W), lambda i: (i, 0)),
            pl.BlockSpec((n, POOL_W), lambda i: (i, 0)),
            pl.BlockSpec((bt, POOL_BUF, POOL_W), lambda i: (i, 0, 0)),
        ],
        out_shape=[
            jax.ShapeDtypeStruct((B * ls, ATTN_W), bf16),
            jax.ShapeDtypeStruct((B * ls, KV_W), f32),
            jax.ShapeDtypeStruct((B * ls, KV_W), f32),
            jax.ShapeDtypeStruct((B * ls, POOL_W), bf16),
            jax.ShapeDtypeStruct((B, POOL_BUF, POOL_W), f32),
        ],
        scratch_shapes=[pltpu.VMEM((bt, HIST + ls, POOL_W), f32)],
        name="inproj_sample",
    )(x, mod3, mod3, g_attn, w_in, gqk, bd, w_pool, pool_scale, state)


def _softmax_sink(parts, sink):
    m = sink
    for s in parts:
        m = jnp.maximum(m, jnp.max(s, axis=-1, keepdims=True))
    ps = [jnp.exp(s - m) for s in parts]
    denom = jnp.exp(sink - m)
    for p in ps:
        denom = denom + jnp.sum(p, axis=-1, keepdims=True)
    inv = 1.0 / denom
    return [(p * inv).astype(bf16) for p in ps]


def _attn_prompt_body(sinks_ref, q_ref, kp_ref, kc_ref, vp_ref, vc_ref, bias_ref, o_ref):
    j = pl.program_id(1)
    qi = lax.broadcasted_iota(i32, (WINDOW, WINDOW), 0)
    kj = lax.broadcasted_iota(i32, (WINDOW, WINDOW), 1)
    valid_prev = (kj > qi) & (j > 0)
    valid_cur = kj <= qi
    outs = []
    for h in range(N_HEADS):
        hs = slice(h * HEAD_DIM, (h + 1) * HEAD_DIM)
        ks = slice((h // GQA) * HEAD_DIM, (h // GQA + 1) * HEAD_DIM)
        qh = q_ref[0, :, hs]
        s_prev = jnp.where(valid_prev, _dot_t(qh, kp_ref[0, :, ks]) + bias_ref[h, :, 0:WINDOW], NEG_INF)
        s_cur = jnp.where(valid_cur, _dot_t(qh, kc_ref[0, :, ks]) + bias_ref[h, :, WINDOW:2 * WINDOW], NEG_INF)
        p_prev, p_cur = _softmax_sink([s_prev, s_cur], sinks_ref[h])
        outs.append(_dot(p_prev, vp_ref[0, :, ks]) + _dot(p_cur, vc_ref[0, :, ks]))
    o_ref[0] = jnp.concatenate(outs, axis=-1).astype(bf16)


def _attn_prompt(q, k, v, bias, sinks):
    B, L, _ = q.shape
    nb = L // WINDOW
    cur = lambda b, j: (b, j, 0)
    prev = lambda b, j: (b, jnp.maximum(j - 1, 0), 0)
    return pl.pallas_call(
        _attn_prompt_body,
        grid=(B, nb),
        in_specs=[
            pl.BlockSpec(memory_space=pltpu.SMEM),
            pl.BlockSpec((1, WINDOW, ATTN_W), cur),
            pl.BlockSpec((1, WINDOW, KV_W), prev),
            pl.BlockSpec((1, WINDOW, KV_W), cur),
            pl.BlockSpec((1, WINDOW, KV_W), prev),
            pl.BlockSpec((1, WINDOW, KV_W), cur),
            pl.BlockSpec((N_HEADS, WINDOW, 2 * WINDOW), lambda b, j: (0, 0, 0)),
        ],
        out_specs=pl.BlockSpec((1, WINDOW, ATTN_W), cur),
        out_shape=jax.ShapeDtypeStruct((B, L, ATTN_W), bf16),
        name="attn_prompt",
    )(sinks, q, k, k, v, v, bias)


def _attn_sample_body(q_ref, kb_ref, vb_ref, kn_ref, vn_ref, bb_ref, bn_ref, sink_ref,
                      o_ref, nk_ref, nv_ref, *, bb, ls):
    W = kb_ref.shape[1]
    rows = GQA * ls
    qi = lax.broadcasted_iota(i32, (rows, W), 0) % ls
    kj = lax.broadcasted_iota(i32, (rows, W), 1)
    valid_buf = kj > qi
    qi2 = lax.broadcasted_iota(i32, (rows, ls), 0) % ls
    kj2 = lax.broadcasted_iota(i32, (rows, ls), 1)
    valid_new = kj2 <= qi2

    def one(b, carry):
        qb = q_ref[b]
        kbuf = kb_ref[b]
        vbuf = vb_ref[b]
        knew = kn_ref[b]
        vnew = vn_ref[b]
        outs = []
        for kv in range(N_KV_HEADS):
            ks = slice(kv * HEAD_DIM, (kv + 1) * HEAD_DIM)
            qg = jnp.concatenate(
                [qb[:, (kv * GQA + g) * HEAD_DIM:(kv * GQA + g + 1) * HEAD_DIM] for g in range(GQA)], axis=0)
            s_buf = jnp.where(valid_buf, _dot_t(qg, kbuf[:, ks].astype(bf16)) + bb_ref[kv], NEG_INF)
            s_new = jnp.where(valid_new, _dot_t(qg, knew[:, ks].astype(bf16)) + bn_ref[kv], NEG_INF)
            p_buf, p_new = _softmax_sink([s_buf, s_new], sink_ref[kv])
            o = _dot(p_buf, vbuf[:, ks].astype(bf16)) + _dot(p_new, vnew[:, ks].astype(bf16))
            outs.extend([o[g * ls:(g + 1) * ls] for g in range(GQA)])
        o_ref[b] = jnp.concatenate(outs, axis=-1).astype(bf16)
        nk_ref[b, 0:W - ls, :] = kbuf[ls:, :]
        nk_ref[b, W - ls:W, :] = knew
        nv_ref[b, 0:W - ls, :] = vbuf[ls:, :]
        nv_ref[b, W - ls:W, :] = vnew
        return carry

    lax.fori_loop(0, bb, one, 0)


def _attn_sample(q, k_buf, v_buf, k_new, v_new, bias_buf, bias_new, sink_col, bb=16):
    B, ls, _ = q.shape
    W = k_buf.shape[1]
    rows = GQA * ls
    blk = lambda shape: pl.BlockSpec(shape, lambda i: (i, 0, 0))
    full = lambda shape: pl.BlockSpec(shape, lambda i: (0, 0, 0))
    return pl.pallas_call(
        functools.partial(_attn_sample_body, bb=bb, ls=ls),
        grid=(B // bb,),
        in_specs=[
            blk((bb, ls, ATTN_W)),
            blk((bb, W, KV_W)),
            blk((bb, W, KV_W)),
            blk((bb, ls, KV_W)),
            blk((bb, ls, KV_W)),
            full((N_KV_HEADS, rows, W)),
            full((N_KV_HEADS, rows, ls)),
            full((N_KV_HEADS, rows, 1)),
        ],
        out_specs=[blk((bb, ls, ATTN_W)), blk((bb, W, KV_W)), blk((bb, W, KV_W))],
        out_shape=[
            jax.ShapeDtypeStruct((B, ls, ATTN_W), bf16),
            jax.ShapeDtypeStruct((B, W, KV_W), f32),
            jax.ShapeDtypeStruct((B, W, KV_W), f32),
        ],
        name="attn_sample",
    )(q, k_buf, v_buf, k_new, v_new, bias_buf, bias_new, sink_col)


def _outproj_core(po, at, x, gt, sc, sh, g_ref, wo_ref, wrh_ref, wrl_ref):
    mixo = _dot(po, wo_ref[0:POOL_W, :]) + _dot(at, wo_ref[POOL_W:, :])
    x1 = x + gt * mixo.reshape(x.shape)
    h2 = _mod_norm(x1, g_ref[...].reshape((1,) * (x.ndim - 1) + (D_MODEL,)), sc, sh).reshape(-1, D_MODEL)
    h_hi, h_lo = _split_bf16(h2)
    wh = wrh_ref[...]
    logits = _dot_t(wh, h_hi) + (_dot_t(wh, h_lo) + _dot_t(wrl_ref[...], h_hi))
    return x1, _pack_pairs(h2), logits


def _outproj_prompt_body(po_ref, at_ref, x_ref, gt_ref, sc_ref, sh_ref, g_ref, wo_ref, wrh_ref, wrl_ref,
                         x1_ref, h2_ref, lg_ref):
    x1, h2p, logits = _outproj_core(po_ref[0], at_ref[0], x_ref[0], gt_ref[0], sc_ref[0], sh_ref[0],
                                    g_ref, wo_ref, wrh_ref, wrl_ref)
    x1_ref[...] = x1
    h2_ref[...] = h2p
    lg_ref[...] = logits


def _outproj_sample_body(po_ref, at_ref, x_ref, gt_ref, sc_ref, sh_ref, g_ref, wo_ref, wrh_ref, wrl_ref,
                         x1_ref, h2_ref, lg_ref):
    x1, h2p, logits = _outproj_core(po_ref[...], at_ref[...], x_ref[...], gt_ref[...], sc_ref[...], sh_ref[...],
                                    g_ref, wo_ref, wrh_ref, wrl_ref)
    x1_ref[...] = x1.reshape(-1, D_MODEL)
    h2_ref[...] = h2p
    lg_ref[...] = logits


def _outproj_prompt(po, at, x, mod3, g_ffn, w_out, wr_hi, wr_lo, tm=512):
    B, L, D = x.shape
    nt = L // tm
    n_tok = B * L
    full = lambda shape: pl.BlockSpec(shape, lambda b, j: (0,) * len(shape))
    modspec = lambda c: pl.BlockSpec((1, 1, D), lambda b, j: (b, 0, c))
    return pl.pallas_call(
        _outproj_prompt_body,
        grid=(B, nt),
        in_specs=[
            pl.BlockSpec((1, tm, POOL_W), lambda b, j: (b, j, 0)),
            pl.BlockSpec((1, tm, ATTN_W), lambda b, j: (b, j, 0)),
            pl.BlockSpec((1, tm, D), lambda b, j: (b, j, 0)),
            modspec(2), modspec(4), modspec(3),
            full((1, D)), full((D, D)), full((N_EXPERTS, D)), full((N_EXPERTS, D)),
        ],
        out_specs=[
            pl.BlockSpec((tm, D), lambda b, j: (b * nt + j, 0)),
            pl.BlockSpec((tm, D // 2), lambda b, j: (b * nt + j, 0)),
            pl.BlockSpec((N_EXPERTS, tm), lambda b, j: (0, b * nt + j)),
        ],
        out_shape=[
            jax.ShapeDtypeStruct((n_tok, D), f32),
            jax.ShapeDtypeStruct((n_tok, D // 2), i32),
            jax.ShapeDtypeStruct((N_EXPERTS, n_tok), f32),
        ],
        name="outproj_prompt",
    )(po, at, x, mod3, mod3, mod3, g_ffn, w_out, wr_hi, wr_lo)


def _outproj_sample(po, at, x, mod3, g_ffn, w_out, wr_hi, wr_lo, bt=64):
    B, ls, D = x.shape
    n = bt * ls
    full = lambda shape: pl.BlockSpec(shape, lambda i: (0,) * len(shape))
    modspec = lambda c: pl.BlockSpec((bt, 1, D), lambda i: (i, 0, c))
    return pl.pallas_call(
        _outproj_sample_body,
        grid=(B // bt,),
        in_specs=[
            pl.BlockSpec((n, POOL_W), lambda i: (i, 0)),
            pl.BlockSpec((n, ATTN_W), lambda i: (i, 0)),
            pl.BlockSpec((bt, ls, D), lambda i: (i, 0, 0)),
            modspec(2), modspec(4), modspec(3),
            full((1, D)), full((D, D)), full((N_EXPERTS, D)), full((N_EXPERTS, D)),
        ],
        out_specs=[
            pl.BlockSpec((n, D), lambda i: (i, 0)),
            pl.BlockSpec((n, D // 2), lambda i: (i, 0)),
            pl.BlockSpec((N_EXPERTS, n), lambda i: (0, i)),
        ],
        out_shape=[
            jax.ShapeDtypeStruct((B * ls, D), f32),
            jax.ShapeDtypeStruct((B * ls, D // 2), i32),
            jax.ShapeDtypeStruct((N_EXPERTS, B * ls), f32),
        ],
        name="outproj_sample",
    )(po, at, x, mod3, mod3, mod3, g_ffn, w_out, wr_hi, wr_lo)


def _route_body(lg_ref, rb_ref, tri_ref, idx_ref, rank_ref, gate_ref, cnt_ref, carry, *, tr, nsteps):
    step = pl.program_id(0)

    @pl.when(step == 0)
    def _():
        carry[...] = jnp.zeros(carry.shape, f32)

    s = jax.nn.sigmoid(lg_ref[...])
    sb = s + rb_ref[...]
    e_iota = lax.broadcasted_iota(i32, (N_EXPERTS, tr), 0)
    g_iota = lax.broadcasted_iota(i32, (GROUP_SIZE, tr), 0)

    gscore = []
    for g in range(N_EXPERT_GROUPS):
        v = sb[g * GROUP_SIZE:(g + 1) * GROUP_SIZE]
        m1 = jnp.max(v, axis=0, keepdims=True)
        i1 = jnp.min(jnp.where(v == m1, g_iota, GROUP_SIZE), axis=0, keepdims=True)
        m2 = jnp.max(jnp.where(g_iota == i1, -jnp.inf, v), axis=0, keepdims=True)
        gscore.append(m1 + m2)
    parts = []
    for g in range(N_EXPERT_GROUPS):
        beaten = jnp.zeros((1, tr), i32)
        for g2 in range(N_EXPERT_GROUPS):
            if g2 == g:
                continue
            ahead = gscore[g2] > gscore[g]
            if g2 < g:
                ahead = ahead | (gscore[g2] == gscore[g])
            beaten = beaten + ahead.astype(i32)
        keep = beaten < TOPK_GROUPS
        parts.append(jnp.where(keep, sb[g * GROUP_SIZE:(g + 1) * GROUP_SIZE], NEG_INF))
    cur = jnp.concatenate(parts, axis=0)

    sel = jnp.zeros((N_EXPERTS, tr), f32)
    idxs, svals = [], []
    for _ in range(TOP_K):
        m = jnp.max(cur, axis=0, keepdims=True)
        ik = jnp.min(jnp.where(cur == m, e_iota, N_EXPERTS), axis=0, keepdims=True)
        hit = e_iota == ik
        svals.append(jnp.sum(jnp.where(hit, s, 0.0), axis=0, keepdims=True))
        cur = jnp.where(hit, -jnp.inf, cur)
        sel = jnp.where(hit, 1.0, sel)
        idxs.append(ik)
    ssum = svals[0]
    for sv in svals[1:]:
        ssum = ssum + sv
    gate_ref[...] = jnp.concatenate([sv / ssum * ROUTED_SCALE for sv in svals], axis=0)
    idx_ref[...] = jnp.concatenate(idxs, axis=0)

    before = carry[...] + _dot(sel.astype(bf16), tri_ref[...])
    ranks = [jnp.sum(jnp.where(e_iota == ik, before, 0.0), axis=0, keepdims=True) for ik in idxs]
    rank_ref[...] = jnp.concatenate(ranks, axis=0).astype(i32)
    carry[...] = carry[...] + jnp.sum(sel, axis=1, keepdims=True)

    @pl.when(step == nsteps - 1)
    def _():
        cnt_ref[...] = carry[...]


def _route(logits_t, router_bias, tr=512):
    E, T = logits_t.shape
    nsteps = T // tr
    tri = jnp.asarray(np.triu(np.ones((tr, tr), np.float32), 1), bf16)
    return pl.pallas_call(
        functools.partial(_route_body, tr=tr, nsteps=nsteps),
        grid=(nsteps,),
        in_specs=[
            pl.BlockSpec((E, tr), lambda i: (0, i)),
            pl.BlockSpec((E, 1), lambda i: (0, 0)),
            pl.BlockSpec((tr, tr), lambda i: (0, 0)),
        ],
        out_specs=[
            pl.BlockSpec((TOP_K, tr), lambda i: (0, i)),
            pl.BlockSpec((TOP_K, tr), lambda i: (0, i)),
            pl.BlockSpec((TOP_K, tr), lambda i: (0, i)),
            pl.BlockSpec((E, 1), lambda i: (0, 0)),
        ],
        out_shape=[
            jax.ShapeDtypeStruct((TOP_K, T), i32),
            jax.ShapeDtypeStruct((TOP_K, T), i32),
            jax.ShapeDtypeStruct((TOP_K, T), f32),
            jax.ShapeDtypeStruct((E, 1), f32),
        ],
        scratch_shapes=[pltpu.VMEM((E, 1), f32)],
        compiler_params=pltpu.CompilerParams(dimension_semantics=("arbitrary",)),
        name="route",
    )(logits_t, router_bias.reshape(E, 1), tri)


def _dest_body(idx_ref, rank_ref, ps_ref, dest_ref, *, tr):
    e_iota = lax.broadcasted_iota(i32, (N_EXPERTS, tr), 0)
    start = ps_ref[...]
    rows = []
    for k in range(TOP_K):
        hit = e_iota == idx_ref[k:k + 1, :]
        rows.append(jnp.sum(jnp.where(hit, start, 0.0), axis=0, keepdims=True))
    dest_ref[...] = jnp.concatenate(rows, axis=0).astype(i32) + rank_ref[...]


def _dest_rows(idx, rank, pad_start, tr=512):
    K, T = idx.shape
    blk = pl.BlockSpec((K, tr), lambda i: (0, i))
    return pl.pallas_call(
        functools.partial(_dest_body, tr=tr),
        grid=(T // tr,),
        in_specs=[blk, blk, pl.BlockSpec((N_EXPERTS, 1), lambda i: (0, 0))],
        out_specs=blk,
        out_shape=jax.ShapeDtypeStruct((K, T), i32),
        name="dest_rows",
    )(idx, rank, pad_start.astype(f32).reshape(N_EXPERTS, 1))


def _sc_mesh():
    return plsc.VectorSubcoreMesh(core_axis_name="c", subcore_axis_name="s")


def _sc_worker_id():
    return lax.axis_index("s") * 2 + lax.axis_index("c")


def _dispatch(h2_a, h2_b, dest, gate, n_rows, chunk=32):
    ta, Dw = h2_a.shape
    T = ta + h2_b.shape[0]
    per_worker = T // SC_WORKERS
    nchunk = per_worker // chunk
    assert per_worker * SC_WORKERS == T and nchunk * chunk == per_worker and ta % chunk == 0

    @functools.partial(
        pl.kernel, mesh=_sc_mesh(),
        out_type=[jax.ShapeDtypeStruct((n_rows, Dw), i32), jax.ShapeDtypeStruct((n_rows, GATE_ROW), f32)],
        scratch_types=[pltpu.VMEM((chunk,), i32), pltpu.VMEM((chunk, Dw), i32),
                       pltpu.VMEM((chunk,), f32), pltpu.VMEM((chunk, GATE_ROW), f32)],
        compiler_params=pltpu.CompilerParams(needs_layout_passes=False),
        name="moe_dispatch",
    )
    def body(ha_hbm, hb_hbm, dest_hbm, gate_hbm, xs_hbm, gs_hbm, idx_v, rows_v, gate_v, grow_v):
        base = _sc_worker_id() * per_worker

        @pl.loop(0, nchunk)
        def _(ci):
            t0 = base + ci * chunk

            @pl.when(t0 < ta)
            def _():
                pltpu.sync_copy(ha_hbm.at[pl.ds(t0, chunk)], rows_v)

            @pl.when(t0 >= ta)
            def _():
                pltpu.sync_copy(hb_hbm.at[pl.ds(t0 - ta, chunk)], rows_v)

            for k in range(TOP_K):
                pltpu.sync_copy(dest_hbm.at[k, pl.ds(t0, chunk)], idx_v)
                pltpu.sync_copy(gate_hbm.at[k, pl.ds(t0, chunk)], gate_v)

                @pl.loop(0, chunk)
                def _(t):
                    g = plsc.load_gather(gate_v, [jnp.zeros((SC_LANES,), i32) + t])
                    for j in range(GATE_ROW // SC_LANES):
                        grow_v[t, pl.ds(j * SC_LANES, SC_LANES)] = g

                pltpu.sync_copy(rows_v, xs_hbm.at[idx_v])
                pltpu.sync_copy(grow_v, gs_hbm.at[idx_v])

    return body(h2_a, h2_b, dest, gate)


def _combine(ys, dest, chunk=8):
    T = dest.shape[1]
    Dw = ys.shape[1]
    per_worker = T // SC_WORKERS
    nchunk = per_worker // chunk
    assert per_worker * SC_WORKERS == T and nchunk * chunk == per_worker and nchunk % 2 == 0

    @functools.partial(
        pl.kernel, mesh=_sc_mesh(),
        out_type=jax.ShapeDtypeStruct((T, 2 * Dw), f32),
        scratch_types=[
            pltpu.VMEM((TOP_K * per_worker,), i32),
            pltpu.VMEM((2, TOP_K, chunk, Dw), i32),
            pltpu.VMEM((chunk, 2 * Dw), f32),
            pltpu.SemaphoreType.DMA((2,)),
        ],
        compiler_params=pltpu.CompilerParams(needs_layout_passes=False),
        name="moe_combine",
    )
    def body(ys_hbm, dest_hbm, out_hbm, idx_v, buf, out_v, sems):
        base = _sc_worker_id() * per_worker
        pltpu.sync_copy(dest_hbm.at[pl.ds(_sc_worker_id() * (TOP_K * per_worker), TOP_K * per_worker)], idx_v)

        def gather(ci, slot):
            return [pltpu.make_async_copy(ys_hbm.at[idx_v.at[pl.ds(k * per_worker + ci * chunk, chunk)]],
                                          buf.at[slot, k], sems.at[slot]) for k in range(TOP_K)]

        for cp in gather(0, 0):
            cp.start()

        @pl.loop(0, nchunk, step=2)
        def _(c0):
            for slot in range(2):
                ci = c0 + slot

                @pl.when(ci + 1 < nchunk)
                def _():
                    for cp in gather(ci + 1, 1 - slot):
                        cp.start()

                for cp in gather(ci, slot):
                    cp.wait()

                @pl.loop(0, chunk)
                def _(t):
                    @pl.loop(0, Dw // SC_LANES)
                    def _(j):
                        sl = pl.ds(j * SC_LANES, SC_LANES)
                        w = buf[slot, 0, t, sl]
                        hi = plsc.bitcast(w & HI_MASK, f32)
                        lo = plsc.bitcast(lax.shift_left(w, 16), f32)
                        for k in range(1, TOP_K):
                            w = buf[slot, k, t, sl]
                            hi = hi + plsc.bitcast(w & HI_MASK, f32)
                            lo = lo + plsc.bitcast(lax.shift_left(w, 16), f32)
                        out_v[t, sl] = hi
                        out_v[t, pl.ds(Dw + j * SC_LANES, SC_LANES)] = lo

                pltpu.sync_copy(out_v, out_hbm.at[pl.ds(base + ci * chunk, chunk)])

    dest_w = dest.reshape(TOP_K, SC_WORKERS, per_worker).transpose(1, 0, 2).reshape(-1)
    return body(ys, dest_w)


def _gmm_body(blk_e_ref, blk_rows_ref, nv_ref, xs_ref, gs_ref, w1_ref, w3_ref, w2_ref, ys_ref, w13_b, w2_b):
    b = pl.program_id(0)
    half = D_MODEL // 2

    @pl.when(b < nv_ref[0])
    def _():
        e = blk_e_ref[b]
        e_prev = blk_e_ref[jnp.maximum(b - 1, 0)]

        @pl.when((b == 0) | (e != e_prev))
        def _():
            w13_b[:, 0:EXPERT_FF] = w1_ref[0].astype(bf16)
            w13_b[:, EXPERT_FF:] = w3_ref[0].astype(bf16)
            w2_b[...] = w2_ref[0].astype(bf16)

        valid = lax.broadcasted_iota(i32, (EXPERT_BLOCK, 1), 0) < blk_rows_ref[b]
        x_hi, x_lo = _unpack_pairs(jnp.where(valid, xs_ref[...], 0))
        ac = _dot(x_hi, w13_b[0:half, :]) + _dot(x_lo, w13_b[half:, :])
        a = ac[:, :EXPERT_FF]
        c = ac[:, EXPERT_FF:]
        hmid = (a * jax.nn.sigmoid(a)) * c
        g = jnp.where(valid, gs_ref[:, 0:1], 0.0)
        ys_ref[...] = _pack_pairs(_dot(hmid.astype(bf16), w2_b[...]) * g)


def _gmm(xs, gs, w1, w3, w2, blk_e, blk_rows, n_valid):
    n_rows, Dw = xs.shape
    D = 2 * Dw
    nb = n_rows // EXPERT_BLOCK
    row_map = lambda b, be, br, nv: (jnp.minimum(b, nv[0] - 1), 0)
    w_map = lambda b, be, br, nv: (be[jnp.minimum(b, nv[0] - 1)], 0, 0)
    return pl.pallas_call(
        _gmm_body,
        grid_spec=pltpu.PrefetchScalarGridSpec(
            num_scalar_prefetch=3,
            grid=(nb,),
            in_specs=[
                pl.BlockSpec((EXPERT_BLOCK, Dw), row_map),
                pl.BlockSpec((EXPERT_BLOCK, GATE_ROW), row_map),
                pl.BlockSpec((1, D, EXPERT_FF), w_map),
                pl.BlockSpec((1, D, EXPERT_FF), w_map),
                pl.BlockSpec((1, EXPERT_FF, D), w_map),
            ],
            out_specs=pl.BlockSpec((EXPERT_BLOCK, Dw), row_map),
            scratch_shapes=[pltpu.VMEM((D, 2 * EXPERT_FF), bf16), pltpu.VMEM((EXPERT_FF, D), bf16)],
        ),
        out_shape=jax.ShapeDtypeStruct((n_rows, Dw), i32),
        compiler_params=pltpu.CompilerParams(dimension_semantics=("arbitrary",)),
        name="moe_gmm",
    )(blk_e, blk_rows, n_valid, xs, gs, w1, w3, w2)


def _final_core(x1, h2p, comb, gt, ws1_ref, ws3_ref, ws2_ref):
    half = D_MODEL // 2
    h_hi, h_lo = _unpack_pairs(h2p)
    a = _dot(h_hi, ws1_ref[0:half, :]) + _dot(h_lo, ws1_ref[half:, :])
    c = _dot(h_hi, ws3_ref[0:half, :]) + _dot(h_lo, ws3_ref[half:, :])
    shared = _dot(((a * jax.nn.sigmoid(a)) * c).astype(bf16), ws2_ref[...])
    return x1, comb + shared, gt


def _final_prompt_body(x1_ref, h2_ref, cb_ref, gt_ref, ws1_ref, ws3_ref, ws2_ref, y_ref):
    x1, ffn, gt = _final_core(x1_ref[...], h2_ref[...], cb_ref[...], gt_ref[0], ws1_ref, ws3_ref, ws2_ref)
    y_ref[0] = x1 + gt * ffn


def _final_sample_body(x1_ref, h2_ref, cb_ref, gt_ref, ws1_ref, ws3_ref, ws2_ref, y_ref):
    x1, ffn, gt = _final_core(x1_ref[...], h2_ref[...], cb_ref[...], gt_ref[...], ws1_ref, ws3_ref, ws2_ref)
    shp = y_ref.shape
    y_ref[...] = x1.reshape(shp) + gt * ffn.reshape(shp)


def _final_prompt(x1, h2, comb, mod3, ws1, ws3, ws2, B, L, tm=512):
    D = D_MODEL
    nt = L // tm
    full = lambda shape: pl.BlockSpec(shape, lambda b, j: (0,) * len(shape))
    rows = pl.BlockSpec((tm, D), lambda b, j: (b * nt + j, 0))
    words = pl.BlockSpec((tm, D // 2), lambda b, j: (b * nt + j, 0))
    return pl.pallas_call(
        _final_prompt_body,
        grid=(B, nt),
        in_specs=[rows, words, rows, pl.BlockSpec((1, 1, D), lambda b, j: (b, 0, 5)),
                  full((D, EXPERT_FF)), full((D, EXPERT_FF)), full((EXPERT_FF, D))],
        out_specs=pl.BlockSpec((1, tm, D), lambda b, j: (b, j, 0)),
        out_shape=jax.ShapeDtypeStruct((B, L, D), f32),
        name="final_prompt",
    )(x1, h2, comb, mod3, ws1, ws3, ws2)


def _final_sample(x1, h2, comb, mod3, ws1, ws3, ws2, B, ls, row0, bt=64):
    D = D_MODEL
    n = bt * ls
    blk0 = row0 // n
    full = lambda shape: pl.BlockSpec(shape, lambda i: (0,) * len(shape))
    rows = pl.BlockSpec((n, D), lambda i: (i, 0))
    words = pl.BlockSpec((n, D // 2), lambda i: (i, 0))
    comb_rows = pl.BlockSpec((n, D), lambda i: (blk0 + i, 0))
    return pl.pallas_call(
        _final_sample_body,
        grid=(B // bt,),
        in_specs=[rows, words, comb_rows, pl.BlockSpec((bt, 1, D), lambda i: (i, 0, 5)),
                  full((D, EXPERT_FF)), full((D, EXPERT_FF)), full((EXPERT_FF, D))],
        out_specs=pl.BlockSpec((bt, ls, D), lambda i: (i, 0, 0)),
        out_shape=jax.ShapeDtypeStruct((B, ls, D), f32),
        name="final_sample",
    )(x1, h2, comb, mod3, ws1, ws3, ws2)


def kernel(x_prompt, x_sample, state_pool, cache_swa_k, cache_swa_v, c_prompt, c_sample, w_ada, b_ada,
           g_attn_norm, w_in, g_q, g_k, w_pool, pool_scale, w_out, attn_sinks, rel_bias, g_ffn_norm,
           w_router, router_bias, w1, w3, w2, ws1, ws3, ws2):
    B, L, D = x_prompt.shape
    BS, LS, _ = x_sample.shape
    depth = w_ada.shape[0]
    assert depth == 1
    W = cache_swa_k.shape[2]
    tp, ts = B * L, BS * LS
    T = tp + ts
    n_rows = (T * TOP_K // EXPERT_BLOCK + N_EXPERTS) * EXPERT_BLOCK
    nb = n_rows // EXPERT_BLOCK

    g_attn = g_attn_norm[0].reshape(1, D)
    g_ffn = g_ffn_norm[0].reshape(1, D)
    w_in_b = w_in[0].astype(bf16)
    w_out_b = w_out[0].astype(bf16)
    w_pool_b = w_pool[0].astype(bf16)
    ps = pool_scale[0].reshape(1, POOL_W)
    gqk = jnp.concatenate([jnp.tile(g_q[0], N_HEADS), jnp.tile(g_k[0], N_KV_HEADS)]).reshape(1, QK_W)
    head_of = np.arange(QK_W) // HEAD_DIM
    bd = jnp.asarray((head_of[:, None] == head_of[None, :]).astype(np.float32), bf16)
    wr_t = w_router[0].T
    wr_hi = wr_t.astype(bf16)
    wr_lo = (wr_t - wr_hi.astype(f32)).astype(bf16)
    ws1_b, ws3_b, ws2_b = ws1[0].astype(bf16), ws3[0].astype(bf16), ws2[0].astype(bf16)
    sinks = attn_sinks[0]

    mod = _ada(jnp.concatenate([c_prompt, c_sample], axis=0), w_ada[0], b_ada[0])
    mod_p = mod[:B].reshape(B, 1, 6 * D)
    mod_s = mod[B:].reshape(BS, 1, 6 * D)

    dist_p = np.arange(WINDOW)[:, None] + WINDOW - np.arange(2 * WINDOW)[None, :]
    bias_p = _relbias(rel_bias, dist_p)
    dist_s = np.arange(LS)[:, None] + W - np.arange(W + LS)[None, :]
    bias_s = _relbias(rel_bias, dist_s)
    bias_s_buf = bias_s[:, :, :W].reshape(N_KV_HEADS, GQA * LS, W)
    bias_s_new = bias_s[:, :, W:].reshape(N_KV_HEADS, GQA * LS, LS)
    sink_col = jnp.repeat(sinks, LS).reshape(N_KV_HEADS, GQA * LS, 1)

    q_p, k_p, v_p, po_p, new_pool_p, kc_p, vc_p = _inproj_prompt(
        x_prompt, mod_p, g_attn, w_in_b, gqk, bd, w_pool_b, ps)
    q_s, k_s, v_s, po_s, new_pool_s = _inproj_sample(
        x_sample, mod_s, g_attn, w_in_b, gqk, bd, w_pool_b, ps, state_pool[0], PAST_LEN)
    at_p = _attn_prompt(q_p, k_p, v_p, bias_p, sinks)
    at_s, nk_s, nv_s = _attn_sample(
        q_s.reshape(BS, LS, ATTN_W), cache_swa_k[0].reshape(BS, W, KV_W), cache_swa_v[0].reshape(BS, W, KV_W),
        k_s.reshape(BS, LS, KV_W), v_s.reshape(BS, LS, KV_W), bias_s_buf, bias_s_new, sink_col)

    x1_p, h2_p, lg_p = _outproj_prompt(po_p, at_p, x_prompt, mod_p, g_ffn, w_out_b, wr_hi, wr_lo)
    x1_s, h2_s, lg_s = _outproj_sample(po_s, at_s.reshape(ts, ATTN_W), x_sample, mod_s, g_ffn, w_out_b,
                                       wr_hi, wr_lo)

    idx, rank, gate, counts = _route(jnp.concatenate([lg_p, lg_s], axis=1), router_bias[0])
    counts = counts.reshape(N_EXPERTS).astype(i32)
    padded = (counts + EXPERT_BLOCK - 1) // EXPERT_BLOCK * EXPERT_BLOCK
    pad_end = jnp.cumsum(padded)
    pad_start = pad_end - padded
    dest = _dest_rows(idx, rank, pad_start)
    n_valid = (pad_end[-1] // EXPERT_BLOCK).astype(i32).reshape(1)
    blk_row0 = jnp.arange(nb, dtype=i32) * EXPERT_BLOCK
    blk_e = jnp.minimum(jnp.sum(blk_row0[:, None] >= pad_end[None, :], axis=1), N_EXPERTS - 1).astype(i32)
    own = jnp.arange(N_EXPERTS, dtype=i32)[None, :] == blk_e[:, None]
    blk_cnt = jnp.sum(jnp.where(own, counts[None, :], 0), axis=1)
    blk_start = jnp.sum(jnp.where(own, pad_start[None, :], 0), axis=1)
    blk_rows = jnp.clip(blk_cnt - (blk_row0 - blk_start), 0, EXPERT_BLOCK).astype(i32)

    xs, gs = _dispatch(h2_p, h2_s, dest, gate, n_rows)
    ys = _gmm(xs, gs, w1[0], w3[0], w2[0], blk_e, blk_rows, n_valid)
    comb = _combine(ys, dest)

    y_p = _final_prompt(x1_p, h2_p, comb, mod_p, ws1_b, ws3_b, ws2_b, B, L)
    y_s = _final_sample(x1_s, h2_s, comb, mod_s, ws1_b, ws3_b, ws2_b, BS, LS, tp)

    return (y_p, y_s, new_pool_p[None], kc_p.reshape(1, B, WINDOW, N_KV_HEADS, HEAD_DIM),
            vc_p.reshape(1, B, WINDOW, N_KV_HEADS, HEAD_DIM), new_pool_s[None],
            nk_s.reshape(1, BS, W, N_KV_HEADS, HEAD_DIM), nv_s.reshape(1, BS, W, N_KV_HEADS, HEAD_DIM))
```

```python
import functools
import math

import numpy as np
import jax
import jax.numpy as jnp
from jax import lax
from jax.experimental import pallas as pl
from jax.experimental.pallas import tpu as pltpu
from jax.experimental.pallas import tpu_sc as plsc

f32 = jnp.float32
bf16 = jnp.bfloat16
i32 = jnp.int32

D_MODEL = 1024
PAST_LEN = 8192
POOL_W = 512
POOL_WINDOWS = (2, 4, 8, 16)
POOL_GC = 128
POOL_BUF = 15
ATTN_W = 512
HEAD_DIM = 64
N_HEADS = 8
N_KV_HEADS = 2
GQA = 4
KV_W = 128
WINDOW = 128
NUM_BUCKETS = 32
MAX_EXACT = 16
REL_MAX_DIST = 128
N_EXPERTS = 256
N_EXPERT_GROUPS = 8
GROUP_SIZE = 32
TOPK_GROUPS = 4
TOP_K = 8
EXPERT_FF = 256
ROUTED_SCALE = 2.5
EXPERT_BLOCK = 128
EPS = 1e-6
NEG_INF = -1e30
QKV_W = POOL_W + ATTN_W + 2 * KV_W
QK_W = ATTN_W + KV_W
HIST = 16

SC_WORKERS = 32
SC_LANES = 16
GATE_ROW = 128


def _dot(a, b):
    return jnp.dot(a, b, preferred_element_type=f32)


def _dot_t(a, b):
    return lax.dot_general(a, b, (((1,), (1,)), ((), ())), preferred_element_type=f32)


def _split_bf16(a):
    hi = a.astype(bf16)
    lo = (a - hi.astype(f32)).astype(bf16)
    return hi, lo


HI_MASK = -65536


def _pack_pairs(a):
    h = a.shape[1] // 2
    hi = lax.bitcast_convert_type(a[:, :h].astype(bf16).astype(f32), i32)
    lo = lax.bitcast_convert_type(a[:, h:].astype(bf16).astype(f32), i32)
    return hi | lax.shift_right_logical(lo, 16)


def _unpack_pairs(w):
    hi = lax.bitcast_convert_type(w & HI_MASK, f32).astype(bf16)
    lo = lax.bitcast_convert_type(lax.shift_left(w, 16), f32).astype(bf16)
    return hi, lo


def _mod_norm(x, g, sc, sh):
    ms = jnp.mean(x * x, axis=-1, keepdims=True)
    y = x * lax.rsqrt(ms + EPS)
    return (y * g) * (1.0 + sc) + sh


def _ada_body(c_ref, w_ref, b_ref, o_ref):
    c = c_ref[...]
    a = (c * jax.nn.sigmoid(c)).astype(bf16)
    o_ref[...] = _dot(a, w_ref[...].astype(bf16)) + b_ref[...]


def _ada(c, w_ada, b_ada):
    n = c.shape[0]
    tn = 1024
    return pl.pallas_call(
        _ada_body,
        grid=(6 * D_MODEL // tn,),
        in_specs=[
            pl.BlockSpec((n, D_MODEL), lambda j: (0, 0)),
            pl.BlockSpec((D_MODEL, tn), lambda j: (0, j)),
            pl.BlockSpec((1, tn), lambda j: (0, j)),
        ],
        out_specs=pl.BlockSpec((n, tn), lambda j: (0, j)),
        out_shape=jax.ShapeDtypeStruct((n, 6 * D_MODEL), f32),
        name="ada_mod",
    )(c, w_ada, b_ada.reshape(1, -1))


def _relbias_body(table_ref, bucket_ref, o_ref):
    bucket = bucket_ref[...]
    for h in range(N_HEADS):
        acc = jnp.zeros(bucket.shape, f32)
        for b in range(NUM_BUCKETS):
            acc = jnp.where(bucket == b, table_ref[b, h], acc)
        o_ref[h] = acc


def _rel_buckets(dist):
    n = np.maximum(dist, 0)
    nf = np.maximum(n, 1).astype(np.float64)
    large = MAX_EXACT + (np.log(nf / MAX_EXACT) / math.log(REL_MAX_DIST / MAX_EXACT)
                         * (NUM_BUCKETS - MAX_EXACT)).astype(np.int32)
    return np.where(n < MAX_EXACT, n, np.minimum(large, NUM_BUCKETS - 1)).astype(np.int32)


def _relbias(table, dist):
    lq, lk = dist.shape
    return pl.pallas_call(
        _relbias_body,
        in_specs=[
            pl.BlockSpec(memory_space=pltpu.SMEM),
            pl.BlockSpec((lq, lk), lambda: (0, 0)),
        ],
        out_specs=pl.BlockSpec((N_HEADS, lq, lk), lambda: (0, 0, 0)),
        out_shape=jax.ShapeDtypeStruct((N_HEADS, lq, lk), f32),
        name="rel_bias",
    )(table, jnp.asarray(_rel_buckets(dist)))


def _qkv_from_h(h, w_ref, gqk_ref, bd_ref):
    u = _dot(h.astype(bf16), w_ref[...])
    qk = u[:, POOL_W:POOL_W + QK_W]
    y_hi, y_lo = _split_bf16(qk * qk)
    bd = bd_ref[...]
    ss = _dot(y_hi, bd) + _dot(y_lo, bd)
    qkn = (qk * lax.rsqrt(ss * (1.0 / HEAD_DIM) + EPS)) * gqk_ref[...]
    q = qkn[:, :ATTN_W] * (HEAD_DIM ** -0.5)
    k = qkn[:, ATTN_W:]
    v = u[:, POOL_W + QK_W:]
    return u[:, :POOL_W], q, k, v


def _inproj_prompt_body(x_ref, sh_ref, sc_ref, g_ref, w_ref, gqk_ref, bd_ref, wp_ref, ps_ref,
                        q_ref, k_ref, v_ref, po_ref, np_ref, kc_ref, vc_ref, hist, *, tl, nt):
    j = pl.program_id(1)
    h = _mod_norm(x_ref[0], g_ref[...], sc_ref[0], sh_ref[0])
    up, q, k, v = _qkv_from_h(h, w_ref, gqk_ref, bd_ref)
    q_ref[0] = q.astype(bf16)
    k_ref[0] = k.astype(bf16)
    v_ref[0] = v.astype(bf16)

    @pl.when(j == nt - 1)
    def _():
        kc_ref[0] = k[tl - WINDOW:, :]
        vc_ref[0] = v[tl - WINDOW:, :]

    @pl.when(j == 0)
    def _():
        hist[0:HIST, :] = jnp.zeros((HIST, POOL_W), f32)

    hist[HIST:HIST + tl, :] = up
    pos = j * tl + lax.broadcasted_iota(i32, (tl, 1), 0)
    for g, w in enumerate(POOL_WINDOWS):
        lanes = slice(g * POOL_GC, (g + 1) * POOL_GC)
        cur = up[:, lanes]
        acc = cur
        for s in range(1, w):
            acc = acc + hist[HIST - s:HIST - s + tl, lanes]
        cnt = jnp.minimum(w, pos + 1).astype(f32)
        d = acc / cnt - cur
        yg = _dot(d.astype(bf16), wp_ref[g]) * ps_ref[:, lanes]
        po_ref[0, :, lanes] = yg.astype(bf16)

    @pl.when(j == nt - 1)
    def _():
        np_ref[0] = hist[tl + 1:tl + HIST, :]

    hist[0:HIST, :] = hist[tl:tl + HIST, :]


def _inproj_prompt(x, mod3, g_attn, w_in, gqk, bd, w_pool, pool_scale, tl=512):
    B, L, D = x.shape
    nt = L // tl
    full = lambda shape: pl.BlockSpec(shape, lambda b, j: (0,) * len(shape))
    return pl.pallas_call(
        functools.partial(_inproj_prompt_body, tl=tl, nt=nt),
        grid=(B, nt),
        in_specs=[
            pl.BlockSpec((1, tl, D), lambda b, j: (b, j, 0)),
            pl.BlockSpec((1, 1, D), lambda b, j: (b, 0, 0)),
            pl.BlockSpec((1, 1, D), lambda b, j: (b, 0, 1)),
            full((1, D)),
            full((D, QKV_W)),
            full((1, QK_W)),
            full((QK_W, QK_W)),
            full((4, POOL_GC, POOL_GC)),
            full((1, POOL_W)),
        ],
        out_specs=[
            pl.BlockSpec((1, tl, ATTN_W), lambda b, j: (b, j, 0)),
            pl.BlockSpec((1, tl, KV_W), lambda b, j: (b, j, 0)),
            pl.BlockSpec((1, tl, KV_W), lambda b, j: (b, j, 0)),
            pl.BlockSpec((1, tl, POOL_W), lambda b, j: (b, j, 0)),
            pl.BlockSpec((1, POOL_BUF, POOL_W), lambda b, j: (b, 0, 0)),
            pl.BlockSpec((1, WINDOW, KV_W), lambda b, j: (b, 0, 0)),
            pl.BlockSpec((1, WINDOW, KV_W), lambda b, j: (b, 0, 0)),
        ],
        out_shape=[
            jax.ShapeDtypeStruct((B, L, ATTN_W), bf16),
            jax.ShapeDtypeStruct((B, L, KV_W), bf16),
            jax.ShapeDtypeStruct((B, L, KV_W), bf16),
            jax.ShapeDtypeStruct((B, L, POOL_W), bf16),
            jax.ShapeDtypeStruct((B, POOL_BUF, POOL_W), f32),
            jax.ShapeDtypeStruct((B, WINDOW, KV_W), f32),
            jax.ShapeDtypeStruct((B, WINDOW, KV_W), f32),
        ],
        scratch_shapes=[pltpu.VMEM((HIST + tl, POOL_W), f32)],
        compiler_params=pltpu.CompilerParams(dimension_semantics=("arbitrary", "arbitrary")),
        name="inproj_prompt",
    )(x, mod3, mod3, g_attn, w_in, gqk, bd, w_pool, pool_scale)


def _inproj_sample_body(x_ref, sh_ref, sc_ref, g_ref, w_ref, gqk_ref, bd_ref, wp_ref, ps_ref, st_ref,
                        q_ref, k_ref, v_ref, po_ref, np_ref, ext, *, bt, ls, pos0):
    n = bt * ls
    h3 = _mod_norm(x_ref[...], g_ref[...][None], sc_ref[...], sh_ref[...])
    up, q, k, v = _qkv_from_h(h3.reshape(n, D_MODEL), w_ref, gqk_ref, bd_ref)
    q_ref[...] = q.astype(bf16)
    k_ref[...] = k
    v_ref[...] = v

    ext[:, 1:HIST, :] = st_ref[...]
    ext[:, HIST:HIST + ls, :] = up.reshape(bt, ls, POOL_W)
    pos = pos0 + lax.broadcasted_iota(i32, (1, ls, 1), 1)
    for g, w in enumerate(POOL_WINDOWS):
        lanes = slice(g * POOL_GC, (g + 1) * POOL_GC)
        cur = ext[:, HIST:HIST + ls, lanes]
        acc = cur
        for s in range(1, w):
            acc = acc + ext[:, HIST - s:HIST - s + ls, lanes]
        cnt = jnp.minimum(w, pos + 1).astype(f32)
        d = (acc / cnt - cur).reshape(n, POOL_GC)
        yg = _dot(d.astype(bf16), wp_ref[g]) * ps_ref[:, lanes]
        po_ref[:, lanes] = yg.astype(bf16)
    np_ref[...] = ext[:, ls + 1:ls + HIST, :]


def _inproj_sample(x, mod3, g_attn, w_in, gqk, bd, w_pool, pool_scale, state, pos0, bt=64):
    B, ls, D = x.shape
    n = bt * ls
    full = lambda shape: pl.BlockSpec(shape, lambda i: (0,) * len(shape))
    return pl.pallas_call(
        functools.partial(_inproj_sample_body, bt=bt, ls=ls, pos0=pos0),
        grid=(B // bt,),
        in_specs=[
            pl.BlockSpec((bt, ls, D), lambda i: (i, 0, 0)),
            pl.BlockSpec((bt, 1, D), lambda i: (i, 0, 0)),
            pl.BlockSpec((bt, 1, D), lambda i: (i, 0, 1)),
            full((1, D)),
            full((D, QKV_W)),
            full((1, QK_W)),
            full((QK_W, QK_W)),
            full((4, POOL_GC, POOL_GC)),
            full((1, POOL_W)),
            pl.BlockSpec((bt, POOL_BUF, POOL_W), lambda i: (i, 0, 0)),
        ],
        out_specs=[
            pl.BlockSpec((n, ATTN_W), lambda i: (i, 0)),
            pl.BlockSpec((n, KV_W), lambda i: (i, 0)),
            pl.BlockSpec((n, KV_W), lambda i: (i, 0)),
            pl.BlockSpec((n, POOL_W), lambda i: (i, 0)),
            pl.BlockSpec((bt, POOL_BUF, POOL_W), lambda i: (i, 0, 0)),
        ],
        out_shape=[
            jax.ShapeDtypeStruct((B * ls, ATTN_W), bf16),
            jax.ShapeDtypeStruct((B * ls, KV_W), f32),
            jax.ShapeDtypeStruct((B * ls, KV_W), f32),
            jax.ShapeDtypeStruct((B * ls, POOL_W), bf16),
            jax.ShapeDtypeStruct((B, POOL_BUF, POOL_W), f32),
        ],
        scratch_shapes=[pltpu.VMEM((bt, HIST + ls, POOL_W), f32)],
        name="inproj_sample",
    )(x, mod3, mod3, g_attn, w_in, gqk, bd, w_pool, pool_scale, state)


def _softmax_sink(parts, sink):
    m = sink
    for s in parts:
        m = jnp.maximum(m, jnp.max(s, axis=-1, keepdims=True))
    ps = [jnp.exp(s - m) for s in parts]
    denom = jnp.exp(sink - m)
    for p in ps:
        denom = denom + jnp.sum(p, axis=-1, keepdims=True)
    inv = 1.0 / denom
    return [(p * inv).astype(bf16) for p in ps]


def _attn_prompt_body(sinks_ref, q_ref, kp_ref, kc_ref, vp_ref, vc_ref, bias_ref, mask_ref, o_ref):
    kk = jnp.concatenate([kp_ref[0], kc_ref[0]], axis=0)
    vv = jnp.concatenate([vp_ref[0], vc_ref[0]], axis=0)
    k0 = jnp.concatenate([kk, kk], axis=1)
    v0 = jnp.concatenate([vv, vv], axis=1)
    k1 = pltpu.roll(k0, HEAD_DIM, 1)
    v1 = pltpu.roll(v0, HEAD_DIM, 1)
    lane_group = lax.broadcasted_iota(i32, k0.shape, 1) // HEAD_DIM
    valid = mask_ref[0] > 0.5
    nk = 2 * WINDOW
    qw = GQA * HEAD_DIM
    for kv in range(N_KV_HEADS):
        def blockdiag(t0, t1):
            return jnp.concatenate(
                [jnp.where(lane_group == g, t0 if g % 2 == kv else t1, jnp.zeros_like(t0)) for g in range(GQA)],
                axis=0)
        s = jnp.where(valid, _dot_t(q_ref[0, :, kv * qw:(kv + 1) * qw], blockdiag(k0, k1)) + bias_ref[kv], NEG_INF)
        ps = [_softmax_sink([s[:, g * nk:(g + 1) * nk]], sinks_ref[kv * GQA + g])[0] for g in range(GQA)]
        o = _dot(jnp.concatenate(ps, axis=1), blockdiag(v0, v1))
        o_ref[0, :, kv * qw:(kv + 1) * qw] = o.astype(bf16)


def _attn_prompt(q, k, v, bias, sinks):
    B, L, _ = q.shape
    nb = L // WINDOW
    cur = lambda b, j: (b, j, 0)
    prev = lambda b, j: (b, jnp.maximum(j - 1, 0), 0)
    qi = np.arange(WINDOW)[:, None]
    kc = np.arange(2 * WINDOW)[None, :]
    own = (kc >= WINDOW) & (kc - WINDOW <= qi)
    prv = (kc < WINDOW) & (kc > qi)
    mask = np.stack([np.tile(own, (1, GQA)), np.tile(own | prv, (1, GQA))]).astype(np.float32)
    return pl.pallas_call(
        _attn_prompt_body,
        grid=(B, nb),
        in_specs=[
            pl.BlockSpec(memory_space=pltpu.SMEM),
            pl.BlockSpec((1, WINDOW, ATTN_W), cur),
            pl.BlockSpec((1, WINDOW, KV_W), prev),
            pl.BlockSpec((1, WINDOW, KV_W), cur),
            pl.BlockSpec((1, WINDOW, KV_W), prev),
            pl.BlockSpec((1, WINDOW, KV_W), cur),
            pl.BlockSpec((N_KV_HEADS, WINDOW, GQA * 2 * WINDOW), lambda b, j: (0, 0, 0)),
            pl.BlockSpec((1, WINDOW, GQA * 2 * WINDOW), lambda b, j: (jnp.minimum(j, 1), 0, 0)),
        ],
        out_specs=pl.BlockSpec((1, WINDOW, ATTN_W), cur),
        out_shape=jax.ShapeDtypeStruct((B, L, ATTN_W), bf16),
        name="attn_prompt",
    )(sinks, q, k, k, v, v, bias, jnp.asarray(mask))


def _attn_sample_body(q_ref, kb_ref, vb_ref, kn_ref, vn_ref, bb_ref, bn_ref, sink_ref,
                      o_ref, nk_ref, nv_ref, *, bb, ls):
    W = kb_ref.shape[1]
    rows = GQA * ls
    qi = lax.broadcasted_iota(i32, (rows, W), 0) % ls
    kj = lax.broadcasted_iota(i32, (rows, W), 1)
    valid_buf = kj > qi
    qi2 = lax.broadcasted_iota(i32, (rows, ls), 0) % ls
    kj2 = lax.broadcasted_iota(i32, (rows, ls), 1)
    valid_new = kj2 <= qi2

    def one(b, carry):
        qb = q_ref[b]
        kbuf = kb_ref[b]
        vbuf = vb_ref[b]
        knew = kn_ref[b]
        vnew = vn_ref[b]
        outs = []
        for kv in range(N_KV_HEADS):
            ks = slice(kv * HEAD_DIM, (kv + 1) * HEAD_DIM)
            qg = jnp.concatenate(
                [qb[:, (kv * GQA + g) * HEAD_DIM:(kv * GQA + g + 1) * HEAD_DIM] for g in range(GQA)], axis=0)
            s_buf = jnp.where(valid_buf, _dot_t(qg, kbuf[:, ks].astype(bf16)) + bb_ref[kv], NEG_INF)
            s_new = jnp.where(valid_new, _dot_t(qg, knew[:, ks].astype(bf16)) + bn_ref[kv], NEG_INF)
            p_buf, p_new = _softmax_sink([s_buf, s_new], sink_ref[kv])
            o = _dot(p_buf, vbuf[:, ks].astype(bf16)) + _dot(p_new, vnew[:, ks].astype(bf16))
            outs.extend([o[g * ls:(g + 1) * ls] for g in range(GQA)])
        o_ref[b] = jnp.concatenate(outs, axis=-1).astype(bf16)
        nk_ref[b, 0:W - ls, :] = kbuf[ls:, :]
        nk_ref[b, W - ls:W, :] = knew
        nv_ref[b, 0:W - ls, :] = vbuf[ls:, :]
        nv_ref[b, W - ls:W, :] = vnew
        return carry

    lax.fori_loop(0, bb, one, 0)


def _attn_sample(q, k_buf, v_buf, k_new, v_new, bias_buf, bias_new, sink_col, bb=16):
    B, ls, _ = q.shape
    W = k_buf.shape[1]
    rows = GQA * ls
    blk = lambda shape: pl.BlockSpec(shape, lambda i: (i, 0, 0))
    full = lambda shape: pl.BlockSpec(shape, lambda i: (0, 0, 0))
    return pl.pallas_call(
        functools.partial(_attn_sample_body, bb=bb, ls=ls),
        grid=(B // bb,),
        in_specs=[
            blk((bb, ls, ATTN_W)),
            blk((bb, W, KV_W)),
            blk((bb, W, KV_W)),
            blk((bb, ls, KV_W)),
            blk((bb, ls, KV_W)),
            full((N_KV_HEADS, rows, W)),
            full((N_KV_HEADS, rows, ls)),
            full((N_KV_HEADS, rows, 1)),
        ],
        out_specs=[blk((bb, ls, ATTN_W)), blk((bb, W, KV_W)), blk((bb, W, KV_W))],
        out_shape=[
            jax.ShapeDtypeStruct((B, ls, ATTN_W), bf16),
            jax.ShapeDtypeStruct((B, W, KV_W), f32),
            jax.ShapeDtypeStruct((B, W, KV_W), f32),
        ],
        name="attn_sample",
    )(q, k_buf, v_buf, k_new, v_new, bias_buf, bias_new, sink_col)


def _outproj_core(po, at, x, gt, sc, sh, g_ref, wo_ref, wrh_ref, wrl_ref):
    mixo = _dot(po, wo_ref[0:POOL_W, :]) + _dot(at, wo_ref[POOL_W:, :])
    x1 = x + gt * mixo.reshape(x.shape)
    h2 = _mod_norm(x1, g_ref[...].reshape((1,) * (x.ndim - 1) + (D_MODEL,)), sc, sh).reshape(-1, D_MODEL)
    h_hi, h_lo = _split_bf16(h2)
    wh = wrh_ref[...]
    logits = _dot_t(wh, h_hi) + (_dot_t(wh, h_lo) + _dot_t(wrl_ref[...], h_hi))
    return x1, _pack_pairs(h2), logits


def _outproj_prompt_body(po_ref, at_ref, x_ref, gt_ref, sc_ref, sh_ref, g_ref, wo_ref, wrh_ref, wrl_ref,
                         x1_ref, h2_ref, lg_ref):
    x1, h2p, logits = _outproj_core(po_ref[0], at_ref[0], x_ref[0], gt_ref[0], sc_ref[0], sh_ref[0],
                                    g_ref, wo_ref, wrh_ref, wrl_ref)
    x1_ref[...] = x1
    h2_ref[...] = h2p
    lg_ref[...] = logits


def _outproj_sample_body(po_ref, at_ref, x_ref, gt_ref, sc_ref, sh_ref, g_ref, wo_ref, wrh_ref, wrl_ref,
                         x1_ref, h2_ref, lg_ref):
    x1, h2p, logits = _outproj_core(po_ref[...], at_ref[...], x_ref[...], gt_ref[...], sc_ref[...], sh_ref[...],
                                    g_ref, wo_ref, wrh_ref, wrl_ref)
    x1_ref[...] = x1.reshape(-1, D_MODEL)
    h2_ref[...] = h2p
    lg_ref[...] = logits


def _outproj_prompt(po, at, x, mod3, g_ffn, w_out, wr_hi, wr_lo, tm=512):
    B, L, D = x.shape
    nt = L // tm
    n_tok = B * L
    full = lambda shape: pl.BlockSpec(shape, lambda b, j: (0,) * len(shape))
    modspec = lambda c: pl.BlockSpec((1, 1, D), lambda b, j: (b, 0, c))
    return pl.pallas_call(
        _outproj_prompt_body,
        grid=(B, nt),
        in_specs=[
            pl.BlockSpec((1, tm, POOL_W), lambda b, j: (b, j, 0)),
            pl.BlockSpec((1, tm, ATTN_W), lambda b, j: (b, j, 0)),
            pl.BlockSpec((1, tm, D), lambda b, j: (b, j, 0)),
            modspec(2), modspec(4), modspec(3),
            full((1, D)), full((D, D)), full((N_EXPERTS, D)), full((N_EXPERTS, D)),
        ],
        out_specs=[
            pl.BlockSpec((tm, D), lambda b, j: (b * nt + j, 0)),
            pl.BlockSpec((tm, D // 2), lambda b, j: (b * nt + j, 0)),
            pl.BlockSpec((N_EXPERTS, tm), lambda b, j: (0, b * nt + j)),
        ],
        out_shape=[
            jax.ShapeDtypeStruct((n_tok, D), f32),
            jax.ShapeDtypeStruct((n_tok, D // 2), i32),
            jax.ShapeDtypeStruct((N_EXPERTS, n_tok), f32),
        ],
        name="outproj_prompt",
    )(po, at, x, mod3, mod3, mod3, g_ffn, w_out, wr_hi, wr_lo)


def _outproj_sample(po, at, x, mod3, g_ffn, w_out, wr_hi, wr_lo, bt=64):
    B, ls, D = x.shape
    n = bt * ls
    full = lambda shape: pl.BlockSpec(shape, lambda i: (0,) * len(shape))
    modspec = lambda c: pl.BlockSpec((bt, 1, D), lambda i: (i, 0, c))
    return pl.pallas_call(
        _outproj_sample_body,
        grid=(B // bt,),
        in_specs=[
            pl.BlockSpec((n, POOL_W), lambda i: (i, 0)),
            pl.BlockSpec((n, ATTN_W), lambda i: (i, 0)),
            pl.BlockSpec((bt, ls, D), lambda i: (i, 0, 0)),
            modspec(2), modspec(4), modspec(3),
            full((1, D)), full((D, D)), full((N_EXPERTS, D)), full((N_EXPERTS, D)),
        ],
        out_specs=[
            pl.BlockSpec((n, D), lambda i: (i, 0)),
            pl.BlockSpec((n, D // 2), lambda i: (i, 0)),
            pl.BlockSpec((N_EXPERTS, n), lambda i: (0, i)),
        ],
        out_shape=[
            jax.ShapeDtypeStruct((B * ls, D), f32),
            jax.ShapeDtypeStruct((B * ls, D // 2), i32),
            jax.ShapeDtypeStruct((N_EXPERTS, B * ls), f32),
        ],
        name="outproj_sample",
    )(po, at, x, mod3, mod3, mod3, g_ffn, w_out, wr_hi, wr_lo)


def _route_body(lg_ref, rb_ref, tri_ref, idx_ref, rank_ref, gate_ref, cnt_ref, carry, *, tr, nsteps):
    step = pl.program_id(0)

    @pl.when(step == 0)
    def _():
        carry[...] = jnp.zeros(carry.shape, f32)

    s = jax.nn.sigmoid(lg_ref[...])
    sb = s + rb_ref[...]
    e_iota = lax.broadcasted_iota(i32, (N_EXPERTS, tr), 0)
    g_iota = lax.broadcasted_iota(i32, (GROUP_SIZE, tr), 0)

    gscore = []
    for g in range(N_EXPERT_GROUPS):
        v = sb[g * GROUP_SIZE:(g + 1) * GROUP_SIZE]
        m1 = jnp.max(v, axis=0, keepdims=True)
        i1 = jnp.min(jnp.where(v == m1, g_iota, GROUP_SIZE), axis=0, keepdims=True)
        m2 = jnp.max(jnp.where(g_iota == i1, -jnp.inf, v), axis=0, keepdims=True)
        gscore.append(m1 + m2)
    parts = []
    for g in range(N_EXPERT_GROUPS):
        beaten = jnp.zeros((1, tr), i32)
        for g2 in range(N_EXPERT_GROUPS):
            if g2 == g:
                continue
            ahead = gscore[g2] > gscore[g]
            if g2 < g:
                ahead = ahead | (gscore[g2] == gscore[g])
            beaten = beaten + ahead.astype(i32)
        keep = beaten < TOPK_GROUPS
        parts.append(jnp.where(keep, sb[g * GROUP_SIZE:(g + 1) * GROUP_SIZE], NEG_INF))
    cur = jnp.concatenate(parts, axis=0)

    sel = jnp.zeros((N_EXPERTS, tr), f32)
    idxs, svals = [], []
    for _ in range(TOP_K):
        m = jnp.max(cur, axis=0, keepdims=True)
        ik = jnp.min(jnp.where(cur == m, e_iota, N_EXPERTS), axis=0, keepdims=True)
        hit = e_iota == ik
        svals.append(jnp.sum(jnp.where(hit, s, 0.0), axis=0, keepdims=True))
        cur = jnp.where(hit, -jnp.inf, cur)
        sel = jnp.where(hit, 1.0, sel)
        idxs.append(ik)
    ssum = svals[0]
    for sv in svals[1:]:
        ssum = ssum + sv
    gate_ref[...] = jnp.concatenate([sv / ssum * ROUTED_SCALE for sv in svals], axis=0)
    idx_ref[...] = jnp.concatenate(idxs, axis=0)

    before = carry[...] + _dot(sel.astype(bf16), tri_ref[...])
    ranks = [jnp.sum(jnp.where(e_iota == ik, before, 0.0), axis=0, keepdims=True) for ik in idxs]
    rank_ref[...] = jnp.concatenate(ranks, axis=0).astype(i32)
    carry[...] = carry[...] + jnp.sum(sel, axis=1, keepdims=True)

    @pl.when(step == nsteps - 1)
    def _():
        cnt_ref[...] = carry[...]


def _route(logits_t, router_bias, tr=512):
    E, T = logits_t.shape
    nsteps = T // tr
    tri = jnp.asarray(np.triu(np.ones((tr, tr), np.float32), 1), bf16)
    return pl.pallas_call(
        functools.partial(_route_body, tr=tr, nsteps=nsteps),
        grid=(nsteps,),
        in_specs=[
            pl.BlockSpec((E, tr), lambda i: (0, i)),
            pl.BlockSpec((E, 1), lambda i: (0, 0)),
            pl.BlockSpec((tr, tr), lambda i: (0, 0)),
        ],
        out_specs=[
            pl.BlockSpec((TOP_K, tr), lambda i: (0, i)),
            pl.BlockSpec((TOP_K, tr), lambda i: (0, i)),
            pl.BlockSpec((TOP_K, tr), lambda i: (0, i)),
            pl.BlockSpec((E, 1), lambda i: (0, 0)),
        ],
        out_shape=[
            jax.ShapeDtypeStruct((TOP_K, T), i32),
            jax.ShapeDtypeStruct((TOP_K, T), i32),
            jax.ShapeDtypeStruct((TOP_K, T), f32),
            jax.ShapeDtypeStruct((E, 1), f32),
        ],
        scratch_shapes=[pltpu.VMEM((E, 1), f32)],
        compiler_params=pltpu.CompilerParams(dimension_semantics=("arbitrary",)),
        name="route",
    )(logits_t, router_bias.reshape(E, 1), tri)


def _dest_body(idx_ref, rank_ref, ps_ref, dest_ref, *, tr):
    e_iota = lax.broadcasted_iota(i32, (N_EXPERTS, tr), 0)
    start = ps_ref[...]
    rows = []
    for k in range(TOP_K):
        hit = e_iota == idx_ref[k:k + 1, :]
        rows.append(jnp.sum(jnp.where(hit, start, 0.0), axis=0, keepdims=True))
    dest_ref[...] = jnp.concatenate(rows, axis=0).astype(i32) + rank_ref[...]


def _dest_rows(idx, rank, pad_start, tr=512):
    K, T = idx.shape
    blk = pl.BlockSpec((K, tr), lambda i: (0, i))
    return pl.pallas_call(
        functools.partial(_dest_body, tr=tr),
        grid=(T // tr,),
        in_specs=[blk, blk, pl.BlockSpec((N_EXPERTS, 1), lambda i: (0, 0))],
        out_specs=blk,
        out_shape=jax.ShapeDtypeStruct((K, T), i32),
        name="dest_rows",
    )(idx, rank, pad_start.astype(f32).reshape(N_EXPERTS, 1))


def _sc_mesh():
    return plsc.VectorSubcoreMesh(core_axis_name="c", subcore_axis_name="s")


def _sc_worker_id():
    return lax.axis_index("s") * 2 + lax.axis_index("c")


def _dispatch(h2_a, h2_b, dest, gate, n_rows, chunk=32):
    ta, Dw = h2_a.shape
    T = ta + h2_b.shape[0]
    per_worker = T // SC_WORKERS
    nchunk = per_worker // chunk
    assert per_worker * SC_WORKERS == T and nchunk * chunk == per_worker and ta % chunk == 0

    @functools.partial(
        pl.kernel, mesh=_sc_mesh(),
        out_type=[jax.ShapeDtypeStruct((n_rows, Dw), i32), jax.ShapeDtypeStruct((n_rows, GATE_ROW), f32)],
        scratch_types=[pltpu.VMEM((chunk,), i32), pltpu.VMEM((chunk, Dw), i32),
                       pltpu.VMEM((chunk,), f32), pltpu.VMEM((chunk, GATE_ROW), f32)],
        compiler_params=pltpu.CompilerParams(needs_layout_passes=False),
        name="moe_dispatch",
    )
    def body(ha_hbm, hb_hbm, dest_hbm, gate_hbm, xs_hbm, gs_hbm, idx_v, rows_v, gate_v, grow_v):
        base = _sc_worker_id() * per_worker

        @pl.loop(0, nchunk)
        def _(ci):
            t0 = base + ci * chunk

            @pl.when(t0 < ta)
            def _():
                pltpu.sync_copy(ha_hbm.at[pl.ds(t0, chunk)], rows_v)

            @pl.when(t0 >= ta)
            def _():
                pltpu.sync_copy(hb_hbm.at[pl.ds(t0 - ta, chunk)], rows_v)

            for k in range(TOP_K):
                pltpu.sync_copy(dest_hbm.at[k, pl.ds(t0, chunk)], idx_v)
                pltpu.sync_copy(gate_hbm.at[k, pl.ds(t0, chunk)], gate_v)

                @pl.loop(0, chunk)
                def _(t):
                    g = plsc.load_gather(gate_v, [jnp.zeros((SC_LANES,), i32) + t])
                    for j in range(GATE_ROW // SC_LANES):
                        grow_v[t, pl.ds(j * SC_LANES, SC_LANES)] = g

                pltpu.sync_copy(rows_v, xs_hbm.at[idx_v])
                pltpu.sync_copy(grow_v, gs_hbm.at[idx_v])

    return body(h2_a, h2_b, dest, gate)


def _combine(ys, dest, chunk=8):
    T = dest.shape[1]
    Dw = ys.shape[1]
    per_worker = T // SC_WORKERS
    nchunk = per_worker // chunk
    assert per_worker * SC_WORKERS == T and nchunk * chunk == per_worker and nchunk % 2 == 0

    @functools.partial(
        pl.kernel, mesh=_sc_mesh(),
        out_type=jax.ShapeDtypeStruct((T, 2 * Dw), f32),
        scratch_types=[
            pltpu.VMEM((TOP_K * per_worker,), i32),
            pltpu.VMEM((2, TOP_K, chunk, Dw), i32),
            pltpu.VMEM((chunk, 2 * Dw), f32),
            pltpu.SemaphoreType.DMA((2,)),
        ],
        compiler_params=pltpu.CompilerParams(needs_layout_passes=False),
        name="moe_combine",
    )
    def body(ys_hbm, dest_hbm, out_hbm, idx_v, buf, out_v, sems):
        base = _sc_worker_id() * per_worker
        pltpu.sync_copy(dest_hbm.at[pl.ds(_sc_worker_id() * (TOP_K * per_worker), TOP_K * per_worker)], idx_v)

        def gather(ci, slot):
            return [pltpu.make_async_copy(ys_hbm.at[idx_v.at[pl.ds(k * per_worker + ci * chunk, chunk)]],
                                          buf.at[slot, k], sems.at[slot]) for k in range(TOP_K)]

        for cp in gather(0, 0):
            cp.start()

        @pl.loop(0, nchunk, step=2)
        def _(c0):
            for slot in range(2):
                ci = c0 + slot

                @pl.when(ci + 1 < nchunk)
                def _():
                    for cp in gather(ci + 1, 1 - slot):
                        cp.start()

                for cp in gather(ci, slot):
                    cp.wait()

                @pl.loop(0, chunk)
                def _(t):
                    @pl.loop(0, Dw // SC_LANES)
                    def _(j):
                        sl = pl.ds(j * SC_LANES, SC_LANES)
                        w = buf[slot, 0, t, sl]
                        hi = plsc.bitcast(w & HI_MASK, f32)
                        lo = plsc.bitcast(lax.shift_left(w, 16), f32)
                        for k in range(1, TOP_K):
                            w = buf[slot, k, t, sl]
                            hi = hi + plsc.bitcast(w & HI_MASK, f32)
                            lo = lo + plsc.bitcast(lax.shift_left(w, 16), f32)
                        out_v[t, sl] = hi
                        out_v[t, pl.ds(Dw + j * SC_LANES, SC_LANES)] = lo

                pltpu.sync_copy(out_v, out_hbm.at[pl.ds(base + ci * chunk, chunk)])

    dest_w = dest.reshape(TOP_K, SC_WORKERS, per_worker).transpose(1, 0, 2).reshape(-1)
    return body(ys, dest_w)


def _gmm_body(blk_e_ref, blk_rows_ref, nv_ref, xs_hbm, gs_hbm, w1_hbm, w3_hbm, w2_hbm, ys_hbm,
              xbuf, gbuf, ybuf, w1f, w3f, w2f, w13_b, w2_b, xsem, gsem, ysem, wsem):
    nv = nv_ref[0]
    nb = blk_e_ref.shape[0]
    half = D_MODEL // 2
    RB = EXPERT_BLOCK

    def row_copies(b, slot):
        r0 = pl.multiple_of(b * RB, RB)
        return (pltpu.make_async_copy(xs_hbm.at[pl.ds(r0, RB)], xbuf.at[slot], xsem.at[slot]),
                pltpu.make_async_copy(gs_hbm.at[pl.ds(r0, RB)], gbuf.at[slot], gsem.at[slot]))

    def out_copy(b, slot):
        r0 = pl.multiple_of(b * RB, RB)
        return pltpu.make_async_copy(ybuf.at[slot], ys_hbm.at[pl.ds(r0, RB)], ysem.at[slot])

    def weight_copies(e, ws):
        return (pltpu.make_async_copy(w1_hbm.at[e], w1f.at[ws], wsem.at[ws, 0]),
                pltpu.make_async_copy(w3_hbm.at[e], w3f.at[ws], wsem.at[ws, 1]),
                pltpu.make_async_copy(w2_hbm.at[e], w2f.at[ws], wsem.at[ws, 2]))

    @pl.when(nv > 0)
    def _():
        for cp in weight_copies(blk_e_ref[0], 0):
            cp.start()
        for cp in row_copies(0, 0):
            cp.start()

    def step(b, ws_prev):
        slot = b % 2
        e = blk_e_ref[b]
        first = (b == 0) | (e != blk_e_ref[jnp.maximum(b - 1, 0)])
        ws = jnp.where(first & (b > 0), 1 - ws_prev, ws_prev)

        @pl.when(b + 1 < nv)
        def _():
            for cp in row_copies(b + 1, 1 - slot):
                cp.start()

        @pl.when(first)
        def _():
            for cp in weight_copies(e, ws):
                cp.wait()
            w13_b[:, 0:EXPERT_FF] = w1f[ws].astype(bf16)
            w13_b[:, EXPERT_FF:] = w3f[ws].astype(bf16)
            w2_b[...] = w2f[ws].astype(bf16)
            nxt = lax.while_loop(lambda i: (i < nv) & (blk_e_ref[jnp.minimum(i, nb - 1)] == e),
                                 lambda i: i + 1, b + 1)

            @pl.when(nxt < nv)
            def _():
                for cp in weight_copies(blk_e_ref[jnp.minimum(nxt, nb - 1)], 1 - ws):
                    cp.start()

        for cp in row_copies(b, slot):
            cp.wait()
        valid = lax.broadcasted_iota(i32, (RB, 1), 0) < blk_rows_ref[b]
        x_hi, x_lo = _unpack_pairs(jnp.where(valid, xbuf[slot], 0))
        ac = _dot(x_hi, w13_b[0:half, :]) + _dot(x_lo, w13_b[half:, :])
        a = ac[:, :EXPERT_FF]
        c = ac[:, EXPERT_FF:]
        hmid = (a * jax.nn.sigmoid(a)) * c
        g = jnp.where(valid, gbuf[slot][:, 0:1], 0.0)
        y = _pack_pairs(_dot(hmid.astype(bf16), w2_b[...]) * g)

        @pl.when(b >= 2)
        def _():
            out_copy(b - 2, slot).wait()

        ybuf[slot] = y
        out_copy(b, slot).start()
        return ws

    lax.fori_loop(0, nv, step, 0)

    @pl.when(nv >= 2)
    def _():
        out_copy(nv - 2, nv % 2).wait()

    @pl.when(nv >= 1)
    def _():
        out_copy(nv - 1, (nv - 1) % 2).wait()


def _gmm(xs, gs, w1, w3, w2, blk_e, blk_rows, n_valid):
    n_rows, Dw = xs.shape
    D = 2 * Dw
    RB = EXPERT_BLOCK
    hbm = pl.BlockSpec(memory_space=pl.ANY)
    return pl.pallas_call(
        _gmm_body,
        grid_spec=pltpu.PrefetchScalarGridSpec(
            num_scalar_prefetch=3,
            grid=(1,),
            in_specs=[hbm, hbm, hbm, hbm, hbm],
            out_specs=hbm,
            scratch_shapes=[
                pltpu.VMEM((2, RB, Dw), i32), pltpu.VMEM((2, RB, GATE_ROW), f32), pltpu.VMEM((2, RB, Dw), i32),
                pltpu.VMEM((2, D, EXPERT_FF), f32), pltpu.VMEM((2, D, EXPERT_FF), f32),
                pltpu.VMEM((2, EXPERT_FF, D), f32),
                pltpu.VMEM((D, 2 * EXPERT_FF), bf16), pltpu.VMEM((EXPERT_FF, D), bf16),
                pltpu.SemaphoreType.DMA((2,)), pltpu.SemaphoreType.DMA((2,)), pltpu.SemaphoreType.DMA((2,)),
                pltpu.SemaphoreType.DMA((2, 3)),
            ],
        ),
        out_shape=jax.ShapeDtypeStruct((n_rows, Dw), i32),
        compiler_params=pltpu.CompilerParams(dimension_semantics=("arbitrary",)),
        name="moe_gmm",
    )(blk_e, blk_rows, n_valid, xs, gs, w1, w3, w2)


def _final_core(x1, h2p, comb, gt, ws1_ref, ws3_ref, ws2_ref):
    half = D_MODEL // 2
    h_hi, h_lo = _unpack_pairs(h2p)
    a = _dot(h_hi, ws1_ref[0:half, :]) + _dot(h_lo, ws1_ref[half:, :])
    c = _dot(h_hi, ws3_ref[0:half, :]) + _dot(h_lo, ws3_ref[half:, :])
    shared = _dot(((a * jax.nn.sigmoid(a)) * c).astype(bf16), ws2_ref[...])
    return x1, comb + shared, gt


def _final_prompt_body(x1_ref, h2_ref, cb_ref, gt_ref, ws1_ref, ws3_ref, ws2_ref, y_ref):
    x1, ffn, gt = _final_core(x1_ref[...], h2_ref[...], cb_ref[...], gt_ref[0], ws1_ref, ws3_ref, ws2_ref)
    y_ref[0] = x1 + gt * ffn


def _final_sample_body(x1_ref, h2_ref, cb_ref, gt_ref, ws1_ref, ws3_ref, ws2_ref, y_ref):
    x1, ffn, gt = _final_core(x1_ref[...], h2_ref[...], cb_ref[...], gt_ref[...], ws1_ref, ws3_ref, ws2_ref)
    shp = y_ref.shape
    y_ref[...] = x1.reshape(shp) + gt * ffn.reshape(shp)


def _final_prompt(x1, h2, comb, mod3, ws1, ws3, ws2, B, L, tm=512):
    D = D_MODEL
    nt = L // tm
    full = lambda shape: pl.BlockSpec(shape, lambda b, j: (0,) * len(shape))
    rows = pl.BlockSpec((tm, D), lambda b, j: (b * nt + j, 0))
    words = pl.BlockSpec((tm, D // 2), lambda b, j: (b * nt + j, 0))
    return pl.pallas_call(
        _final_prompt_body,
        grid=(B, nt),
        in_specs=[rows, words, rows, pl.BlockSpec((1, 1, D), lambda b, j: (b, 0, 5)),
                  full((D, EXPERT_FF)), full((D, EXPERT_FF)), full((EXPERT_FF, D))],
        out_specs=pl.BlockSpec((1, tm, D), lambda b, j: (b, j, 0)),
        out_shape=jax.ShapeDtypeStruct((B, L, D), f32),
        name="final_prompt",
    )(x1, h2, comb, mod3, ws1, ws3, ws2)


def _final_sample(x1, h2, comb, mod3, ws1, ws3, ws2, B, ls, row0, bt=64):
    D = D_MODEL
    n = bt * ls
    blk0 = row0 // n
    full = lambda shape: pl.BlockSpec(shape, lambda i: (0,) * len(shape))
    rows = pl.BlockSpec((n, D), lambda i: (i, 0))
    words = pl.BlockSpec((n, D // 2), lambda i: (i, 0))
    comb_rows = pl.BlockSpec((n, D), lambda i: (blk0 + i, 0))
    return pl.pallas_call(
        _final_sample_body,
        grid=(B // bt,),
        in_specs=[rows, words, comb_rows, pl.BlockSpec((bt, 1, D), lambda i: (i, 0, 5)),
                  full((D, EXPERT_FF)), full((D, EXPERT_FF)), full((EXPERT_FF, D))],
        out_specs=pl.BlockSpec((bt, ls, D), lambda i: (i, 0, 0)),
        out_shape=jax.ShapeDtypeStruct((B, ls, D), f32),
        name="final_sample",
    )(x1, h2, comb, mod3, ws1, ws3, ws2)


def kernel(x_prompt, x_sample, state_pool, cache_swa_k, cache_swa_v, c_prompt, c_sample, w_ada, b_ada,
           g_attn_norm, w_in, g_q, g_k, w_pool, pool_scale, w_out, attn_sinks, rel_bias, g_ffn_norm,
           w_router, router_bias, w1, w3, w2, ws1, ws3, ws2):
    B, L, D = x_prompt.shape
    BS, LS, _ = x_sample.shape
    depth = w_ada.shape[0]
    assert depth == 1
    W = cache_swa_k.shape[2]
    tp, ts = B * L, BS * LS
    T = tp + ts
    n_rows = (T * TOP_K // EXPERT_BLOCK + N_EXPERTS) * EXPERT_BLOCK
    nb = n_rows // EXPERT_BLOCK

    g_attn = g_attn_norm[0].reshape(1, D)
    g_ffn = g_ffn_norm[0].reshape(1, D)
    w_in_b = w_in[0].astype(bf16)
    w_out_b = w_out[0].astype(bf16)
    w_pool_b = w_pool[0].astype(bf16)
    ps = pool_scale[0].reshape(1, POOL_W)
    gqk = jnp.concatenate([jnp.tile(g_q[0], N_HEADS), jnp.tile(g_k[0], N_KV_HEADS)]).reshape(1, QK_W)
    head_of = np.arange(QK_W) // HEAD_DIM
    bd = jnp.asarray((head_of[:, None] == head_of[None, :]).astype(np.float32), bf16)
    wr_t = w_router[0].T
    wr_hi = wr_t.astype(bf16)
    wr_lo = (wr_t - wr_hi.astype(f32)).astype(bf16)
    ws1_b, ws3_b, ws2_b = ws1[0].astype(bf16), ws3[0].astype(bf16), ws2[0].astype(bf16)
    sinks = attn_sinks[0]

    mod = _ada(jnp.concatenate([c_prompt, c_sample], axis=0), w_ada[0], b_ada[0])
    mod_p = mod[:B].reshape(B, 1, 6 * D)
    mod_s = mod[B:].reshape(BS, 1, 6 * D)

    dist_p = np.arange(WINDOW)[:, None] + WINDOW - np.arange(2 * WINDOW)[None, :]
    bias_p = _relbias(rel_bias, dist_p)
    bias_p = bias_p.reshape(N_KV_HEADS, GQA, WINDOW, 2 * WINDOW).transpose(0, 2, 1, 3).reshape(
        N_KV_HEADS, WINDOW, GQA * 2 * WINDOW)
    dist_s = np.arange(LS)[:, None] + W - np.arange(W + LS)[None, :]
    bias_s = _relbias(rel_bias, dist_s)
    bias_s_buf = bias_s[:, :, :W].reshape(N_KV_HEADS, GQA * LS, W)
    bias_s_new = bias_s[:, :, W:].reshape(N_KV_HEADS, GQA * LS, LS)
    sink_col = jnp.repeat(sinks, LS).reshape(N_KV_HEADS, GQA * LS, 1)

    q_p, k_p, v_p, po_p, new_pool_p, kc_p, vc_p = _inproj_prompt(
        x_prompt, mod_p, g_attn, w_in_b, gqk, bd, w_pool_b, ps)
    q_s, k_s, v_s, po_s, new_pool_s = _inproj_sample(
        x_sample, mod_s, g_attn, w_in_b, gqk, bd, w_pool_b, ps, state_pool[0], PAST_LEN)
    at_p = _attn_prompt(q_p, k_p, v_p, bias_p, sinks)
    at_s, nk_s, nv_s = _attn_sample(
        q_s.reshape(BS, LS, ATTN_W), cache_swa_k[0].reshape(BS, W, KV_W), cache_swa_v[0].reshape(BS, W, KV_W),
        k_s.reshape(BS, LS, KV_W), v_s.reshape(BS, LS, KV_W), bias_s_buf, bias_s_new, sink_col)

    x1_p, h2_p, lg_p = _outproj_prompt(po_p, at_p, x_prompt, mod_p, g_ffn, w_out_b, wr_hi, wr_lo)
    x1_s, h2_s, lg_s = _outproj_sample(po_s, at_s.reshape(ts, ATTN_W), x_sample, mod_s, g_ffn, w_out_b,
                                       wr_hi, wr_lo)

    idx, rank, gate, counts = _route(jnp.concatenate([lg_p, lg_s], axis=1), router_bias[0])
    counts = counts.reshape(N_EXPERTS).astype(i32)
    padded = (counts + EXPERT_BLOCK - 1) // EXPERT_BLOCK * EXPERT_BLOCK
    pad_end = jnp.cumsum(padded)
    pad_start = pad_end - padded
    dest = _dest_rows(idx, rank, pad_start)
    n_valid = (pad_end[-1] // EXPERT_BLOCK).astype(i32).reshape(1)
    blk_row0 = jnp.arange(nb, dtype=i32) * EXPERT_BLOCK
    blk_e = jnp.minimum(jnp.sum(blk_row0[:, None] >= pad_end[None, :], axis=1), N_EXPERTS - 1).astype(i32)
    own = jnp.arange(N_EXPERTS, dtype=i32)[None, :] == blk_e[:, None]
    blk_cnt = jnp.sum(jnp.where(own, counts[None, :], 0), axis=1)
    blk_start = jnp.sum(jnp.where(own, pad_start[None, :], 0), axis=1)
    blk_rows = jnp.clip(blk_cnt - (blk_row0 - blk_start), 0, EXPERT_BLOCK).astype(i32)

    xs, gs = _dispatch(h2_p, h2_s, dest, gate, n_rows)
    ys = _gmm(xs, gs, w1[0], w3[0], w2[0], blk_e, blk_rows, n_valid)
    comb = _combine(ys, dest)

    y_p = _final_prompt(x1_p, h2_p, comb, mod_p, ws1_b, ws3_b, ws2_b, B, L)
    y_s = _final_sample(x1_s, h2_s, comb, mod_s, ws1_b, ws3_b, ws2_b, BS, LS, tp)

    return (y_p, y_s, new_pool_p[None], kc_p.reshape(1, B, WINDOW, N_KV_HEADS, HEAD_DIM),
            vc_p.reshape(1, B, WINDOW, N_KV_HEADS, HEAD_DIM), new_pool_s[None],
            nk_s.reshape(1, BS, W, N_KV_HEADS, HEAD_DIM), nv_s.reshape(1, BS, W, N_KV_HEADS, HEAD_DIM))
```

```python
import functools
import math

import numpy as np
import jax
import jax.numpy as jnp
from jax import lax
from jax.experimental import pallas as pl
from jax.experimental.pallas import tpu as pltpu
from jax.experimental.pallas import tpu_sc as plsc

f32 = jnp.float32
bf16 = jnp.bfloat16
i32 = jnp.int32

D_MODEL = 1024
PAST_LEN = 8192
POOL_W = 512
POOL_WINDOWS = (2, 4, 8, 16)
POOL_GC = 128
POOL_BUF = 15
ATTN_W = 512
HEAD_DIM = 64
N_HEADS = 8
N_KV_HEADS = 2
GQA = 4
KV_W = 128
WINDOW = 128
NUM_BUCKETS = 32
MAX_EXACT = 16
REL_MAX_DIST = 128
N_EXPERTS = 256
N_EXPERT_GROUPS = 8
GROUP_SIZE = 32
TOPK_GROUPS = 4
TOP_K = 8
EXPERT_FF = 256
ROUTED_SCALE = 2.5
EXPERT_BLOCK = 128
EPS = 1e-6
NEG_INF = -1e30
QKV_W = POOL_W + ATTN_W + 2 * KV_W
QK_W = ATTN_W + KV_W
HIST = 16

SC_WORKERS = 32
SC_LANES = 16
GATE_ROW = 128


def _dot(a, b):
    return jnp.dot(a, b, preferred_element_type=f32)


def _dot_t(a, b):
    return lax.dot_general(a, b, (((1,), (1,)), ((), ())), preferred_element_type=f32)


def _split_bf16(a):
    hi = a.astype(bf16)
    lo = (a - hi.astype(f32)).astype(bf16)
    return hi, lo


ROW_RING = 4
W_RING = 3
HI_MASK = -65536


def _pack_pairs(a):
    h = a.shape[1] // 2
    hi = lax.bitcast_convert_type(a[:, :h].astype(bf16).astype(f32), i32)
    lo = lax.bitcast_convert_type(a[:, h:].astype(bf16).astype(f32), i32)
    return hi | lax.shift_right_logical(lo, 16)


def _unpack_pairs(w):
    hi = lax.bitcast_convert_type(w & HI_MASK, f32).astype(bf16)
    lo = lax.bitcast_convert_type(lax.shift_left(w, 16), f32).astype(bf16)
    return hi, lo


def _mod_norm(x, g, sc, sh):
    ms = jnp.mean(x * x, axis=-1, keepdims=True)
    y = x * lax.rsqrt(ms + EPS)
    return (y * g) * (1.0 + sc) + sh


def _ada_body(c_ref, w_ref, b_ref, o_ref):
    c = c_ref[...]
    a = (c * jax.nn.sigmoid(c)).astype(bf16)
    o_ref[...] = _dot(a, w_ref[...].astype(bf16)) + b_ref[...]


def _ada(c, w_ada, b_ada):
    n = c.shape[0]
    tn = 1024
    return pl.pallas_call(
        _ada_body,
        grid=(6 * D_MODEL // tn,),
        in_specs=[
            pl.BlockSpec((n, D_MODEL), lambda j: (0, 0)),
            pl.BlockSpec((D_MODEL, tn), lambda j: (0, j)),
            pl.BlockSpec((1, tn), lambda j: (0, j)),
        ],
        out_specs=pl.BlockSpec((n, tn), lambda j: (0, j)),
        out_shape=jax.ShapeDtypeStruct((n, 6 * D_MODEL), f32),
        name="ada_mod",
    )(c, w_ada, b_ada.reshape(1, -1))


def _relbias_body(table_ref, bucket_ref, o_ref):
    bucket = bucket_ref[...]
    for h in range(N_HEADS):
        acc = jnp.zeros(bucket.shape, f32)
        for b in range(NUM_BUCKETS):
            acc = jnp.where(bucket == b, table_ref[b, h], acc)
        o_ref[h] = acc


def _rel_buckets(dist):
    n = np.maximum(dist, 0)
    nf = np.maximum(n, 1).astype(np.float64)
    large = MAX_EXACT + (np.log(nf / MAX_EXACT) / math.log(REL_MAX_DIST / MAX_EXACT)
                         * (NUM_BUCKETS - MAX_EXACT)).astype(np.int32)
    return np.where(n < MAX_EXACT, n, np.minimum(large, NUM_BUCKETS - 1)).astype(np.int32)


def _relbias(table, dist):
    lq, lk = dist.shape
    return pl.pallas_call(
        _relbias_body,
        in_specs=[
            pl.BlockSpec(memory_space=pltpu.SMEM),
            pl.BlockSpec((lq, lk), lambda: (0, 0)),
        ],
        out_specs=pl.BlockSpec((N_HEADS, lq, lk), lambda: (0, 0, 0)),
        out_shape=jax.ShapeDtypeStruct((N_HEADS, lq, lk), f32),
        name="rel_bias",
    )(table, jnp.asarray(_rel_buckets(dist)))


def _qkv_from_h(h, w_ref, gqk_ref, bd_ref):
    u = _dot(h.astype(bf16), w_ref[...])
    qk = u[:, POOL_W:POOL_W + QK_W]
    y_hi, y_lo = _split_bf16(qk * qk)
    bd = bd_ref[...]
    ss = _dot(y_hi, bd) + _dot(y_lo, bd)
    qkn = (qk * lax.rsqrt(ss * (1.0 / HEAD_DIM) + EPS)) * gqk_ref[...]
    q = qkn[:, :ATTN_W] * (HEAD_DIM ** -0.5)
    k = qkn[:, ATTN_W:]
    v = u[:, POOL_W + QK_W:]
    return u[:, :POOL_W], q, k, v


def _inproj_prompt_body(x_ref, sh_ref, sc_ref, g_ref, w_ref, gqk_ref, bd_ref, wp_ref, ps_ref,
                        q_ref, k_ref, v_ref, po_ref, np_ref, kc_ref, vc_ref, hist, *, tl, nt):
    j = pl.program_id(1)
    h = _mod_norm(x_ref[0], g_ref[...], sc_ref[0], sh_ref[0])
    up, q, k, v = _qkv_from_h(h, w_ref, gqk_ref, bd_ref)
    q_ref[0] = q.astype(bf16)
    k_ref[0] = k.astype(bf16)
    v_ref[0] = v.astype(bf16)

    @pl.when(j == nt - 1)
    def _():
        kc_ref[0] = k[tl - WINDOW:, :]
        vc_ref[0] = v[tl - WINDOW:, :]

    @pl.when(j == 0)
    def _():
        hist[0:HIST, :] = jnp.zeros((HIST, POOL_W), f32)

    hist[HIST:HIST + tl, :] = up
    pos = j * tl + lax.broadcasted_iota(i32, (tl, 1), 0)
    for g, w in enumerate(POOL_WINDOWS):
        lanes = slice(g * POOL_GC, (g + 1) * POOL_GC)
        cur = up[:, lanes]
        acc = cur
        for s in range(1, w):
            acc = acc + hist[HIST - s:HIST - s + tl, lanes]
        cnt = jnp.minimum(w, pos + 1).astype(f32)
        d = acc / cnt - cur
        yg = _dot(d.astype(bf16), wp_ref[g]) * ps_ref[:, lanes]
        po_ref[0, :, lanes] = yg.astype(bf16)

    @pl.when(j == nt - 1)
    def _():
        np_ref[0] = hist[tl + 1:tl + HIST, :]

    hist[0:HIST, :] = hist[tl:tl + HIST, :]


def _inproj_prompt(x, mod3, g_attn, w_in, gqk, bd, w_pool, pool_scale, tl=512):
    B, L, D = x.shape
    nt = L // tl
    full = lambda shape: pl.BlockSpec(shape, lambda b, j: (0,) * len(shape))
    return pl.pallas_call(
        functools.partial(_inproj_prompt_body, tl=tl, nt=nt),
        grid=(B, nt),
        in_specs=[
            pl.BlockSpec((1, tl, D), lambda b, j: (b, j, 0)),
            pl.BlockSpec((1, 1, D), lambda b, j: (b, 0, 0)),
            pl.BlockSpec((1, 1, D), lambda b, j: (b, 0, 1)),
            full((1, D)),
            full((D, QKV_W)),
            full((1, QK_W)),
            full((QK_W, QK_W)),
            full((4, POOL_GC, POOL_GC)),
            full((1, POOL_W)),
        ],
        out_specs=[
            pl.BlockSpec((1, tl, ATTN_W), lambda b, j: (b, j, 0)),
            pl.BlockSpec((1, tl, KV_W), lambda b, j: (b, j, 0)),
            pl.BlockSpec((1, tl, KV_W), lambda b, j: (b, j, 0)),
            pl.BlockSpec((1, tl, POOL_W), lambda b, j: (b, j, 0)),
            pl.BlockSpec((1, POOL_BUF, POOL_W), lambda b, j: (b, 0, 0)),
            pl.BlockSpec((1, WINDOW, KV_W), lambda b, j: (b, 0, 0)),
            pl.BlockSpec((1, WINDOW, KV_W), lambda b, j: (b, 0, 0)),
        ],
        out_shape=[
            jax.ShapeDtypeStruct((B, L, ATTN_W), bf16),
            jax.ShapeDtypeStruct((B, L, KV_W), bf16),
            jax.ShapeDtypeStruct((B, L, KV_W), bf16),
            jax.ShapeDtypeStruct((B, L, POOL_W), bf16),
            jax.ShapeDtypeStruct((B, POOL_BUF, POOL_W), f32),
            jax.ShapeDtypeStruct((B, WINDOW, KV_W), f32),
            jax.ShapeDtypeStruct((B, WINDOW, KV_W), f32),
        ],
        scratch_shapes=[pltpu.VMEM((HIST + tl, POOL_W), f32)],
        compiler_params=pltpu.CompilerParams(dimension_semantics=("arbitrary", "arbitrary")),
        name="inproj_prompt",
    )(x, mod3, mod3, g_attn, w_in, gqk, bd, w_pool, pool_scale)


def _inproj_sample_body(x_ref, sh_ref, sc_ref, g_ref, w_ref, gqk_ref, bd_ref, wp_ref, ps_ref, st_ref,
                        q_ref, k_ref, v_ref, po_ref, np_ref, ext, *, bt, ls, pos0):
    n = bt * ls
    h3 = _mod_norm(x_ref[...], g_ref[...][None], sc_ref[...], sh_ref[...])
    up, q, k, v = _qkv_from_h(h3.reshape(n, D_MODEL), w_ref, gqk_ref, bd_ref)
    q_ref[...] = q.astype(bf16)
    k_ref[...] = k
    v_ref[...] = v

    ext[:, 1:HIST, :] = st_ref[...]
    ext[:, HIST:HIST + ls, :] = up.reshape(bt, ls, POOL_W)
    pos = pos0 + lax.broadcasted_iota(i32, (1, ls, 1), 1)
    for g, w in enumerate(POOL_WINDOWS):
        lanes = slice(g * POOL_GC, (g + 1) * POOL_GC)
        cur = ext[:, HIST:HIST + ls, lanes]
        acc = cur
        for s in range(1, w):
            acc = acc + ext[:, HIST - s:HIST - s + ls, lanes]
        cnt = jnp.minimum(w, pos + 1).astype(f32)
        d = (acc / cnt - cur).reshape(n, POOL_GC)
        yg = _dot(d.astype(bf16), wp_ref[g]) * ps_ref[:, lanes]
        po_ref[:, lanes] = yg.astype(bf16)
    np_ref[...] = ext[:, ls + 1:ls + HIST, :]


def _inproj_sample(x, mod3, g_attn, w_in, gqk, bd, w_pool, pool_scale, state, pos0, bt=64):
    B, ls, D = x.shape
    n = bt * ls
    full = lambda shape: pl.BlockSpec(shape, lambda i: (0,) * len(shape))
    return pl.pallas_call(
        functools.partial(_inproj_sample_body, bt=bt, ls=ls, pos0=pos0),
        grid=(B // bt,),
        in_specs=[
            pl.BlockSpec((bt, ls, D), lambda i: (i, 0, 0)),
            pl.BlockSpec((bt, 1, D), lambda i: (i, 0, 0)),
            pl.BlockSpec((bt, 1, D), lambda i: (i, 0, 1)),
            full((1, D)),
            full((D, QKV_W)),
            full((1, QK_W)),
            full((QK_W, QK_W)),
            full((4, POOL_GC, POOL_GC)),
            full((1, POOL_W)),
            pl.BlockSpec((bt, POOL_BUF, POOL_W), lambda i: (i, 0, 0)),
        ],
        out_specs=[
            pl.BlockSpec((n, ATTN_W), lambda i: (i, 0)),
            pl.BlockSpec((n, KV_W), lambda i: (i, 0)),
            pl.BlockSpec((n, KV_W), lambda i: (i, 0)),
            pl.BlockSpec((n, POOL_W), lambda i: (i, 0)),
            pl.BlockSpec((bt, POOL_BUF, POOL_W), lambda i: (i, 0, 0)),
        ],
        out_shape=[
            jax.ShapeDtypeStruct((B * ls, ATTN_W), bf16),
            jax.ShapeDtypeStruct((B * ls, KV_W), f32),
            jax.ShapeDtypeStruct((B * ls, KV_W), f32),
            jax.ShapeDtypeStruct((B * ls, POOL_W), bf16),
            jax.ShapeDtypeStruct((B, POOL_BUF, POOL_W), f32),
        ],
        scratch_shapes=[pltpu.VMEM((bt, HIST + ls, POOL_W), f32)],
        name="inproj_sample",
    )(x, mod3, mod3, g_attn, w_in, gqk, bd, w_pool, pool_scale, state)


def _softmax_sink(parts, sink):
    m = sink
    for s in parts:
        m = jnp.maximum(m, jnp.max(s, axis=-1, keepdims=True))
    ps = [jnp.exp(s - m) for s in parts]
    denom = jnp.exp(sink - m)
    for p in ps:
        denom = denom + jnp.sum(p, axis=-1, keepdims=True)
    inv = 1.0 / denom
    return [(p * inv).astype(bf16) for p in ps]


def _attn_prompt_body(sinks_ref, q_ref, kp_ref, kc_ref, vp_ref, vc_ref, bias_ref, mask_ref, o_ref):
    kk = jnp.concatenate([kp_ref[0], kc_ref[0]], axis=0)
    vv = jnp.concatenate([vp_ref[0], vc_ref[0]], axis=0)
    k0 = jnp.concatenate([kk, kk], axis=1)
    v0 = jnp.concatenate([vv, vv], axis=1)
    k1 = pltpu.roll(k0, HEAD_DIM, 1)
    v1 = pltpu.roll(v0, HEAD_DIM, 1)
    lane_group = lax.broadcasted_iota(i32, k0.shape, 1) // HEAD_DIM
    valid = mask_ref[0] > 0.5
    nk = 2 * WINDOW
    qw = GQA * HEAD_DIM
    for kv in range(N_KV_HEADS):
        def blockdiag(t0, t1):
            return jnp.concatenate(
                [jnp.where(lane_group == g, t0 if g % 2 == kv else t1, jnp.zeros_like(t0)) for g in range(GQA)],
                axis=0)
        s = jnp.where(valid, _dot_t(q_ref[0, :, kv * qw:(kv + 1) * qw], blockdiag(k0, k1)) + bias_ref[kv], NEG_INF)
        ps = [_softmax_sink([s[:, g * nk:(g + 1) * nk]], sinks_ref[kv * GQA + g])[0] for g in range(GQA)]
        o = _dot(jnp.concatenate(ps, axis=1), blockdiag(v0, v1))
        o_ref[0, :, kv * qw:(kv + 1) * qw] = o.astype(bf16)


def _attn_prompt(q, k, v, bias, sinks):
    B, L, _ = q.shape
    nb = L // WINDOW
    cur = lambda b, j: (b, j, 0)
    prev = lambda b, j: (b, jnp.maximum(j - 1, 0), 0)
    qi = np.arange(WINDOW)[:, None]
    kc = np.arange(2 * WINDOW)[None, :]
    own = (kc >= WINDOW) & (kc - WINDOW <= qi)
    prv = (kc < WINDOW) & (kc > qi)
    mask = np.stack([np.tile(own, (1, GQA)), np.tile(own | prv, (1, GQA))]).astype(np.float32)
    return pl.pallas_call(
        _attn_prompt_body,
        grid=(B, nb),
        in_specs=[
            pl.BlockSpec(memory_space=pltpu.SMEM),
            pl.BlockSpec((1, WINDOW, ATTN_W), cur),
            pl.BlockSpec((1, WINDOW, KV_W), prev),
            pl.BlockSpec((1, WINDOW, KV_W), cur),
            pl.BlockSpec((1, WINDOW, KV_W), prev),
            pl.BlockSpec((1, WINDOW, KV_W), cur),
            pl.BlockSpec((N_KV_HEADS, WINDOW, GQA * 2 * WINDOW), lambda b, j: (0, 0, 0)),
            pl.BlockSpec((1, WINDOW, GQA * 2 * WINDOW), lambda b, j: (jnp.minimum(j, 1), 0, 0)),
        ],
        out_specs=pl.BlockSpec((1, WINDOW, ATTN_W), cur),
        out_shape=jax.ShapeDtypeStruct((B, L, ATTN_W), bf16),
        name="attn_prompt",
    )(sinks, q, k, k, v, v, bias, jnp.asarray(mask))


def _attn_sample_body(q_ref, kb_ref, vb_ref, kn_ref, vn_ref, bb_ref, bn_ref, sink_ref,
                      o_ref, nk_ref, nv_ref, *, bb, ls):
    W = kb_ref.shape[1]
    rows = GQA * ls
    qi = lax.broadcasted_iota(i32, (rows, W), 0) % ls
    kj = lax.broadcasted_iota(i32, (rows, W), 1)
    valid_buf = kj > qi
    qi2 = lax.broadcasted_iota(i32, (rows, ls), 0) % ls
    kj2 = lax.broadcasted_iota(i32, (rows, ls), 1)
    valid_new = kj2 <= qi2

    def one(b, carry):
        qb = q_ref[b]
        kbuf = kb_ref[b]
        vbuf = vb_ref[b]
        knew = kn_ref[b]
        vnew = vn_ref[b]
        outs = []
        for kv in range(N_KV_HEADS):
            ks = slice(kv * HEAD_DIM, (kv + 1) * HEAD_DIM)
            qg = jnp.concatenate(
                [qb[:, (kv * GQA + g) * HEAD_DIM:(kv * GQA + g + 1) * HEAD_DIM] for g in range(GQA)], axis=0)
            s_buf = jnp.where(valid_buf, _dot_t(qg, kbuf[:, ks].astype(bf16)) + bb_ref[kv], NEG_INF)
            s_new = jnp.where(valid_new, _dot_t(qg, knew[:, ks].astype(bf16)) + bn_ref[kv], NEG_INF)
            p_buf, p_new = _softmax_sink([s_buf, s_new], sink_ref[kv])
            o = _dot(p_buf, vbuf[:, ks].astype(bf16)) + _dot(p_new, vnew[:, ks].astype(bf16))
            outs.extend([o[g * ls:(g + 1) * ls] for g in range(GQA)])
        o_ref[b] = jnp.concatenate(outs, axis=-1).astype(bf16)
        nk_ref[b, 0:W - ls, :] = kbuf[ls:, :]
        nk_ref[b, W - ls:W, :] = knew
        nv_ref[b, 0:W - ls, :] = vbuf[ls:, :]
        nv_ref[b, W - ls:W, :] = vnew
        return carry

    lax.fori_loop(0, bb, one, 0)


def _attn_sample(q, k_buf, v_buf, k_new, v_new, bias_buf, bias_new, sink_col, bb=16):
    B, ls, _ = q.shape
    W = k_buf.shape[1]
    rows = GQA * ls
    blk = lambda shape: pl.BlockSpec(shape, lambda i: (i, 0, 0))
    full = lambda shape: pl.BlockSpec(shape, lambda i: (0, 0, 0))
    return pl.pallas_call(
        functools.partial(_attn_sample_body, bb=bb, ls=ls),
        grid=(B // bb,),
        in_specs=[
            blk((bb, ls, ATTN_W)),
            blk((bb, W, KV_W)),
            blk((bb, W, KV_W)),
            blk((bb, ls, KV_W)),
            blk((bb, ls, KV_W)),
            full((N_KV_HEADS, rows, W)),
            full((N_KV_HEADS, rows, ls)),
            full((N_KV_HEADS, rows, 1)),
        ],
        out_specs=[blk((bb, ls, ATTN_W)), blk((bb, W, KV_W)), blk((bb, W, KV_W))],
        out_shape=[
            jax.ShapeDtypeStruct((B, ls, ATTN_W), bf16),
            jax.ShapeDtypeStruct((B, W, KV_W), f32),
            jax.ShapeDtypeStruct((B, W, KV_W), f32),
        ],
        name="attn_sample",
    )(q, k_buf, v_buf, k_new, v_new, bias_buf, bias_new, sink_col)


def _outproj_core(po, at, x, gt, sc, sh, g_ref, wo_ref, wrh_ref, wrl_ref):
    mixo = _dot(po, wo_ref[0:POOL_W, :]) + _dot(at, wo_ref[POOL_W:, :])
    x1 = x + gt * mixo.reshape(x.shape)
    h2 = _mod_norm(x1, g_ref[...].reshape((1,) * (x.ndim - 1) + (D_MODEL,)), sc, sh).reshape(-1, D_MODEL)
    h_hi, h_lo = _split_bf16(h2)
    wh = wrh_ref[...]
    logits = _dot_t(wh, h_hi) + (_dot_t(wh, h_lo) + _dot_t(wrl_ref[...], h_hi))
    return x1, _pack_pairs(h2), logits


def _outproj_prompt_body(po_ref, at_ref, x_ref, gt_ref, sc_ref, sh_ref, g_ref, wo_ref, wrh_ref, wrl_ref,
                         x1_ref, h2_ref, lg_ref):
    x1, h2p, logits = _outproj_core(po_ref[0], at_ref[0], x_ref[0], gt_ref[0], sc_ref[0], sh_ref[0],
                                    g_ref, wo_ref, wrh_ref, wrl_ref)
    x1_ref[...] = x1
    h2_ref[...] = h2p
    lg_ref[...] = logits


def _outproj_sample_body(po_ref, at_ref, x_ref, gt_ref, sc_ref, sh_ref, g_ref, wo_ref, wrh_ref, wrl_ref,
                         x1_ref, h2_ref, lg_ref):
    x1, h2p, logits = _outproj_core(po_ref[...], at_ref[...], x_ref[...], gt_ref[...], sc_ref[...], sh_ref[...],
                                    g_ref, wo_ref, wrh_ref, wrl_ref)
    x1_ref[...] = x1.reshape(-1, D_MODEL)
    h2_ref[...] = h2p
    lg_ref[...] = logits


def _outproj_prompt(po, at, x, mod3, g_ffn, w_out, wr_hi, wr_lo, tm=512):
    B, L, D = x.shape
    nt = L // tm
    n_tok = B * L
    full = lambda shape: pl.BlockSpec(shape, lambda b, j: (0,) * len(shape))
    modspec = lambda c: pl.BlockSpec((1, 1, D), lambda b, j: (b, 0, c))
    return pl.pallas_call(
        _outproj_prompt_body,
        grid=(B, nt),
        in_specs=[
            pl.BlockSpec((1, tm, POOL_W), lambda b, j: (b, j, 0)),
            pl.BlockSpec((1, tm, ATTN_W), lambda b, j: (b, j, 0)),
            pl.BlockSpec((1, tm, D), lambda b, j: (b, j, 0)),
            modspec(2), modspec(4), modspec(3),
            full((1, D)), full((D, D)), full((N_EXPERTS, D)), full((N_EXPERTS, D)),
        ],
        out_specs=[
            pl.BlockSpec((tm, D), lambda b, j: (b * nt + j, 0)),
            pl.BlockSpec((tm, D // 2), lambda b, j: (b * nt + j, 0)),
            pl.BlockSpec((N_EXPERTS, tm), lambda b, j: (0, b * nt + j)),
        ],
        out_shape=[
            jax.ShapeDtypeStruct((n_tok, D), f32),
            jax.ShapeDtypeStruct((n_tok, D // 2), i32),
            jax.ShapeDtypeStruct((N_EXPERTS, n_tok), f32),
        ],
        name="outproj_prompt",
    )(po, at, x, mod3, mod3, mod3, g_ffn, w_out, wr_hi, wr_lo)


def _outproj_sample(po, at, x, mod3, g_ffn, w_out, wr_hi, wr_lo, bt=64):
    B, ls, D = x.shape
    n = bt * ls
    full = lambda shape: pl.BlockSpec(shape, lambda i: (0,) * len(shape))
    modspec = lambda c: pl.BlockSpec((bt, 1, D), lambda i: (i, 0, c))
    return pl.pallas_call(
        _outproj_sample_body,
        grid=(B // bt,),
        in_specs=[
            pl.BlockSpec((n, POOL_W), lambda i: (i, 0)),
            pl.BlockSpec((n, ATTN_W), lambda i: (i, 0)),
            pl.BlockSpec((bt, ls, D), lambda i: (i, 0, 0)),
            modspec(2), modspec(4), modspec(3),
            full((1, D)), full((D, D)), full((N_EXPERTS, D)), full((N_EXPERTS, D)),
        ],
        out_specs=[
            pl.BlockSpec((n, D), lambda i: (i, 0)),
            pl.BlockSpec((n, D // 2), lambda i: (i, 0)),
            pl.BlockSpec((N_EXPERTS, n), lambda i: (0, i)),
        ],
        out_shape=[
            jax.ShapeDtypeStruct((B * ls, D), f32),
            jax.ShapeDtypeStruct((B * ls, D // 2), i32),
            jax.ShapeDtypeStruct((N_EXPERTS, B * ls), f32),
        ],
        name="outproj_sample",
    )(po, at, x, mod3, mod3, mod3, g_ffn, w_out, wr_hi, wr_lo)


def _route_body(lg_ref, rb_ref, tri_ref, idx_ref, rank_ref, gate_ref, cnt_ref, carry, *, tr, nsteps):
    step = pl.program_id(0)

    @pl.when(step == 0)
    def _():
        carry[...] = jnp.zeros(carry.shape, f32)

    s = jax.nn.sigmoid(lg_ref[...])
    sb = s + rb_ref[...]
    e_iota = lax.broadcasted_iota(i32, (N_EXPERTS, tr), 0)
    g_iota = lax.broadcasted_iota(i32, (GROUP_SIZE, tr), 0)

    gscore = []
    for g in range(N_EXPERT_GROUPS):
        v = sb[g * GROUP_SIZE:(g + 1) * GROUP_SIZE]
        m1 = jnp.max(v, axis=0, keepdims=True)
        i1 = jnp.min(jnp.where(v == m1, g_iota, GROUP_SIZE), axis=0, keepdims=True)
        m2 = jnp.max(jnp.where(g_iota == i1, -jnp.inf, v), axis=0, keepdims=True)
        gscore.append(m1 + m2)
    parts = []
    for g in range(N_EXPERT_GROUPS):
        beaten = jnp.zeros((1, tr), i32)
        for g2 in range(N_EXPERT_GROUPS):
            if g2 == g:
                continue
            ahead = gscore[g2] > gscore[g]
            if g2 < g:
                ahead = ahead | (gscore[g2] == gscore[g])
            beaten = beaten + ahead.astype(i32)
        keep = beaten < TOPK_GROUPS
        parts.append(jnp.where(keep, sb[g * GROUP_SIZE:(g + 1) * GROUP_SIZE], NEG_INF))
    cur = jnp.concatenate(parts, axis=0)

    sel = jnp.zeros((N_EXPERTS, tr), f32)
    idxs, svals = [], []
    for _ in range(TOP_K):
        m = jnp.max(cur, axis=0, keepdims=True)
        ik = jnp.min(jnp.where(cur == m, e_iota, N_EXPERTS), axis=0, keepdims=True)
        hit = e_iota == ik
        svals.append(jnp.sum(jnp.where(hit, s, 0.0), axis=0, keepdims=True))
        cur = jnp.where(hit, -jnp.inf, cur)
        sel = jnp.where(hit, 1.0, sel)
        idxs.append(ik)
    ssum = svals[0]
    for sv in svals[1:]:
        ssum = ssum + sv
    gate_ref[...] = jnp.concatenate([sv / ssum * ROUTED_SCALE for sv in svals], axis=0)
    idx_ref[...] = jnp.concatenate(idxs, axis=0)

    before = carry[...] + _dot(sel.astype(bf16), tri_ref[...])
    ranks = [jnp.sum(jnp.where(e_iota == ik, before, 0.0), axis=0, keepdims=True) for ik in idxs]
    rank_ref[...] = jnp.concatenate(ranks, axis=0).astype(i32)
    carry[...] = carry[...] + jnp.sum(sel, axis=1, keepdims=True)

    @pl.when(step == nsteps - 1)
    def _():
        cnt_ref[...] = carry[...]


def _route(logits_t, router_bias, tr=512):
    E, T = logits_t.shape
    nsteps = T // tr
    tri = jnp.asarray(np.triu(np.ones((tr, tr), np.float32), 1), bf16)
    return pl.pallas_call(
        functools.partial(_route_body, tr=tr, nsteps=nsteps),
        grid=(nsteps,),
        in_specs=[
            pl.BlockSpec((E, tr), lambda i: (0, i)),
            pl.BlockSpec((E, 1), lambda i: (0, 0)),
            pl.BlockSpec((tr, tr), lambda i: (0, 0)),
        ],
        out_specs=[
            pl.BlockSpec((TOP_K, tr), lambda i: (0, i)),
            pl.BlockSpec((TOP_K, tr), lambda i: (0, i)),
            pl.BlockSpec((TOP_K, tr), lambda i: (0, i)),
            pl.BlockSpec((E, 1), lambda i: (0, 0)),
        ],
        out_shape=[
            jax.ShapeDtypeStruct((TOP_K, T), i32),
            jax.ShapeDtypeStruct((TOP_K, T), i32),
            jax.ShapeDtypeStruct((TOP_K, T), f32),
            jax.ShapeDtypeStruct((E, 1), f32),
        ],
        scratch_shapes=[pltpu.VMEM((E, 1), f32)],
        compiler_params=pltpu.CompilerParams(dimension_semantics=("arbitrary",)),
        name="route",
    )(logits_t, router_bias.reshape(E, 1), tri)


def _dest_body(idx_ref, rank_ref, ps_ref, dest_ref, *, tr):
    e_iota = lax.broadcasted_iota(i32, (N_EXPERTS, tr), 0)
    start = ps_ref[...]
    rows = []
    for k in range(TOP_K):
        hit = e_iota == idx_ref[k:k + 1, :]
        rows.append(jnp.sum(jnp.where(hit, start, 0.0), axis=0, keepdims=True))
    dest_ref[...] = jnp.concatenate(rows, axis=0).astype(i32) + rank_ref[...]


def _dest_rows(idx, rank, pad_start, tr=512):
    K, T = idx.shape
    blk = pl.BlockSpec((K, tr), lambda i: (0, i))
    return pl.pallas_call(
        functools.partial(_dest_body, tr=tr),
        grid=(T // tr,),
        in_specs=[blk, blk, pl.BlockSpec((N_EXPERTS, 1), lambda i: (0, 0))],
        out_specs=blk,
        out_shape=jax.ShapeDtypeStruct((K, T), i32),
        name="dest_rows",
    )(idx, rank, pad_start.astype(f32).reshape(N_EXPERTS, 1))


def _sc_mesh():
    return plsc.VectorSubcoreMesh(core_axis_name="c", subcore_axis_name="s")


def _sc_worker_id():
    return lax.axis_index("s") * 2 + lax.axis_index("c")


def _dispatch(h2_a, h2_b, dest, gate, n_rows, chunk=32):
    ta, Dw = h2_a.shape
    T = ta + h2_b.shape[0]
    per_worker = T // SC_WORKERS
    nchunk = per_worker // chunk
    assert per_worker * SC_WORKERS == T and nchunk * chunk == per_worker and ta % chunk == 0

    @functools.partial(
        pl.kernel, mesh=_sc_mesh(),
        out_type=[jax.ShapeDtypeStruct((n_rows, Dw), i32), jax.ShapeDtypeStruct((n_rows, GATE_ROW), f32)],
        scratch_types=[pltpu.VMEM((chunk,), i32), pltpu.VMEM((chunk, Dw), i32),
                       pltpu.VMEM((chunk,), f32), pltpu.VMEM((chunk, GATE_ROW), f32)],
        compiler_params=pltpu.CompilerParams(needs_layout_passes=False),
        name="moe_dispatch",
    )
    def body(ha_hbm, hb_hbm, dest_hbm, gate_hbm, xs_hbm, gs_hbm, idx_v, rows_v, gate_v, grow_v):
        base = _sc_worker_id() * per_worker

        @pl.loop(0, nchunk)
        def _(ci):
            t0 = base + ci * chunk

            @pl.when(t0 < ta)
            def _():
                pltpu.sync_copy(ha_hbm.at[pl.ds(t0, chunk)], rows_v)

            @pl.when(t0 >= ta)
            def _():
                pltpu.sync_copy(hb_hbm.at[pl.ds(t0 - ta, chunk)], rows_v)

            for k in range(TOP_K):
                pltpu.sync_copy(dest_hbm.at[k, pl.ds(t0, chunk)], idx_v)
                pltpu.sync_copy(gate_hbm.at[k, pl.ds(t0, chunk)], gate_v)

                @pl.loop(0, chunk)
                def _(t):
                    g = plsc.load_gather(gate_v, [jnp.zeros((SC_LANES,), i32) + t])
                    for j in range(GATE_ROW // SC_LANES):
                        grow_v[t, pl.ds(j * SC_LANES, SC_LANES)] = g

                pltpu.sync_copy(rows_v, xs_hbm.at[idx_v])
                pltpu.sync_copy(grow_v, gs_hbm.at[idx_v])

    return body(h2_a, h2_b, dest, gate)


def _combine(ys, dest, chunk=8):
    T = dest.shape[1]
    Dw = ys.shape[1]
    per_worker = T // SC_WORKERS
    nchunk = per_worker // chunk
    assert per_worker * SC_WORKERS == T and nchunk * chunk == per_worker and nchunk % 2 == 0

    @functools.partial(
        pl.kernel, mesh=_sc_mesh(),
        out_type=jax.ShapeDtypeStruct((T, 2 * Dw), f32),
        scratch_types=[
            pltpu.VMEM((TOP_K * per_worker,), i32),
            pltpu.VMEM((2, TOP_K, chunk, Dw), i32),
            pltpu.VMEM((chunk, 2 * Dw), f32),
            pltpu.SemaphoreType.DMA((2,)),
        ],
        compiler_params=pltpu.CompilerParams(needs_layout_passes=False),
        name="moe_combine",
    )
    def body(ys_hbm, dest_hbm, out_hbm, idx_v, buf, out_v, sems):
        base = _sc_worker_id() * per_worker
        pltpu.sync_copy(dest_hbm.at[pl.ds(_sc_worker_id() * (TOP_K * per_worker), TOP_K * per_worker)], idx_v)

        def gather(ci, slot):
            return [pltpu.make_async_copy(ys_hbm.at[idx_v.at[pl.ds(k * per_worker + ci * chunk, chunk)]],
                                          buf.at[slot, k], sems.at[slot]) for k in range(TOP_K)]

        for cp in gather(0, 0):
            cp.start()

        @pl.loop(0, nchunk, step=2)
        def _(c0):
            for slot in range(2):
                ci = c0 + slot

                @pl.when(ci + 1 < nchunk)
                def _():
                    for cp in gather(ci + 1, 1 - slot):
                        cp.start()

                for cp in gather(ci, slot):
                    cp.wait()

                @pl.loop(0, chunk)
                def _(t):
                    @pl.loop(0, Dw // SC_LANES)
                    def _(j):
                        sl = pl.ds(j * SC_LANES, SC_LANES)
                        w = buf[slot, 0, t, sl]
                        hi = plsc.bitcast(w & HI_MASK, f32)
                        lo = plsc.bitcast(lax.shift_left(w, 16), f32)
                        for k in range(1, TOP_K):
                            w = buf[slot, k, t, sl]
                            hi = hi + plsc.bitcast(w & HI_MASK, f32)
                            lo = lo + plsc.bitcast(lax.shift_left(w, 16), f32)
                        out_v[t, sl] = hi
                        out_v[t, pl.ds(Dw + j * SC_LANES, SC_LANES)] = lo

                pltpu.sync_copy(out_v, out_hbm.at[pl.ds(base + ci * chunk, chunk)])

    dest_w = dest.reshape(TOP_K, SC_WORKERS, per_worker).transpose(1, 0, 2).reshape(-1)
    return body(ys, dest_w)


def _gmm_body(blk_e_ref, blk_rows_ref, nv_ref, xs_hbm, gs_hbm, w1_hbm, w3_hbm, w2_hbm, ys_hbm,
              xbuf, gbuf, ybuf, w1f, w3f, w2f, w13_b, w2_b, xsem, gsem, ysem, wsem):
    nv = nv_ref[0]
    nb = blk_e_ref.shape[0]
    half = D_MODEL // 2
    RB = EXPERT_BLOCK

    def expert_of(blk):
        return blk_e_ref[jnp.minimum(blk, nb - 1)]

    def next_expert_block(blk):
        e0 = expert_of(blk)
        return lax.while_loop(lambda i: (i < nv) & (expert_of(i) == e0), lambda i: i + 1, blk + 1)

    def start_weights(blk, ordinal):
        @pl.when(blk < nv)
        def _():
            for cp in weight_copies(expert_of(blk), lax.rem(ordinal, W_RING)):
                cp.start()

    def row_copies(b, slot):
        r0 = pl.multiple_of(b * RB, RB)
        return (pltpu.make_async_copy(xs_hbm.at[pl.ds(r0, RB)], xbuf.at[slot], xsem.at[slot]),
                pltpu.make_async_copy(gs_hbm.at[pl.ds(r0, RB)], gbuf.at[slot], gsem.at[slot]))

    def out_copy(b, slot):
        r0 = pl.multiple_of(b * RB, RB)
        return pltpu.make_async_copy(ybuf.at[slot], ys_hbm.at[pl.ds(r0, RB)], ysem.at[slot])

    def weight_copies(e, ws):
        return (pltpu.make_async_copy(w1_hbm.at[e], w1f.at[ws], wsem.at[ws, 0]),
                pltpu.make_async_copy(w3_hbm.at[e], w3f.at[ws], wsem.at[ws, 1]),
                pltpu.make_async_copy(w2_hbm.at[e], w2f.at[ws], wsem.at[ws, 2]))

    start_weights(0, 0)
    start_weights(next_expert_block(0), 1)
    for i in range(ROW_RING - 1):
        @pl.when(i < nv)
        def _():
            for cp in row_copies(i, i):
                cp.start()

    def step(b, ordinal_prev):
        slot = lax.rem(b, ROW_RING)
        e = blk_e_ref[b]
        first = (b == 0) | (e != blk_e_ref[jnp.maximum(b - 1, 0)])
        ordinal = jnp.where(first & (b > 0), ordinal_prev + 1, ordinal_prev)

        ahead = b + ROW_RING - 1

        @pl.when(ahead < nv)
        def _():
            for cp in row_copies(ahead, lax.rem(ahead, ROW_RING)):
                cp.start()

        @pl.when(first)
        def _():
            ws = lax.rem(ordinal, W_RING)
            for cp in weight_copies(e, ws):
                cp.wait()
            w13_b[:, 0:EXPERT_FF] = w1f[ws].astype(bf16)
            w13_b[:, EXPERT_FF:] = w3f[ws].astype(bf16)
            w2_b[...] = w2f[ws].astype(bf16)
            start_weights(next_expert_block(next_expert_block(b)), ordinal + 2)

        for cp in row_copies(b, slot):
            cp.wait()
        valid = lax.broadcasted_iota(i32, (RB, 1), 0) < blk_rows_ref[b]
        x_hi, x_lo = _unpack_pairs(jnp.where(valid, xbuf[slot], 0))
        ac = _dot(x_hi, w13_b[0:half, :]) + _dot(x_lo, w13_b[half:, :])
        a = ac[:, :EXPERT_FF]
        c = ac[:, EXPERT_FF:]
        hmid = (a * jax.nn.sigmoid(a)) * c
        g = jnp.where(valid, gbuf[slot][:, 0:1], 0.0)
        y = _pack_pairs(_dot(hmid.astype(bf16), w2_b[...]) * g)

        @pl.when(b >= ROW_RING)
        def _():
            out_copy(b - ROW_RING, slot).wait()

        ybuf[slot] = y
        out_copy(b, slot).start()
        return ordinal

    lax.fori_loop(0, nv, step, 0)

    for i in range(1, ROW_RING + 1):
        @pl.when(nv >= i)
        def _():
            out_copy(nv - i, lax.rem(nv - i, ROW_RING)).wait()


def _gmm(xs, gs, w1, w3, w2, blk_e, blk_rows, n_valid):
    n_rows, Dw = xs.shape
    D = 2 * Dw
    RB = EXPERT_BLOCK
    hbm = pl.BlockSpec(memory_space=pl.ANY)
    return pl.pallas_call(
        _gmm_body,
        grid_spec=pltpu.PrefetchScalarGridSpec(
            num_scalar_prefetch=3,
            grid=(1,),
            in_specs=[hbm, hbm, hbm, hbm, hbm],
            out_specs=hbm,
            scratch_shapes=[
                pltpu.VMEM((ROW_RING, RB, Dw), i32), pltpu.VMEM((ROW_RING, RB, GATE_ROW), f32),
                pltpu.VMEM((ROW_RING, RB, Dw), i32),
                pltpu.VMEM((W_RING, D, EXPERT_FF), f32), pltpu.VMEM((W_RING, D, EXPERT_FF), f32),
                pltpu.VMEM((W_RING, EXPERT_FF, D), f32),
                pltpu.VMEM((D, 2 * EXPERT_FF), bf16), pltpu.VMEM((EXPERT_FF, D), bf16),
                pltpu.SemaphoreType.DMA((ROW_RING,)), pltpu.SemaphoreType.DMA((ROW_RING,)),
                pltpu.SemaphoreType.DMA((ROW_RING,)), pltpu.SemaphoreType.DMA((W_RING, 3)),
            ],
        ),
        out_shape=jax.ShapeDtypeStruct((n_rows, Dw), i32),
        compiler_params=pltpu.CompilerParams(dimension_semantics=("arbitrary",)),
        name="moe_gmm",
    )(blk_e, blk_rows, n_valid, xs, gs, w1, w3, w2)


def _final_core(x1, h2p, comb, gt, ws1_ref, ws3_ref, ws2_ref):
    half = D_MODEL // 2
    h_hi, h_lo = _unpack_pairs(h2p)
    a = _dot(h_hi, ws1_ref[0:half, :]) + _dot(h_lo, ws1_ref[half:, :])
    c = _dot(h_hi, ws3_ref[0:half, :]) + _dot(h_lo, ws3_ref[half:, :])
    shared = _dot(((a * jax.nn.sigmoid(a)) * c).astype(bf16), ws2_ref[...])
    return x1, comb + shared, gt


def _final_prompt_body(x1_ref, h2_ref, cb_ref, gt_ref, ws1_ref, ws3_ref, ws2_ref, y_ref):
    x1, ffn, gt = _final_core(x1_ref[...], h2_ref[...], cb_ref[...], gt_ref[0], ws1_ref, ws3_ref, ws2_ref)
    y_ref[0] = x1 + gt * ffn


def _final_sample_body(x1_ref, h2_ref, cb_ref, gt_ref, ws1_ref, ws3_ref, ws2_ref, y_ref):
    x1, ffn, gt = _final_core(x1_ref[...], h2_ref[...], cb_ref[...], gt_ref[...], ws1_ref, ws3_ref, ws2_ref)
    shp = y_ref.shape
    y_ref[...] = x1.reshape(shp) + gt * ffn.reshape(shp)


def _final_prompt(x1, h2, comb, mod3, ws1, ws3, ws2, B, L, tm=512):
    D = D_MODEL
    nt = L // tm
    full = lambda shape: pl.BlockSpec(shape, lambda b, j: (0,) * len(shape))
    rows = pl.BlockSpec((tm, D), lambda b, j: (b * nt + j, 0))
    words = pl.BlockSpec((tm, D // 2), lambda b, j: (b * nt + j, 0))
    return pl.pallas_call(
        _final_prompt_body,
        grid=(B, nt),
        in_specs=[rows, words, rows, pl.BlockSpec((1, 1, D), lambda b, j: (b, 0, 5)),
                  full((D, EXPERT_FF)), full((D, EXPERT_FF)), full((EXPERT_FF, D))],
        out_specs=pl.BlockSpec((1, tm, D), lambda b, j: (b, j, 0)),
        out_shape=jax.ShapeDtypeStruct((B, L, D), f32),
        name="final_prompt",
    )(x1, h2, comb, mod3, ws1, ws3, ws2)


def _final_sample(x1, h2, comb, mod3, ws1, ws3, ws2, B, ls, row0, bt=64):
    D = D_MODEL
    n = bt * ls
    blk0 = row0 // n
    full = lambda shape: pl.BlockSpec(shape, lambda i: (0,) * len(shape))
    rows = pl.BlockSpec((n, D), lambda i: (i, 0))
    words = pl.BlockSpec((n, D // 2), lambda i: (i, 0))
    comb_rows = pl.BlockSpec((n, D), lambda i: (blk0 + i, 0))
    return pl.pallas_call(
        _final_sample_body,
        grid=(B // bt,),
        in_specs=[rows, words, comb_rows, pl.BlockSpec((bt, 1, D), lambda i: (i, 0, 5)),
                  full((D, EXPERT_FF)), full((D, EXPERT_FF)), full((EXPERT_FF, D))],
        out_specs=pl.BlockSpec((bt, ls, D), lambda i: (i, 0, 0)),
        out_shape=jax.ShapeDtypeStruct((B, ls, D), f32),
        name="final_sample",
    )(x1, h2, comb, mod3, ws1, ws3, ws2)


def kernel(x_prompt, x_sample, state_pool, cache_swa_k, cache_swa_v, c_prompt, c_sample, w_ada, b_ada,
           g_attn_norm, w_in, g_q, g_k, w_pool, pool_scale, w_out, attn_sinks, rel_bias, g_ffn_norm,
           w_router, router_bias, w1, w3, w2, ws1, ws3, ws2):
    B, L, D = x_prompt.shape
    BS, LS, _ = x_sample.shape
    depth = w_ada.shape[0]
    assert depth == 1
    W = cache_swa_k.shape[2]
    tp, ts = B * L, BS * LS
    T = tp + ts
    n_rows = (T * TOP_K // EXPERT_BLOCK + N_EXPERTS) * EXPERT_BLOCK
    nb = n_rows // EXPERT_BLOCK

    g_attn = g_attn_norm[0].reshape(1, D)
    g_ffn = g_ffn_norm[0].reshape(1, D)
    w_in_b = w_in[0].astype(bf16)
    w_out_b = w_out[0].astype(bf16)
    w_pool_b = w_pool[0].astype(bf16)
    ps = pool_scale[0].reshape(1, POOL_W)
    gqk = jnp.concatenate([jnp.tile(g_q[0], N_HEADS), jnp.tile(g_k[0], N_KV_HEADS)]).reshape(1, QK_W)
    head_of = np.arange(QK_W) // HEAD_DIM
    bd = jnp.asarray((head_of[:, None] == head_of[None, :]).astype(np.float32), bf16)
    wr_t = w_router[0].T
    wr_hi = wr_t.astype(bf16)
    wr_lo = (wr_t - wr_hi.astype(f32)).astype(bf16)
    ws1_b, ws3_b, ws2_b = ws1[0].astype(bf16), ws3[0].astype(bf16), ws2[0].astype(bf16)
    sinks = attn_sinks[0]

    mod = _ada(jnp.concatenate([c_prompt, c_sample], axis=0), w_ada[0], b_ada[0])
    mod_p = mod[:B].reshape(B, 1, 6 * D)
    mod_s = mod[B:].reshape(BS, 1, 6 * D)

    dist_p = np.arange(WINDOW)[:, None] + WINDOW - np.arange(2 * WINDOW)[None, :]
    bias_p = _relbias(rel_bias, dist_p)
    bias_p = bias_p.reshape(N_KV_HEADS, GQA, WINDOW, 2 * WINDOW).transpose(0, 2, 1, 3).reshape(
        N_KV_HEADS, WINDOW, GQA * 2 * WINDOW)
    dist_s = np.arange(LS)[:, None] + W - np.arange(W + LS)[None, :]
    bias_s = _relbias(rel_bias, dist_s)
    bias_s_buf = bias_s[:, :, :W].reshape(N_KV_HEADS, GQA * LS, W)
    bias_s_new = bias_s[:, :, W:].reshape(N_KV_HEADS, GQA * LS, LS)
    sink_col = jnp.repeat(sinks, LS).reshape(N_KV_HEADS, GQA * LS, 1)

    q_p, k_p, v_p, po_p, new_pool_p, kc_p, vc_p = _inproj_prompt(
        x_prompt, mod_p, g_attn, w_in_b, gqk, bd, w_pool_b, ps)
    q_s, k_s, v_s, po_s, new_pool_s = _inproj_sample(
        x_sample, mod_s, g_attn, w_in_b, gqk, bd, w_pool_b, ps, state_pool[0], PAST_LEN)
    at_p = _attn_prompt(q_p, k_p, v_p, bias_p, sinks)
    at_s, nk_s, nv_s = _attn_sample(
        q_s.reshape(BS, LS, ATTN_W), cache_swa_k[0].reshape(BS, W, KV_W), cache_swa_v[0].reshape(BS, W, KV_W),
        k_s.reshape(BS, LS, KV_W), v_s.reshape(BS, LS, KV_W), bias_s_buf, bias_s_new, sink_col)

    x1_p, h2_p, lg_p = _outproj_prompt(po_p, at_p, x_prompt, mod_p, g_ffn, w_out_b, wr_hi, wr_lo)
    x1_s, h2_s, lg_s = _outproj_sample(po_s, at_s.reshape(ts, ATTN_W), x_sample, mod_s, g_ffn, w_out_b,
                                       wr_hi, wr_lo)

    idx, rank, gate, counts = _route(jnp.concatenate([lg_p, lg_s], axis=1), router_bias[0])
    counts = counts.reshape(N_EXPERTS).astype(i32)
    padded = (counts + EXPERT_BLOCK - 1) // EXPERT_BLOCK * EXPERT_BLOCK
    pad_end = jnp.cumsum(padded)
    pad_start = pad_end - padded
    dest = _dest_rows(idx, rank, pad_start)
    n_valid = (pad_end[-1] // EXPERT_BLOCK).astype(i32).reshape(1)
    blk_row0 = jnp.arange(nb, dtype=i32) * EXPERT_BLOCK
    blk_e = jnp.minimum(jnp.sum(blk_row0[:, None] >= pad_end[None, :], axis=1), N_EXPERTS - 1).astype(i32)
    own = jnp.arange(N_EXPERTS, dtype=i32)[None, :] == blk_e[:, None]
    blk_cnt = jnp.sum(jnp.where(own, counts[None, :], 0), axis=1)
    blk_start = jnp.sum(jnp.where(own, pad_start[None, :], 0), axis=1)
    blk_rows = jnp.clip(blk_cnt - (blk_row0 - blk_start), 0, EXPERT_BLOCK).astype(i32)

    xs, gs = _dispatch(h2_p, h2_s, dest, gate, n_rows)
    ys = _gmm(xs, gs, w1[0], w3[0], w2[0], blk_e, blk_rows, n_valid)
    comb = _combine(ys, dest)

    y_p = _final_prompt(x1_p, h2_p, comb, mod_p, ws1_b, ws3_b, ws2_b, B, L)
    y_s = _final_sample(x1_s, h2_s, comb, mod_s, ws1_b, ws3_b, ws2_b, BS, LS, tp)

    return (y_p, y_s, new_pool_p[None], kc_p.reshape(1, B, WINDOW, N_KV_HEADS, HEAD_DIM),
            vc_p.reshape(1, B, WINDOW, N_KV_HEADS, HEAD_DIM), new_pool_s[None],
            nk_s.reshape(1, BS, W, N_KV_HEADS, HEAD_DIM), nv_s.reshape(1, BS, W, N_KV_HEADS, HEAD_DIM))
```

```python
import functools
import math

import numpy as np
import jax
import jax.numpy as jnp
from jax import lax
from jax.experimental import pallas as pl
from jax.experimental.pallas import tpu as pltpu
from jax.experimental.pallas import tpu_sc as plsc

f32 = jnp.float32
bf16 = jnp.bfloat16
i32 = jnp.int32

D_MODEL = 1024
PAST_LEN = 8192
POOL_W = 512
POOL_WINDOWS = (2, 4, 8, 16)
POOL_GC = 128
POOL_BUF = 15
ATTN_W = 512
HEAD_DIM = 64
N_HEADS = 8
N_KV_HEADS = 2
GQA = 4
KV_W = 128
WINDOW = 128
NUM_BUCKETS = 32
MAX_EXACT = 16
REL_MAX_DIST = 128
N_EXPERTS = 256
N_EXPERT_GROUPS = 8
GROUP_SIZE = 32
TOPK_GROUPS = 4
TOP_K = 8
EXPERT_FF = 256
ROUTED_SCALE = 2.5
EXPERT_BLOCK = 128
EPS = 1e-6
NEG_INF = -1e30
QKV_W = POOL_W + ATTN_W + 2 * KV_W
QK_W = ATTN_W + KV_W
HIST = 16

SC_WORKERS = 32
SC_LANES = 16
GATE_ROW = 128


def _dot(a, b):
    return jnp.dot(a, b, preferred_element_type=f32)


def _dot_t(a, b):
    return lax.dot_general(a, b, (((1,), (1,)), ((), ())), preferred_element_type=f32)


def _split_bf16(a):
    hi = a.astype(bf16)
    lo = (a - hi.astype(f32)).astype(bf16)
    return hi, lo


ROW_RING = 6
W_RING = 4
HI_MASK = -65536


def _pack_pairs(a):
    h = a.shape[1] // 2
    hi = lax.bitcast_convert_type(a[:, :h].astype(bf16).astype(f32), i32)
    lo = lax.bitcast_convert_type(a[:, h:].astype(bf16).astype(f32), i32)
    return hi | lax.shift_right_logical(lo, 16)


def _unpack_pairs(w):
    hi = lax.bitcast_convert_type(w & HI_MASK, f32).astype(bf16)
    lo = lax.bitcast_convert_type(lax.shift_left(w, 16), f32).astype(bf16)
    return hi, lo


def _mod_norm(x, g, sc, sh):
    ms = jnp.mean(x * x, axis=-1, keepdims=True)
    y = x * lax.rsqrt(ms + EPS)
    return (y * g) * (1.0 + sc) + sh


def _ada_body(c_ref, w_ref, b_ref, o_ref):
    c = c_ref[...]
    a = (c * jax.nn.sigmoid(c)).astype(bf16)
    o_ref[...] = _dot(a, w_ref[...].astype(bf16)) + b_ref[...]


def _ada(c, w_ada, b_ada):
    n = c.shape[0]
    tn = 1024
    return pl.pallas_call(
        _ada_body,
        grid=(6 * D_MODEL // tn,),
        in_specs=[
            pl.BlockSpec((n, D_MODEL), lambda j: (0, 0)),
            pl.BlockSpec((D_MODEL, tn), lambda j: (0, j)),
            pl.BlockSpec((1, tn), lambda j: (0, j)),
        ],
        out_specs=pl.BlockSpec((n, tn), lambda j: (0, j)),
        out_shape=jax.ShapeDtypeStruct((n, 6 * D_MODEL), f32),
        name="ada_mod",
    )(c, w_ada, b_ada.reshape(1, -1))


def _relbias_body(table_ref, bucket_ref, o_ref):
    bucket = bucket_ref[...]
    for h in range(N_HEADS):
        acc = jnp.zeros(bucket.shape, f32)
        for b in range(NUM_BUCKETS):
            acc = jnp.where(bucket == b, table_ref[b, h], acc)
        o_ref[h] = acc


def _rel_buckets(dist):
    n = np.maximum(dist, 0)
    nf = np.maximum(n, 1).astype(np.float64)
    large = MAX_EXACT + (np.log(nf / MAX_EXACT) / math.log(REL_MAX_DIST / MAX_EXACT)
                         * (NUM_BUCKETS - MAX_EXACT)).astype(np.int32)
    return np.where(n < MAX_EXACT, n, np.minimum(large, NUM_BUCKETS - 1)).astype(np.int32)


def _relbias(table, dist):
    lq, lk = dist.shape
    return pl.pallas_call(
        _relbias_body,
        in_specs=[
            pl.BlockSpec(memory_space=pltpu.SMEM),
            pl.BlockSpec((lq, lk), lambda: (0, 0)),
        ],
        out_specs=pl.BlockSpec((N_HEADS, lq, lk), lambda: (0, 0, 0)),
        out_shape=jax.ShapeDtypeStruct((N_HEADS, lq, lk), f32),
        name="rel_bias",
    )(table, jnp.asarray(_rel_buckets(dist)))


def _qkv_from_h(h, w_ref, gqk_ref, bd_ref):
    u = _dot(h.astype(bf16), w_ref[...])
    qk = u[:, POOL_W:POOL_W + QK_W]
    y_hi, y_lo = _split_bf16(qk * qk)
    bd = bd_ref[...]
    ss = _dot(y_hi, bd) + _dot(y_lo, bd)
    qkn = (qk * lax.rsqrt(ss * (1.0 / HEAD_DIM) + EPS)) * gqk_ref[...]
    q = qkn[:, :ATTN_W] * (HEAD_DIM ** -0.5)
    k = qkn[:, ATTN_W:]
    v = u[:, POOL_W + QK_W:]
    return u[:, :POOL_W], q, k, v


def _inproj_prompt_body(x_ref, sh_ref, sc_ref, g_ref, w_ref, gqk_ref, bd_ref, wp_ref, ps_ref,
                        q_ref, k_ref, v_ref, po_ref, np_ref, kc_ref, vc_ref, hist, *, tl, nt):
    j = pl.program_id(1)
    h = _mod_norm(x_ref[0], g_ref[...], sc_ref[0], sh_ref[0])
    up, q, k, v = _qkv_from_h(h, w_ref, gqk_ref, bd_ref)
    q_ref[0] = q.astype(bf16)
    k_ref[0] = k.astype(bf16)
    v_ref[0] = v.astype(bf16)

    @pl.when(j == nt - 1)
    def _():
        kc_ref[0] = k[tl - WINDOW:, :]
        vc_ref[0] = v[tl - WINDOW:, :]

    @pl.when(j == 0)
    def _():
        hist[0:HIST, :] = jnp.zeros((HIST, POOL_W), f32)

    hist[HIST:HIST + tl, :] = up
    pos = j * tl + lax.broadcasted_iota(i32, (tl, 1), 0)
    for g, w in enumerate(POOL_WINDOWS):
        lanes = slice(g * POOL_GC, (g + 1) * POOL_GC)
        cur = up[:, lanes]
        acc = cur
        for s in range(1, w):
            acc = acc + hist[HIST - s:HIST - s + tl, lanes]
        cnt = jnp.minimum(w, pos + 1).astype(f32)
        d = acc / cnt - cur
        yg = _dot(d.astype(bf16), wp_ref[g]) * ps_ref[:, lanes]
        po_ref[0, :, lanes] = yg.astype(bf16)

    @pl.when(j == nt - 1)
    def _():
        np_ref[0] = hist[tl + 1:tl + HIST, :]

    hist[0:HIST, :] = hist[tl:tl + HIST, :]


def _inproj_prompt(x, mod3, g_attn, w_in, gqk, bd, w_pool, pool_scale, tl=512):
    B, L, D = x.shape
    nt = L // tl
    full = lambda shape: pl.BlockSpec(shape, lambda b, j: (0,) * len(shape))
    return pl.pallas_call(
        functools.partial(_inproj_prompt_body, tl=tl, nt=nt),
        grid=(B, nt),
        in_specs=[
            pl.BlockSpec((1, tl, D), lambda b, j: (b, j, 0)),
            pl.BlockSpec((1, 1, D), lambda b, j: (b, 0, 0)),
            pl.BlockSpec((1, 1, D), lambda b, j: (b, 0, 1)),
            full((1, D)),
            full((D, QKV_W)),
            full((1, QK_W)),
            full((QK_W, QK_W)),
            full((4, POOL_GC, POOL_GC)),
            full((1, POOL_W)),
        ],
        out_specs=[
            pl.BlockSpec((1, tl, ATTN_W), lambda b, j: (b, j, 0)),
            pl.BlockSpec((1, tl, KV_W), lambda b, j: (b, j, 0)),
            pl.BlockSpec((1, tl, KV_W), lambda b, j: (b, j, 0)),
            pl.BlockSpec((1, tl, POOL_W), lambda b, j: (b, j, 0)),
            pl.BlockSpec((1, POOL_BUF, POOL_W), lambda b, j: (b, 0, 0)),
            pl.BlockSpec((1, WINDOW, KV_W), lambda b, j: (b, 0, 0)),
            pl.BlockSpec((1, WINDOW, KV_W), lambda b, j: (b, 0, 0)),
        ],
        out_shape=[
            jax.ShapeDtypeStruct((B, L, ATTN_W), bf16),
            jax.ShapeDtypeStruct((B, L, KV_W), bf16),
            jax.ShapeDtypeStruct((B, L, KV_W), bf16),
            jax.ShapeDtypeStruct((B, L, POOL_W), bf16),
            jax.ShapeDtypeStruct((B, POOL_BUF, POOL_W), f32),
            jax.ShapeDtypeStruct((B, WINDOW, KV_W), f32),
            jax.ShapeDtypeStruct((B, WINDOW, KV_W), f32),
        ],
        scratch_shapes=[pltpu.VMEM((HIST + tl, POOL_W), f32)],
        compiler_params=pltpu.CompilerParams(dimension_semantics=("arbitrary", "arbitrary")),
        name="inproj_prompt",
    )(x, mod3, mod3, g_attn, w_in, gqk, bd, w_pool, pool_scale)


def _inproj_sample_body(x_ref, sh_ref, sc_ref, g_ref, w_ref, gqk_ref, bd_ref, wp_ref, ps_ref, st_ref,
                        q_ref, k_ref, v_ref, po_ref, np_ref, ext, *, bt, ls, pos0):
    n = bt * ls
    h3 = _mod_norm(x_ref[...], g_ref[...][None], sc_ref[...], sh_ref[...])
    up, q, k, v = _qkv_from_h(h3.reshape(n, D_MODEL), w_ref, gqk_ref, bd_ref)
    q_ref[...] = q.astype(bf16)
    k_ref[...] = k
    v_ref[...] = v

    ext[:, 1:HIST, :] = st_ref[...]
    ext[:, HIST:HIST + ls, :] = up.reshape(bt, ls, POOL_W)
    pos = pos0 + lax.broadcasted_iota(i32, (1, ls, 1), 1)
    for g, w in enumerate(POOL_WINDOWS):
        lanes = slice(g * POOL_GC, (g + 1) * POOL_GC)
        cur = ext[:, HIST:HIST + ls, lanes]
        acc = cur
        for s in range(1, w):
            acc = acc + ext[:, HIST - s:HIST - s + ls, lanes]
        cnt = jnp.minimum(w, pos + 1).astype(f32)
        d = (acc / cnt - cur).reshape(n, POOL_GC)
        yg = _dot(d.astype(bf16), wp_ref[g]) * ps_ref[:, lanes]
        po_ref[:, lanes] = yg.astype(bf16)
    np_ref[...] = ext[:, ls + 1:ls + HIST, :]


def _inproj_sample(x, mod3, g_attn, w_in, gqk, bd, w_pool, pool_scale, state, pos0, bt=64):
    B, ls, D = x.shape
    n = bt * ls
    full = lambda shape: pl.BlockSpec(shape, lambda i: (0,) * len(shape))
    return pl.pallas_call(
        functools.partial(_inproj_sample_body, bt=bt, ls=ls, pos0=pos0),
        grid=(B // bt,),
        in_specs=[
            pl.BlockSpec((bt, ls, D), lambda i: (i, 0, 0)),
            pl.BlockSpec((bt, 1, D), lambda i: (i, 0, 0)),
            pl.BlockSpec((bt, 1, D), lambda i: (i, 0, 1)),
            full((1, D)),
            full((D, QKV_W)),
            full((1, QK_W)),
            full((QK_W, QK_W)),
            full((4, POOL_GC, POOL_GC)),
            full((1, POOL_W)),
            pl.BlockSpec((bt, POOL_BUF, POOL_W), lambda i: (i, 0, 0)),
        ],
        out_specs=[
            pl.BlockSpec((n, ATTN_W), lambda i: (i, 0)),
            pl.BlockSpec((n, KV_W), lambda i: (i, 0)),
            pl.BlockSpec((n, KV_W), lambda i: (i, 0)),
            pl.BlockSpec((n, POOL_W), lambda i: (i, 0)),
            pl.BlockSpec((bt, POOL_BUF, POOL_W), lambda i: (i, 0, 0)),
        ],
        out_shape=[
            jax.ShapeDtypeStruct((B * ls, ATTN_W), bf16),
            jax.ShapeDtypeStruct((B * ls, KV_W), f32),
            jax.ShapeDtypeStruct((B * ls, KV_W), f32),
            jax.ShapeDtypeStruct((B * ls, POOL_W), bf16),
            jax.ShapeDtypeStruct((B, POOL_BUF, POOL_W), f32),
        ],
        scratch_shapes=[pltpu.VMEM((bt, HIST + ls, POOL_W), f32)],
        name="inproj_sample",
    )(x, mod3, mod3, g_attn, w_in, gqk, bd, w_pool, pool_scale, state)


def _softmax_sink(parts, sink):
    m = sink
    for s in parts:
        m = jnp.maximum(m, jnp.max(s, axis=-1, keepdims=True))
    ps = [jnp.exp(s - m) for s in parts]
    denom = jnp.exp(sink - m)
    for p in ps:
        denom = denom + jnp.sum(p, axis=-1, keepdims=True)
    inv = 1.0 / denom
    return [(p * inv).astype(bf16) for p in ps]


def _attn_prompt_body(sinks_ref, q_ref, kp_ref, kc_ref, vp_ref, vc_ref, bias_ref, mask_ref, o_ref):
    kk = jnp.concatenate([kp_ref[0], kc_ref[0]], axis=0)
    vv = jnp.concatenate([vp_ref[0], vc_ref[0]], axis=0)
    k0 = jnp.concatenate([kk, kk], axis=1)
    v0 = jnp.concatenate([vv, vv], axis=1)
    k1 = pltpu.roll(k0, HEAD_DIM, 1)
    v1 = pltpu.roll(v0, HEAD_DIM, 1)
    lane_group = lax.broadcasted_iota(i32, k0.shape, 1) // HEAD_DIM
    valid = mask_ref[0] > 0.5
    nk = 2 * WINDOW
    qw = GQA * HEAD_DIM
    for kv in range(N_KV_HEADS):
        def blockdiag(t0, t1):
            return jnp.concatenate(
                [jnp.where(lane_group == g, t0 if g % 2 == kv else t1, jnp.zeros_like(t0)) for g in range(GQA)],
                axis=0)
        s = jnp.where(valid, _dot_t(q_ref[0, :, kv * qw:(kv + 1) * qw], blockdiag(k0, k1)) + bias_ref[kv], NEG_INF)
        ps = [_softmax_sink([s[:, g * nk:(g + 1) * nk]], sinks_ref[kv * GQA + g])[0] for g in range(GQA)]
        o = _dot(jnp.concatenate(ps, axis=1), blockdiag(v0, v1))
        o_ref[0, :, kv * qw:(kv + 1) * qw] = o.astype(bf16)


def _attn_prompt(q, k, v, bias, sinks):
    B, L, _ = q.shape
    nb = L // WINDOW
    cur = lambda b, j: (b, j, 0)
    prev = lambda b, j: (b, jnp.maximum(j - 1, 0), 0)
    qi = np.arange(WINDOW)[:, None]
    kc = np.arange(2 * WINDOW)[None, :]
    own = (kc >= WINDOW) & (kc - WINDOW <= qi)
    prv = (kc < WINDOW) & (kc > qi)
    mask = np.stack([np.tile(own, (1, GQA)), np.tile(own | prv, (1, GQA))]).astype(np.float32)
    return pl.pallas_call(
        _attn_prompt_body,
        grid=(B, nb),
        in_specs=[
            pl.BlockSpec(memory_space=pltpu.SMEM),
            pl.BlockSpec((1, WINDOW, ATTN_W), cur),
            pl.BlockSpec((1, WINDOW, KV_W), prev),
            pl.BlockSpec((1, WINDOW, KV_W), cur),
            pl.BlockSpec((1, WINDOW, KV_W), prev),
            pl.BlockSpec((1, WINDOW, KV_W), cur),
            pl.BlockSpec((N_KV_HEADS, WINDOW, GQA * 2 * WINDOW), lambda b, j: (0, 0, 0)),
            pl.BlockSpec((1, WINDOW, GQA * 2 * WINDOW), lambda b, j: (jnp.minimum(j, 1), 0, 0)),
        ],
        out_specs=pl.BlockSpec((1, WINDOW, ATTN_W), cur),
        out_shape=jax.ShapeDtypeStruct((B, L, ATTN_W), bf16),
        name="attn_prompt",
    )(sinks, q, k, k, v, v, bias, jnp.asarray(mask))


def _attn_sample_body(q_ref, kb_ref, vb_ref, kn_ref, vn_ref, bb_ref, bn_ref, sink_ref,
                      o_ref, nk_ref, nv_ref, *, bb, ls):
    W = kb_ref.shape[1]
    rows = GQA * ls
    qi = lax.broadcasted_iota(i32, (rows, W), 0) % ls
    kj = lax.broadcasted_iota(i32, (rows, W), 1)
    valid_buf = kj > qi
    qi2 = lax.broadcasted_iota(i32, (rows, ls), 0) % ls
    kj2 = lax.broadcasted_iota(i32, (rows, ls), 1)
    valid_new = kj2 <= qi2

    def one(b, carry):
        qb = q_ref[b]
        kbuf = kb_ref[b]
        vbuf = vb_ref[b]
        knew = kn_ref[b]
        vnew = vn_ref[b]
        outs = []
        for kv in range(N_KV_HEADS):
            ks = slice(kv * HEAD_DIM, (kv + 1) * HEAD_DIM)
            qg = jnp.concatenate(
                [qb[:, (kv * GQA + g) * HEAD_DIM:(kv * GQA + g + 1) * HEAD_DIM] for g in range(GQA)], axis=0)
            s_buf = jnp.where(valid_buf, _dot_t(qg, kbuf[:, ks].astype(bf16)) + bb_ref[kv], NEG_INF)
            s_new = jnp.where(valid_new, _dot_t(qg, knew[:, ks].astype(bf16)) + bn_ref[kv], NEG_INF)
            p_buf, p_new = _softmax_sink([s_buf, s_new], sink_ref[kv])
            o = _dot(p_buf, vbuf[:, ks].astype(bf16)) + _dot(p_new, vnew[:, ks].astype(bf16))
            outs.extend([o[g * ls:(g + 1) * ls] for g in range(GQA)])
        o_ref[b] = jnp.concatenate(outs, axis=-1).astype(bf16)
        nk_ref[b, 0:W - ls, :] = kbuf[ls:, :]
        nk_ref[b, W - ls:W, :] = knew
        nv_ref[b, 0:W - ls, :] = vbuf[ls:, :]
        nv_ref[b, W - ls:W, :] = vnew
        return carry

    lax.fori_loop(0, bb, one, 0)


def _attn_sample(q, k_buf, v_buf, k_new, v_new, bias_buf, bias_new, sink_col, bb=16):
    B, ls, _ = q.shape
    W = k_buf.shape[1]
    rows = GQA * ls
    blk = lambda shape: pl.BlockSpec(shape, lambda i: (i, 0, 0))
    full = lambda shape: pl.BlockSpec(shape, lambda i: (0, 0, 0))
    return pl.pallas_call(
        functools.partial(_attn_sample_body, bb=bb, ls=ls),
        grid=(B // bb,),
        in_specs=[
            blk((bb, ls, ATTN_W)),
            blk((bb, W, KV_W)),
            blk((bb, W, KV_W)),
            blk((bb, ls, KV_W)),
            blk((bb, ls, KV_W)),
            full((N_KV_HEADS, rows, W)),
            full((N_KV_HEADS, rows, ls)),
            full((N_KV_HEADS, rows, 1)),
        ],
        out_specs=[blk((bb, ls, ATTN_W)), blk((bb, W, KV_W)), blk((bb, W, KV_W))],
        out_shape=[
            jax.ShapeDtypeStruct((B, ls, ATTN_W), bf16),
            jax.ShapeDtypeStruct((B, W, KV_W), f32),
            jax.ShapeDtypeStruct((B, W, KV_W), f32),
        ],
        name="attn_sample",
    )(q, k_buf, v_buf, k_new, v_new, bias_buf, bias_new, sink_col)


def _outproj_core(po, at, x, gt, sc, sh, g_ref, wo_ref, wrh_ref, wrl_ref):
    mixo = _dot(po, wo_ref[0:POOL_W, :]) + _dot(at, wo_ref[POOL_W:, :])
    x1 = x + gt * mixo.reshape(x.shape)
    h2 = _mod_norm(x1, g_ref[...].reshape((1,) * (x.ndim - 1) + (D_MODEL,)), sc, sh).reshape(-1, D_MODEL)
    h_hi, h_lo = _split_bf16(h2)
    wh = wrh_ref[...]
    logits = _dot_t(wh, h_hi) + (_dot_t(wh, h_lo) + _dot_t(wrl_ref[...], h_hi))
    return x1, _pack_pairs(h2), logits


def _outproj_prompt_body(po_ref, at_ref, x_ref, gt_ref, sc_ref, sh_ref, g_ref, wo_ref, wrh_ref, wrl_ref,
                         x1_ref, h2_ref, lg_ref):
    x1, h2p, logits = _outproj_core(po_ref[0], at_ref[0], x_ref[0], gt_ref[0], sc_ref[0], sh_ref[0],
                                    g_ref, wo_ref, wrh_ref, wrl_ref)
    x1_ref[...] = x1
    h2_ref[...] = h2p
    lg_ref[...] = logits


def _outproj_sample_body(po_ref, at_ref, x_ref, gt_ref, sc_ref, sh_ref, g_ref, wo_ref, wrh_ref, wrl_ref,
                         x1_ref, h2_ref, lg_ref):
    x1, h2p, logits = _outproj_core(po_ref[...], at_ref[...], x_ref[...], gt_ref[...], sc_ref[...], sh_ref[...],
                                    g_ref, wo_ref, wrh_ref, wrl_ref)
    x1_ref[...] = x1.reshape(-1, D_MODEL)
    h2_ref[...] = h2p
    lg_ref[...] = logits


def _outproj_prompt(po, at, x, mod3, g_ffn, w_out, wr_hi, wr_lo, tm=512):
    B, L, D = x.shape
    nt = L // tm
    n_tok = B * L
    full = lambda shape: pl.BlockSpec(shape, lambda b, j: (0,) * len(shape))
    modspec = lambda c: pl.BlockSpec((1, 1, D), lambda b, j: (b, 0, c))
    return pl.pallas_call(
        _outproj_prompt_body,
        grid=(B, nt),
        in_specs=[
            pl.BlockSpec((1, tm, POOL_W), lambda b, j: (b, j, 0)),
            pl.BlockSpec((1, tm, ATTN_W), lambda b, j: (b, j, 0)),
            pl.BlockSpec((1, tm, D), lambda b, j: (b, j, 0)),
            modspec(2), modspec(4), modspec(3),
            full((1, D)), full((D, D)), full((N_EXPERTS, D)), full((N_EXPERTS, D)),
        ],
        out_specs=[
            pl.BlockSpec((tm, D), lambda b, j: (b * nt + j, 0)),
            pl.BlockSpec((tm, D // 2), lambda b, j: (b * nt + j, 0)),
            pl.BlockSpec((N_EXPERTS, tm), lambda b, j: (0, b * nt + j)),
        ],
        out_shape=[
            jax.ShapeDtypeStruct((n_tok, D), f32),
            jax.ShapeDtypeStruct((n_tok, D // 2), i32),
            jax.ShapeDtypeStruct((N_EXPERTS, n_tok), f32),
        ],
        name="outproj_prompt",
    )(po, at, x, mod3, mod3, mod3, g_ffn, w_out, wr_hi, wr_lo)


def _outproj_sample(po, at, x, mod3, g_ffn, w_out, wr_hi, wr_lo, bt=64):
    B, ls, D = x.shape
    n = bt * ls
    full = lambda shape: pl.BlockSpec(shape, lambda i: (0,) * len(shape))
    modspec = lambda c: pl.BlockSpec((bt, 1, D), lambda i: (i, 0, c))
    return pl.pallas_call(
        _outproj_sample_body,
        grid=(B // bt,),
        in_specs=[
            pl.BlockSpec((n, POOL_W), lambda i: (i, 0)),
            pl.BlockSpec((n, ATTN_W), lambda i: (i, 0)),
            pl.BlockSpec((bt, ls, D), lambda i: (i, 0, 0)),
            modspec(2), modspec(4), modspec(3),
            full((1, D)), full((D, D)), full((N_EXPERTS, D)), full((N_EXPERTS, D)),
        ],
        out_specs=[
            pl.BlockSpec((n, D), lambda i: (i, 0)),
            pl.BlockSpec((n, D // 2), lambda i: (i, 0)),
            pl.BlockSpec((N_EXPERTS, n), lambda i: (0, i)),
        ],
        out_shape=[
            jax.ShapeDtypeStruct((B * ls, D), f32),
            jax.ShapeDtypeStruct((B * ls, D // 2), i32),
            jax.ShapeDtypeStruct((N_EXPERTS, B * ls), f32),
        ],
        name="outproj_sample",
    )(po, at, x, mod3, mod3, mod3, g_ffn, w_out, wr_hi, wr_lo)


def _route_body(lg_ref, rb_ref, tri_ref, idx_ref, rank_ref, gate_ref, cnt_ref, carry, *, tr, nsteps):
    step = pl.program_id(0)

    @pl.when(step == 0)
    def _():
        carry[...] = jnp.zeros(carry.shape, f32)

    s = jax.nn.sigmoid(lg_ref[...])
    sb = s + rb_ref[...]
    e_iota = lax.broadcasted_iota(i32, (N_EXPERTS, tr), 0)
    g_iota = lax.broadcasted_iota(i32, (GROUP_SIZE, tr), 0)

    gscore = []
    for g in range(N_EXPERT_GROUPS):
        v = sb[g * GROUP_SIZE:(g + 1) * GROUP_SIZE]
        m1 = jnp.max(v, axis=0, keepdims=True)
        i1 = jnp.min(jnp.where(v == m1, g_iota, GROUP_SIZE), axis=0, keepdims=True)
        m2 = jnp.max(jnp.where(g_iota == i1, -jnp.inf, v), axis=0, keepdims=True)
        gscore.append(m1 + m2)
    parts = []
    for g in range(N_EXPERT_GROUPS):
        beaten = jnp.zeros((1, tr), i32)
        for g2 in range(N_EXPERT_GROUPS):
            if g2 == g:
                continue
            ahead = gscore[g2] > gscore[g]
            if g2 < g:
                ahead = ahead | (gscore[g2] == gscore[g])
            beaten = beaten + ahead.astype(i32)
        keep = beaten < TOPK_GROUPS
        parts.append(jnp.where(keep, sb[g * GROUP_SIZE:(g + 1) * GROUP_SIZE], NEG_INF))
    cur = jnp.concatenate(parts, axis=0)

    sel = jnp.zeros((N_EXPERTS, tr), f32)
    idxs, svals = [], []
    for _ in range(TOP_K):
        m = jnp.max(cur, axis=0, keepdims=True)
        ik = jnp.min(jnp.where(cur == m, e_iota, N_EXPERTS), axis=0, keepdims=True)
        hit = e_iota == ik
        svals.append(jnp.sum(jnp.where(hit, s, 0.0), axis=0, keepdims=True))
        cur = jnp.where(hit, -jnp.inf, cur)
        sel = jnp.where(hit, 1.0, sel)
        idxs.append(ik)
    ssum = svals[0]
    for sv in svals[1:]:
        ssum = ssum + sv
    gate_ref[...] = jnp.concatenate([sv / ssum * ROUTED_SCALE for sv in svals], axis=0)
    idx_ref[...] = jnp.concatenate(idxs, axis=0)

    before = carry[...] + _dot(sel.astype(bf16), tri_ref[...])
    ranks = [jnp.sum(jnp.where(e_iota == ik, before, 0.0), axis=0, keepdims=True) for ik in idxs]
    rank_ref[...] = jnp.concatenate(ranks, axis=0).astype(i32)
    carry[...] = carry[...] + jnp.sum(sel, axis=1, keepdims=True)

    @pl.when(step == nsteps - 1)
    def _():
        cnt_ref[...] = carry[...]


def _route(logits_t, router_bias, tr=512):
    E, T = logits_t.shape
    nsteps = T // tr
    tri = jnp.asarray(np.triu(np.ones((tr, tr), np.float32), 1), bf16)
    return pl.pallas_call(
        functools.partial(_route_body, tr=tr, nsteps=nsteps),
        grid=(nsteps,),
        in_specs=[
            pl.BlockSpec((E, tr), lambda i: (0, i)),
            pl.BlockSpec((E, 1), lambda i: (0, 0)),
            pl.BlockSpec((tr, tr), lambda i: (0, 0)),
        ],
        out_specs=[
            pl.BlockSpec((TOP_K, tr), lambda i: (0, i)),
            pl.BlockSpec((TOP_K, tr), lambda i: (0, i)),
            pl.BlockSpec((TOP_K, tr), lambda i: (0, i)),
            pl.BlockSpec((E, 1), lambda i: (0, 0)),
        ],
        out_shape=[
            jax.ShapeDtypeStruct((TOP_K, T), i32),
            jax.ShapeDtypeStruct((TOP_K, T), i32),
            jax.ShapeDtypeStruct((TOP_K, T), f32),
            jax.ShapeDtypeStruct((E, 1), f32),
        ],
        scratch_shapes=[pltpu.VMEM((E, 1), f32)],
        compiler_params=pltpu.CompilerParams(dimension_semantics=("arbitrary",)),
        name="route",
    )(logits_t, router_bias.reshape(E, 1), tri)


def _dest_body(idx_ref, rank_ref, ps_ref, dest_ref, *, tr):
    e_iota = lax.broadcasted_iota(i32, (N_EXPERTS, tr), 0)
    start = ps_ref[...]
    rows = []
    for k in range(TOP_K):
        hit = e_iota == idx_ref[k:k + 1, :]
        rows.append(jnp.sum(jnp.where(hit, start, 0.0), axis=0, keepdims=True))
    dest_ref[...] = jnp.concatenate(rows, axis=0).astype(i32) + rank_ref[...]


def _dest_rows(idx, rank, pad_start, tr=512):
    K, T = idx.shape
    blk = pl.BlockSpec((K, tr), lambda i: (0, i))
    return pl.pallas_call(
        functools.partial(_dest_body, tr=tr),
        grid=(T // tr,),
        in_specs=[blk, blk, pl.BlockSpec((N_EXPERTS, 1), lambda i: (0, 0))],
        out_specs=blk,
        out_shape=jax.ShapeDtypeStruct((K, T), i32),
        name="dest_rows",
    )(idx, rank, pad_start.astype(f32).reshape(N_EXPERTS, 1))


def _sc_mesh():
    return plsc.VectorSubcoreMesh(core_axis_name="c", subcore_axis_name="s")


def _sc_worker_id():
    return lax.axis_index("s") * 2 + lax.axis_index("c")


def _dispatch(h2_a, h2_b, dest, gate, n_rows, chunk=32):
    ta, Dw = h2_a.shape
    T = ta + h2_b.shape[0]
    per_worker = T // SC_WORKERS
    nchunk = per_worker // chunk
    assert per_worker * SC_WORKERS == T and nchunk * chunk == per_worker and ta % chunk == 0

    nrow_idx = nchunk * TOP_K

    @functools.partial(
        pl.kernel, mesh=_sc_mesh(),
        out_type=[jax.ShapeDtypeStruct((n_rows, Dw), i32), jax.ShapeDtypeStruct((n_rows, GATE_ROW), f32)],
        scratch_types=[pltpu.VMEM((nrow_idx, chunk), i32), pltpu.VMEM((nrow_idx, chunk), f32),
                       pltpu.VMEM((chunk, Dw), i32), pltpu.VMEM((TOP_K, chunk, GATE_ROW), f32),
                       pltpu.SemaphoreType.DMA],
        compiler_params=pltpu.CompilerParams(needs_layout_passes=False),
        name="moe_dispatch",
    )
    def body(ha_hbm, hb_hbm, dest_hbm, gate_hbm, xs_hbm, gs_hbm, idx_v, gate_v, rows_v, grow_v, sem):
        wid = _sc_worker_id()
        base = wid * per_worker
        pltpu.sync_copy(dest_hbm.at[wid], idx_v)
        pltpu.sync_copy(gate_hbm.at[wid], gate_v)
        zero = jnp.zeros((SC_LANES,), f32)
        for k in range(TOP_K):
            @pl.loop(0, chunk)
            def _(t):
                for j in range(GATE_ROW // SC_LANES):
                    grow_v[k, t, pl.ds(j * SC_LANES, SC_LANES)] = zero

        @pl.loop(0, nchunk)
        def _(ci):
            t0 = base + ci * chunk

            @pl.when(t0 < ta)
            def _():
                pltpu.sync_copy(ha_hbm.at[pl.ds(t0, chunk)], rows_v)

            @pl.when(t0 >= ta)
            def _():
                pltpu.sync_copy(hb_hbm.at[pl.ds(t0 - ta, chunk)], rows_v)

            for k in range(TOP_K):
                @pl.loop(0, chunk)
                def _(t):
                    row = jnp.zeros((SC_LANES,), i32) + (ci * TOP_K + k)
                    grow_v[k, t, pl.ds(0, SC_LANES)] = plsc.load_gather(
                        gate_v, [row, jnp.zeros((SC_LANES,), i32) + t])

            cps = []
            for k in range(TOP_K):
                idx = idx_v.at[ci * TOP_K + k]
                cps.append(pltpu.make_async_copy(rows_v, xs_hbm.at[idx], sem))
                cps.append(pltpu.make_async_copy(grow_v.at[k], gs_hbm.at[idx], sem))
            for cp in cps:
                cp.start()
            for cp in cps:
                cp.wait()

    def per_worker_rows(a):
        return a.reshape(TOP_K, SC_WORKERS, nchunk, chunk).transpose(1, 2, 0, 3).reshape(
            SC_WORKERS, nrow_idx, chunk)

    return body(h2_a, h2_b, per_worker_rows(dest), per_worker_rows(gate))


def _combine(ys, dest, chunk=8):
    T = dest.shape[1]
    Dw = ys.shape[1]
    per_worker = T // SC_WORKERS
    nchunk = per_worker // chunk
    assert per_worker * SC_WORKERS == T and nchunk * chunk == per_worker and nchunk % 2 == 0

    @functools.partial(
        pl.kernel, mesh=_sc_mesh(),
        out_type=jax.ShapeDtypeStruct((T, 2 * Dw), f32),
        scratch_types=[
            pltpu.VMEM((TOP_K * per_worker,), i32),
            pltpu.VMEM((2, TOP_K, chunk, Dw), i32),
            pltpu.VMEM((chunk, 2 * Dw), f32),
            pltpu.SemaphoreType.DMA((2,)),
        ],
        compiler_params=pltpu.CompilerParams(needs_layout_passes=False),
        name="moe_combine",
    )
    def body(ys_hbm, dest_hbm, out_hbm, idx_v, buf, out_v, sems):
        base = _sc_worker_id() * per_worker
        pltpu.sync_copy(dest_hbm.at[pl.ds(_sc_worker_id() * (TOP_K * per_worker), TOP_K * per_worker)], idx_v)

        def gather(ci, slot):
            return [pltpu.make_async_copy(ys_hbm.at[idx_v.at[pl.ds(k * per_worker + ci * chunk, chunk)]],
                                          buf.at[slot, k], sems.at[slot]) for k in range(TOP_K)]

        for cp in gather(0, 0):
            cp.start()

        @pl.loop(0, nchunk, step=2)
        def _(c0):
            for slot in range(2):
                ci = c0 + slot

                @pl.when(ci + 1 < nchunk)
                def _():
                    for cp in gather(ci + 1, 1 - slot):
                        cp.start()

                for cp in gather(ci, slot):
                    cp.wait()

                @pl.loop(0, chunk)
                def _(t):
                    @pl.loop(0, Dw // SC_LANES)
                    def _(j):
                        sl = pl.ds(j * SC_LANES, SC_LANES)
                        w = buf[slot, 0, t, sl]
                        hi = plsc.bitcast(w & HI_MASK, f32)
                        lo = plsc.bitcast(lax.shift_left(w, 16), f32)
                        for k in range(1, TOP_K):
                            w = buf[slot, k, t, sl]
                            hi = hi + plsc.bitcast(w & HI_MASK, f32)
                            lo = lo + plsc.bitcast(lax.shift_left(w, 16), f32)
                        out_v[t, sl] = hi
                        out_v[t, pl.ds(Dw + j * SC_LANES, SC_LANES)] = lo

                pltpu.sync_copy(out_v, out_hbm.at[pl.ds(base + ci * chunk, chunk)])

    dest_w = dest.reshape(TOP_K, SC_WORKERS, per_worker).transpose(1, 0, 2).reshape(-1)
    return body(ys, dest_w)


def _gmm_body(blk_e_ref, blk_rows_ref, nv_ref, xs_hbm, gs_hbm, w1_hbm, w3_hbm, w2_hbm, ys_hbm,
              xbuf, gbuf, ybuf, w1f, w3f, w2f, w13_b, w2_b, xsem, gsem, ysem, wsem):
    nv = nv_ref[0]
    nb = blk_e_ref.shape[0]
    half = D_MODEL // 2
    RB = EXPERT_BLOCK

    def expert_of(blk):
        return blk_e_ref[jnp.minimum(blk, nb - 1)]

    def next_expert_block(blk):
        e0 = expert_of(blk)
        return lax.while_loop(lambda i: (i < nv) & (expert_of(i) == e0), lambda i: i + 1, blk + 1)

    def start_weights(blk, ordinal):
        @pl.when(blk < nv)
        def _():
            for cp in weight_copies(expert_of(blk), lax.rem(ordinal, W_RING)):
                cp.start()

    def row_copies(b, slot):
        r0 = pl.multiple_of(b * RB, RB)
        return (pltpu.make_async_copy(xs_hbm.at[pl.ds(r0, RB)], xbuf.at[slot], xsem.at[slot]),
                pltpu.make_async_copy(gs_hbm.at[pl.ds(r0, RB)], gbuf.at[slot], gsem.at[slot]))

    def out_copy(b, slot):
        r0 = pl.multiple_of(b * RB, RB)
        return pltpu.make_async_copy(ybuf.at[slot], ys_hbm.at[pl.ds(r0, RB)], ysem.at[slot])

    def weight_copies(e, ws):
        return (pltpu.make_async_copy(w1_hbm.at[e], w1f.at[ws], wsem.at[ws, 0]),
                pltpu.make_async_copy(w3_hbm.at[e], w3f.at[ws], wsem.at[ws, 1]),
                pltpu.make_async_copy(w2_hbm.at[e], w2f.at[ws], wsem.at[ws, 2]))

    blk = jnp.int32(0)
    for n in range(W_RING - 1):
        start_weights(blk, n)
        blk = next_expert_block(blk)
    for i in range(ROW_RING - 1):
        @pl.when(i < nv)
        def _():
            for cp in row_copies(i, i):
                cp.start()

    def step(b, ordinal_prev):
        slot = lax.rem(b, ROW_RING)
        e = blk_e_ref[b]
        first = (b == 0) | (e != blk_e_ref[jnp.maximum(b - 1, 0)])
        ordinal = jnp.where(first & (b > 0), ordinal_prev + 1, ordinal_prev)

        ahead = b + ROW_RING - 1

        @pl.when(ahead < nv)
        def _():
            for cp in row_copies(ahead, lax.rem(ahead, ROW_RING)):
                cp.start()

        @pl.when(first)
        def _():
            ws = lax.rem(ordinal, W_RING)
            for cp in weight_copies(e, ws):
                cp.wait()
            w13_b[:, 0:EXPERT_FF] = w1f[ws].astype(bf16)
            w13_b[:, EXPERT_FF:] = w3f[ws].astype(bf16)
            w2_b[...] = w2f[ws].astype(bf16)
            blk = b
            for _ in range(W_RING - 1):
                blk = next_expert_block(blk)
            start_weights(blk, ordinal + W_RING - 1)

        for cp in row_copies(b, slot):
            cp.wait()
        valid = lax.broadcasted_iota(i32, (RB, 1), 0) < blk_rows_ref[b]
        x_hi, x_lo = _unpack_pairs(jnp.where(valid, xbuf[slot], 0))
        ac = _dot(x_hi, w13_b[0:half, :]) + _dot(x_lo, w13_b[half:, :])
        a = ac[:, :EXPERT_FF]
        c = ac[:, EXPERT_FF:]
        hmid = (a * jax.nn.sigmoid(a)) * c
        g = jnp.where(valid, gbuf[slot][:, 0:1], 0.0)
        y = _pack_pairs(_dot(hmid.astype(bf16), w2_b[...]) * g)

        @pl.when(b >= ROW_RING)
        def _():
            out_copy(b - ROW_RING, slot).wait()

        ybuf[slot] = y
        out_copy(b, slot).start()
        return ordinal

    lax.fori_loop(0, nv, step, 0)

    for i in range(1, ROW_RING + 1):
        @pl.when(nv >= i)
        def _():
            out_copy(nv - i, lax.rem(nv - i, ROW_RING)).wait()


def _gmm(xs, gs, w1, w3, w2, blk_e, blk_rows, n_valid):
    n_rows, Dw = xs.shape
    D = 2 * Dw
    RB = EXPERT_BLOCK
    hbm = pl.BlockSpec(memory_space=pl.ANY)
    return pl.pallas_call(
        _gmm_body,
        grid_spec=pltpu.PrefetchScalarGridSpec(
            num_scalar_prefetch=3,
            grid=(1,),
            in_specs=[hbm, hbm, hbm, hbm, hbm],
            out_specs=hbm,
            scratch_shapes=[
                pltpu.VMEM((ROW_RING, RB, Dw), i32), pltpu.VMEM((ROW_RING, RB, GATE_ROW), f32),
                pltpu.VMEM((ROW_RING, RB, Dw), i32),
                pltpu.VMEM((W_RING, D, EXPERT_FF), f32), pltpu.VMEM((W_RING, D, EXPERT_FF), f32),
                pltpu.VMEM((W_RING, EXPERT_FF, D), f32),
                pltpu.VMEM((D, 2 * EXPERT_FF), bf16), pltpu.VMEM((EXPERT_FF, D), bf16),
                pltpu.SemaphoreType.DMA((ROW_RING,)), pltpu.SemaphoreType.DMA((ROW_RING,)),
                pltpu.SemaphoreType.DMA((ROW_RING,)), pltpu.SemaphoreType.DMA((W_RING, 3)),
            ],
        ),
        out_shape=jax.ShapeDtypeStruct((n_rows, Dw), i32),
        compiler_params=pltpu.CompilerParams(dimension_semantics=("arbitrary",)),
        name="moe_gmm",
    )(blk_e, blk_rows, n_valid, xs, gs, w1, w3, w2)


def _final_core(x1, h2p, comb, gt, ws1_ref, ws3_ref, ws2_ref):
    half = D_MODEL // 2
    h_hi, h_lo = _unpack_pairs(h2p)
    a = _dot(h_hi, ws1_ref[0:half, :]) + _dot(h_lo, ws1_ref[half:, :])
    c = _dot(h_hi, ws3_ref[0:half, :]) + _dot(h_lo, ws3_ref[half:, :])
    shared = _dot(((a * jax.nn.sigmoid(a)) * c).astype(bf16), ws2_ref[...])
    return x1, comb + shared, gt


def _final_prompt_body(x1_ref, h2_ref, cb_ref, gt_ref, ws1_ref, ws3_ref, ws2_ref, y_ref):
    x1, ffn, gt = _final_core(x1_ref[...], h2_ref[...], cb_ref[...], gt_ref[0], ws1_ref, ws3_ref, ws2_ref)
    y_ref[0] = x1 + gt * ffn


def _final_sample_body(x1_ref, h2_ref, cb_ref, gt_ref, ws1_ref, ws3_ref, ws2_ref, y_ref):
    x1, ffn, gt = _final_core(x1_ref[...], h2_ref[...], cb_ref[...], gt_ref[...], ws1_ref, ws3_ref, ws2_ref)
    shp = y_ref.shape
    y_ref[...] = x1.reshape(shp) + gt * ffn.reshape(shp)


def _final_prompt(x1, h2, comb, mod3, ws1, ws3, ws2, B, L, tm=512):
    D = D_MODEL
    nt = L // tm
    full = lambda shape: pl.BlockSpec(shape, lambda b, j: (0,) * len(shape))
    rows = pl.BlockSpec((tm, D), lambda b, j: (b * nt + j, 0))
    words = pl.BlockSpec((tm, D // 2), lambda b, j: (b * nt + j, 0))
    return pl.pallas_call(
        _final_prompt_body,
        grid=(B, nt),
        in_specs=[rows, words, rows, pl.BlockSpec((1, 1, D), lambda b, j: (b, 0, 5)),
                  full((D, EXPERT_FF)), full((D, EXPERT_FF)), full((EXPERT_FF, D))],
        out_specs=pl.BlockSpec((1, tm, D), lambda b, j: (b, j, 0)),
        out_shape=jax.ShapeDtypeStruct((B, L, D), f32),
        name="final_prompt",
    )(x1, h2, comb, mod3, ws1, ws3, ws2)


def _final_sample(x1, h2, comb, mod3, ws1, ws3, ws2, B, ls, row0, bt=64):
    D = D_MODEL
    n = bt * ls
    blk0 = row0 // n
    full = lambda shape: pl.BlockSpec(shape, lambda i: (0,) * len(shape))
    rows = pl.BlockSpec((n, D), lambda i: (i, 0))
    words = pl.BlockSpec((n, D // 2), lambda i: (i, 0))
    comb_rows = pl.BlockSpec((n, D), lambda i: (blk0 + i, 0))
    return pl.pallas_call(
        _final_sample_body,
        grid=(B // bt,),
        in_specs=[rows, words, comb_rows, pl.BlockSpec((bt, 1, D), lambda i: (i, 0, 5)),
                  full((D, EXPERT_FF)), full((D, EXPERT_FF)), full((EXPERT_FF, D))],
        out_specs=pl.BlockSpec((bt, ls, D), lambda i: (i, 0, 0)),
        out_shape=jax.ShapeDtypeStruct((B, ls, D), f32),
        name="final_sample",
    )(x1, h2, comb, mod3, ws1, ws3, ws2)


def kernel(x_prompt, x_sample, state_pool, cache_swa_k, cache_swa_v, c_prompt, c_sample, w_ada, b_ada,
           g_attn_norm, w_in, g_q, g_k, w_pool, pool_scale, w_out, attn_sinks, rel_bias, g_ffn_norm,
           w_router, router_bias, w1, w3, w2, ws1, ws3, ws2):
    B, L, D = x_prompt.shape
    BS, LS, _ = x_sample.shape
    depth = w_ada.shape[0]
    assert depth == 1
    W = cache_swa_k.shape[2]
    tp, ts = B * L, BS * LS
    T = tp + ts
    n_rows = (T * TOP_K // EXPERT_BLOCK + N_EXPERTS) * EXPERT_BLOCK
    nb = n_rows // EXPERT_BLOCK

    g_attn = g_attn_norm[0].reshape(1, D)
    g_ffn = g_ffn_norm[0].reshape(1, D)
    w_in_b = w_in[0].astype(bf16)
    w_out_b = w_out[0].astype(bf16)
    w_pool_b = w_pool[0].astype(bf16)
    ps = pool_scale[0].reshape(1, POOL_W)
    gqk = jnp.concatenate([jnp.tile(g_q[0], N_HEADS), jnp.tile(g_k[0], N_KV_HEADS)]).reshape(1, QK_W)
    head_of = np.arange(QK_W) // HEAD_DIM
    bd = jnp.asarray((head_of[:, None] == head_of[None, :]).astype(np.float32), bf16)
    wr_t = w_router[0].T
    wr_hi = wr_t.astype(bf16)
    wr_lo = (wr_t - wr_hi.astype(f32)).astype(bf16)
    ws1_b, ws3_b, ws2_b = ws1[0].astype(bf16), ws3[0].astype(bf16), ws2[0].astype(bf16)
    sinks = attn_sinks[0]

    mod = _ada(jnp.concatenate([c_prompt, c_sample], axis=0), w_ada[0], b_ada[0])
    mod_p = mod[:B].reshape(B, 1, 6 * D)
    mod_s = mod[B:].reshape(BS, 1, 6 * D)

    dist_p = np.arange(WINDOW)[:, None] + WINDOW - np.arange(2 * WINDOW)[None, :]
    bias_p = _relbias(rel_bias, dist_p)
    bias_p = bias_p.reshape(N_KV_HEADS, GQA, WINDOW, 2 * WINDOW).transpose(0, 2, 1, 3).reshape(
        N_KV_HEADS, WINDOW, GQA * 2 * WINDOW)
    dist_s = np.arange(LS)[:, None] + W - np.arange(W + LS)[None, :]
    bias_s = _relbias(rel_bias, dist_s)
    bias_s_buf = bias_s[:, :, :W].reshape(N_KV_HEADS, GQA * LS, W)
    bias_s_new = bias_s[:, :, W:].reshape(N_KV_HEADS, GQA * LS, LS)
    sink_col = jnp.repeat(sinks, LS).reshape(N_KV_HEADS, GQA * LS, 1)

    q_p, k_p, v_p, po_p, new_pool_p, kc_p, vc_p = _inproj_prompt(
        x_prompt, mod_p, g_attn, w_in_b, gqk, bd, w_pool_b, ps)
    q_s, k_s, v_s, po_s, new_pool_s = _inproj_sample(
        x_sample, mod_s, g_attn, w_in_b, gqk, bd, w_pool_b, ps, state_pool[0], PAST_LEN)
    at_p = _attn_prompt(q_p, k_p, v_p, bias_p, sinks)
    at_s, nk_s, nv_s = _attn_sample(
        q_s.reshape(BS, LS, ATTN_W), cache_swa_k[0].reshape(BS, W, KV_W), cache_swa_v[0].reshape(BS, W, KV_W),
        k_s.reshape(BS, LS, KV_W), v_s.reshape(BS, LS, KV_W), bias_s_buf, bias_s_new, sink_col)

    x1_p, h2_p, lg_p = _outproj_prompt(po_p, at_p, x_prompt, mod_p, g_ffn, w_out_b, wr_hi, wr_lo)
    x1_s, h2_s, lg_s = _outproj_sample(po_s, at_s.reshape(ts, ATTN_W), x_sample, mod_s, g_ffn, w_out_b,
                                       wr_hi, wr_lo)

    idx, rank, gate, counts = _route(jnp.concatenate([lg_p, lg_s], axis=1), router_bias[0])
    counts = counts.reshape(N_EXPERTS).astype(i32)
    padded = (counts + EXPERT_BLOCK - 1) // EXPERT_BLOCK * EXPERT_BLOCK
    pad_end = jnp.cumsum(padded)
    pad_start = pad_end - padded
    dest = _dest_rows(idx, rank, pad_start)
    n_valid = (pad_end[-1] // EXPERT_BLOCK).astype(i32).reshape(1)
    blk_row0 = jnp.arange(nb, dtype=i32) * EXPERT_BLOCK
    blk_e = jnp.minimum(jnp.sum(blk_row0[:, None] >= pad_end[None, :], axis=1), N_EXPERTS - 1).astype(i32)
    own = jnp.arange(N_EXPERTS, dtype=i32)[None, :] == blk_e[:, None]
    blk_cnt = jnp.sum(jnp.where(own, counts[None, :], 0), axis=1)
    blk_start = jnp.sum(jnp.where(own, pad_start[None, :], 0), axis=1)
    blk_rows = jnp.clip(blk_cnt - (blk_row0 - blk_start), 0, EXPERT_BLOCK).astype(i32)

    xs, gs = _dispatch(h2_p, h2_s, dest, gate, n_rows)
    ys = _gmm(xs, gs, w1[0], w3[0], w2[0], blk_e, blk_rows, n_valid)
    comb = _combine(ys, dest)

    y_p = _final_prompt(x1_p, h2_p, comb, mod_p, ws1_b, ws3_b, ws2_b, B, L)
    y_s = _final_sample(x1_s, h2_s, comb, mod_s, ws1_b, ws3_b, ws2_b, BS, LS, tp)

    return (y_p, y_s, new_pool_p[None], kc_p.reshape(1, B, WINDOW, N_KV_HEADS, HEAD_DIM),
            vc_p.reshape(1, B, WINDOW, N_KV_HEADS, HEAD_DIM), new_pool_s[None],
            nk_s.reshape(1, BS, W, N_KV_HEADS, HEAD_DIM), nv_s.reshape(1, BS, W, N_KV_HEADS, HEAD_DIM))
```

```python
import functools
import math

import numpy as np
import jax
import jax.numpy as jnp
from jax import lax
from jax.experimental import pallas as pl
from jax.experimental.pallas import tpu as pltpu
from jax.experimental.pallas import tpu_sc as plsc

f32 = jnp.float32
bf16 = jnp.bfloat16
i32 = jnp.int32

D_MODEL = 1024
PAST_LEN = 8192
POOL_W = 512
POOL_WINDOWS = (2, 4, 8, 16)
POOL_GC = 128
POOL_BUF = 15
ATTN_W = 512
HEAD_DIM = 64
N_HEADS = 8
N_KV_HEADS = 2
GQA = 4
KV_W = 128
WINDOW = 128
NUM_BUCKETS = 32
MAX_EXACT = 16
REL_MAX_DIST = 128
N_EXPERTS = 256
N_EXPERT_GROUPS = 8
GROUP_SIZE = 32
TOPK_GROUPS = 4
TOP_K = 8
EXPERT_FF = 256
ROUTED_SCALE = 2.5
EXPERT_BLOCK = 128
EPS = 1e-6
NEG_INF = -1e30
QKV_W = POOL_W + ATTN_W + 2 * KV_W
QK_W = ATTN_W + KV_W
HIST = 16

SC_WORKERS = 32
SC_LANES = 16
GATE_ROW = 128


def _dot(a, b):
    return jnp.dot(a, b, preferred_element_type=f32)


def _dot_t(a, b):
    return lax.dot_general(a, b, (((1,), (1,)), ((), ())), preferred_element_type=f32)


def _split_bf16(a):
    hi = a.astype(bf16)
    lo = (a - hi.astype(f32)).astype(bf16)
    return hi, lo


ROW_RING = 6
W_RING = 4
HI_MASK = -65536


def _pack_pairs(a):
    h = a.shape[1] // 2
    hi = lax.bitcast_convert_type(a[:, :h].astype(bf16).astype(f32), i32)
    lo = lax.bitcast_convert_type(a[:, h:].astype(bf16).astype(f32), i32)
    return hi | lax.shift_right_logical(lo, 16)


def _unpack_pairs(w):
    hi = lax.bitcast_convert_type(w & HI_MASK, f32).astype(bf16)
    lo = lax.bitcast_convert_type(lax.shift_left(w, 16), f32).astype(bf16)
    return hi, lo


def _mod_norm(x, g, sc, sh):
    ms = jnp.mean(x * x, axis=-1, keepdims=True)
    y = x * lax.rsqrt(ms + EPS)
    return (y * g) * (1.0 + sc) + sh


def _ada_body(c_ref, w_ref, b_ref, o_ref):
    c = c_ref[...]
    a = (c * jax.nn.sigmoid(c)).astype(bf16)
    o_ref[...] = _dot(a, w_ref[...].astype(bf16)) + b_ref[...]


def _ada(c, w_ada, b_ada):
    n = c.shape[0]
    tn = 1024
    return pl.pallas_call(
        _ada_body,
        grid=(6 * D_MODEL // tn,),
        in_specs=[
            pl.BlockSpec((n, D_MODEL), lambda j: (0, 0)),
            pl.BlockSpec((D_MODEL, tn), lambda j: (0, j)),
            pl.BlockSpec((1, tn), lambda j: (0, j)),
        ],
        out_specs=pl.BlockSpec((n, tn), lambda j: (0, j)),
        out_shape=jax.ShapeDtypeStruct((n, 6 * D_MODEL), f32),
        name="ada_mod",
    )(c, w_ada, b_ada.reshape(1, -1))


def _relbias_body(table_ref, bucket_ref, o_ref):
    bucket = bucket_ref[...]
    for h in range(N_HEADS):
        acc = jnp.zeros(bucket.shape, f32)
        for b in range(NUM_BUCKETS):
            acc = jnp.where(bucket == b, table_ref[b, h], acc)
        o_ref[h] = acc


def _rel_buckets(dist):
    n = np.maximum(dist, 0)
    nf = np.maximum(n, 1).astype(np.float64)
    large = MAX_EXACT + (np.log(nf / MAX_EXACT) / math.log(REL_MAX_DIST / MAX_EXACT)
                         * (NUM_BUCKETS - MAX_EXACT)).astype(np.int32)
    return np.where(n < MAX_EXACT, n, np.minimum(large, NUM_BUCKETS - 1)).astype(np.int32)


def _relbias(table, dist):
    lq, lk = dist.shape
    return pl.pallas_call(
        _relbias_body,
        in_specs=[
            pl.BlockSpec(memory_space=pltpu.SMEM),
            pl.BlockSpec((lq, lk), lambda: (0, 0)),
        ],
        out_specs=pl.BlockSpec((N_HEADS, lq, lk), lambda: (0, 0, 0)),
        out_shape=jax.ShapeDtypeStruct((N_HEADS, lq, lk), f32),
        name="rel_bias",
    )(table, jnp.asarray(_rel_buckets(dist)))


def _qkv_from_h(h, w_ref, gqk_ref, bd_ref):
    u = _dot(h.astype(bf16), w_ref[...])
    qk = u[:, POOL_W:POOL_W + QK_W]
    y_hi, y_lo = _split_bf16(qk * qk)
    bd = bd_ref[...]
    ss = _dot(y_hi, bd) + _dot(y_lo, bd)
    qkn = (qk * lax.rsqrt(ss * (1.0 / HEAD_DIM) + EPS)) * gqk_ref[...]
    q = qkn[:, :ATTN_W] * (HEAD_DIM ** -0.5)
    k = qkn[:, ATTN_W:]
    v = u[:, POOL_W + QK_W:]
    return u[:, :POOL_W], q, k, v


def _inproj_prompt_body(x_ref, sh_ref, sc_ref, g_ref, w_ref, gqk_ref, bd_ref, wp_ref, ps_ref,
                        q_ref, k_ref, v_ref, po_ref, np_ref, kc_ref, vc_ref, hist, *, tl, nt):
    j = pl.program_id(1)
    h = _mod_norm(x_ref[0], g_ref[...], sc_ref[0], sh_ref[0])
    up, q, k, v = _qkv_from_h(h, w_ref, gqk_ref, bd_ref)
    q_ref[0] = q.astype(bf16)
    k_ref[0] = k.astype(bf16)
    v_ref[0] = v.astype(bf16)

    @pl.when(j == nt - 1)
    def _():
        kc_ref[0] = k[tl - WINDOW:, :]
        vc_ref[0] = v[tl - WINDOW:, :]

    @pl.when(j == 0)
    def _():
        hist[0:HIST, :] = jnp.zeros((HIST, POOL_W), f32)

    hist[HIST:HIST + tl, :] = up
    pos = j * tl + lax.broadcasted_iota(i32, (tl, 1), 0)
    for g, w in enumerate(POOL_WINDOWS):
        lanes = slice(g * POOL_GC, (g + 1) * POOL_GC)
        cur = up[:, lanes]
        acc = cur
        for s in range(1, w):
            acc = acc + hist[HIST - s:HIST - s + tl, lanes]
        cnt = jnp.minimum(w, pos + 1).astype(f32)
        d = acc / cnt - cur
        yg = _dot(d.astype(bf16), wp_ref[g]) * ps_ref[:, lanes]
        po_ref[0, :, lanes] = yg.astype(bf16)

    @pl.when(j == nt - 1)
    def _():
        np_ref[0] = hist[tl + 1:tl + HIST, :]

    hist[0:HIST, :] = hist[tl:tl + HIST, :]


def _inproj_prompt(x, mod3, g_attn, w_in, gqk, bd, w_pool, pool_scale, tl=512):
    B, L, D = x.shape
    nt = L // tl
    full = lambda shape: pl.BlockSpec(shape, lambda b, j: (0,) * len(shape))
    return pl.pallas_call(
        functools.partial(_inproj_prompt_body, tl=tl, nt=nt),
        grid=(B, nt),
        in_specs=[
            pl.BlockSpec((1, tl, D), lambda b, j: (b, j, 0)),
            pl.BlockSpec((1, 1, D), lambda b, j: (b, 0, 0)),
            pl.BlockSpec((1, 1, D), lambda b, j: (b, 0, 1)),
            full((1, D)),
            full((D, QKV_W)),
            full((1, QK_W)),
            full((QK_W, QK_W)),
            full((4, POOL_GC, POOL_GC)),
            full((1, POOL_W)),
        ],
        out_specs=[
            pl.BlockSpec((1, tl, ATTN_W), lambda b, j: (b, j, 0)),
            pl.BlockSpec((1, tl, KV_W), lambda b, j: (b, j, 0)),
            pl.BlockSpec((1, tl, KV_W), lambda b, j: (b, j, 0)),
            pl.BlockSpec((1, tl, POOL_W), lambda b, j: (b, j, 0)),
            pl.BlockSpec((1, POOL_BUF, POOL_W), lambda b, j: (b, 0, 0)),
            pl.BlockSpec((1, WINDOW, KV_W), lambda b, j: (b, 0, 0)),
            pl.BlockSpec((1, WINDOW, KV_W), lambda b, j: (b, 0, 0)),
        ],
        out_shape=[
            jax.ShapeDtypeStruct((B, L, ATTN_W), bf16),
            jax.ShapeDtypeStruct((B, L, KV_W), bf16),
            jax.ShapeDtypeStruct((B, L, KV_W), bf16),
            jax.ShapeDtypeStruct((B, L, POOL_W), bf16),
            jax.ShapeDtypeStruct((B, POOL_BUF, POOL_W), f32),
            jax.ShapeDtypeStruct((B, WINDOW, KV_W), f32),
            jax.ShapeDtypeStruct((B, WINDOW, KV_W), f32),
        ],
        scratch_shapes=[pltpu.VMEM((HIST + tl, POOL_W), f32)],
        compiler_params=pltpu.CompilerParams(dimension_semantics=("arbitrary", "arbitrary")),
        name="inproj_prompt",
    )(x, mod3, mod3, g_attn, w_in, gqk, bd, w_pool, pool_scale)


def _inproj_sample_body(x_ref, sh_ref, sc_ref, g_ref, w_ref, gqk_ref, bd_ref, wp_ref, ps_ref, st_ref,
                        q_ref, k_ref, v_ref, po_ref, np_ref, ext, *, bt, ls, pos0):
    n = bt * ls
    h3 = _mod_norm(x_ref[...], g_ref[...][None], sc_ref[...], sh_ref[...])
    up, q, k, v = _qkv_from_h(h3.reshape(n, D_MODEL), w_ref, gqk_ref, bd_ref)
    q_ref[...] = q.astype(bf16)
    k_ref[...] = k
    v_ref[...] = v

    ext[:, 1:HIST, :] = st_ref[...]
    ext[:, HIST:HIST + ls, :] = up.reshape(bt, ls, POOL_W)
    pos = pos0 + lax.broadcasted_iota(i32, (1, ls, 1), 1)
    for g, w in enumerate(POOL_WINDOWS):
        lanes = slice(g * POOL_GC, (g + 1) * POOL_GC)
        cur = ext[:, HIST:HIST + ls, lanes]
        acc = cur
        for s in range(1, w):
            acc = acc + ext[:, HIST - s:HIST - s + ls, lanes]
        cnt = jnp.minimum(w, pos + 1).astype(f32)
        d = (acc / cnt - cur).reshape(n, POOL_GC)
        yg = _dot(d.astype(bf16), wp_ref[g]) * ps_ref[:, lanes]
        po_ref[:, lanes] = yg.astype(bf16)
    np_ref[...] = ext[:, ls + 1:ls + HIST, :]


def _inproj_sample(x, mod3, g_attn, w_in, gqk, bd, w_pool, pool_scale, state, pos0, bt=64):
    B, ls, D = x.shape
    n = bt * ls
    full = lambda shape: pl.BlockSpec(shape, lambda i: (0,) * len(shape))
    return pl.pallas_call(
        functools.partial(_inproj_sample_body, bt=bt, ls=ls, pos0=pos0),
        grid=(B // bt,),
        in_specs=[
            pl.BlockSpec((bt, ls, D), lambda i: (i, 0, 0)),
            pl.BlockSpec((bt, 1, D), lambda i: (i, 0, 0)),
            pl.BlockSpec((bt, 1, D), lambda i: (i, 0, 1)),
            full((1, D)),
            full((D, QKV_W)),
            full((1, QK_W)),
            full((QK_W, QK_W)),
            full((4, POOL_GC, POOL_GC)),
            full((1, POOL_W)),
            pl.BlockSpec((bt, POOL_BUF, POOL_W), lambda i: (i, 0, 0)),
        ],
        out_specs=[
            pl.BlockSpec((n, ATTN_W), lambda i: (i, 0)),
            pl.BlockSpec((n, KV_W), lambda i: (i, 0)),
            pl.BlockSpec((n, KV_W), lambda i: (i, 0)),
            pl.BlockSpec((n, POOL_W), lambda i: (i, 0)),
            pl.BlockSpec((bt, POOL_BUF, POOL_W), lambda i: (i, 0, 0)),
        ],
        out_shape=[
            jax.ShapeDtypeStruct((B * ls, ATTN_W), bf16),
            jax.ShapeDtypeStruct((B * ls, KV_W), f32),
            jax.ShapeDtypeStruct((B * ls, KV_W), f32),
            jax.ShapeDtypeStruct((B * ls, POOL_W), bf16),
            jax.ShapeDtypeStruct((B, POOL_BUF, POOL_W), f32),
        ],
        scratch_shapes=[pltpu.VMEM((bt, HIST + ls, POOL_W), f32)],
        name="inproj_sample",
    )(x, mod3, mod3, g_attn, w_in, gqk, bd, w_pool, pool_scale, state)


def _softmax_sink(parts, sink):
    m = sink
    for s in parts:
        m = jnp.maximum(m, jnp.max(s, axis=-1, keepdims=True))
    ps = [jnp.exp(s - m) for s in parts]
    denom = jnp.exp(sink - m)
    for p in ps:
        denom = denom + jnp.sum(p, axis=-1, keepdims=True)
    inv = 1.0 / denom
    return [(p * inv).astype(bf16) for p in ps]


def _attn_prompt_body(sinks_ref, q_ref, kp_ref, kc_ref, vp_ref, vc_ref, bias_ref, mask_ref, o_ref):
    j = pl.program_id(1)
    nk = 2 * WINDOW
    qw = GQA * HEAD_DIM
    nq = q_ref.shape[1] // WINDOW
    kall = jnp.concatenate([kp_ref[0], kc_ref[0]], axis=0)
    vall = jnp.concatenate([vp_ref[0], vc_ref[0]], axis=0)
    lane_group = lax.broadcasted_iota(i32, (nk, 2 * KV_W), 1) // HEAD_DIM
    first_has_prev = jnp.minimum(j, 1)

    def spread(t):
        t0 = jnp.concatenate([t, t], axis=1)
        return t0, pltpu.roll(t0, HEAD_DIM, 1)

    def blockdiag(t01, kv):
        t0, t1 = t01
        return jnp.concatenate(
            [jnp.where(lane_group == g, t0 if g % 2 == kv else t1, jnp.zeros_like(t0)) for g in range(GQA)],
            axis=0)

    chains = [(qb, kv) for qb in range(nq) for kv in range(N_KV_HEADS)]
    ksp = [spread(kall[qb * WINDOW:qb * WINDOW + nk]) for qb in range(nq)]
    vsp = [spread(vall[qb * WINDOW:qb * WINDOW + nk]) for qb in range(nq)]
    valid = [mask_ref[first_has_prev] > 0.5] + [mask_ref[1] > 0.5] * (nq - 1)
    s = [_dot_t(q_ref[0, qb * WINDOW:(qb + 1) * WINDOW, kv * qw:(kv + 1) * qw], blockdiag(ksp[qb], kv))
         for qb, kv in chains]
    s = [jnp.where(valid[qb], s[c] + bias_ref[kv], NEG_INF) for c, (qb, kv) in enumerate(chains)]
    p = [jnp.concatenate([_softmax_sink([s[c][:, g * nk:(g + 1) * nk]], sinks_ref[kv * GQA + g])[0]
                          for g in range(GQA)], axis=1) for c, (qb, kv) in enumerate(chains)]
    o = [_dot(p[c], blockdiag(vsp[qb], kv)) for c, (qb, kv) in enumerate(chains)]
    for c, (qb, kv) in enumerate(chains):
        o_ref[0, qb * WINDOW:(qb + 1) * WINDOW, kv * qw:(kv + 1) * qw] = o[c].astype(bf16)


def _attn_prompt(q, k, v, bias, sinks):
    B, L, _ = q.shape
    nq = 2
    nb = L // (nq * WINDOW)
    cur = lambda b, j: (b, j, 0)
    prev = lambda b, j: (b, jnp.maximum(nq * j - 1, 0), 0)
    qi = np.arange(WINDOW)[:, None]
    kc = np.arange(2 * WINDOW)[None, :]
    own = (kc >= WINDOW) & (kc - WINDOW <= qi)
    prv = (kc < WINDOW) & (kc > qi)
    mask = np.stack([np.tile(own, (1, GQA)), np.tile(own | prv, (1, GQA))]).astype(np.float32)
    return pl.pallas_call(
        _attn_prompt_body,
        grid=(B, nb),
        in_specs=[
            pl.BlockSpec(memory_space=pltpu.SMEM),
            pl.BlockSpec((1, nq * WINDOW, ATTN_W), cur),
            pl.BlockSpec((1, WINDOW, KV_W), prev),
            pl.BlockSpec((1, nq * WINDOW, KV_W), cur),
            pl.BlockSpec((1, WINDOW, KV_W), prev),
            pl.BlockSpec((1, nq * WINDOW, KV_W), cur),
            pl.BlockSpec((N_KV_HEADS, WINDOW, GQA * 2 * WINDOW), lambda b, j: (0, 0, 0)),
            pl.BlockSpec((2, WINDOW, GQA * 2 * WINDOW), lambda b, j: (0, 0, 0)),
        ],
        out_specs=pl.BlockSpec((1, nq * WINDOW, ATTN_W), cur),
        out_shape=jax.ShapeDtypeStruct((B, L, ATTN_W), bf16),
        name="attn_prompt",
    )(sinks, q, k, k, v, v, bias, jnp.asarray(mask))


def _attn_sample_body(q_ref, kb_ref, vb_ref, kn_ref, vn_ref, bb_ref, bn_ref, sink_ref,
                      o_ref, nk_ref, nv_ref, *, bb, ls):
    W = kb_ref.shape[1]
    rows = GQA * ls
    qi = lax.broadcasted_iota(i32, (rows, W), 0) % ls
    kj = lax.broadcasted_iota(i32, (rows, W), 1)
    valid_buf = kj > qi
    qi2 = lax.broadcasted_iota(i32, (rows, ls), 0) % ls
    kj2 = lax.broadcasted_iota(i32, (rows, ls), 1)
    valid_new = kj2 <= qi2

    nbat = 4
    ks = [slice(kv * HEAD_DIM, (kv + 1) * HEAD_DIM) for kv in range(N_KV_HEADS)]

    def group(i, carry):
        bs = [i * nbat + u for u in range(nbat)]
        chains = [(u, kv) for u in range(nbat) for kv in range(N_KV_HEADS)]
        qb = [q_ref[b] for b in bs]
        kbuf = [kb_ref[b] for b in bs]
        vbuf = [vb_ref[b] for b in bs]
        knew = [kn_ref[b] for b in bs]
        vnew = [vn_ref[b] for b in bs]
        qg = [jnp.concatenate([qb[u][:, (kv * GQA + g) * HEAD_DIM:(kv * GQA + g + 1) * HEAD_DIM]
                               for g in range(GQA)], axis=0) for u, kv in chains]
        s_buf = [_dot_t(qg[c], kbuf[u][:, ks[kv]].astype(bf16)) for c, (u, kv) in enumerate(chains)]
        s_new = [_dot_t(qg[c], knew[u][:, ks[kv]].astype(bf16)) for c, (u, kv) in enumerate(chains)]
        s_buf = [jnp.where(valid_buf, s_buf[c] + bb_ref[kv], NEG_INF) for c, (u, kv) in enumerate(chains)]
        s_new = [jnp.where(valid_new, s_new[c] + bn_ref[kv], NEG_INF) for c, (u, kv) in enumerate(chains)]
        probs = [_softmax_sink([s_buf[c], s_new[c]], sink_ref[kv]) for c, (u, kv) in enumerate(chains)]
        outs = [_dot(probs[c][0], vbuf[u][:, ks[kv]].astype(bf16)) + _dot(probs[c][1], vnew[u][:, ks[kv]].astype(bf16))
                for c, (u, kv) in enumerate(chains)]
        for u, b in enumerate(bs):
            heads = [outs[u * N_KV_HEADS + kv][g * ls:(g + 1) * ls] for kv in range(N_KV_HEADS) for g in range(GQA)]
            o_ref[b] = jnp.concatenate(heads, axis=-1).astype(bf16)
            nk_ref[b, 0:W - ls, :] = kbuf[u][ls:, :]
            nk_ref[b, W - ls:W, :] = knew[u]
            nv_ref[b, 0:W - ls, :] = vbuf[u][ls:, :]
            nv_ref[b, W - ls:W, :] = vnew[u]
        return carry

    lax.fori_loop(0, bb // nbat, group, 0)


def _attn_sample(q, k_buf, v_buf, k_new, v_new, bias_buf, bias_new, sink_col, bb=16):
    B, ls, _ = q.shape
    W = k_buf.shape[1]
    rows = GQA * ls
    blk = lambda shape: pl.BlockSpec(shape, lambda i: (i, 0, 0))
    full = lambda shape: pl.BlockSpec(shape, lambda i: (0, 0, 0))
    return pl.pallas_call(
        functools.partial(_attn_sample_body, bb=bb, ls=ls),
        grid=(B // bb,),
        in_specs=[
            blk((bb, ls, ATTN_W)),
            blk((bb, W, KV_W)),
            blk((bb, W, KV_W)),
            blk((bb, ls, KV_W)),
            blk((bb, ls, KV_W)),
            full((N_KV_HEADS, rows, W)),
            full((N_KV_HEADS, rows, ls)),
            full((N_KV_HEADS, rows, 1)),
        ],
        out_specs=[blk((bb, ls, ATTN_W)), blk((bb, W, KV_W)), blk((bb, W, KV_W))],
        out_shape=[
            jax.ShapeDtypeStruct((B, ls, ATTN_W), bf16),
            jax.ShapeDtypeStruct((B, W, KV_W), f32),
            jax.ShapeDtypeStruct((B, W, KV_W), f32),
        ],
        name="attn_sample",
    )(q, k_buf, v_buf, k_new, v_new, bias_buf, bias_new, sink_col)


def _outproj_core(po, at, x, gt, sc, sh, g_ref, wo_ref, wrh_ref, wrl_ref):
    mixo = _dot(po, wo_ref[0:POOL_W, :]) + _dot(at, wo_ref[POOL_W:, :])
    x1 = x + gt * mixo.reshape(x.shape)
    h2 = _mod_norm(x1, g_ref[...].reshape((1,) * (x.ndim - 1) + (D_MODEL,)), sc, sh).reshape(-1, D_MODEL)
    h_hi, h_lo = _split_bf16(h2)
    wh = wrh_ref[...]
    logits = _dot_t(wh, h_hi) + (_dot_t(wh, h_lo) + _dot_t(wrl_ref[...], h_hi))
    return x1, _pack_pairs(h2), logits


def _outproj_prompt_body(po_ref, at_ref, x_ref, gt_ref, sc_ref, sh_ref, g_ref, wo_ref, wrh_ref, wrl_ref,
                         x1_ref, h2_ref, lg_ref):
    x1, h2p, logits = _outproj_core(po_ref[0], at_ref[0], x_ref[0], gt_ref[0], sc_ref[0], sh_ref[0],
                                    g_ref, wo_ref, wrh_ref, wrl_ref)
    x1_ref[...] = x1
    h2_ref[...] = h2p
    lg_ref[...] = logits


def _outproj_sample_body(po_ref, at_ref, x_ref, gt_ref, sc_ref, sh_ref, g_ref, wo_ref, wrh_ref, wrl_ref,
                         x1_ref, h2_ref, lg_ref):
    x1, h2p, logits = _outproj_core(po_ref[...], at_ref[...], x_ref[...], gt_ref[...], sc_ref[...], sh_ref[...],
                                    g_ref, wo_ref, wrh_ref, wrl_ref)
    x1_ref[...] = x1.reshape(-1, D_MODEL)
    h2_ref[...] = h2p
    lg_ref[...] = logits


def _outproj_prompt(po, at, x, mod3, g_ffn, w_out, wr_hi, wr_lo, tm=512):
    B, L, D = x.shape
    nt = L // tm
    n_tok = B * L
    full = lambda shape: pl.BlockSpec(shape, lambda b, j: (0,) * len(shape))
    modspec = lambda c: pl.BlockSpec((1, 1, D), lambda b, j: (b, 0, c))
    return pl.pallas_call(
        _outproj_prompt_body,
        grid=(B, nt),
        in_specs=[
            pl.BlockSpec((1, tm, POOL_W), lambda b, j: (b, j, 0)),
            pl.BlockSpec((1, tm, ATTN_W), lambda b, j: (b, j, 0)),
            pl.BlockSpec((1, tm, D), lambda b, j: (b, j, 0)),
            modspec(2), modspec(4), modspec(3),
            full((1, D)), full((D, D)), full((N_EXPERTS, D)), full((N_EXPERTS, D)),
        ],
        out_specs=[
            pl.BlockSpec((tm, D), lambda b, j: (b * nt + j, 0)),
            pl.BlockSpec((tm, D // 2), lambda b, j: (b * nt + j, 0)),
            pl.BlockSpec((N_EXPERTS, tm), lambda b, j: (0, b * nt + j)),
        ],
        out_shape=[
            jax.ShapeDtypeStruct((n_tok, D), f32),
            jax.ShapeDtypeStruct((n_tok, D // 2), i32),
            jax.ShapeDtypeStruct((N_EXPERTS, n_tok), f32),
        ],
        name="outproj_prompt",
    )(po, at, x, mod3, mod3, mod3, g_ffn, w_out, wr_hi, wr_lo)


def _outproj_sample(po, at, x, mod3, g_ffn, w_out, wr_hi, wr_lo, bt=64):
    B, ls, D = x.shape
    n = bt * ls
    full = lambda shape: pl.BlockSpec(shape, lambda i: (0,) * len(shape))
    modspec = lambda c: pl.BlockSpec((bt, 1, D), lambda i: (i, 0, c))
    return pl.pallas_call(
        _outproj_sample_body,
        grid=(B // bt,),
        in_specs=[
            pl.BlockSpec((n, POOL_W), lambda i: (i, 0)),
            pl.BlockSpec((n, ATTN_W), lambda i: (i, 0)),
            pl.BlockSpec((bt, ls, D), lambda i: (i, 0, 0)),
            modspec(2), modspec(4), modspec(3),
            full((1, D)), full((D, D)), full((N_EXPERTS, D)), full((N_EXPERTS, D)),
        ],
        out_specs=[
            pl.BlockSpec((n, D), lambda i: (i, 0)),
            pl.BlockSpec((n, D // 2), lambda i: (i, 0)),
            pl.BlockSpec((N_EXPERTS, n), lambda i: (0, i)),
        ],
        out_shape=[
            jax.ShapeDtypeStruct((B * ls, D), f32),
            jax.ShapeDtypeStruct((B * ls, D // 2), i32),
            jax.ShapeDtypeStruct((N_EXPERTS, B * ls), f32),
        ],
        name="outproj_sample",
    )(po, at, x, mod3, mod3, mod3, g_ffn, w_out, wr_hi, wr_lo)


def _route_body(lg_ref, rb_ref, tri_ref, idx_ref, rank_ref, gate_ref, cnt_ref, carry, *, tr, nsteps):
    step = pl.program_id(0)

    @pl.when(step == 0)
    def _():
        carry[...] = jnp.zeros(carry.shape, f32)

    s = jax.nn.sigmoid(lg_ref[...])
    sb = s + rb_ref[...]
    e_iota = lax.broadcasted_iota(i32, (N_EXPERTS, tr), 0)
    g_iota = lax.broadcasted_iota(i32, (GROUP_SIZE, tr), 0)

    gscore = []
    for g in range(N_EXPERT_GROUPS):
        v = sb[g * GROUP_SIZE:(g + 1) * GROUP_SIZE]
        m1 = jnp.max(v, axis=0, keepdims=True)
        i1 = jnp.min(jnp.where(v == m1, g_iota, GROUP_SIZE), axis=0, keepdims=True)
        m2 = jnp.max(jnp.where(g_iota == i1, -jnp.inf, v), axis=0, keepdims=True)
        gscore.append(m1 + m2)
    parts = []
    for g in range(N_EXPERT_GROUPS):
        beaten = jnp.zeros((1, tr), i32)
        for g2 in range(N_EXPERT_GROUPS):
            if g2 == g:
                continue
            ahead = gscore[g2] > gscore[g]
            if g2 < g:
                ahead = ahead | (gscore[g2] == gscore[g])
            beaten = beaten + ahead.astype(i32)
        keep = beaten < TOPK_GROUPS
        parts.append(jnp.where(keep, sb[g * GROUP_SIZE:(g + 1) * GROUP_SIZE], NEG_INF))
    cur = jnp.concatenate(parts, axis=0)

    sel = jnp.zeros((N_EXPERTS, tr), f32)
    idxs, svals = [], []
    for _ in range(TOP_K):
        m = jnp.max(cur, axis=0, keepdims=True)
        ik = jnp.min(jnp.where(cur == m, e_iota, N_EXPERTS), axis=0, keepdims=True)
        hit = e_iota == ik
        svals.append(jnp.sum(jnp.where(hit, s, 0.0), axis=0, keepdims=True))
        cur = jnp.where(hit, -jnp.inf, cur)
        sel = jnp.where(hit, 1.0, sel)
        idxs.append(ik)
    ssum = svals[0]
    for sv in svals[1:]:
        ssum = ssum + sv
    gate_ref[...] = jnp.concatenate([sv / ssum * ROUTED_SCALE for sv in svals], axis=0)
    idx_ref[...] = jnp.concatenate(idxs, axis=0)

    before = carry[...] + _dot(sel.astype(bf16), tri_ref[...])
    ranks = [jnp.sum(jnp.where(e_iota == ik, before, 0.0), axis=0, keepdims=True) for ik in idxs]
    rank_ref[...] = jnp.concatenate(ranks, axis=0).astype(i32)
    carry[...] = carry[...] + jnp.sum(sel, axis=1, keepdims=True)

    @pl.when(step == nsteps - 1)
    def _():
        cnt_ref[...] = carry[...]


def _route(logits_t, router_bias, tr=512):
    E, T = logits_t.shape
    nsteps = T // tr
    tri = jnp.asarray(np.triu(np.ones((tr, tr), np.float32), 1), bf16)
    return pl.pallas_call(
        functools.partial(_route_body, tr=tr, nsteps=nsteps),
        grid=(nsteps,),
        in_specs=[
            pl.BlockSpec((E, tr), lambda i: (0, i)),
            pl.BlockSpec((E, 1), lambda i: (0, 0)),
            pl.BlockSpec((tr, tr), lambda i: (0, 0)),
        ],
        out_specs=[
            pl.BlockSpec((TOP_K, tr), lambda i: (0, i)),
            pl.BlockSpec((TOP_K, tr), lambda i: (0, i)),
            pl.BlockSpec((TOP_K, tr), lambda i: (0, i)),
            pl.BlockSpec((E, 1), lambda i: (0, 0)),
        ],
        out_shape=[
            jax.ShapeDtypeStruct((TOP_K, T), i32),
            jax.ShapeDtypeStruct((TOP_K, T), i32),
            jax.ShapeDtypeStruct((TOP_K, T), f32),
            jax.ShapeDtypeStruct((E, 1), f32),
        ],
        scratch_shapes=[pltpu.VMEM((E, 1), f32)],
        compiler_params=pltpu.CompilerParams(dimension_semantics=("arbitrary",)),
        name="route",
    )(logits_t, router_bias.reshape(E, 1), tri)


def _dest_body(idx_ref, rank_ref, ps_ref, dest_ref, *, tr):
    e_iota = lax.broadcasted_iota(i32, (N_EXPERTS, tr), 0)
    start = ps_ref[...]
    rows = []
    for k in range(TOP_K):
        hit = e_iota == idx_ref[k:k + 1, :]
        rows.append(jnp.sum(jnp.where(hit, start, 0.0), axis=0, keepdims=True))
    dest_ref[...] = jnp.concatenate(rows, axis=0).astype(i32) + rank_ref[...]


def _dest_rows(idx, rank, pad_start, tr=512):
    K, T = idx.shape
    blk = pl.BlockSpec((K, tr), lambda i: (0, i))
    return pl.pallas_call(
        functools.partial(_dest_body, tr=tr),
        grid=(T // tr,),
        in_specs=[blk, blk, pl.BlockSpec((N_EXPERTS, 1), lambda i: (0, 0))],
        out_specs=blk,
        out_shape=jax.ShapeDtypeStruct((K, T), i32),
        name="dest_rows",
    )(idx, rank, pad_start.astype(f32).reshape(N_EXPERTS, 1))


def _sc_mesh():
    return plsc.VectorSubcoreMesh(core_axis_name="c", subcore_axis_name="s")


def _sc_worker_id():
    return lax.axis_index("s") * 2 + lax.axis_index("c")


def _dispatch(h2_a, h2_b, dest, gate, n_rows, chunk=32):
    ta, Dw = h2_a.shape
    T = ta + h2_b.shape[0]
    per_worker = T // SC_WORKERS
    nchunk = per_worker // chunk
    assert per_worker * SC_WORKERS == T and nchunk * chunk == per_worker and ta % chunk == 0

    nrow_idx = nchunk * TOP_K

    @functools.partial(
        pl.kernel, mesh=_sc_mesh(),
        out_type=[jax.ShapeDtypeStruct((n_rows, Dw), i32), jax.ShapeDtypeStruct((n_rows, GATE_ROW), f32)],
        scratch_types=[pltpu.VMEM((nrow_idx, chunk), i32), pltpu.VMEM((nrow_idx, chunk), f32),
                       pltpu.VMEM((chunk, Dw), i32), pltpu.VMEM((TOP_K, chunk, GATE_ROW), f32),
                       pltpu.SemaphoreType.DMA],
        compiler_params=pltpu.CompilerParams(needs_layout_passes=False),
        name="moe_dispatch",
    )
    def body(ha_hbm, hb_hbm, dest_hbm, gate_hbm, xs_hbm, gs_hbm, idx_v, gate_v, rows_v, grow_v, sem):
        wid = _sc_worker_id()
        base = wid * per_worker
        pltpu.sync_copy(dest_hbm.at[wid], idx_v)
        pltpu.sync_copy(gate_hbm.at[wid], gate_v)
        zero = jnp.zeros((SC_LANES,), f32)
        for k in range(TOP_K):
            @pl.loop(0, chunk)
            def _(t):
                for j in range(GATE_ROW // SC_LANES):
                    grow_v[k, t, pl.ds(j * SC_LANES, SC_LANES)] = zero

        @pl.loop(0, nchunk)
        def _(ci):
            t0 = base + ci * chunk

            @pl.when(t0 < ta)
            def _():
                pltpu.sync_copy(ha_hbm.at[pl.ds(t0, chunk)], rows_v)

            @pl.when(t0 >= ta)
            def _():
                pltpu.sync_copy(hb_hbm.at[pl.ds(t0 - ta, chunk)], rows_v)

            for k in range(TOP_K):
                @pl.loop(0, chunk)
                def _(t):
                    row = jnp.zeros((SC_LANES,), i32) + (ci * TOP_K + k)
                    grow_v[k, t, pl.ds(0, SC_LANES)] = plsc.load_gather(
                        gate_v, [row, jnp.zeros((SC_LANES,), i32) + t])

            cps = []
            for k in range(TOP_K):
                idx = idx_v.at[ci * TOP_K + k]
                cps.append(pltpu.make_async_copy(rows_v, xs_hbm.at[idx], sem))
                cps.append(pltpu.make_async_copy(grow_v.at[k], gs_hbm.at[idx], sem))
            for cp in cps:
                cp.start()
            for cp in cps:
                cp.wait()

    def per_worker_rows(a):
        return a.reshape(TOP_K, SC_WORKERS, nchunk, chunk).transpose(1, 2, 0, 3).reshape(
            SC_WORKERS, nrow_idx, chunk)

    return body(h2_a, h2_b, per_worker_rows(dest), per_worker_rows(gate))


def _combine(ys, dest, chunk=8):
    T = dest.shape[1]
    Dw = ys.shape[1]
    per_worker = T // SC_WORKERS
    nchunk = per_worker // chunk
    assert per_worker * SC_WORKERS == T and nchunk * chunk == per_worker and nchunk % 2 == 0

    @functools.partial(
        pl.kernel, mesh=_sc_mesh(),
        out_type=jax.ShapeDtypeStruct((T, 2 * Dw), f32),
        scratch_types=[
            pltpu.VMEM((TOP_K * per_worker,), i32),
            pltpu.VMEM((2, TOP_K, chunk, Dw), i32),
            pltpu.VMEM((chunk, 2 * Dw), f32),
            pltpu.SemaphoreType.DMA((2,)),
        ],
        compiler_params=pltpu.CompilerParams(needs_layout_passes=False),
        name="moe_combine",
    )
    def body(ys_hbm, dest_hbm, out_hbm, idx_v, buf, out_v, sems):
        base = _sc_worker_id() * per_worker
        pltpu.sync_copy(dest_hbm.at[pl.ds(_sc_worker_id() * (TOP_K * per_worker), TOP_K * per_worker)], idx_v)

        def gather(ci, slot):
            return [pltpu.make_async_copy(ys_hbm.at[idx_v.at[pl.ds(k * per_worker + ci * chunk, chunk)]],
                                          buf.at[slot, k], sems.at[slot]) for k in range(TOP_K)]

        for cp in gather(0, 0):
            cp.start()

        @pl.loop(0, nchunk, step=2)
        def _(c0):
            for slot in range(2):
                ci = c0 + slot

                @pl.when(ci + 1 < nchunk)
                def _():
                    for cp in gather(ci + 1, 1 - slot):
                        cp.start()

                for cp in gather(ci, slot):
                    cp.wait()

                @pl.loop(0, chunk)
                def _(t):
                    @plsc.parallel_loop(0, Dw // SC_LANES, unroll=4)
                    def _(j):
                        sl = pl.ds(j * SC_LANES, SC_LANES)
                        w = buf[slot, 0, t, sl]
                        hi = plsc.bitcast(w & HI_MASK, f32)
                        lo = plsc.bitcast(lax.shift_left(w, 16), f32)
                        for k in range(1, TOP_K):
                            w = buf[slot, k, t, sl]
                            hi = hi + plsc.bitcast(w & HI_MASK, f32)
                            lo = lo + plsc.bitcast(lax.shift_left(w, 16), f32)
                        out_v[t, sl] = hi
                        out_v[t, pl.ds(Dw + j * SC_LANES, SC_LANES)] = lo

                pltpu.sync_copy(out_v, out_hbm.at[pl.ds(base + ci * chunk, chunk)])

    dest_w = dest.reshape(TOP_K, SC_WORKERS, per_worker).transpose(1, 0, 2).reshape(-1)
    return body(ys, dest_w)


def _gmm_body(blk_e_ref, blk_rows_ref, nv_ref, xs_hbm, gs_hbm, w1_hbm, w3_hbm, w2_hbm, ys_hbm,
              xbuf, gbuf, ybuf, w1f, w3f, w2f, w13_b, w2_b, xsem, gsem, ysem, wsem):
    nv = nv_ref[0]
    nb = blk_e_ref.shape[0]
    half = D_MODEL // 2
    RB = EXPERT_BLOCK

    def expert_of(blk):
        return blk_e_ref[jnp.minimum(blk, nb - 1)]

    def next_expert_block(blk):
        e0 = expert_of(blk)
        return lax.while_loop(lambda i: (i < nv) & (expert_of(i) == e0), lambda i: i + 1, blk + 1)

    def start_weights(blk, ordinal):
        @pl.when(blk < nv)
        def _():
            for cp in weight_copies(expert_of(blk), lax.rem(ordinal, W_RING)):
                cp.start()

    def row_copies(b, slot):
        r0 = pl.multiple_of(b * RB, RB)
        return (pltpu.make_async_copy(xs_hbm.at[pl.ds(r0, RB)], xbuf.at[slot], xsem.at[slot]),
                pltpu.make_async_copy(gs_hbm.at[pl.ds(r0, RB)], gbuf.at[slot], gsem.at[slot]))

    def out_copy(b, slot):
        r0 = pl.multiple_of(b * RB, RB)
        return pltpu.make_async_copy(ybuf.at[slot], ys_hbm.at[pl.ds(r0, RB)], ysem.at[slot])

    def weight_copies(e, ws):
        return (pltpu.make_async_copy(w1_hbm.at[e], w1f.at[ws], wsem.at[ws, 0]),
                pltpu.make_async_copy(w3_hbm.at[e], w3f.at[ws], wsem.at[ws, 1]),
                pltpu.make_async_copy(w2_hbm.at[e], w2f.at[ws], wsem.at[ws, 2]))

    blk = jnp.int32(0)
    for n in range(W_RING - 1):
        start_weights(blk, n)
        blk = next_expert_block(blk)
    for i in range(ROW_RING - 1):
        @pl.when(i < nv)
        def _():
            for cp in row_copies(i, i):
                cp.start()

    def step(b, ordinal_prev):
        slot = lax.rem(b, ROW_RING)
        e = blk_e_ref[b]
        first = (b == 0) | (e != blk_e_ref[jnp.maximum(b - 1, 0)])
        ordinal = jnp.where(first & (b > 0), ordinal_prev + 1, ordinal_prev)

        ahead = b + ROW_RING - 1

        @pl.when(ahead < nv)
        def _():
            for cp in row_copies(ahead, lax.rem(ahead, ROW_RING)):
                cp.start()

        @pl.when(first)
        def _():
            ws = lax.rem(ordinal, W_RING)
            for cp in weight_copies(e, ws):
                cp.wait()
            w13_b[:, 0:EXPERT_FF] = w1f[ws].astype(bf16)
            w13_b[:, EXPERT_FF:] = w3f[ws].astype(bf16)
            w2_b[...] = w2f[ws].astype(bf16)
            blk = b
            for _ in range(W_RING - 1):
                blk = next_expert_block(blk)
            start_weights(blk, ordinal + W_RING - 1)

        for cp in row_copies(b, slot):
            cp.wait()
        valid = lax.broadcasted_iota(i32, (RB, 1), 0) < blk_rows_ref[b]
        x_hi, x_lo = _unpack_pairs(jnp.where(valid, xbuf[slot], 0))
        ac = _dot(x_hi, w13_b[0:half, :]) + _dot(x_lo, w13_b[half:, :])
        a = ac[:, :EXPERT_FF]
        c = ac[:, EXPERT_FF:]
        hmid = (a * jax.nn.sigmoid(a)) * c
        g = jnp.where(valid, gbuf[slot][:, 0:1], 0.0)
        y = _pack_pairs(_dot(hmid.astype(bf16), w2_b[...]) * g)

        @pl.when(b >= ROW_RING)
        def _():
            out_copy(b - ROW_RING, slot).wait()

        ybuf[slot] = y
        out_copy(b, slot).start()
        return ordinal

    lax.fori_loop(0, nv, step, 0)

    for i in range(1, ROW_RING + 1):
        @pl.when(nv >= i)
        def _():
            out_copy(nv - i, lax.rem(nv - i, ROW_RING)).wait()


def _gmm(xs, gs, w1, w3, w2, blk_e, blk_rows, n_valid):
    n_rows, Dw = xs.shape
    D = 2 * Dw
    RB = EXPERT_BLOCK
    hbm = pl.BlockSpec(memory_space=pl.ANY)
    return pl.pallas_call(
        _gmm_body,
        grid_spec=pltpu.PrefetchScalarGridSpec(
            num_scalar_prefetch=3,
            grid=(1,),
            in_specs=[hbm, hbm, hbm, hbm, hbm],
            out_specs=hbm,
            scratch_shapes=[
                pltpu.VMEM((ROW_RING, RB, Dw), i32), pltpu.VMEM((ROW_RING, RB, GATE_ROW), f32),
                pltpu.VMEM((ROW_RING, RB, Dw), i32),
                pltpu.VMEM((W_RING, D, EXPERT_FF), f32), pltpu.VMEM((W_RING, D, EXPERT_FF), f32),
                pltpu.VMEM((W_RING, EXPERT_FF, D), f32),
                pltpu.VMEM((D, 2 * EXPERT_FF), bf16), pltpu.VMEM((EXPERT_FF, D), bf16),
                pltpu.SemaphoreType.DMA((ROW_RING,)), pltpu.SemaphoreType.DMA((ROW_RING,)),
                pltpu.SemaphoreType.DMA((ROW_RING,)), pltpu.SemaphoreType.DMA((W_RING, 3)),
            ],
        ),
        out_shape=jax.ShapeDtypeStruct((n_rows, Dw), i32),
        compiler_params=pltpu.CompilerParams(dimension_semantics=("arbitrary",)),
        name="moe_gmm",
    )(blk_e, blk_rows, n_valid, xs, gs, w1, w3, w2)


def _final_core(x1, h2p, comb, gt, ws1_ref, ws3_ref, ws2_ref):
    half = D_MODEL // 2
    h_hi, h_lo = _unpack_pairs(h2p)
    a = _dot(h_hi, ws1_ref[0:half, :]) + _dot(h_lo, ws1_ref[half:, :])
    c = _dot(h_hi, ws3_ref[0:half, :]) + _dot(h_lo, ws3_ref[half:, :])
    shared = _dot(((a * jax.nn.sigmoid(a)) * c).astype(bf16), ws2_ref[...])
    return x1, comb + shared, gt


def _final_prompt_body(x1_ref, h2_ref, cb_ref, gt_ref, ws1_ref, ws3_ref, ws2_ref, y_ref):
    x1, ffn, gt = _final_core(x1_ref[...], h2_ref[...], cb_ref[...], gt_ref[0], ws1_ref, ws3_ref, ws2_ref)
    y_ref[0] = x1 + gt * ffn


def _final_sample_body(x1_ref, h2_ref, cb_ref, gt_ref, ws1_ref, ws3_ref, ws2_ref, y_ref):
    x1, ffn, gt = _final_core(x1_ref[...], h2_ref[...], cb_ref[...], gt_ref[...], ws1_ref, ws3_ref, ws2_ref)
    shp = y_ref.shape
    y_ref[...] = x1.reshape(shp) + gt * ffn.reshape(shp)


def _final_prompt(x1, h2, comb, mod3, ws1, ws3, ws2, B, L, tm=512):
    D = D_MODEL
    nt = L // tm
    full = lambda shape: pl.BlockSpec(shape, lambda b, j: (0,) * len(shape))
    rows = pl.BlockSpec((tm, D), lambda b, j: (b * nt + j, 0))
    words = pl.BlockSpec((tm, D // 2), lambda b, j: (b * nt + j, 0))
    return pl.pallas_call(
        _final_prompt_body,
        grid=(B, nt),
        in_specs=[rows, words, rows, pl.BlockSpec((1, 1, D), lambda b, j: (b, 0, 5)),
                  full((D, EXPERT_FF)), full((D, EXPERT_FF)), full((EXPERT_FF, D))],
        out_specs=pl.BlockSpec((1, tm, D), lambda b, j: (b, j, 0)),
        out_shape=jax.ShapeDtypeStruct((B, L, D), f32),
        name="final_prompt",
    )(x1, h2, comb, mod3, ws1, ws3, ws2)


def _final_sample(x1, h2, comb, mod3, ws1, ws3, ws2, B, ls, row0, bt=64):
    D = D_MODEL
    n = bt * ls
    blk0 = row0 // n
    full = lambda shape: pl.BlockSpec(shape, lambda i: (0,) * len(shape))
    rows = pl.BlockSpec((n, D), lambda i: (i, 0))
    words = pl.BlockSpec((n, D // 2), lambda i: (i, 0))
    comb_rows = pl.BlockSpec((n, D), lambda i: (blk0 + i, 0))
    return pl.pallas_call(
        _final_sample_body,
        grid=(B // bt,),
        in_specs=[rows, words, comb_rows, pl.BlockSpec((bt, 1, D), lambda i: (i, 0, 5)),
                  full((D, EXPERT_FF)), full((D, EXPERT_FF)), full((EXPERT_FF, D))],
        out_specs=pl.BlockSpec((bt, ls, D), lambda i: (i, 0, 0)),
        out_shape=jax.ShapeDtypeStruct((B, ls, D), f32),
        name="final_sample",
    )(x1, h2, comb, mod3, ws1, ws3, ws2)


def kernel(x_prompt, x_sample, state_pool, cache_swa_k, cache_swa_v, c_prompt, c_sample, w_ada, b_ada,
           g_attn_norm, w_in, g_q, g_k, w_pool, pool_scale, w_out, attn_sinks, rel_bias, g_ffn_norm,
           w_router, router_bias, w1, w3, w2, ws1, ws3, ws2):
    B, L, D = x_prompt.shape
    BS, LS, _ = x_sample.shape
    depth = w_ada.shape[0]
    assert depth == 1
    W = cache_swa_k.shape[2]
    tp, ts = B * L, BS * LS
    T = tp + ts
    n_rows = (T * TOP_K // EXPERT_BLOCK + N_EXPERTS) * EXPERT_BLOCK
    nb = n_rows // EXPERT_BLOCK

    g_attn = g_attn_norm[0].reshape(1, D)
    g_ffn = g_ffn_norm[0].reshape(1, D)
    w_in_b = w_in[0].astype(bf16)
    w_out_b = w_out[0].astype(bf16)
    w_pool_b = w_pool[0].astype(bf16)
    ps = pool_scale[0].reshape(1, POOL_W)
    gqk = jnp.concatenate([jnp.tile(g_q[0], N_HEADS), jnp.tile(g_k[0], N_KV_HEADS)]).reshape(1, QK_W)
    head_of = np.arange(QK_W) // HEAD_DIM
    bd = jnp.asarray((head_of[:, None] == head_of[None, :]).astype(np.float32), bf16)
    wr_t = w_router[0].T
    wr_hi = wr_t.astype(bf16)
    wr_lo = (wr_t - wr_hi.astype(f32)).astype(bf16)
    ws1_b, ws3_b, ws2_b = ws1[0].astype(bf16), ws3[0].astype(bf16), ws2[0].astype(bf16)
    sinks = attn_sinks[0]

    mod = _ada(jnp.concatenate([c_prompt, c_sample], axis=0), w_ada[0], b_ada[0])
    mod_p = mod[:B].reshape(B, 1, 6 * D)
    mod_s = mod[B:].reshape(BS, 1, 6 * D)

    dist_p = np.arange(WINDOW)[:, None] + WINDOW - np.arange(2 * WINDOW)[None, :]
    bias_p = _relbias(rel_bias, dist_p)
    bias_p = bias_p.reshape(N_KV_HEADS, GQA, WINDOW, 2 * WINDOW).transpose(0, 2, 1, 3).reshape(
        N_KV_HEADS, WINDOW, GQA * 2 * WINDOW)
    dist_s = np.arange(LS)[:, None] + W - np.arange(W + LS)[None, :]
    bias_s = _relbias(rel_bias, dist_s)
    bias_s_buf = bias_s[:, :, :W].reshape(N_KV_HEADS, GQA * LS, W)
    bias_s_new = bias_s[:, :, W:].reshape(N_KV_HEADS, GQA * LS, LS)
    sink_col = jnp.repeat(sinks, LS).reshape(N_KV_HEADS, GQA * LS, 1)

    q_p, k_p, v_p, po_p, new_pool_p, kc_p, vc_p = _inproj_prompt(
        x_prompt, mod_p, g_attn, w_in_b, gqk, bd, w_pool_b, ps)
    q_s, k_s, v_s, po_s, new_pool_s = _inproj_sample(
        x_sample, mod_s, g_attn, w_in_b, gqk, bd, w_pool_b, ps, state_pool[0], PAST_LEN)
    at_p = _attn_prompt(q_p, k_p, v_p, bias_p, sinks)
    at_s, nk_s, nv_s = _attn_sample(
        q_s.reshape(BS, LS, ATTN_W), cache_swa_k[0].reshape(BS, W, KV_W), cache_swa_v[0].reshape(BS, W, KV_W),
        k_s.reshape(BS, LS, KV_W), v_s.reshape(BS, LS, KV_W), bias_s_buf, bias_s_new, sink_col)

    x1_p, h2_p, lg_p = _outproj_prompt(po_p, at_p, x_prompt, mod_p, g_ffn, w_out_b, wr_hi, wr_lo)
    x1_s, h2_s, lg_s = _outproj_sample(po_s, at_s.reshape(ts, ATTN_W), x_sample, mod_s, g_ffn, w_out_b,
                                       wr_hi, wr_lo)

    idx, rank, gate, counts = _route(jnp.concatenate([lg_p, lg_s], axis=1), router_bias[0])
    counts = counts.reshape(N_EXPERTS).astype(i32)
    padded = (counts + EXPERT_BLOCK - 1) // EXPERT_BLOCK * EXPERT_BLOCK
    pad_end = jnp.cumsum(padded)
    pad_start = pad_end - padded
    dest = _dest_rows(idx, rank, pad_start)
    n_valid = (pad_end[-1] // EXPERT_BLOCK).astype(i32).reshape(1)
    blk_row0 = jnp.arange(nb, dtype=i32) * EXPERT_BLOCK
    blk_e = jnp.minimum(jnp.sum(blk_row0[:, None] >= pad_end[None, :], axis=1), N_EXPERTS - 1).astype(i32)
    own = jnp.arange(N_EXPERTS, dtype=i32)[None, :] == blk_e[:, None]
    blk_cnt = jnp.sum(jnp.where(own, counts[None, :], 0), axis=1)
    blk_start = jnp.sum(jnp.where(own, pad_start[None, :], 0), axis=1)
    blk_rows = jnp.clip(blk_cnt - (blk_row0 - blk_start), 0, EXPERT_BLOCK).astype(i32)

    xs, gs = _dispatch(h2_p, h2_s, dest, gate, n_rows)
    ys = _gmm(xs, gs, w1[0], w3[0], w2[0], blk_e, blk_rows, n_valid)
    comb = _combine(ys, dest)

    y_p = _final_prompt(x1_p, h2_p, comb, mod_p, ws1_b, ws3_b, ws2_b, B, L)
    y_s = _final_sample(x1_s, h2_s, comb, mod_s, ws1_b, ws3_b, ws2_b, BS, LS, tp)

    return (y_p, y_s, new_pool_p[None], kc_p.reshape(1, B, WINDOW, N_KV_HEADS, HEAD_DIM),
            vc_p.reshape(1, B, WINDOW, N_KV_HEADS, HEAD_DIM), new_pool_s[None],
            nk_s.reshape(1, BS, W, N_KV_HEADS, HEAD_DIM), nv_s.reshape(1, BS, W, N_KV_HEADS, HEAD_DIM))
```

```python
import functools
import math

import numpy as np
import jax
import jax.numpy as jnp
from jax import lax
from jax.experimental import pallas as pl
from jax.experimental.pallas import tpu as pltpu
from jax.experimental.pallas import tpu_sc as plsc

f32 = jnp.float32
bf16 = jnp.bfloat16
i32 = jnp.int32

D_MODEL = 1024
PAST_LEN = 8192
POOL_W = 512
POOL_WINDOWS = (2, 4, 8, 16)
POOL_GC = 128
POOL_BUF = 15
ATTN_W = 512
HEAD_DIM = 64
N_HEADS = 8
N_KV_HEADS = 2
GQA = 4
KV_W = 128
WINDOW = 128
NUM_BUCKETS = 32
MAX_EXACT = 16
REL_MAX_DIST = 128
N_EXPERTS = 256
N_EXPERT_GROUPS = 8
GROUP_SIZE = 32
TOPK_GROUPS = 4
TOP_K = 8
EXPERT_FF = 256
ROUTED_SCALE = 2.5
EXPERT_BLOCK = 128
EPS = 1e-6
NEG_INF = -1e30
QKV_W = POOL_W + ATTN_W + 2 * KV_W
QK_W = ATTN_W + KV_W
HIST = 16

SC_WORKERS = 32
SC_LANES = 16
GATE_ROW = 128


def _dot(a, b):
    return jnp.dot(a, b, preferred_element_type=f32)


def _dot_t(a, b):
    return lax.dot_general(a, b, (((1,), (1,)), ((), ())), preferred_element_type=f32)


def _split_bf16(a):
    hi = a.astype(bf16)
    lo = (a - hi.astype(f32)).astype(bf16)
    return hi, lo


ROW_RING = 6
W_RING = 4
HI_MASK = -65536


def _pack_pairs(a):
    h = a.shape[1] // 2
    hi = lax.bitcast_convert_type(a[:, :h].astype(bf16).astype(f32), i32)
    lo = lax.bitcast_convert_type(a[:, h:].astype(bf16).astype(f32), i32)
    return hi | lax.shift_right_logical(lo, 16)


def _unpack_pairs(w):
    hi = lax.bitcast_convert_type(w & HI_MASK, f32).astype(bf16)
    lo = lax.bitcast_convert_type(lax.shift_left(w, 16), f32).astype(bf16)
    return hi, lo


def _mod_norm(x, g, sc, sh):
    ms = jnp.mean(x * x, axis=-1, keepdims=True)
    y = x * lax.rsqrt(ms + EPS)
    return (y * g) * (1.0 + sc) + sh


def _ada_body(c_ref, w_ref, b_ref, o_ref):
    c = c_ref[...]
    a = (c * jax.nn.sigmoid(c)).astype(bf16)
    o_ref[...] = _dot(a, w_ref[...].astype(bf16)) + b_ref[...]


def _ada(c, w_ada, b_ada):
    n = c.shape[0]
    tn = 1024
    return pl.pallas_call(
        _ada_body,
        grid=(6 * D_MODEL // tn,),
        in_specs=[
            pl.BlockSpec((n, D_MODEL), lambda j: (0, 0)),
            pl.BlockSpec((D_MODEL, tn), lambda j: (0, j)),
            pl.BlockSpec((1, tn), lambda j: (0, j)),
        ],
        out_specs=pl.BlockSpec((n, tn), lambda j: (0, j)),
        out_shape=jax.ShapeDtypeStruct((n, 6 * D_MODEL), f32),
        name="ada_mod",
    )(c, w_ada, b_ada.reshape(1, -1))


def _relbias_body(table_ref, bucket_ref, o_ref):
    bucket = bucket_ref[...]
    for h in range(N_HEADS):
        acc = jnp.zeros(bucket.shape, f32)
        for b in range(NUM_BUCKETS):
            acc = jnp.where(bucket == b, table_ref[b, h], acc)
        o_ref[h] = acc


def _rel_buckets(dist):
    n = np.maximum(dist, 0)
    nf = np.maximum(n, 1).astype(np.float64)
    large = MAX_EXACT + (np.log(nf / MAX_EXACT) / math.log(REL_MAX_DIST / MAX_EXACT)
                         * (NUM_BUCKETS - MAX_EXACT)).astype(np.int32)
    return np.where(n < MAX_EXACT, n, np.minimum(large, NUM_BUCKETS - 1)).astype(np.int32)


def _relbias(table, dist):
    lq, lk = dist.shape
    return pl.pallas_call(
        _relbias_body,
        in_specs=[
            pl.BlockSpec(memory_space=pltpu.SMEM),
            pl.BlockSpec((lq, lk), lambda: (0, 0)),
        ],
        out_specs=pl.BlockSpec((N_HEADS, lq, lk), lambda: (0, 0, 0)),
        out_shape=jax.ShapeDtypeStruct((N_HEADS, lq, lk), f32),
        name="rel_bias",
    )(table, jnp.asarray(_rel_buckets(dist)))


def _qkv_from_h(h, w_ref, gqk_ref, bd_ref):
    u = _dot(h.astype(bf16), w_ref[...])
    qk = u[:, POOL_W:POOL_W + QK_W]
    y_hi, y_lo = _split_bf16(qk * qk)
    bd = bd_ref[...]
    ss = _dot(y_hi, bd) + _dot(y_lo, bd)
    qkn = (qk * lax.rsqrt(ss * (1.0 / HEAD_DIM) + EPS)) * gqk_ref[...]
    q = qkn[:, :ATTN_W] * (HEAD_DIM ** -0.5)
    k = qkn[:, ATTN_W:]
    v = u[:, POOL_W + QK_W:]
    return u[:, :POOL_W], q, k, v


def _inproj_prompt_body(x_ref, sh_ref, sc_ref, g_ref, w_ref, gqk_ref, bd_ref, wp_ref, ps_ref,
                        q_ref, k_ref, v_ref, po_ref, np_ref, kc_ref, vc_ref, hist, *, tl, nt):
    j = pl.program_id(1)
    h = _mod_norm(x_ref[0], g_ref[...], sc_ref[0], sh_ref[0])
    up, q, k, v = _qkv_from_h(h, w_ref, gqk_ref, bd_ref)
    q_ref[0] = q.astype(bf16)
    k_ref[0] = k.astype(bf16)
    v_ref[0] = v.astype(bf16)

    @pl.when(j == nt - 1)
    def _():
        kc_ref[0] = k[tl - WINDOW:, :]
        vc_ref[0] = v[tl - WINDOW:, :]

    @pl.when(j == 0)
    def _():
        hist[0:HIST, :] = jnp.zeros((HIST, POOL_W), f32)

    hist[HIST:HIST + tl, :] = up
    pos = j * tl + lax.broadcasted_iota(i32, (tl, 1), 0)
    for g, w in enumerate(POOL_WINDOWS):
        lanes = slice(g * POOL_GC, (g + 1) * POOL_GC)
        cur = up[:, lanes]
        acc = cur
        for s in range(1, w):
            acc = acc + hist[HIST - s:HIST - s + tl, lanes]
        cnt = jnp.minimum(w, pos + 1).astype(f32)
        d = acc / cnt - cur
        yg = _dot(d.astype(bf16), wp_ref[g]) * ps_ref[:, lanes]
        po_ref[0, :, lanes] = yg.astype(bf16)

    @pl.when(j == nt - 1)
    def _():
        np_ref[0] = hist[tl + 1:tl + HIST, :]

    hist[0:HIST, :] = hist[tl:tl + HIST, :]


def _inproj_prompt(x, mod3, g_attn, w_in, gqk, bd, w_pool, pool_scale, tl=512):
    B, L, D = x.shape
    nt = L // tl
    full = lambda shape: pl.BlockSpec(shape, lambda b, j: (0,) * len(shape))
    return pl.pallas_call(
        functools.partial(_inproj_prompt_body, tl=tl, nt=nt),
        grid=(B, nt),
        in_specs=[
            pl.BlockSpec((1, tl, D), lambda b, j: (b, j, 0)),
            pl.BlockSpec((1, 1, D), lambda b, j: (b, 0, 0)),
            pl.BlockSpec((1, 1, D), lambda b, j: (b, 0, 1)),
            full((1, D)),
            full((D, QKV_W)),
            full((1, QK_W)),
            full((QK_W, QK_W)),
            full((4, POOL_GC, POOL_GC)),
            full((1, POOL_W)),
        ],
        out_specs=[
            pl.BlockSpec((1, tl, ATTN_W), lambda b, j: (b, j, 0)),
            pl.BlockSpec((1, tl, KV_W), lambda b, j: (b, j, 0)),
            pl.BlockSpec((1, tl, KV_W), lambda b, j: (b, j, 0)),
            pl.BlockSpec((1, tl, POOL_W), lambda b, j: (b, j, 0)),
            pl.BlockSpec((1, POOL_BUF, POOL_W), lambda b, j: (b, 0, 0)),
            pl.BlockSpec((1, WINDOW, KV_W), lambda b, j: (b, 0, 0)),
            pl.BlockSpec((1, WINDOW, KV_W), lambda b, j: (b, 0, 0)),
        ],
        out_shape=[
            jax.ShapeDtypeStruct((B, L, ATTN_W), bf16),
            jax.ShapeDtypeStruct((B, L, KV_W), bf16),
            jax.ShapeDtypeStruct((B, L, KV_W), bf16),
            jax.ShapeDtypeStruct((B, L, POOL_W), bf16),
            jax.ShapeDtypeStruct((B, POOL_BUF, POOL_W), f32),
            jax.ShapeDtypeStruct((B, WINDOW, KV_W), f32),
            jax.ShapeDtypeStruct((B, WINDOW, KV_W), f32),
        ],
        scratch_shapes=[pltpu.VMEM((HIST + tl, POOL_W), f32)],
        compiler_params=pltpu.CompilerParams(dimension_semantics=("arbitrary", "arbitrary")),
        name="inproj_prompt",
    )(x, mod3, mod3, g_attn, w_in, gqk, bd, w_pool, pool_scale)


def _inproj_sample_body(x_ref, sh_ref, sc_ref, g_ref, w_ref, gqk_ref, bd_ref, wp_ref, ps_ref, st_ref,
                        q_ref, k_ref, v_ref, po_ref, np_ref, ext, *, bt, ls, pos0):
    n = bt * ls
    h3 = _mod_norm(x_ref[...], g_ref[...][None], sc_ref[...], sh_ref[...])
    up, q, k, v = _qkv_from_h(h3.reshape(n, D_MODEL), w_ref, gqk_ref, bd_ref)
    q_ref[...] = q.astype(bf16)
    k_ref[...] = k
    v_ref[...] = v

    ext[:, 1:HIST, :] = st_ref[...]
    ext[:, HIST:HIST + ls, :] = up.reshape(bt, ls, POOL_W)
    pos = pos0 + lax.broadcasted_iota(i32, (1, ls, 1), 1)
    for g, w in enumerate(POOL_WINDOWS):
        lanes = slice(g * POOL_GC, (g + 1) * POOL_GC)
        cur = ext[:, HIST:HIST + ls, lanes]
        acc = cur
        for s in range(1, w):
            acc = acc + ext[:, HIST - s:HIST - s + ls, lanes]
        cnt = jnp.minimum(w, pos + 1).astype(f32)
        d = (acc / cnt - cur).reshape(n, POOL_GC)
        yg = _dot(d.astype(bf16), wp_ref[g]) * ps_ref[:, lanes]
        po_ref[:, lanes] = yg.astype(bf16)
    np_ref[...] = ext[:, ls + 1:ls + HIST, :]


def _inproj_sample(x, mod3, g_attn, w_in, gqk, bd, w_pool, pool_scale, state, pos0, bt=64):
    B, ls, D = x.shape
    n = bt * ls
    full = lambda shape: pl.BlockSpec(shape, lambda i: (0,) * len(shape))
    return pl.pallas_call(
        functools.partial(_inproj_sample_body, bt=bt, ls=ls, pos0=pos0),
        grid=(B // bt,),
        in_specs=[
            pl.BlockSpec((bt, ls, D), lambda i: (i, 0, 0)),
            pl.BlockSpec((bt, 1, D), lambda i: (i, 0, 0)),
            pl.BlockSpec((bt, 1, D), lambda i: (i, 0, 1)),
            full((1, D)),
            full((D, QKV_W)),
            full((1, QK_W)),
            full((QK_W, QK_W)),
            full((4, POOL_GC, POOL_GC)),
            full((1, POOL_W)),
            pl.BlockSpec((bt, POOL_BUF, POOL_W), lambda i: (i, 0, 0)),
        ],
        out_specs=[
            pl.BlockSpec((n, ATTN_W), lambda i: (i, 0)),
            pl.BlockSpec((n, KV_W), lambda i: (i, 0)),
            pl.BlockSpec((n, KV_W), lambda i: (i, 0)),
            pl.BlockSpec((n, POOL_W), lambda i: (i, 0)),
            pl.BlockSpec((bt, POOL_BUF, POOL_W), lambda i: (i, 0, 0)),
        ],
        out_shape=[
            jax.ShapeDtypeStruct((B * ls, ATTN_W), bf16),
            jax.ShapeDtypeStruct((B * ls, KV_W), f32),
            jax.ShapeDtypeStruct((B * ls, KV_W), f32),
            jax.ShapeDtypeStruct((B * ls, POOL_W), bf16),
            jax.ShapeDtypeStruct((B, POOL_BUF, POOL_W), f32),
        ],
        scratch_shapes=[pltpu.VMEM((bt, HIST + ls, POOL_W), f32)],
        name="inproj_sample",
    )(x, mod3, mod3, g_attn, w_in, gqk, bd, w_pool, pool_scale, state)


def _softmax_sink(parts, sink):
    m = sink
    for s in parts:
        m = jnp.maximum(m, jnp.max(s, axis=-1, keepdims=True))
    ps = [jnp.exp(s - m) for s in parts]
    denom = jnp.exp(sink - m)
    for p in ps:
        denom = denom + jnp.sum(p, axis=-1, keepdims=True)
    inv = 1.0 / denom
    return [(p * inv).astype(bf16) for p in ps]


def _attn_prompt_body(sinks_ref, q_ref, kp_ref, kc_ref, vp_ref, vc_ref, bias_ref, mask_ref, o_ref):
    j = pl.program_id(1)
    nk = 2 * WINDOW
    qw = GQA * HEAD_DIM
    nq = q_ref.shape[1] // WINDOW
    kall = jnp.concatenate([kp_ref[0], kc_ref[0]], axis=0)
    vall = jnp.concatenate([vp_ref[0], vc_ref[0]], axis=0)
    lane_group = lax.broadcasted_iota(i32, (nk, 2 * KV_W), 1) // HEAD_DIM
    first_has_prev = jnp.minimum(j, 1)

    def spread(t):
        t0 = jnp.concatenate([t, t], axis=1)
        return t0, pltpu.roll(t0, HEAD_DIM, 1)

    def blockdiag(t01, kv):
        t0, t1 = t01
        return jnp.concatenate(
            [jnp.where(lane_group == g, t0 if g % 2 == kv else t1, jnp.zeros_like(t0)) for g in range(GQA)],
            axis=0)

    chains = [(qb, kv) for qb in range(nq) for kv in range(N_KV_HEADS)]
    ksp = [spread(kall[qb * WINDOW:qb * WINDOW + nk]) for qb in range(nq)]
    vsp = [spread(vall[qb * WINDOW:qb * WINDOW + nk]) for qb in range(nq)]
    valid = [mask_ref[first_has_prev] > 0.5] + [mask_ref[1] > 0.5] * (nq - 1)
    s = [_dot_t(q_ref[0, qb * WINDOW:(qb + 1) * WINDOW, kv * qw:(kv + 1) * qw], blockdiag(ksp[qb], kv))
         for qb, kv in chains]
    s = [jnp.where(valid[qb], s[c] + bias_ref[kv], NEG_INF) for c, (qb, kv) in enumerate(chains)]
    p = [jnp.concatenate([_softmax_sink([s[c][:, g * nk:(g + 1) * nk]], sinks_ref[kv * GQA + g])[0]
                          for g in range(GQA)], axis=1) for c, (qb, kv) in enumerate(chains)]
    o = [_dot(p[c], blockdiag(vsp[qb], kv)) for c, (qb, kv) in enumerate(chains)]
    for c, (qb, kv) in enumerate(chains):
        o_ref[0, qb * WINDOW:(qb + 1) * WINDOW, kv * qw:(kv + 1) * qw] = o[c].astype(bf16)


def _attn_prompt(q, k, v, bias, sinks):
    B, L, _ = q.shape
    nq = 2
    nb = L // (nq * WINDOW)
    cur = lambda b, j: (b, j, 0)
    prev = lambda b, j: (b, jnp.maximum(nq * j - 1, 0), 0)
    qi = np.arange(WINDOW)[:, None]
    kc = np.arange(2 * WINDOW)[None, :]
    own = (kc >= WINDOW) & (kc - WINDOW <= qi)
    prv = (kc < WINDOW) & (kc > qi)
    mask = np.stack([np.tile(own, (1, GQA)), np.tile(own | prv, (1, GQA))]).astype(np.float32)
    return pl.pallas_call(
        _attn_prompt_body,
        grid=(B, nb),
        in_specs=[
            pl.BlockSpec(memory_space=pltpu.SMEM),
            pl.BlockSpec((1, nq * WINDOW, ATTN_W), cur),
            pl.BlockSpec((1, WINDOW, KV_W), prev),
            pl.BlockSpec((1, nq * WINDOW, KV_W), cur),
            pl.BlockSpec((1, WINDOW, KV_W), prev),
            pl.BlockSpec((1, nq * WINDOW, KV_W), cur),
            pl.BlockSpec((N_KV_HEADS, WINDOW, GQA * 2 * WINDOW), lambda b, j: (0, 0, 0)),
            pl.BlockSpec((2, WINDOW, GQA * 2 * WINDOW), lambda b, j: (0, 0, 0)),
        ],
        out_specs=pl.BlockSpec((1, nq * WINDOW, ATTN_W), cur),
        out_shape=jax.ShapeDtypeStruct((B, L, ATTN_W), bf16),
        name="attn_prompt",
    )(sinks, q, k, k, v, v, bias, jnp.asarray(mask))


def _attn_sample_body(q_ref, kb_ref, vb_ref, kn_ref, vn_ref, bb_ref, bn_ref, sink_ref,
                      o_ref, nk_ref, nv_ref, *, bb, ls):
    W = kb_ref.shape[1]
    rows = GQA * ls
    qi = lax.broadcasted_iota(i32, (rows, W), 0) % ls
    kj = lax.broadcasted_iota(i32, (rows, W), 1)
    valid_buf = kj > qi
    qi2 = lax.broadcasted_iota(i32, (rows, ls), 0) % ls
    kj2 = lax.broadcasted_iota(i32, (rows, ls), 1)
    valid_new = kj2 <= qi2

    nbat = 4
    ks = [slice(kv * HEAD_DIM, (kv + 1) * HEAD_DIM) for kv in range(N_KV_HEADS)]

    def group(i, carry):
        bs = [i * nbat + u for u in range(nbat)]
        chains = [(u, kv) for u in range(nbat) for kv in range(N_KV_HEADS)]
        qb = [q_ref[b] for b in bs]
        kbuf = [kb_ref[b] for b in bs]
        vbuf = [vb_ref[b] for b in bs]
        knew = [kn_ref[b] for b in bs]
        vnew = [vn_ref[b] for b in bs]
        qg = [jnp.concatenate([qb[u][:, (kv * GQA + g) * HEAD_DIM:(kv * GQA + g + 1) * HEAD_DIM]
                               for g in range(GQA)], axis=0) for u, kv in chains]
        s_buf = [_dot_t(qg[c], kbuf[u][:, ks[kv]].astype(bf16)) for c, (u, kv) in enumerate(chains)]
        s_new = [_dot_t(qg[c], knew[u][:, ks[kv]].astype(bf16)) for c, (u, kv) in enumerate(chains)]
        s_buf = [jnp.where(valid_buf, s_buf[c] + bb_ref[kv], NEG_INF) for c, (u, kv) in enumerate(chains)]
        s_new = [jnp.where(valid_new, s_new[c] + bn_ref[kv], NEG_INF) for c, (u, kv) in enumerate(chains)]
        probs = [_softmax_sink([s_buf[c], s_new[c]], sink_ref[kv]) for c, (u, kv) in enumerate(chains)]
        outs = [_dot(probs[c][0], vbuf[u][:, ks[kv]].astype(bf16)) + _dot(probs[c][1], vnew[u][:, ks[kv]].astype(bf16))
                for c, (u, kv) in enumerate(chains)]
        for u, b in enumerate(bs):
            heads = [outs[u * N_KV_HEADS + kv][g * ls:(g + 1) * ls] for kv in range(N_KV_HEADS) for g in range(GQA)]
            o_ref[b] = jnp.concatenate(heads, axis=-1).astype(bf16)
            nk_ref[b, 0:W - ls, :] = kbuf[u][ls:, :]
            nk_ref[b, W - ls:W, :] = knew[u]
            nv_ref[b, 0:W - ls, :] = vbuf[u][ls:, :]
            nv_ref[b, W - ls:W, :] = vnew[u]
        return carry

    lax.fori_loop(0, bb // nbat, group, 0)


def _attn_sample(q, k_buf, v_buf, k_new, v_new, bias_buf, bias_new, sink_col, bb=16):
    B, ls, _ = q.shape
    W = k_buf.shape[1]
    rows = GQA * ls
    blk = lambda shape: pl.BlockSpec(shape, lambda i: (i, 0, 0))
    full = lambda shape: pl.BlockSpec(shape, lambda i: (0, 0, 0))
    return pl.pallas_call(
        functools.partial(_attn_sample_body, bb=bb, ls=ls),
        grid=(B // bb,),
        in_specs=[
            blk((bb, ls, ATTN_W)),
            blk((bb, W, KV_W)),
            blk((bb, W, KV_W)),
            blk((bb, ls, KV_W)),
            blk((bb, ls, KV_W)),
            full((N_KV_HEADS, rows, W)),
            full((N_KV_HEADS, rows, ls)),
            full((N_KV_HEADS, rows, 1)),
        ],
        out_specs=[blk((bb, ls, ATTN_W)), blk((bb, W, KV_W)), blk((bb, W, KV_W))],
        out_shape=[
            jax.ShapeDtypeStruct((B, ls, ATTN_W), bf16),
            jax.ShapeDtypeStruct((B, W, KV_W), f32),
            jax.ShapeDtypeStruct((B, W, KV_W), f32),
        ],
        name="attn_sample",
    )(q, k_buf, v_buf, k_new, v_new, bias_buf, bias_new, sink_col)


def _outproj_core(po, at, x, gt, sc, sh, g_ref, wo_ref, wrh_ref, wrl_ref):
    mixo = _dot(po, wo_ref[0:POOL_W, :]) + _dot(at, wo_ref[POOL_W:, :])
    x1 = x + gt * mixo.reshape(x.shape)
    h2 = _mod_norm(x1, g_ref[...].reshape((1,) * (x.ndim - 1) + (D_MODEL,)), sc, sh).reshape(-1, D_MODEL)
    h_hi, h_lo = _split_bf16(h2)
    wh = wrh_ref[...]
    logits = _dot_t(wh, h_hi) + (_dot_t(wh, h_lo) + _dot_t(wrl_ref[...], h_hi))
    return x1, _pack_pairs(h2), logits


def _outproj_prompt_body(po_ref, at_ref, x_ref, gt_ref, sc_ref, sh_ref, g_ref, wo_ref, wrh_ref, wrl_ref,
                         x1_ref, h2_ref, lg_ref):
    x1, h2p, logits = _outproj_core(po_ref[0], at_ref[0], x_ref[0], gt_ref[0], sc_ref[0], sh_ref[0],
                                    g_ref, wo_ref, wrh_ref, wrl_ref)
    x1_ref[...] = x1
    h2_ref[...] = h2p
    lg_ref[...] = logits


def _outproj_sample_body(po_ref, at_ref, x_ref, gt_ref, sc_ref, sh_ref, g_ref, wo_ref, wrh_ref, wrl_ref,
                         x1_ref, h2_ref, lg_ref):
    x1, h2p, logits = _outproj_core(po_ref[...], at_ref[...], x_ref[...], gt_ref[...], sc_ref[...], sh_ref[...],
                                    g_ref, wo_ref, wrh_ref, wrl_ref)
    x1_ref[...] = x1.reshape(-1, D_MODEL)
    h2_ref[...] = h2p
    lg_ref[...] = logits


def _outproj_prompt(po, at, x, mod3, g_ffn, w_out, wr_hi, wr_lo, tm=512):
    B, L, D = x.shape
    nt = L // tm
    n_tok = B * L
    full = lambda shape: pl.BlockSpec(shape, lambda b, j: (0,) * len(shape))
    modspec = lambda c: pl.BlockSpec((1, 1, D), lambda b, j: (b, 0, c))
    return pl.pallas_call(
        _outproj_prompt_body,
        grid=(B, nt),
        in_specs=[
            pl.BlockSpec((1, tm, POOL_W), lambda b, j: (b, j, 0)),
            pl.BlockSpec((1, tm, ATTN_W), lambda b, j: (b, j, 0)),
            pl.BlockSpec((1, tm, D), lambda b, j: (b, j, 0)),
            modspec(2), modspec(4), modspec(3),
            full((1, D)), full((D, D)), full((N_EXPERTS, D)), full((N_EXPERTS, D)),
        ],
        out_specs=[
            pl.BlockSpec((tm, D), lambda b, j: (b * nt + j, 0)),
            pl.BlockSpec((tm, D // 2), lambda b, j: (b * nt + j, 0)),
            pl.BlockSpec((N_EXPERTS, tm), lambda b, j: (0, b * nt + j)),
        ],
        out_shape=[
            jax.ShapeDtypeStruct((n_tok, D), f32),
            jax.ShapeDtypeStruct((n_tok, D // 2), i32),
            jax.ShapeDtypeStruct((N_EXPERTS, n_tok), f32),
        ],
        name="outproj_prompt",
    )(po, at, x, mod3, mod3, mod3, g_ffn, w_out, wr_hi, wr_lo)


def _outproj_sample(po, at, x, mod3, g_ffn, w_out, wr_hi, wr_lo, bt=64):
    B, ls, D = x.shape
    n = bt * ls
    full = lambda shape: pl.BlockSpec(shape, lambda i: (0,) * len(shape))
    modspec = lambda c: pl.BlockSpec((bt, 1, D), lambda i: (i, 0, c))
    return pl.pallas_call(
        _outproj_sample_body,
        grid=(B // bt,),
        in_specs=[
            pl.BlockSpec((n, POOL_W), lambda i: (i, 0)),
            pl.BlockSpec((n, ATTN_W), lambda i: (i, 0)),
            pl.BlockSpec((bt, ls, D), lambda i: (i, 0, 0)),
            modspec(2), modspec(4), modspec(3),
            full((1, D)), full((D, D)), full((N_EXPERTS, D)), full((N_EXPERTS, D)),
        ],
        out_specs=[
            pl.BlockSpec((n, D), lambda i: (i, 0)),
            pl.BlockSpec((n, D // 2), lambda i: (i, 0)),
            pl.BlockSpec((N_EXPERTS, n), lambda i: (0, i)),
        ],
        out_shape=[
            jax.ShapeDtypeStruct((B * ls, D), f32),
            jax.ShapeDtypeStruct((B * ls, D // 2), i32),
            jax.ShapeDtypeStruct((N_EXPERTS, B * ls), f32),
        ],
        name="outproj_sample",
    )(po, at, x, mod3, mod3, mod3, g_ffn, w_out, wr_hi, wr_lo)


def _route_body(lg_ref, rb_ref, tri_ref, idx_ref, rank_ref, gate_ref, cnt_ref, carry, *, tr, nsteps):
    step = pl.program_id(0)

    @pl.when(step == 0)
    def _():
        carry[...] = jnp.zeros(carry.shape, f32)

    s = jax.nn.sigmoid(lg_ref[...])
    sb = s + rb_ref[...]
    e_iota = lax.broadcasted_iota(i32, (N_EXPERTS, tr), 0)
    g_iota = lax.broadcasted_iota(i32, (GROUP_SIZE, tr), 0)

    gscore = []
    for g in range(N_EXPERT_GROUPS):
        v = sb[g * GROUP_SIZE:(g + 1) * GROUP_SIZE]
        m1 = jnp.max(v, axis=0, keepdims=True)
        i1 = jnp.min(jnp.where(v == m1, g_iota, GROUP_SIZE), axis=0, keepdims=True)
        m2 = jnp.max(jnp.where(g_iota == i1, -jnp.inf, v), axis=0, keepdims=True)
        gscore.append(m1 + m2)
    parts = []
    for g in range(N_EXPERT_GROUPS):
        beaten = jnp.zeros((1, tr), i32)
        for g2 in range(N_EXPERT_GROUPS):
            if g2 == g:
                continue
            ahead = gscore[g2] > gscore[g]
            if g2 < g:
                ahead = ahead | (gscore[g2] == gscore[g])
            beaten = beaten + ahead.astype(i32)
        keep = beaten < TOPK_GROUPS
        parts.append(jnp.where(keep, sb[g * GROUP_SIZE:(g + 1) * GROUP_SIZE], NEG_INF))
    cur = jnp.concatenate(parts, axis=0)

    sel = jnp.zeros((N_EXPERTS, tr), f32)
    idxs, svals = [], []
    for _ in range(TOP_K):
        m = jnp.max(cur, axis=0, keepdims=True)
        ik = jnp.min(jnp.where(cur == m, e_iota, N_EXPERTS), axis=0, keepdims=True)
        hit = e_iota == ik
        svals.append(jnp.sum(jnp.where(hit, s, 0.0), axis=0, keepdims=True))
        cur = jnp.where(hit, -jnp.inf, cur)
        sel = jnp.where(hit, 1.0, sel)
        idxs.append(ik)
    ssum = svals[0]
    for sv in svals[1:]:
        ssum = ssum + sv
    gate_ref[...] = jnp.concatenate([sv / ssum * ROUTED_SCALE for sv in svals], axis=0)
    idx_ref[...] = jnp.concatenate(idxs, axis=0)

    before = carry[...] + _dot(sel.astype(bf16), tri_ref[...])
    ranks = [jnp.sum(jnp.where(e_iota == ik, before, 0.0), axis=0, keepdims=True) for ik in idxs]
    rank_ref[...] = jnp.concatenate(ranks, axis=0).astype(i32)
    carry[...] = carry[...] + jnp.sum(sel, axis=1, keepdims=True)

    @pl.when(step == nsteps - 1)
    def _():
        cnt_ref[...] = carry[...]


def _route(logits_t, router_bias, tr=512):
    E, T = logits_t.shape
    nsteps = T // tr
    tri = jnp.asarray(np.triu(np.ones((tr, tr), np.float32), 1), bf16)
    return pl.pallas_call(
        functools.partial(_route_body, tr=tr, nsteps=nsteps),
        grid=(nsteps,),
        in_specs=[
            pl.BlockSpec((E, tr), lambda i: (0, i)),
            pl.BlockSpec((E, 1), lambda i: (0, 0)),
            pl.BlockSpec((tr, tr), lambda i: (0, 0)),
        ],
        out_specs=[
            pl.BlockSpec((TOP_K, tr), lambda i: (0, i)),
            pl.BlockSpec((TOP_K, tr), lambda i: (0, i)),
            pl.BlockSpec((TOP_K, tr), lambda i: (0, i)),
            pl.BlockSpec((E, 1), lambda i: (0, 0)),
        ],
        out_shape=[
            jax.ShapeDtypeStruct((TOP_K, T), i32),
            jax.ShapeDtypeStruct((TOP_K, T), i32),
            jax.ShapeDtypeStruct((TOP_K, T), f32),
            jax.ShapeDtypeStruct((E, 1), f32),
        ],
        scratch_shapes=[pltpu.VMEM((E, 1), f32)],
        compiler_params=pltpu.CompilerParams(dimension_semantics=("arbitrary",)),
        name="route",
    )(logits_t, router_bias.reshape(E, 1), tri)


def _dest_body(idx_ref, rank_ref, ps_ref, dest_ref, *, tr):
    e_iota = lax.broadcasted_iota(i32, (N_EXPERTS, tr), 0)
    start = ps_ref[...]
    rows = []
    for k in range(TOP_K):
        hit = e_iota == idx_ref[k:k + 1, :]
        rows.append(jnp.sum(jnp.where(hit, start, 0.0), axis=0, keepdims=True))
    dest_ref[...] = jnp.concatenate(rows, axis=0).astype(i32) + rank_ref[...]


def _dest_rows(idx, rank, pad_start, tr=512):
    K, T = idx.shape
    blk = pl.BlockSpec((K, tr), lambda i: (0, i))
    return pl.pallas_call(
        functools.partial(_dest_body, tr=tr),
        grid=(T // tr,),
        in_specs=[blk, blk, pl.BlockSpec((N_EXPERTS, 1), lambda i: (0, 0))],
        out_specs=blk,
        out_shape=jax.ShapeDtypeStruct((K, T), i32),
        name="dest_rows",
    )(idx, rank, pad_start.astype(f32).reshape(N_EXPERTS, 1))


def _sc_mesh():
    return plsc.VectorSubcoreMesh(core_axis_name="c", subcore_axis_name="s")


def _sc_worker_id():
    return lax.axis_index("s") * 2 + lax.axis_index("c")


def _dispatch(h2_a, h2_b, dest, gate, n_rows, chunk=32):
    ta, Dw = h2_a.shape
    T = ta + h2_b.shape[0]
    per_worker = T // SC_WORKERS
    nchunk = per_worker // chunk
    assert per_worker * SC_WORKERS == T and nchunk * chunk == per_worker and ta % chunk == 0

    nrow_idx = nchunk * TOP_K

    @functools.partial(
        pl.kernel, mesh=_sc_mesh(),
        out_type=[jax.ShapeDtypeStruct((n_rows, Dw), i32), jax.ShapeDtypeStruct((n_rows, GATE_ROW), f32)],
        scratch_types=[pltpu.VMEM((nrow_idx, chunk), i32), pltpu.VMEM((nrow_idx, chunk), f32),
                       pltpu.VMEM((chunk, Dw), i32), pltpu.VMEM((TOP_K, chunk, GATE_ROW), f32),
                       pltpu.SemaphoreType.DMA],
        compiler_params=pltpu.CompilerParams(needs_layout_passes=False),
        name="moe_dispatch",
    )
    def body(ha_hbm, hb_hbm, dest_hbm, gate_hbm, xs_hbm, gs_hbm, idx_v, gate_v, rows_v, grow_v, sem):
        wid = _sc_worker_id()
        base = wid * per_worker
        pltpu.sync_copy(dest_hbm.at[wid], idx_v)
        pltpu.sync_copy(gate_hbm.at[wid], gate_v)
        zero = jnp.zeros((SC_LANES,), f32)
        for k in range(TOP_K):
            @pl.loop(0, chunk)
            def _(t):
                for j in range(GATE_ROW // SC_LANES):
                    grow_v[k, t, pl.ds(j * SC_LANES, SC_LANES)] = zero

        @pl.loop(0, nchunk)
        def _(ci):
            t0 = base + ci * chunk

            @pl.when(t0 < ta)
            def _():
                pltpu.sync_copy(ha_hbm.at[pl.ds(t0, chunk)], rows_v)

            @pl.when(t0 >= ta)
            def _():
                pltpu.sync_copy(hb_hbm.at[pl.ds(t0 - ta, chunk)], rows_v)

            for k in range(TOP_K):
                @pl.loop(0, chunk)
                def _(t):
                    row = jnp.zeros((SC_LANES,), i32) + (ci * TOP_K + k)
                    grow_v[k, t, pl.ds(0, SC_LANES)] = plsc.load_gather(
                        gate_v, [row, jnp.zeros((SC_LANES,), i32) + t])

            cps = []
            for k in range(TOP_K):
                idx = idx_v.at[ci * TOP_K + k]
                cps.append(pltpu.make_async_copy(rows_v, xs_hbm.at[idx], sem))
                cps.append(pltpu.make_async_copy(grow_v.at[k], gs_hbm.at[idx], sem))
            for cp in cps:
                cp.start()
            for cp in cps:
                cp.wait()

    def per_worker_rows(a):
        return a.reshape(TOP_K, SC_WORKERS, nchunk, chunk).transpose(1, 2, 0, 3).reshape(
            SC_WORKERS, nrow_idx, chunk)

    return body(h2_a, h2_b, per_worker_rows(dest), per_worker_rows(gate))


def _combine(ys, dest, chunk=8):
    T = dest.shape[1]
    Dw = ys.shape[1]
    per_worker = T // SC_WORKERS
    nchunk = per_worker // chunk
    assert per_worker * SC_WORKERS == T and nchunk * chunk == per_worker and nchunk % 2 == 0

    @functools.partial(
        pl.kernel, mesh=_sc_mesh(),
        out_type=jax.ShapeDtypeStruct((T, 2 * Dw), f32),
        scratch_types=[
            pltpu.VMEM((TOP_K * per_worker,), i32),
            pltpu.VMEM((2, TOP_K, chunk, Dw), i32),
            pltpu.VMEM((chunk, 2 * Dw), f32),
            pltpu.SemaphoreType.DMA((2,)),
        ],
        compiler_params=pltpu.CompilerParams(needs_layout_passes=False),
        name="moe_combine",
    )
    def body(ys_hbm, dest_hbm, out_hbm, idx_v, buf, out_v, sems):
        base = _sc_worker_id() * per_worker
        pltpu.sync_copy(dest_hbm.at[pl.ds(_sc_worker_id() * (TOP_K * per_worker), TOP_K * per_worker)], idx_v)

        def gather(ci, slot):
            return [pltpu.make_async_copy(ys_hbm.at[idx_v.at[pl.ds(k * per_worker + ci * chunk, chunk)]],
                                          buf.at[slot, k], sems.at[slot]) for k in range(TOP_K)]

        for cp in gather(0, 0):
            cp.start()

        @pl.loop(0, nchunk, step=2)
        def _(c0):
            for slot in range(2):
                ci = c0 + slot

                @pl.when(ci + 1 < nchunk)
                def _():
                    for cp in gather(ci + 1, 1 - slot):
                        cp.start()

                for cp in gather(ci, slot):
                    cp.wait()

                @pl.loop(0, chunk)
                def _(t):
                    @plsc.parallel_loop(0, Dw // SC_LANES, unroll=4)
                    def _(j):
                        sl = pl.ds(j * SC_LANES, SC_LANES)
                        w = buf[slot, 0, t, sl]
                        hi = plsc.bitcast(w & HI_MASK, f32)
                        lo = plsc.bitcast(lax.shift_left(w, 16), f32)
                        for k in range(1, TOP_K):
                            w = buf[slot, k, t, sl]
                            hi = hi + plsc.bitcast(w & HI_MASK, f32)
                            lo = lo + plsc.bitcast(lax.shift_left(w, 16), f32)
                        out_v[t, sl] = hi
                        out_v[t, pl.ds(Dw + j * SC_LANES, SC_LANES)] = lo

                pltpu.sync_copy(out_v, out_hbm.at[pl.ds(base + ci * chunk, chunk)])

    dest_w = dest.reshape(TOP_K, SC_WORKERS, per_worker).transpose(1, 0, 2).reshape(-1)
    return body(ys, dest_w)


def _gmm_body(blk_e_ref, blk_rows_ref, blk_next_ref, nv_ref, xs_hbm, gs_hbm, w1_hbm, w3_hbm, w2_hbm, ys_hbm,
              xbuf, gbuf, ybuf, w1f, w3f, w2f, xsem, gsem, ysem, wsem):
    nv = nv_ref[0]
    nb = blk_e_ref.shape[0]
    half = D_MODEL // 2
    RB = EXPERT_BLOCK
    ahead_w = W_RING - 2

    def expert_of(blk):
        return blk_e_ref[jnp.minimum(blk, nb - 1)]

    def next_expert_block(blk):
        return jnp.where(blk < nv, blk_next_ref[jnp.minimum(blk, nb - 1)], blk)

    def start_weights(blk, ordinal):
        @pl.when(blk < nv)
        def _():
            for cp in weight_copies(expert_of(blk), lax.rem(ordinal, W_RING)):
                cp.start()

    def row_copies(b, slot):
        r0 = pl.multiple_of(b * RB, RB)
        return (pltpu.make_async_copy(xs_hbm.at[pl.ds(r0, RB)], xbuf.at[slot], xsem.at[slot]),
                pltpu.make_async_copy(gs_hbm.at[pl.ds(r0, RB)], gbuf.at[slot], gsem.at[slot]))

    def out_copy(b, slot):
        r0 = pl.multiple_of(b * RB, RB)
        return pltpu.make_async_copy(ybuf.at[slot], ys_hbm.at[pl.ds(r0, RB)], ysem.at[slot])

    def weight_copies(e, ws):
        return (pltpu.make_async_copy(w1_hbm.at[e], w1f.at[ws], wsem.at[ws, 0]),
                pltpu.make_async_copy(w3_hbm.at[e], w3f.at[ws], wsem.at[ws, 1]),
                pltpu.make_async_copy(w2_hbm.at[e], w2f.at[ws], wsem.at[ws, 2]))

    blk = jnp.int32(0)
    for n in range(ahead_w):
        start_weights(blk, n)
        blk = next_expert_block(blk)
    for i in range(ROW_RING - 2):
        @pl.when(i < nv)
        def _():
            for cp in row_copies(i, i):
                cp.start()

    def enter(b, live, ordinal_prev):
        e = expert_of(b)
        first = live & ((b == 0) | (e != expert_of(jnp.maximum(b - 1, 0))))
        ordinal = jnp.where(first & (b > 0), ordinal_prev + 1, ordinal_prev)

        @pl.when(first)
        def _():
            for cp in weight_copies(e, lax.rem(ordinal, W_RING)):
                cp.wait()
            nxt = b
            for _ in range(ahead_w):
                nxt = next_expert_block(nxt)
            start_weights(nxt, ordinal + ahead_w)

        return ordinal

    def load_block(b, slot, ws):
        valid = lax.broadcasted_iota(i32, (RB, 1), 0) < blk_rows_ref[jnp.minimum(b, nb - 1)]
        x_hi, x_lo = _unpack_pairs(jnp.where(valid, xbuf[slot], 0))
        g = jnp.where(valid, gbuf[slot][:, 0:1], 0.0)
        return x_hi, x_lo, g, ws

    def pair(p, ordinal_prev):
        b0 = 2 * p
        b1 = b0 + 1
        live1 = b1 < nv
        slot0 = lax.rem(b0, ROW_RING)
        slots = (slot0, jnp.where(live1, lax.rem(b1, ROW_RING), slot0))

        for b in (b0 + ROW_RING - 2, b1 + ROW_RING - 2):
            @pl.when(b < nv)
            def _():
                for cp in row_copies(b, lax.rem(b, ROW_RING)):
                    cp.start()

        ord0 = enter(b0, b0 < nv, ordinal_prev)
        ord1 = enter(b1, live1, ord0)
        for cp in row_copies(b0, slots[0]):
            cp.wait()

        @pl.when(live1)
        def _():
            for cp in row_copies(b1, slots[1]):
                cp.wait()

        blocks = [load_block(b0, slots[0], lax.rem(ord0, W_RING)),
                  load_block(jnp.where(live1, b1, b0), slots[1], lax.rem(ord1, W_RING))]
        a = [_dot(xh, w1f[ws, 0:half, :].astype(bf16)) + _dot(xl, w1f[ws, half:, :].astype(bf16))
             for xh, xl, _, ws in blocks]
        c = [_dot(xh, w3f[ws, 0:half, :].astype(bf16)) + _dot(xl, w3f[ws, half:, :].astype(bf16))
             for xh, xl, _, ws in blocks]
        hmid = [((a[i] * jax.nn.sigmoid(a[i])) * c[i]).astype(bf16) for i in range(2)]
        y = [_pack_pairs(_dot(hmid[i], w2f[blocks[i][3]].astype(bf16)) * blocks[i][2]) for i in range(2)]

        for i, (b, live) in enumerate(((b0, b0 < nv), (b1, live1))):
            @pl.when(live & (b >= ROW_RING))
            def _():
                out_copy(b - ROW_RING, slots[i]).wait()

            @pl.when(live)
            def _():
                ybuf[slots[i]] = y[i]
                out_copy(b, slots[i]).start()

        return ord1

    lax.fori_loop(0, (nv + 1) // 2, pair, 0)

    for i in range(1, ROW_RING + 1):
        @pl.when(nv >= i)
        def _():
            out_copy(nv - i, lax.rem(nv - i, ROW_RING)).wait()


def _gmm(xs, gs, w1, w3, w2, blk_e, blk_rows, blk_next, n_valid):
    n_rows, Dw = xs.shape
    D = 2 * Dw
    RB = EXPERT_BLOCK
    hbm = pl.BlockSpec(memory_space=pl.ANY)
    return pl.pallas_call(
        _gmm_body,
        grid_spec=pltpu.PrefetchScalarGridSpec(
            num_scalar_prefetch=4,
            grid=(1,),
            in_specs=[hbm, hbm, hbm, hbm, hbm],
            out_specs=hbm,
            scratch_shapes=[
                pltpu.VMEM((ROW_RING, RB, Dw), i32), pltpu.VMEM((ROW_RING, RB, GATE_ROW), f32),
                pltpu.VMEM((ROW_RING, RB, Dw), i32),
                pltpu.VMEM((W_RING, D, EXPERT_FF), f32), pltpu.VMEM((W_RING, D, EXPERT_FF), f32),
                pltpu.VMEM((W_RING, EXPERT_FF, D), f32),
                pltpu.SemaphoreType.DMA((ROW_RING,)), pltpu.SemaphoreType.DMA((ROW_RING,)),
                pltpu.SemaphoreType.DMA((ROW_RING,)), pltpu.SemaphoreType.DMA((W_RING, 3)),
            ],
        ),
        out_shape=jax.ShapeDtypeStruct((n_rows, Dw), i32),
        compiler_params=pltpu.CompilerParams(dimension_semantics=("arbitrary",)),
        name="moe_gmm",
    )(blk_e, blk_rows, blk_next, n_valid, xs, gs, w1, w3, w2)


def _final_core(x1, h2p, comb, gt, ws1_ref, ws3_ref, ws2_ref):
    half = D_MODEL // 2
    h_hi, h_lo = _unpack_pairs(h2p)
    a = _dot(h_hi, ws1_ref[0:half, :]) + _dot(h_lo, ws1_ref[half:, :])
    c = _dot(h_hi, ws3_ref[0:half, :]) + _dot(h_lo, ws3_ref[half:, :])
    shared = _dot(((a * jax.nn.sigmoid(a)) * c).astype(bf16), ws2_ref[...])
    return x1, comb + shared, gt


def _final_prompt_body(x1_ref, h2_ref, cb_ref, gt_ref, ws1_ref, ws3_ref, ws2_ref, y_ref):
    x1, ffn, gt = _final_core(x1_ref[...], h2_ref[...], cb_ref[...], gt_ref[0], ws1_ref, ws3_ref, ws2_ref)
    y_ref[0] = x1 + gt * ffn


def _final_sample_body(x1_ref, h2_ref, cb_ref, gt_ref, ws1_ref, ws3_ref, ws2_ref, y_ref):
    x1, ffn, gt = _final_core(x1_ref[...], h2_ref[...], cb_ref[...], gt_ref[...], ws1_ref, ws3_ref, ws2_ref)
    shp = y_ref.shape
    y_ref[...] = x1.reshape(shp) + gt * ffn.reshape(shp)


def _final_prompt(x1, h2, comb, mod3, ws1, ws3, ws2, B, L, tm=512):
    D = D_MODEL
    nt = L // tm
    full = lambda shape: pl.BlockSpec(shape, lambda b, j: (0,) * len(shape))
    rows = pl.BlockSpec((tm, D), lambda b, j: (b * nt + j, 0))
    words = pl.BlockSpec((tm, D // 2), lambda b, j: (b * nt + j, 0))
    return pl.pallas_call(
        _final_prompt_body,
        grid=(B, nt),
        in_specs=[rows, words, rows, pl.BlockSpec((1, 1, D), lambda b, j: (b, 0, 5)),
                  full((D, EXPERT_FF)), full((D, EXPERT_FF)), full((EXPERT_FF, D))],
        out_specs=pl.BlockSpec((1, tm, D), lambda b, j: (b, j, 0)),
        out_shape=jax.ShapeDtypeStruct((B, L, D), f32),
        name="final_prompt",
    )(x1, h2, comb, mod3, ws1, ws3, ws2)


def _final_sample(x1, h2, comb, mod3, ws1, ws3, ws2, B, ls, row0, bt=64):
    D = D_MODEL
    n = bt * ls
    blk0 = row0 // n
    full = lambda shape: pl.BlockSpec(shape, lambda i: (0,) * len(shape))
    rows = pl.BlockSpec((n, D), lambda i: (i, 0))
    words = pl.BlockSpec((n, D // 2), lambda i: (i, 0))
    comb_rows = pl.BlockSpec((n, D), lambda i: (blk0 + i, 0))
    return pl.pallas_call(
        _final_sample_body,
        grid=(B // bt,),
        in_specs=[rows, words, comb_rows, pl.BlockSpec((bt, 1, D), lambda i: (i, 0, 5)),
                  full((D, EXPERT_FF)), full((D, EXPERT_FF)), full((EXPERT_FF, D))],
        out_specs=pl.BlockSpec((bt, ls, D), lambda i: (i, 0, 0)),
        out_shape=jax.ShapeDtypeStruct((B, ls, D), f32),
        name="final_sample",
    )(x1, h2, comb, mod3, ws1, ws3, ws2)


def kernel(x_prompt, x_sample, state_pool, cache_swa_k, cache_swa_v, c_prompt, c_sample, w_ada, b_ada,
           g_attn_norm, w_in, g_q, g_k, w_pool, pool_scale, w_out, attn_sinks, rel_bias, g_ffn_norm,
           w_router, router_bias, w1, w3, w2, ws1, ws3, ws2):
    B, L, D = x_prompt.shape
    BS, LS, _ = x_sample.shape
    depth = w_ada.shape[0]
    assert depth == 1
    W = cache_swa_k.shape[2]
    tp, ts = B * L, BS * LS
    T = tp + ts
    n_rows = (T * TOP_K // EXPERT_BLOCK + N_EXPERTS) * EXPERT_BLOCK
    nb = n_rows // EXPERT_BLOCK

    g_attn = g_attn_norm[0].reshape(1, D)
    g_ffn = g_ffn_norm[0].reshape(1, D)
    w_in_b = w_in[0].astype(bf16)
    w_out_b = w_out[0].astype(bf16)
    w_pool_b = w_pool[0].astype(bf16)
    ps = pool_scale[0].reshape(1, POOL_W)
    gqk = jnp.concatenate([jnp.tile(g_q[0], N_HEADS), jnp.tile(g_k[0], N_KV_HEADS)]).reshape(1, QK_W)
    head_of = np.arange(QK_W) // HEAD_DIM
    bd = jnp.asarray((head_of[:, None] == head_of[None, :]).astype(np.float32), bf16)
    wr_t = w_router[0].T
    wr_hi = wr_t.astype(bf16)
    wr_lo = (wr_t - wr_hi.astype(f32)).astype(bf16)
    ws1_b, ws3_b, ws2_b = ws1[0].astype(bf16), ws3[0].astype(bf16), ws2[0].astype(bf16)
    sinks = attn_sinks[0]

    mod = _ada(jnp.concatenate([c_prompt, c_sample], axis=0), w_ada[0], b_ada[0])
    mod_p = mod[:B].reshape(B, 1, 6 * D)
    mod_s = mod[B:].reshape(BS, 1, 6 * D)

    dist_p = np.arange(WINDOW)[:, None] + WINDOW - np.arange(2 * WINDOW)[None, :]
    bias_p = _relbias(rel_bias, dist_p)
    bias_p = bias_p.reshape(N_KV_HEADS, GQA, WINDOW, 2 * WINDOW).transpose(0, 2, 1, 3).reshape(
        N_KV_HEADS, WINDOW, GQA * 2 * WINDOW)
    dist_s = np.arange(LS)[:, None] + W - np.arange(W + LS)[None, :]
    bias_s = _relbias(rel_bias, dist_s)
    bias_s_buf = bias_s[:, :, :W].reshape(N_KV_HEADS, GQA * LS, W)
    bias_s_new = bias_s[:, :, W:].reshape(N_KV_HEADS, GQA * LS, LS)
    sink_col = jnp.repeat(sinks, LS).reshape(N_KV_HEADS, GQA * LS, 1)

    q_p, k_p, v_p, po_p, new_pool_p, kc_p, vc_p = _inproj_prompt(
        x_prompt, mod_p, g_attn, w_in_b, gqk, bd, w_pool_b, ps)
    q_s, k_s, v_s, po_s, new_pool_s = _inproj_sample(
        x_sample, mod_s, g_attn, w_in_b, gqk, bd, w_pool_b, ps, state_pool[0], PAST_LEN)
    at_p = _attn_prompt(q_p, k_p, v_p, bias_p, sinks)
    at_s, nk_s, nv_s = _attn_sample(
        q_s.reshape(BS, LS, ATTN_W), cache_swa_k[0].reshape(BS, W, KV_W), cache_swa_v[0].reshape(BS, W, KV_W),
        k_s.reshape(BS, LS, KV_W), v_s.reshape(BS, LS, KV_W), bias_s_buf, bias_s_new, sink_col)

    x1_p, h2_p, lg_p = _outproj_prompt(po_p, at_p, x_prompt, mod_p, g_ffn, w_out_b, wr_hi, wr_lo)
    x1_s, h2_s, lg_s = _outproj_sample(po_s, at_s.reshape(ts, ATTN_W), x_sample, mod_s, g_ffn, w_out_b,
                                       wr_hi, wr_lo)

    idx, rank, gate, counts = _route(jnp.concatenate([lg_p, lg_s], axis=1), router_bias[0])
    counts = counts.reshape(N_EXPERTS).astype(i32)
    padded = (counts + EXPERT_BLOCK - 1) // EXPERT_BLOCK * EXPERT_BLOCK
    pad_end = jnp.cumsum(padded)
    pad_start = pad_end - padded
    dest = _dest_rows(idx, rank, pad_start)
    n_valid = (pad_end[-1] // EXPERT_BLOCK).astype(i32).reshape(1)
    blk_row0 = jnp.arange(nb, dtype=i32) * EXPERT_BLOCK
    blk_e = jnp.minimum(jnp.sum(blk_row0[:, None] >= pad_end[None, :], axis=1), N_EXPERTS - 1).astype(i32)
    own = jnp.arange(N_EXPERTS, dtype=i32)[None, :] == blk_e[:, None]
    blk_cnt = jnp.sum(jnp.where(own, counts[None, :], 0), axis=1)
    blk_start = jnp.sum(jnp.where(own, pad_start[None, :], 0), axis=1)
    blk_rows = jnp.clip(blk_cnt - (blk_row0 - blk_start), 0, EXPERT_BLOCK).astype(i32)
    blk_next = (jnp.sum(jnp.where(own, pad_end[None, :], 0), axis=1) // EXPERT_BLOCK).astype(i32)

    xs, gs = _dispatch(h2_p, h2_s, dest, gate, n_rows)
    ys = _gmm(xs, gs, w1[0], w3[0], w2[0], blk_e, blk_rows, blk_next, n_valid)
    comb = _combine(ys, dest)

    y_p = _final_prompt(x1_p, h2_p, comb, mod_p, ws1_b, ws3_b, ws2_b, B, L)
    y_s = _final_sample(x1_s, h2_s, comb, mod_s, ws1_b, ws3_b, ws2_b, BS, LS, tp)

    return (y_p, y_s, new_pool_p[None], kc_p.reshape(1, B, WINDOW, N_KV_HEADS, HEAD_DIM),
            vc_p.reshape(1, B, WINDOW, N_KV_HEADS, HEAD_DIM), new_pool_s[None],
            nk_s.reshape(1, BS, W, N_KV_HEADS, HEAD_DIM), nv_s.reshape(1, BS, W, N_KV_HEADS, HEAD_DIM))
```

```python
import functools
import math

import numpy as np
import jax
import jax.numpy as jnp
from jax import lax
from jax.experimental import pallas as pl
from jax.experimental.pallas import tpu as pltpu
from jax.experimental.pallas import tpu_sc as plsc

f32 = jnp.float32
bf16 = jnp.bfloat16
i32 = jnp.int32

D_MODEL = 1024
PAST_LEN = 8192
POOL_W = 512
POOL_WINDOWS = (2, 4, 8, 16)
POOL_GC = 128
POOL_BUF = 15
ATTN_W = 512
HEAD_DIM = 64
N_HEADS = 8
N_KV_HEADS = 2
GQA = 4
KV_W = 128
WINDOW = 128
NUM_BUCKETS = 32
MAX_EXACT = 16
REL_MAX_DIST = 128
N_EXPERTS = 256
N_EXPERT_GROUPS = 8
GROUP_SIZE = 32
TOPK_GROUPS = 4
TOP_K = 8
EXPERT_FF = 256
ROUTED_SCALE = 2.5
EXPERT_BLOCK = 128
EPS = 1e-6
NEG_INF = -1e30
QKV_W = POOL_W + ATTN_W + 2 * KV_W
QK_W = ATTN_W + KV_W
HIST = 16

SC_WORKERS = 32
SC_LANES = 16
GATE_ROW = 128


def _dot(a, b):
    return jnp.dot(a, b, preferred_element_type=f32)


def _dot_t(a, b):
    return lax.dot_general(a, b, (((1,), (1,)), ((), ())), preferred_element_type=f32)


def _split_bf16(a):
    hi = a.astype(bf16)
    lo = (a - hi.astype(f32)).astype(bf16)
    return hi, lo


ROW_RING = 6
W_RING = 4
HI_MASK = -65536


def _pack_pairs(a):
    h = a.shape[1] // 2
    hi = lax.bitcast_convert_type(a[:, :h].astype(bf16).astype(f32), i32)
    lo = lax.bitcast_convert_type(a[:, h:].astype(bf16).astype(f32), i32)
    return hi | lax.shift_right_logical(lo, 16)


def _unpack_pairs(w):
    hi = lax.bitcast_convert_type(w & HI_MASK, f32).astype(bf16)
    lo = lax.bitcast_convert_type(lax.shift_left(w, 16), f32).astype(bf16)
    return hi, lo


def _mod_norm(x, g, sc, sh):
    ms = jnp.mean(x * x, axis=-1, keepdims=True)
    y = x * lax.rsqrt(ms + EPS)
    return (y * g) * (1.0 + sc) + sh


def _ada_body(c_ref, w_ref, b_ref, o_ref):
    c = c_ref[...]
    a = (c * jax.nn.sigmoid(c)).astype(bf16)
    o_ref[...] = _dot(a, w_ref[...].astype(bf16)) + b_ref[...]


def _ada(c, w_ada, b_ada):
    n = c.shape[0]
    tn = 1024
    return pl.pallas_call(
        _ada_body,
        grid=(6 * D_MODEL // tn,),
        in_specs=[
            pl.BlockSpec((n, D_MODEL), lambda j: (0, 0)),
            pl.BlockSpec((D_MODEL, tn), lambda j: (0, j)),
            pl.BlockSpec((1, tn), lambda j: (0, j)),
        ],
        out_specs=pl.BlockSpec((n, tn), lambda j: (0, j)),
        out_shape=jax.ShapeDtypeStruct((n, 6 * D_MODEL), f32),
        name="ada_mod",
    )(c, w_ada, b_ada.reshape(1, -1))


def _relbias_body(table_ref, bucket_ref, o_ref):
    bucket = bucket_ref[...]
    for h in range(N_HEADS):
        acc = jnp.zeros(bucket.shape, f32)
        for b in range(NUM_BUCKETS):
            acc = jnp.where(bucket == b, table_ref[b, h], acc)
        o_ref[h] = acc


def _rel_buckets(dist):
    n = np.maximum(dist, 0)
    nf = np.maximum(n, 1).astype(np.float64)
    large = MAX_EXACT + (np.log(nf / MAX_EXACT) / math.log(REL_MAX_DIST / MAX_EXACT)
                         * (NUM_BUCKETS - MAX_EXACT)).astype(np.int32)
    return np.where(n < MAX_EXACT, n, np.minimum(large, NUM_BUCKETS - 1)).astype(np.int32)


def _relbias(table, dist):
    lq, lk = dist.shape
    return pl.pallas_call(
        _relbias_body,
        in_specs=[
            pl.BlockSpec(memory_space=pltpu.SMEM),
            pl.BlockSpec((lq, lk), lambda: (0, 0)),
        ],
        out_specs=pl.BlockSpec((N_HEADS, lq, lk), lambda: (0, 0, 0)),
        out_shape=jax.ShapeDtypeStruct((N_HEADS, lq, lk), f32),
        name="rel_bias",
    )(table, jnp.asarray(_rel_buckets(dist)))


def _qkv_from_h(hs, w_ref, gqk_ref, bd_ref):
    us = [_dot(h.astype(bf16), w_ref[...]) for h in hs]
    qks = [u[:, POOL_W:POOL_W + QK_W] for u in us]
    sq = [_split_bf16(qk * qk) for qk in qks]
    bd = bd_ref[...]
    ss = [_dot(y_hi, bd) + _dot(y_lo, bd) for y_hi, y_lo in sq]
    qkn = [(qk * lax.rsqrt(s * (1.0 / HEAD_DIM) + EPS)) * gqk_ref[...] for qk, s in zip(qks, ss)]
    return [(u[:, :POOL_W], n[:, :ATTN_W] * (HEAD_DIM ** -0.5), n[:, ATTN_W:], u[:, POOL_W + QK_W:])
            for u, n in zip(us, qkn)]


def _inproj_prompt_body(x_ref, sh_ref, sc_ref, g_ref, w_ref, gqk_ref, bd_ref, wp_ref, ps_ref,
                        q_ref, k_ref, v_ref, po_ref, np_ref, kc_ref, vc_ref, hist, *, tl, nt):
    j = pl.program_id(1)
    nsub = 2
    ts = tl // nsub
    hs = [_mod_norm(x_ref[0, i * ts:(i + 1) * ts, :], g_ref[...], sc_ref[0], sh_ref[0]) for i in range(nsub)]
    parts = _qkv_from_h(hs, w_ref, gqk_ref, bd_ref)

    @pl.when(j == 0)
    def _():
        hist[0:HIST, :] = jnp.zeros((HIST, POOL_W), f32)

    for i, (up, q, k, v) in enumerate(parts):
        rows = slice(i * ts, (i + 1) * ts)
        q_ref[0, rows, :] = q.astype(bf16)
        k_ref[0, rows, :] = k.astype(bf16)
        v_ref[0, rows, :] = v.astype(bf16)
        hist[HIST + i * ts:HIST + (i + 1) * ts, :] = up

    @pl.when(j == nt - 1)
    def _():
        kc_ref[0] = parts[-1][2][ts - WINDOW:, :]
        vc_ref[0] = parts[-1][3][ts - WINDOW:, :]

    pos = j * tl + lax.broadcasted_iota(i32, (tl, 1), 0)
    for g, w in enumerate(POOL_WINDOWS):
        lanes = slice(g * POOL_GC, (g + 1) * POOL_GC)
        cur = hist[HIST:HIST + tl, lanes]
        acc = cur
        for s in range(1, w):
            acc = acc + hist[HIST - s:HIST - s + tl, lanes]
        cnt = jnp.minimum(w, pos + 1).astype(f32)
        d = acc / cnt - cur
        yg = _dot(d.astype(bf16), wp_ref[g]) * ps_ref[:, lanes]
        po_ref[0, :, lanes] = yg.astype(bf16)

    @pl.when(j == nt - 1)
    def _():
        np_ref[0] = hist[tl + 1:tl + HIST, :]

    hist[0:HIST, :] = hist[tl:tl + HIST, :]


def _inproj_prompt(x, mod3, g_attn, w_in, gqk, bd, w_pool, pool_scale, tl=512):
    B, L, D = x.shape
    nt = L // tl
    full = lambda shape: pl.BlockSpec(shape, lambda b, j: (0,) * len(shape))
    return pl.pallas_call(
        functools.partial(_inproj_prompt_body, tl=tl, nt=nt),
        grid=(B, nt),
        in_specs=[
            pl.BlockSpec((1, tl, D), lambda b, j: (b, j, 0)),
            pl.BlockSpec((1, 1, D), lambda b, j: (b, 0, 0)),
            pl.BlockSpec((1, 1, D), lambda b, j: (b, 0, 1)),
            full((1, D)),
            full((D, QKV_W)),
            full((1, QK_W)),
            full((QK_W, QK_W)),
            full((4, POOL_GC, POOL_GC)),
            full((1, POOL_W)),
        ],
        out_specs=[
            pl.BlockSpec((1, tl, ATTN_W), lambda b, j: (b, j, 0)),
            pl.BlockSpec((1, tl, KV_W), lambda b, j: (b, j, 0)),
            pl.BlockSpec((1, tl, KV_W), lambda b, j: (b, j, 0)),
            pl.BlockSpec((1, tl, POOL_W), lambda b, j: (b, j, 0)),
            pl.BlockSpec((1, POOL_BUF, POOL_W), lambda b, j: (b, 0, 0)),
            pl.BlockSpec((1, WINDOW, KV_W), lambda b, j: (b, 0, 0)),
            pl.BlockSpec((1, WINDOW, KV_W), lambda b, j: (b, 0, 0)),
        ],
        out_shape=[
            jax.ShapeDtypeStruct((B, L, ATTN_W), bf16),
            jax.ShapeDtypeStruct((B, L, KV_W), bf16),
            jax.ShapeDtypeStruct((B, L, KV_W), bf16),
            jax.ShapeDtypeStruct((B, L, POOL_W), bf16),
            jax.ShapeDtypeStruct((B, POOL_BUF, POOL_W), f32),
            jax.ShapeDtypeStruct((B, WINDOW, KV_W), f32),
            jax.ShapeDtypeStruct((B, WINDOW, KV_W), f32),
        ],
        scratch_shapes=[pltpu.VMEM((HIST + tl, POOL_W), f32)],
        compiler_params=pltpu.CompilerParams(dimension_semantics=("arbitrary", "arbitrary")),
        name="inproj_prompt",
    )(x, mod3, mod3, g_attn, w_in, gqk, bd, w_pool, pool_scale)


def _inproj_sample_body(x_ref, sh_ref, sc_ref, g_ref, w_ref, gqk_ref, bd_ref, wp_ref, ps_ref, st_ref,
                        q_ref, k_ref, v_ref, po_ref, np_ref, ext, *, bt, ls, pos0):
    n = bt * ls
    h3 = _mod_norm(x_ref[...], g_ref[...][None], sc_ref[...], sh_ref[...])
    (up, q, k, v), = _qkv_from_h([h3.reshape(n, D_MODEL)], w_ref, gqk_ref, bd_ref)
    q_ref[...] = q.astype(bf16)
    k_ref[...] = k
    v_ref[...] = v

    ext[:, 1:HIST, :] = st_ref[...]
    ext[:, HIST:HIST + ls, :] = up.reshape(bt, ls, POOL_W)
    pos = pos0 + lax.broadcasted_iota(i32, (1, ls, 1), 1)
    for g, w in enumerate(POOL_WINDOWS):
        lanes = slice(g * POOL_GC, (g + 1) * POOL_GC)
        cur = ext[:, HIST:HIST + ls, lanes]
        acc = cur
        for s in range(1, w):
            acc = acc + ext[:, HIST - s:HIST - s + ls, lanes]
        cnt = jnp.minimum(w, pos + 1).astype(f32)
        d = (acc / cnt - cur).reshape(n, POOL_GC)
        yg = _dot(d.astype(bf16), wp_ref[g]) * ps_ref[:, lanes]
        po_ref[:, lanes] = yg.astype(bf16)
    np_ref[...] = ext[:, ls + 1:ls + HIST, :]


def _inproj_sample(x, mod3, g_attn, w_in, gqk, bd, w_pool, pool_scale, state, pos0, bt=64):
    B, ls, D = x.shape
    n = bt * ls
    full = lambda shape: pl.BlockSpec(shape, lambda i: (0,) * len(shape))
    return pl.pallas_call(
        functools.partial(_inproj_sample_body, bt=bt, ls=ls, pos0=pos0),
        grid=(B // bt,),
        in_specs=[
            pl.BlockSpec((bt, ls, D), lambda i: (i, 0, 0)),
            pl.BlockSpec((bt, 1, D), lambda i: (i, 0, 0)),
            pl.BlockSpec((bt, 1, D), lambda i: (i, 0, 1)),
            full((1, D)),
            full((D, QKV_W)),
            full((1, QK_W)),
            full((QK_W, QK_W)),
            full((4, POOL_GC, POOL_GC)),
            full((1, POOL_W)),
            pl.BlockSpec((bt, POOL_BUF, POOL_W), lambda i: (i, 0, 0)),
        ],
        out_specs=[
            pl.BlockSpec((n, ATTN_W), lambda i: (i, 0)),
            pl.BlockSpec((n, KV_W), lambda i: (i, 0)),
            pl.BlockSpec((n, KV_W), lambda i: (i, 0)),
            pl.BlockSpec((n, POOL_W), lambda i: (i, 0)),
            pl.BlockSpec((bt, POOL_BUF, POOL_W), lambda i: (i, 0, 0)),
        ],
        out_shape=[
            jax.ShapeDtypeStruct((B * ls, ATTN_W), bf16),
            jax.ShapeDtypeStruct((B * ls, KV_W), f32),
            jax.ShapeDtypeStruct((B * ls, KV_W), f32),
            jax.ShapeDtypeStruct((B * ls, POOL_W), bf16),
            jax.ShapeDtypeStruct((B, POOL_BUF, POOL_W), f32),
        ],
        scratch_shapes=[pltpu.VMEM((bt, HIST + ls, POOL_W), f32)],
        name="inproj_sample",
    )(x, mod3, mod3, g_attn, w_in, gqk, bd, w_pool, pool_scale, state)


def _softmax_sink(parts, sink):
    m = sink
    for s in parts:
        m = jnp.maximum(m, jnp.max(s, axis=-1, keepdims=True))
    ps = [jnp.exp(s - m) for s in parts]
    denom = jnp.exp(sink - m)
    for p in ps:
        denom = denom + jnp.sum(p, axis=-1, keepdims=True)
    inv = 1.0 / denom
    return [(p * inv).astype(bf16) for p in ps]


def _attn_prompt_body(sinks_ref, q_ref, kp_ref, kc_ref, vp_ref, vc_ref, bias_ref, mask_ref, o_ref):
    j = pl.program_id(1)
    nk = 2 * WINDOW
    qw = GQA * HEAD_DIM
    nq = q_ref.shape[1] // WINDOW
    kall = jnp.concatenate([kp_ref[0], kc_ref[0]], axis=0)
    vall = jnp.concatenate([vp_ref[0], vc_ref[0]], axis=0)
    lane_group = lax.broadcasted_iota(i32, (nk, 2 * KV_W), 1) // HEAD_DIM
    first_has_prev = jnp.minimum(j, 1)

    def spread(t):
        t0 = jnp.concatenate([t, t], axis=1)
        return t0, pltpu.roll(t0, HEAD_DIM, 1)

    def blockdiag(t01, kv):
        t0, t1 = t01
        return jnp.concatenate(
            [jnp.where(lane_group == g, t0 if g % 2 == kv else t1, jnp.zeros_like(t0)) for g in range(GQA)],
            axis=0)

    chains = [(qb, kv) for qb in range(nq) for kv in range(N_KV_HEADS)]
    ksp = [spread(kall[qb * WINDOW:qb * WINDOW + nk]) for qb in range(nq)]
    vsp = [spread(vall[qb * WINDOW:qb * WINDOW + nk]) for qb in range(nq)]
    valid = [mask_ref[first_has_prev] > 0.5] + [mask_ref[1] > 0.5] * (nq - 1)
    s = [_dot_t(q_ref[0, qb * WINDOW:(qb + 1) * WINDOW, kv * qw:(kv + 1) * qw], blockdiag(ksp[qb], kv))
         for qb, kv in chains]
    s = [jnp.where(valid[qb], s[c] + bias_ref[kv], NEG_INF) for c, (qb, kv) in enumerate(chains)]
    p = [jnp.concatenate([_softmax_sink([s[c][:, g * nk:(g + 1) * nk]], sinks_ref[kv * GQA + g])[0]
                          for g in range(GQA)], axis=1) for c, (qb, kv) in enumerate(chains)]
    o = [_dot(p[c], blockdiag(vsp[qb], kv)) for c, (qb, kv) in enumerate(chains)]
    for c, (qb, kv) in enumerate(chains):
        o_ref[0, qb * WINDOW:(qb + 1) * WINDOW, kv * qw:(kv + 1) * qw] = o[c].astype(bf16)


def _attn_prompt(q, k, v, bias, sinks):
    B, L, _ = q.shape
    nq = 4
    nb = L // (nq * WINDOW)
    cur = lambda b, j: (b, j, 0)
    prev = lambda b, j: (b, jnp.maximum(nq * j - 1, 0), 0)
    qi = np.arange(WINDOW)[:, None]
    kc = np.arange(2 * WINDOW)[None, :]
    own = (kc >= WINDOW) & (kc - WINDOW <= qi)
    prv = (kc < WINDOW) & (kc > qi)
    mask = np.stack([np.tile(own, (1, GQA)), np.tile(own | prv, (1, GQA))]).astype(np.float32)
    return pl.pallas_call(
        _attn_prompt_body,
        grid=(B, nb),
        in_specs=[
            pl.BlockSpec(memory_space=pltpu.SMEM),
            pl.BlockSpec((1, nq * WINDOW, ATTN_W), cur),
            pl.BlockSpec((1, WINDOW, KV_W), prev),
            pl.BlockSpec((1, nq * WINDOW, KV_W), cur),
            pl.BlockSpec((1, WINDOW, KV_W), prev),
            pl.BlockSpec((1, nq * WINDOW, KV_W), cur),
            pl.BlockSpec((N_KV_HEADS, WINDOW, GQA * 2 * WINDOW), lambda b, j: (0, 0, 0)),
            pl.BlockSpec((2, WINDOW, GQA * 2 * WINDOW), lambda b, j: (0, 0, 0)),
        ],
        out_specs=pl.BlockSpec((1, nq * WINDOW, ATTN_W), cur),
        out_shape=jax.ShapeDtypeStruct((B, L, ATTN_W), bf16),
        name="attn_prompt",
    )(sinks, q, k, k, v, v, bias, jnp.asarray(mask))


def _attn_sample_body(q_ref, kb_ref, vb_ref, kn_ref, vn_ref, bb_ref, bn_ref, sink_ref,
                      o_ref, nk_ref, nv_ref, *, bb, ls):
    W = kb_ref.shape[1]
    rows = GQA * ls
    qi = lax.broadcasted_iota(i32, (rows, W), 0) % ls
    kj = lax.broadcasted_iota(i32, (rows, W), 1)
    valid_buf = kj > qi
    qi2 = lax.broadcasted_iota(i32, (rows, ls), 0) % ls
    kj2 = lax.broadcasted_iota(i32, (rows, ls), 1)
    valid_new = kj2 <= qi2

    nbat = 4
    ks = [slice(kv * HEAD_DIM, (kv + 1) * HEAD_DIM) for kv in range(N_KV_HEADS)]

    def group(i, carry):
        bs = [i * nbat + u for u in range(nbat)]
        chains = [(u, kv) for u in range(nbat) for kv in range(N_KV_HEADS)]
        qb = [q_ref[b] for b in bs]
        kbuf = [kb_ref[b] for b in bs]
        vbuf = [vb_ref[b] for b in bs]
        knew = [kn_ref[b] for b in bs]
        vnew = [vn_ref[b] for b in bs]
        qg = [jnp.concatenate([qb[u][:, (kv * GQA + g) * HEAD_DIM:(kv * GQA + g + 1) * HEAD_DIM]
                               for g in range(GQA)], axis=0) for u, kv in chains]
        s_buf = [_dot_t(qg[c], kbuf[u][:, ks[kv]].astype(bf16)) for c, (u, kv) in enumerate(chains)]
        s_new = [_dot_t(qg[c], knew[u][:, ks[kv]].astype(bf16)) for c, (u, kv) in enumerate(chains)]
        s_buf = [jnp.where(valid_buf, s_buf[c] + bb_ref[kv], NEG_INF) for c, (u, kv) in enumerate(chains)]
        s_new = [jnp.where(valid_new, s_new[c] + bn_ref[kv], NEG_INF) for c, (u, kv) in enumerate(chains)]
        probs = [_softmax_sink([s_buf[c], s_new[c]], sink_ref[kv]) for c, (u, kv) in enumerate(chains)]
        outs = [_dot(probs[c][0], vbuf[u][:, ks[kv]].astype(bf16)) + _dot(probs[c][1], vnew[u][:, ks[kv]].astype(bf16))
                for c, (u, kv) in enumerate(chains)]
        for u, b in enumerate(bs):
            heads = [outs[u * N_KV_HEADS + kv][g * ls:(g + 1) * ls] for kv in range(N_KV_HEADS) for g in range(GQA)]
            o_ref[b] = jnp.concatenate(heads, axis=-1).astype(bf16)
            nk_ref[b, 0:W - ls, :] = kbuf[u][ls:, :]
            nk_ref[b, W - ls:W, :] = knew[u]
            nv_ref[b, 0:W - ls, :] = vbuf[u][ls:, :]
            nv_ref[b, W - ls:W, :] = vnew[u]
        return carry

    lax.fori_loop(0, bb // nbat, group, 0)


def _attn_sample(q, k_buf, v_buf, k_new, v_new, bias_buf, bias_new, sink_col, bb=16):
    B, ls, _ = q.shape
    W = k_buf.shape[1]
    rows = GQA * ls
    blk = lambda shape: pl.BlockSpec(shape, lambda i: (i, 0, 0))
    full = lambda shape: pl.BlockSpec(shape, lambda i: (0, 0, 0))
    return pl.pallas_call(
        functools.partial(_attn_sample_body, bb=bb, ls=ls),
        grid=(B // bb,),
        in_specs=[
            blk((bb, ls, ATTN_W)),
            blk((bb, W, KV_W)),
            blk((bb, W, KV_W)),
            blk((bb, ls, KV_W)),
            blk((bb, ls, KV_W)),
            full((N_KV_HEADS, rows, W)),
            full((N_KV_HEADS, rows, ls)),
            full((N_KV_HEADS, rows, 1)),
        ],
        out_specs=[blk((bb, ls, ATTN_W)), blk((bb, W, KV_W)), blk((bb, W, KV_W))],
        out_shape=[
            jax.ShapeDtypeStruct((B, ls, ATTN_W), bf16),
            jax.ShapeDtypeStruct((B, W, KV_W), f32),
            jax.ShapeDtypeStruct((B, W, KV_W), f32),
        ],
        name="attn_sample",
    )(q, k_buf, v_buf, k_new, v_new, bias_buf, bias_new, sink_col)


def _outproj_core(po, at, x, gt, sc, sh, g_ref, wo_ref, wrh_ref, wrl_ref):
    mixo = _dot(po, wo_ref[0:POOL_W, :]) + _dot(at, wo_ref[POOL_W:, :])
    x1 = x + gt * mixo.reshape(x.shape)
    h2 = _mod_norm(x1, g_ref[...].reshape((1,) * (x.ndim - 1) + (D_MODEL,)), sc, sh).reshape(-1, D_MODEL)
    h_hi, h_lo = _split_bf16(h2)
    wh = wrh_ref[...]
    logits = _dot_t(wh, h_hi) + (_dot_t(wh, h_lo) + _dot_t(wrl_ref[...], h_hi))
    return x1, _pack_pairs(h2), logits


def _outproj_prompt_body(po_ref, at_ref, x_ref, gt_ref, sc_ref, sh_ref, g_ref, wo_ref, wrh_ref, wrl_ref,
                         x1_ref, h2_ref, lg_ref):
    x1, h2p, logits = _outproj_core(po_ref[0], at_ref[0], x_ref[0], gt_ref[0], sc_ref[0], sh_ref[0],
                                    g_ref, wo_ref, wrh_ref, wrl_ref)
    x1_ref[...] = x1
    h2_ref[...] = h2p
    lg_ref[...] = logits


def _outproj_sample_body(po_ref, at_ref, x_ref, gt_ref, sc_ref, sh_ref, g_ref, wo_ref, wrh_ref, wrl_ref,
                         x1_ref, h2_ref, lg_ref):
    x1, h2p, logits = _outproj_core(po_ref[...], at_ref[...], x_ref[...], gt_ref[...], sc_ref[...], sh_ref[...],
                                    g_ref, wo_ref, wrh_ref, wrl_ref)
    x1_ref[...] = x1.reshape(-1, D_MODEL)
    h2_ref[...] = h2p
    lg_ref[...] = logits


def _outproj_prompt(po, at, x, mod3, g_ffn, w_out, wr_hi, wr_lo, tm=512):
    B, L, D = x.shape
    nt = L // tm
    n_tok = B * L
    full = lambda shape: pl.BlockSpec(shape, lambda b, j: (0,) * len(shape))
    modspec = lambda c: pl.BlockSpec((1, 1, D), lambda b, j: (b, 0, c))
    return pl.pallas_call(
        _outproj_prompt_body,
        grid=(B, nt),
        in_specs=[
            pl.BlockSpec((1, tm, POOL_W), lambda b, j: (b, j, 0)),
            pl.BlockSpec((1, tm, ATTN_W), lambda b, j: (b, j, 0)),
            pl.BlockSpec((1, tm, D), lambda b, j: (b, j, 0)),
            modspec(2), modspec(4), modspec(3),
            full((1, D)), full((D, D)), full((N_EXPERTS, D)), full((N_EXPERTS, D)),
        ],
        out_specs=[
            pl.BlockSpec((tm, D), lambda b, j: (b * nt + j, 0)),
            pl.BlockSpec((tm, D // 2), lambda b, j: (b * nt + j, 0)),
            pl.BlockSpec((N_EXPERTS, tm), lambda b, j: (0, b * nt + j)),
        ],
        out_shape=[
            jax.ShapeDtypeStruct((n_tok, D), f32),
            jax.ShapeDtypeStruct((n_tok, D // 2), i32),
            jax.ShapeDtypeStruct((N_EXPERTS, n_tok), f32),
        ],
        name="outproj_prompt",
    )(po, at, x, mod3, mod3, mod3, g_ffn, w_out, wr_hi, wr_lo)


def _outproj_sample(po, at, x, mod3, g_ffn, w_out, wr_hi, wr_lo, bt=64):
    B, ls, D = x.shape
    n = bt * ls
    full = lambda shape: pl.BlockSpec(shape, lambda i: (0,) * len(shape))
    modspec = lambda c: pl.BlockSpec((bt, 1, D), lambda i: (i, 0, c))
    return pl.pallas_call(
        _outproj_sample_body,
        grid=(B // bt,),
        in_specs=[
            pl.BlockSpec((n, POOL_W), lambda i: (i, 0)),
            pl.BlockSpec((n, ATTN_W), lambda i: (i, 0)),
            pl.BlockSpec((bt, ls, D), lambda i: (i, 0, 0)),
            modspec(2), modspec(4), modspec(3),
            full((1, D)), full((D, D)), full((N_EXPERTS, D)), full((N_EXPERTS, D)),
        ],
        out_specs=[
            pl.BlockSpec((n, D), lambda i: (i, 0)),
            pl.BlockSpec((n, D // 2), lambda i: (i, 0)),
            pl.BlockSpec((N_EXPERTS, n), lambda i: (0, i)),
        ],
        out_shape=[
            jax.ShapeDtypeStruct((B * ls, D), f32),
            jax.ShapeDtypeStruct((B * ls, D // 2), i32),
            jax.ShapeDtypeStruct((N_EXPERTS, B * ls), f32),
        ],
        name="outproj_sample",
    )(po, at, x, mod3, mod3, mod3, g_ffn, w_out, wr_hi, wr_lo)


def _route_body(lga_ref, lgb_ref, rb_ref, tri_ref, idx_ref, rank_ref, gate_ref, cnt_ref, carry, *, tr, nsteps, na):
    step = pl.program_id(0)

    @pl.when(step == 0)
    def _():
        carry[...] = jnp.zeros(carry.shape, f32)

    logits = jnp.where(step < na, lga_ref[...], lgb_ref[...])
    s = jax.nn.sigmoid(logits)
    sb = s + rb_ref[...]
    e_iota = lax.broadcasted_iota(i32, (N_EXPERTS, tr), 0)
    g_iota = lax.broadcasted_iota(i32, (GROUP_SIZE, tr), 0)

    gscore = []
    for g in range(N_EXPERT_GROUPS):
        v = sb[g * GROUP_SIZE:(g + 1) * GROUP_SIZE]
        m1 = jnp.max(v, axis=0, keepdims=True)
        i1 = jnp.min(jnp.where(v == m1, g_iota, GROUP_SIZE), axis=0, keepdims=True)
        m2 = jnp.max(jnp.where(g_iota == i1, -jnp.inf, v), axis=0, keepdims=True)
        gscore.append(m1 + m2)
    parts = []
    for g in range(N_EXPERT_GROUPS):
        beaten = jnp.zeros((1, tr), i32)
        for g2 in range(N_EXPERT_GROUPS):
            if g2 == g:
                continue
            ahead = gscore[g2] > gscore[g]
            if g2 < g:
                ahead = ahead | (gscore[g2] == gscore[g])
            beaten = beaten + ahead.astype(i32)
        keep = beaten < TOPK_GROUPS
        parts.append(jnp.where(keep, sb[g * GROUP_SIZE:(g + 1) * GROUP_SIZE], NEG_INF))
    cur = jnp.concatenate(parts, axis=0)

    sel = jnp.zeros((N_EXPERTS, tr), f32)
    idxs, svals = [], []
    for _ in range(TOP_K):
        m = jnp.max(cur, axis=0, keepdims=True)
        ik = jnp.min(jnp.where(cur == m, e_iota, N_EXPERTS), axis=0, keepdims=True)
        hit = e_iota == ik
        svals.append(jnp.sum(jnp.where(hit, s, 0.0), axis=0, keepdims=True))
        cur = jnp.where(hit, -jnp.inf, cur)
        sel = jnp.where(hit, 1.0, sel)
        idxs.append(ik)
    ssum = svals[0]
    for sv in svals[1:]:
        ssum = ssum + sv
    gate_ref[...] = jnp.concatenate([sv / ssum * ROUTED_SCALE for sv in svals], axis=0)
    idx_ref[...] = jnp.concatenate(idxs, axis=0)

    before = carry[...] + _dot(sel.astype(bf16), tri_ref[...])
    ranks = [jnp.sum(jnp.where(e_iota == ik, before, 0.0), axis=0, keepdims=True) for ik in idxs]
    rank_ref[...] = jnp.concatenate(ranks, axis=0).astype(i32)
    carry[...] = carry[...] + jnp.sum(sel, axis=1, keepdims=True)

    @pl.when(step == nsteps - 1)
    def _():
        cnt_ref[...] = carry[...]


def _route(logits_a, logits_b, router_bias, tr=512):
    E, Ta = logits_a.shape
    T = Ta + logits_b.shape[1]
    na = Ta // tr
    nsteps = T // tr
    tri = jnp.asarray(np.triu(np.ones((tr, tr), np.float32), 1), bf16)
    return pl.pallas_call(
        functools.partial(_route_body, tr=tr, nsteps=nsteps, na=na),
        grid=(nsteps,),
        in_specs=[
            pl.BlockSpec((E, tr), lambda i: (0, jnp.minimum(i, na - 1))),
            pl.BlockSpec((E, tr), lambda i: (0, jnp.maximum(i - na, 0))),
            pl.BlockSpec((E, 1), lambda i: (0, 0)),
            pl.BlockSpec((tr, tr), lambda i: (0, 0)),
        ],
        out_specs=[
            pl.BlockSpec((TOP_K, tr), lambda i: (0, i)),
            pl.BlockSpec((TOP_K, tr), lambda i: (0, i)),
            pl.BlockSpec((TOP_K, tr), lambda i: (0, i)),
            pl.BlockSpec((E, 1), lambda i: (0, 0)),
        ],
        out_shape=[
            jax.ShapeDtypeStruct((TOP_K, T), i32),
            jax.ShapeDtypeStruct((TOP_K, T), i32),
            jax.ShapeDtypeStruct((TOP_K, T), f32),
            jax.ShapeDtypeStruct((E, 1), f32),
        ],
        scratch_shapes=[pltpu.VMEM((E, 1), f32)],
        compiler_params=pltpu.CompilerParams(dimension_semantics=("arbitrary",)),
        name="route",
    )(logits_a, logits_b, router_bias.reshape(E, 1), tri)


def _dest_body(idx_ref, rank_ref, ps_ref, dest_ref, *, tr):
    e_iota = lax.broadcasted_iota(i32, (N_EXPERTS, tr), 0)
    start = ps_ref[...]
    rows = []
    for k in range(TOP_K):
        hit = e_iota == idx_ref[k:k + 1, :]
        rows.append(jnp.sum(jnp.where(hit, start, 0.0), axis=0, keepdims=True))
    dest_ref[...] = jnp.concatenate(rows, axis=0).astype(i32) + rank_ref[...]


def _dest_rows(idx, rank, pad_start, tr=512):
    K, T = idx.shape
    blk = pl.BlockSpec((K, tr), lambda i: (0, i))
    return pl.pallas_call(
        functools.partial(_dest_body, tr=tr),
        grid=(T // tr,),
        in_specs=[blk, blk, pl.BlockSpec((N_EXPERTS, 1), lambda i: (0, 0))],
        out_specs=blk,
        out_shape=jax.ShapeDtypeStruct((K, T), i32),
        name="dest_rows",
    )(idx, rank, pad_start.astype(f32).reshape(N_EXPERTS, 1))


def _sc_mesh():
    return plsc.VectorSubcoreMesh(core_axis_name="c", subcore_axis_name="s")


def _sc_worker_id():
    return lax.axis_index("s") * 2 + lax.axis_index("c")


def _dispatch(h2_a, h2_b, dest, gate, n_rows, chunk=32):
    ta, Dw = h2_a.shape
    T = ta + h2_b.shape[0]
    per_worker = T // SC_WORKERS
    nchunk = per_worker // chunk
    assert per_worker * SC_WORKERS == T and nchunk * chunk == per_worker and ta % chunk == 0

    nrow_idx = nchunk * TOP_K

    @functools.partial(
        pl.kernel, mesh=_sc_mesh(),
        out_type=[jax.ShapeDtypeStruct((n_rows, Dw), i32), jax.ShapeDtypeStruct((n_rows, GATE_ROW), f32)],
        scratch_types=[pltpu.VMEM((nrow_idx, chunk), i32), pltpu.VMEM((nrow_idx, chunk), f32),
                       pltpu.VMEM((chunk, Dw), i32), pltpu.VMEM((TOP_K, chunk, GATE_ROW), f32),
                       pltpu.SemaphoreType.DMA],
        compiler_params=pltpu.CompilerParams(needs_layout_passes=False),
        name="moe_dispatch",
    )
    def body(ha_hbm, hb_hbm, dest_hbm, gate_hbm, xs_hbm, gs_hbm, idx_v, gate_v, rows_v, grow_v, sem):
        wid = _sc_worker_id()
        base = wid * per_worker
        pltpu.sync_copy(dest_hbm.at[wid], idx_v)
        pltpu.sync_copy(gate_hbm.at[wid], gate_v)
        zero = jnp.zeros((SC_LANES,), f32)
        for k in range(TOP_K):
            @pl.loop(0, chunk)
            def _(t):
                for j in range(GATE_ROW // SC_LANES):
                    grow_v[k, t, pl.ds(j * SC_LANES, SC_LANES)] = zero

        @pl.loop(0, nchunk)
        def _(ci):
            t0 = base + ci * chunk

            @pl.when(t0 < ta)
            def _():
                pltpu.sync_copy(ha_hbm.at[pl.ds(t0, chunk)], rows_v)

            @pl.when(t0 >= ta)
            def _():
                pltpu.sync_copy(hb_hbm.at[pl.ds(t0 - ta, chunk)], rows_v)

            for k in range(TOP_K):
                @pl.loop(0, chunk)
                def _(t):
                    row = jnp.zeros((SC_LANES,), i32) + (ci * TOP_K + k)
                    grow_v[k, t, pl.ds(0, SC_LANES)] = plsc.load_gather(
                        gate_v, [row, jnp.zeros((SC_LANES,), i32) + t])

            cps = []
            for k in range(TOP_K):
                idx = idx_v.at[ci * TOP_K + k]
                cps.append(pltpu.make_async_copy(rows_v, xs_hbm.at[idx], sem))
                cps.append(pltpu.make_async_copy(grow_v.at[k], gs_hbm.at[idx], sem))
            for cp in cps:
                cp.start()
            for cp in cps:
                cp.wait()

    def per_worker_rows(a):
        return a.reshape(TOP_K, SC_WORKERS, nchunk, chunk).transpose(1, 2, 0, 3).reshape(
            SC_WORKERS, nrow_idx, chunk)

    return body(h2_a, h2_b, per_worker_rows(dest), per_worker_rows(gate))


def _combine(ys, dest, chunk=8):
    T = dest.shape[1]
    Dw = ys.shape[1]
    per_worker = T // SC_WORKERS
    nchunk = per_worker // chunk
    assert per_worker * SC_WORKERS == T and nchunk * chunk == per_worker and nchunk % 2 == 0

    @functools.partial(
        pl.kernel, mesh=_sc_mesh(),
        out_type=jax.ShapeDtypeStruct((T, 2 * Dw), f32),
        scratch_types=[
            pltpu.VMEM((TOP_K * per_worker,), i32),
            pltpu.VMEM((2, TOP_K, chunk, Dw), i32),
            pltpu.VMEM((chunk, 2 * Dw), f32),
            pltpu.SemaphoreType.DMA((2,)),
        ],
        compiler_params=pltpu.CompilerParams(needs_layout_passes=False),
        name="moe_combine",
    )
    def body(ys_hbm, dest_hbm, out_hbm, idx_v, buf, out_v, sems):
        base = _sc_worker_id() * per_worker
        pltpu.sync_copy(dest_hbm.at[pl.ds(_sc_worker_id() * (TOP_K * per_worker), TOP_K * per_worker)], idx_v)

        def gather(ci, slot):
            return [pltpu.make_async_copy(ys_hbm.at[idx_v.at[pl.ds(k * per_worker + ci * chunk, chunk)]],
                                          buf.at[slot, k], sems.at[slot]) for k in range(TOP_K)]

        for cp in gather(0, 0):
            cp.start()

        @pl.loop(0, nchunk, step=2)
        def _(c0):
            for slot in range(2):
                ci = c0 + slot

                @pl.when(ci + 1 < nchunk)
                def _():
                    for cp in gather(ci + 1, 1 - slot):
                        cp.start()

                for cp in gather(ci, slot):
                    cp.wait()

                @pl.loop(0, chunk)
                def _(t):
                    @plsc.parallel_loop(0, Dw // SC_LANES, unroll=4)
                    def _(j):
                        sl = pl.ds(j * SC_LANES, SC_LANES)
                        w = buf[slot, 0, t, sl]
                        hi = plsc.bitcast(w & HI_MASK, f32)
                        lo = plsc.bitcast(lax.shift_left(w, 16), f32)
                        for k in range(1, TOP_K):
                            w = buf[slot, k, t, sl]
                            hi = hi + plsc.bitcast(w & HI_MASK, f32)
                            lo = lo + plsc.bitcast(lax.shift_left(w, 16), f32)
                        out_v[t, sl] = hi
                        out_v[t, pl.ds(Dw + j * SC_LANES, SC_LANES)] = lo

                pltpu.sync_copy(out_v, out_hbm.at[pl.ds(base + ci * chunk, chunk)])

    dest_w = dest.reshape(TOP_K, SC_WORKERS, per_worker).transpose(1, 0, 2).reshape(-1)
    return body(ys, dest_w)


def _gmm_body(blk_e_ref, blk_rows_ref, blk_next_ref, nv_ref, xs_hbm, gs_hbm, w1_hbm, w3_hbm, w2_hbm, ys_hbm,
              xbuf, gbuf, ybuf, w1f, w3f, w2f, xsem, gsem, ysem, wsem):
    nv = nv_ref[0]
    nb = blk_e_ref.shape[0]
    half = D_MODEL // 2
    RB = EXPERT_BLOCK
    ahead_w = W_RING - 2

    def expert_of(blk):
        return blk_e_ref[jnp.minimum(blk, nb - 1)]

    def next_expert_block(blk):
        return jnp.where(blk < nv, blk_next_ref[jnp.minimum(blk, nb - 1)], blk)

    def start_weights(blk, ordinal):
        @pl.when(blk < nv)
        def _():
            for cp in weight_copies(expert_of(blk), lax.rem(ordinal, W_RING)):
                cp.start()

    def row_copies(b, slot):
        r0 = pl.multiple_of(b * RB, RB)
        return (pltpu.make_async_copy(xs_hbm.at[pl.ds(r0, RB)], xbuf.at[slot], xsem.at[slot]),
                pltpu.make_async_copy(gs_hbm.at[pl.ds(r0, RB)], gbuf.at[slot], gsem.at[slot]))

    def out_copy(b, slot):
        r0 = pl.multiple_of(b * RB, RB)
        return pltpu.make_async_copy(ybuf.at[slot], ys_hbm.at[pl.ds(r0, RB)], ysem.at[slot])

    def weight_copies(e, ws):
        return (pltpu.make_async_copy(w1_hbm.at[e], w1f.at[ws], wsem.at[ws, 0]),
                pltpu.make_async_copy(w3_hbm.at[e], w3f.at[ws], wsem.at[ws, 1]),
                pltpu.make_async_copy(w2_hbm.at[e], w2f.at[ws], wsem.at[ws, 2]))

    blk = jnp.int32(0)
    for n in range(ahead_w):
        start_weights(blk, n)
        blk = next_expert_block(blk)
    for i in range(ROW_RING - 2):
        @pl.when(i < nv)
        def _():
            for cp in row_copies(i, i):
                cp.start()

    def enter(b, live, ordinal_prev):
        e = expert_of(b)
        first = live & ((b == 0) | (e != expert_of(jnp.maximum(b - 1, 0))))
        ordinal = jnp.where(first & (b > 0), ordinal_prev + 1, ordinal_prev)

        @pl.when(first)
        def _():
            for cp in weight_copies(e, lax.rem(ordinal, W_RING)):
                cp.wait()
            nxt = b
            for _ in range(ahead_w):
                nxt = next_expert_block(nxt)
            start_weights(nxt, ordinal + ahead_w)

        return ordinal

    def load_block(b, slot, ws):
        valid = lax.broadcasted_iota(i32, (RB, 1), 0) < blk_rows_ref[jnp.minimum(b, nb - 1)]
        x_hi, x_lo = _unpack_pairs(jnp.where(valid, xbuf[slot], 0))
        g = jnp.where(valid, gbuf[slot][:, 0:1], 0.0)
        return x_hi, x_lo, g, ws

    def pair(p, ordinal_prev):
        b0 = 2 * p
        b1 = b0 + 1
        live1 = b1 < nv
        slot0 = lax.rem(b0, ROW_RING)
        slots = (slot0, jnp.where(live1, lax.rem(b1, ROW_RING), slot0))

        for b in (b0 + ROW_RING - 2, b1 + ROW_RING - 2):
            @pl.when(b < nv)
            def _():
                for cp in row_copies(b, lax.rem(b, ROW_RING)):
                    cp.start()

        ord0 = enter(b0, b0 < nv, ordinal_prev)
        ord1 = enter(b1, live1, ord0)
        for cp in row_copies(b0, slots[0]):
            cp.wait()

        @pl.when(live1)
        def _():
            for cp in row_copies(b1, slots[1]):
                cp.wait()

        blocks = [load_block(b0, slots[0], lax.rem(ord0, W_RING)),
                  load_block(jnp.where(live1, b1, b0), slots[1], lax.rem(ord1, W_RING))]
        a = [_dot(xh, w1f[ws, 0:half, :].astype(bf16)) + _dot(xl, w1f[ws, half:, :].astype(bf16))
             for xh, xl, _, ws in blocks]
        c = [_dot(xh, w3f[ws, 0:half, :].astype(bf16)) + _dot(xl, w3f[ws, half:, :].astype(bf16))
             for xh, xl, _, ws in blocks]
        hmid = [((a[i] * jax.nn.sigmoid(a[i])) * c[i]).astype(bf16) for i in range(2)]
        y = [_pack_pairs(_dot(hmid[i], w2f[blocks[i][3]].astype(bf16)) * blocks[i][2]) for i in range(2)]

        for i, (b, live) in enumerate(((b0, b0 < nv), (b1, live1))):
            @pl.when(live & (b >= ROW_RING))
            def _():
                out_copy(b - ROW_RING, slots[i]).wait()

            @pl.when(live)
            def _():
                ybuf[slots[i]] = y[i]
                out_copy(b, slots[i]).start()

        return ord1

    lax.fori_loop(0, (nv + 1) // 2, pair, 0)

    for i in range(1, ROW_RING + 1):
        @pl.when(nv >= i)
        def _():
            out_copy(nv - i, lax.rem(nv - i, ROW_RING)).wait()


def _gmm(xs, gs, w1, w3, w2, blk_e, blk_rows, blk_next, n_valid):
    n_rows, Dw = xs.shape
    D = 2 * Dw
    RB = EXPERT_BLOCK
    hbm = pl.BlockSpec(memory_space=pl.ANY)
    return pl.pallas_call(
        _gmm_body,
        grid_spec=pltpu.PrefetchScalarGridSpec(
            num_scalar_prefetch=4,
            grid=(1,),
            in_specs=[hbm, hbm, hbm, hbm, hbm],
            out_specs=hbm,
            scratch_shapes=[
                pltpu.VMEM((ROW_RING, RB, Dw), i32), pltpu.VMEM((ROW_RING, RB, GATE_ROW), f32),
                pltpu.VMEM((ROW_RING, RB, Dw), i32),
                pltpu.VMEM((W_RING, D, EXPERT_FF), f32), pltpu.VMEM((W_RING, D, EXPERT_FF), f32),
                pltpu.VMEM((W_RING, EXPERT_FF, D), f32),
                pltpu.SemaphoreType.DMA((ROW_RING,)), pltpu.SemaphoreType.DMA((ROW_RING,)),
                pltpu.SemaphoreType.DMA((ROW_RING,)), pltpu.SemaphoreType.DMA((W_RING, 3)),
            ],
        ),
        out_shape=jax.ShapeDtypeStruct((n_rows, Dw), i32),
        compiler_params=pltpu.CompilerParams(dimension_semantics=("arbitrary",)),
        name="moe_gmm",
    )(blk_e, blk_rows, blk_next, n_valid, xs, gs, w1, w3, w2)


def _final_core(x1, h2p, comb, gt, ws1_ref, ws3_ref, ws2_ref):
    half = D_MODEL // 2
    h_hi, h_lo = _unpack_pairs(h2p)
    a = _dot(h_hi, ws1_ref[0:half, :]) + _dot(h_lo, ws1_ref[half:, :])
    c = _dot(h_hi, ws3_ref[0:half, :]) + _dot(h_lo, ws3_ref[half:, :])
    shared = _dot(((a * jax.nn.sigmoid(a)) * c).astype(bf16), ws2_ref[...])
    return x1, comb + shared, gt


def _final_prompt_body(x1_ref, h2_ref, cb_ref, gt_ref, ws1_ref, ws3_ref, ws2_ref, *rest):
    y_ref = rest[-1]
    x1, ffn, gt = _final_core(x1_ref[...], h2_ref[...], cb_ref[...], gt_ref[0], ws1_ref, ws3_ref, ws2_ref)
    y_ref[0] = x1 + gt * ffn


def _final_sample_body(x1_ref, h2_ref, cb_ref, gt_ref, ws1_ref, ws3_ref, ws2_ref, y_ref):
    x1, ffn, gt = _final_core(x1_ref[...], h2_ref[...], cb_ref[...], gt_ref[...], ws1_ref, ws3_ref, ws2_ref)
    shp = y_ref.shape
    y_ref[...] = x1.reshape(shp) + gt * ffn.reshape(shp)


def _final_prompt(x1, h2, comb, mod3, ws1, ws3, ws2, B, L, b0, nbat, y_prev=None, tm=512):
    D = D_MODEL
    nt = L // tm
    full = lambda shape: pl.BlockSpec(shape, lambda b, j: (0,) * len(shape))
    rows = pl.BlockSpec((tm, D), lambda b, j: ((b0 + b) * nt + j, 0))
    words = pl.BlockSpec((tm, D // 2), lambda b, j: ((b0 + b) * nt + j, 0))
    comb_rows = pl.BlockSpec((tm, D), lambda b, j: (b * nt + j, 0))
    extra_specs, extra_args, aliases = [], [], {}
    if y_prev is not None:
        extra_specs, extra_args, aliases = [pl.BlockSpec(memory_space=pl.ANY)], [y_prev], {7: 0}
    return pl.pallas_call(
        _final_prompt_body,
        grid=(nbat, nt),
        in_specs=[rows, words, comb_rows, pl.BlockSpec((1, 1, D), lambda b, j: (b0 + b, 0, 5)),
                  full((D, EXPERT_FF)), full((D, EXPERT_FF)), full((EXPERT_FF, D))] + extra_specs,
        out_specs=pl.BlockSpec((1, tm, D), lambda b, j: (b0 + b, j, 0)),
        out_shape=jax.ShapeDtypeStruct((B, L, D), f32),
        input_output_aliases=aliases,
        name="final_prompt",
    )(x1, h2, comb, mod3, ws1, ws3, ws2, *extra_args)


def _final_sample(x1, h2, comb, mod3, ws1, ws3, ws2, B, ls, row0, bt=64):
    D = D_MODEL
    n = bt * ls
    blk0 = row0 // n
    full = lambda shape: pl.BlockSpec(shape, lambda i: (0,) * len(shape))
    rows = pl.BlockSpec((n, D), lambda i: (i, 0))
    words = pl.BlockSpec((n, D // 2), lambda i: (i, 0))
    comb_rows = pl.BlockSpec((n, D), lambda i: (blk0 + i, 0))
    return pl.pallas_call(
        _final_sample_body,
        grid=(B // bt,),
        in_specs=[rows, words, comb_rows, pl.BlockSpec((bt, 1, D), lambda i: (i, 0, 5)),
                  full((D, EXPERT_FF)), full((D, EXPERT_FF)), full((EXPERT_FF, D))],
        out_specs=pl.BlockSpec((bt, ls, D), lambda i: (i, 0, 0)),
        out_shape=jax.ShapeDtypeStruct((B, ls, D), f32),
        name="final_sample",
    )(x1, h2, comb, mod3, ws1, ws3, ws2)


def kernel(x_prompt, x_sample, state_pool, cache_swa_k, cache_swa_v, c_prompt, c_sample, w_ada, b_ada,
           g_attn_norm, w_in, g_q, g_k, w_pool, pool_scale, w_out, attn_sinks, rel_bias, g_ffn_norm,
           w_router, router_bias, w1, w3, w2, ws1, ws3, ws2):
    B, L, D = x_prompt.shape
    BS, LS, _ = x_sample.shape
    depth = w_ada.shape[0]
    assert depth == 1
    W = cache_swa_k.shape[2]
    tp, ts = B * L, BS * LS
    T = tp + ts
    n_rows = (T * TOP_K // EXPERT_BLOCK + N_EXPERTS) * EXPERT_BLOCK
    nb = n_rows // EXPERT_BLOCK

    g_attn = g_attn_norm[0].reshape(1, D)
    g_ffn = g_ffn_norm[0].reshape(1, D)
    w_in_b = w_in[0].astype(bf16)
    w_out_b = w_out[0].astype(bf16)
    w_pool_b = w_pool[0].astype(bf16)
    ps = pool_scale[0].reshape(1, POOL_W)
    gqk = jnp.concatenate([jnp.tile(g_q[0], N_HEADS), jnp.tile(g_k[0], N_KV_HEADS)]).reshape(1, QK_W)
    head_of = np.arange(QK_W) // HEAD_DIM
    bd = jnp.asarray((head_of[:, None] == head_of[None, :]).astype(np.float32), bf16)
    wr_t = w_router[0].T
    wr_hi = wr_t.astype(bf16)
    wr_lo = (wr_t - wr_hi.astype(f32)).astype(bf16)
    ws1_b, ws3_b, ws2_b = ws1[0].astype(bf16), ws3[0].astype(bf16), ws2[0].astype(bf16)
    sinks = attn_sinks[0]

    mod = _ada(jnp.concatenate([c_prompt, c_sample], axis=0), w_ada[0], b_ada[0])
    mod_p = mod[:B].reshape(B, 1, 6 * D)
    mod_s = mod[B:].reshape(BS, 1, 6 * D)

    dist_p = np.arange(WINDOW)[:, None] + WINDOW - np.arange(2 * WINDOW)[None, :]
    bias_p = _relbias(rel_bias, dist_p)
    bias_p = bias_p.reshape(N_KV_HEADS, GQA, WINDOW, 2 * WINDOW).transpose(0, 2, 1, 3).reshape(
        N_KV_HEADS, WINDOW, GQA * 2 * WINDOW)
    dist_s = np.arange(LS)[:, None] + W - np.arange(W + LS)[None, :]
    bias_s = _relbias(rel_bias, dist_s)
    bias_s_buf = bias_s[:, :, :W].reshape(N_KV_HEADS, GQA * LS, W)
    bias_s_new = bias_s[:, :, W:].reshape(N_KV_HEADS, GQA * LS, LS)
    sink_col = jnp.repeat(sinks, LS).reshape(N_KV_HEADS, GQA * LS, 1)

    q_p, k_p, v_p, po_p, new_pool_p, kc_p, vc_p = _inproj_prompt(
        x_prompt, mod_p, g_attn, w_in_b, gqk, bd, w_pool_b, ps)
    q_s, k_s, v_s, po_s, new_pool_s = _inproj_sample(
        x_sample, mod_s, g_attn, w_in_b, gqk, bd, w_pool_b, ps, state_pool[0], PAST_LEN)
    at_p = _attn_prompt(q_p, k_p, v_p, bias_p, sinks)
    at_s, nk_s, nv_s = _attn_sample(
        q_s.reshape(BS, LS, ATTN_W), cache_swa_k[0].reshape(BS, W, KV_W), cache_swa_v[0].reshape(BS, W, KV_W),
        k_s.reshape(BS, LS, KV_W), v_s.reshape(BS, LS, KV_W), bias_s_buf, bias_s_new, sink_col)

    x1_p, h2_p, lg_p = _outproj_prompt(po_p, at_p, x_prompt, mod_p, g_ffn, w_out_b, wr_hi, wr_lo)
    x1_s, h2_s, lg_s = _outproj_sample(po_s, at_s.reshape(ts, ATTN_W), x_sample, mod_s, g_ffn, w_out_b,
                                       wr_hi, wr_lo)

    idx, rank, gate, counts = _route(lg_p, lg_s, router_bias[0])
    counts = counts.reshape(N_EXPERTS).astype(i32)
    padded = (counts + EXPERT_BLOCK - 1) // EXPERT_BLOCK * EXPERT_BLOCK
    pad_end = jnp.cumsum(padded)
    pad_start = pad_end - padded
    dest = _dest_rows(idx, rank, pad_start)
    n_valid = (pad_end[-1] // EXPERT_BLOCK).astype(i32).reshape(1)
    blk_row0 = jnp.arange(nb, dtype=i32) * EXPERT_BLOCK
    blk_e = jnp.minimum(jnp.sum(blk_row0[:, None] >= pad_end[None, :], axis=1), N_EXPERTS - 1).astype(i32)
    own = jnp.arange(N_EXPERTS, dtype=i32)[None, :] == blk_e[:, None]
    blk_cnt = jnp.sum(jnp.where(own, counts[None, :], 0), axis=1)
    blk_start = jnp.sum(jnp.where(own, pad_start[None, :], 0), axis=1)
    blk_rows = jnp.clip(blk_cnt - (blk_row0 - blk_start), 0, EXPERT_BLOCK).astype(i32)
    blk_next = (jnp.sum(jnp.where(own, pad_end[None, :], 0), axis=1) // EXPERT_BLOCK).astype(i32)

    xs, gs = _dispatch(h2_p, h2_s, dest, gate, n_rows)
    ys = _gmm(xs, gs, w1[0], w3[0], w2[0], blk_e, blk_rows, blk_next, n_valid)
    bh = B // 2
    th = bh * L
    comb_a = _combine(ys, dest[:, :th])
    comb_b = _combine(ys, dest[:, th:])
    y_p = _final_prompt(x1_p, h2_p, comb_a, mod_p, ws1_b, ws3_b, ws2_b, B, L, 0, bh)
    y_p = _final_prompt(x1_p, h2_p, comb_b, mod_p, ws1_b, ws3_b, ws2_b, B, L, bh, B - bh, y_prev=y_p)
    y_s = _final_sample(x1_s, h2_s, comb_b, mod_s, ws1_b, ws3_b, ws2_b, BS, LS, tp - th)

    return (y_p, y_s, new_pool_p[None], kc_p.reshape(1, B, WINDOW, N_KV_HEADS, HEAD_DIM),
            vc_p.reshape(1, B, WINDOW, N_KV_HEADS, HEAD_DIM), new_pool_s[None],
            nk_s.reshape(1, BS, W, N_KV_HEADS, HEAD_DIM), nv_s.reshape(1, BS, W, N_KV_HEADS, HEAD_DIM))
```

```python
import functools
import math

import numpy as np
import jax
import jax.numpy as jnp
from jax import lax
from jax.experimental import pallas as pl
from jax.experimental.pallas import tpu as pltpu
from jax.experimental.pallas import tpu_sc as plsc

f32 = jnp.float32
bf16 = jnp.bfloat16
i32 = jnp.int32

D_MODEL = 1024
PAST_LEN = 8192
POOL_W = 512
POOL_WINDOWS = (2, 4, 8, 16)
POOL_GC = 128
POOL_BUF = 15
ATTN_W = 512
HEAD_DIM = 64
N_HEADS = 8
N_KV_HEADS = 2
GQA = 4
KV_W = 128
WINDOW = 128
NUM_BUCKETS = 32
MAX_EXACT = 16
REL_MAX_DIST = 128
N_EXPERTS = 256
N_EXPERT_GROUPS = 8
GROUP_SIZE = 32
TOPK_GROUPS = 4
TOP_K = 8
EXPERT_FF = 256
ROUTED_SCALE = 2.5
EXPERT_BLOCK = 128
EPS = 1e-6
NEG_INF = -1e30
QKV_W = POOL_W + ATTN_W + 2 * KV_W
QK_W = ATTN_W + KV_W
HIST = 16

SC_WORKERS = 32
SC_LANES = 16
GATE_ROW = 128


def _dot(a, b):
    return jnp.dot(a, b, preferred_element_type=f32)


def _dot_t(a, b):
    return lax.dot_general(a, b, (((1,), (1,)), ((), ())), preferred_element_type=f32)


def _split_bf16(a):
    hi = a.astype(bf16)
    lo = (a - hi.astype(f32)).astype(bf16)
    return hi, lo


ROW_RING = 8
W_RING = 5
HI_MASK = -65536


def _pack_pairs(a):
    h = a.shape[1] // 2
    hi = lax.bitcast_convert_type(a[:, :h].astype(bf16).astype(f32), i32)
    lo = lax.bitcast_convert_type(a[:, h:].astype(bf16).astype(f32), i32)
    return hi | lax.shift_right_logical(lo, 16)


def _unpack_pairs(w):
    hi = lax.bitcast_convert_type(w & HI_MASK, f32).astype(bf16)
    lo = lax.bitcast_convert_type(lax.shift_left(w, 16), f32).astype(bf16)
    return hi, lo


def _mod_norm(x, g, sc, sh):
    ms = jnp.mean(x * x, axis=-1, keepdims=True)
    y = x * lax.rsqrt(ms + EPS)
    return (y * g) * (1.0 + sc) + sh


def _ada_body(c_ref, w_ref, b_ref, o_ref):
    c = c_ref[...]
    a = (c * jax.nn.sigmoid(c)).astype(bf16)
    o_ref[...] = _dot(a, w_ref[...].astype(bf16)) + b_ref[...]


def _ada(c, w_ada, b_ada):
    n = c.shape[0]
    tn = 1024
    return pl.pallas_call(
        _ada_body,
        grid=(6 * D_MODEL // tn,),
        in_specs=[
            pl.BlockSpec((n, D_MODEL), lambda j: (0, 0)),
            pl.BlockSpec((D_MODEL, tn), lambda j: (0, j)),
            pl.BlockSpec((1, tn), lambda j: (0, j)),
        ],
        out_specs=pl.BlockSpec((n, tn), lambda j: (0, j)),
        out_shape=jax.ShapeDtypeStruct((n, 6 * D_MODEL), f32),
        name="ada_mod",
    )(c, w_ada, b_ada.reshape(1, -1))


def _relbias_body(table_ref, bucket_ref, o_ref):
    bucket = bucket_ref[...]
    for h in range(N_HEADS):
        acc = jnp.zeros(bucket.shape, f32)
        for b in range(NUM_BUCKETS):
            acc = jnp.where(bucket == b, table_ref[b, h], acc)
        o_ref[h] = acc


def _rel_buckets(dist):
    n = np.maximum(dist, 0)
    nf = np.maximum(n, 1).astype(np.float64)
    large = MAX_EXACT + (np.log(nf / MAX_EXACT) / math.log(REL_MAX_DIST / MAX_EXACT)
                         * (NUM_BUCKETS - MAX_EXACT)).astype(np.int32)
    return np.where(n < MAX_EXACT, n, np.minimum(large, NUM_BUCKETS - 1)).astype(np.int32)


def _relbias(table, dist):
    lq, lk = dist.shape
    return pl.pallas_call(
        _relbias_body,
        in_specs=[
            pl.BlockSpec(memory_space=pltpu.SMEM),
            pl.BlockSpec((lq, lk), lambda: (0, 0)),
        ],
        out_specs=pl.BlockSpec((N_HEADS, lq, lk), lambda: (0, 0, 0)),
        out_shape=jax.ShapeDtypeStruct((N_HEADS, lq, lk), f32),
        name="rel_bias",
    )(table, jnp.asarray(_rel_buckets(dist)))


def _qkv_from_h(hs, w_ref, gqk_ref, bd_ref):
    us = [_dot(h.astype(bf16), w_ref[...]) for h in hs]
    qks = [u[:, POOL_W:POOL_W + QK_W] for u in us]
    sq = [_split_bf16(qk * qk) for qk in qks]
    bd = bd_ref[...]
    ss = [_dot(y_hi, bd) + _dot(y_lo, bd) for y_hi, y_lo in sq]
    qkn = [(qk * lax.rsqrt(s * (1.0 / HEAD_DIM) + EPS)) * gqk_ref[...] for qk, s in zip(qks, ss)]
    return [(u[:, :POOL_W], n[:, :ATTN_W] * (HEAD_DIM ** -0.5), n[:, ATTN_W:], u[:, POOL_W + QK_W:])
            for u, n in zip(us, qkn)]


def _inproj_prompt_body(x_ref, sh_ref, sc_ref, g_ref, w_ref, gqk_ref, bd_ref, wp_ref, ps_ref,
                        q_ref, k_ref, v_ref, po_ref, np_ref, kc_ref, vc_ref, hist, wsum, *, tl, nt):
    j = pl.program_id(1)
    nsub = 2
    ts = tl // nsub
    hs = [_mod_norm(x_ref[0, i * ts:(i + 1) * ts, :], g_ref[...], sc_ref[0], sh_ref[0]) for i in range(nsub)]
    parts = _qkv_from_h(hs, w_ref, gqk_ref, bd_ref)

    base = 2 * HIST
    end = base + tl

    @pl.when(j == 0)
    def _():
        hist[0:base, :] = jnp.zeros((base, POOL_W), f32)
        wsum[0:HIST, :] = jnp.zeros((HIST, POOL_W), f32)

    for i, (up, q, k, v) in enumerate(parts):
        rows = slice(i * ts, (i + 1) * ts)
        q_ref[0, rows, :] = q.astype(bf16)
        k_ref[0, rows, :] = k.astype(bf16)
        v_ref[0, rows, :] = v.astype(bf16)
        hist[base + i * ts:base + (i + 1) * ts, :] = up

    @pl.when(j == nt - 1)
    def _():
        kc_ref[0] = parts[-1][2][ts - WINDOW:, :]
        vc_ref[0] = parts[-1][3][ts - WINDOW:, :]

    pos = j * tl + lax.broadcasted_iota(i32, (tl, 1), 0)
    src = hist
    for p, w in enumerate(POOL_WINDOWS):
        sh = w // 2
        live = slice(p * POOL_GC, POOL_W)
        summed = src[HIST:end, live] + src[HIST - sh:end - sh, live]
        wsum[HIST:end, live] = summed
        src = wsum
        lanes = slice(p * POOL_GC, (p + 1) * POOL_GC)
        cur = hist[base:end, lanes]
        cnt = jnp.minimum(w, pos + 1).astype(f32)
        d = summed[HIST:, 0:POOL_GC] / cnt - cur
        yg = _dot(d.astype(bf16), wp_ref[p]) * ps_ref[:, lanes]
        po_ref[0, :, lanes] = yg.astype(bf16)

    @pl.when(j == nt - 1)
    def _():
        np_ref[0] = hist[end - POOL_BUF:end, :]

    hist[HIST:base, :] = hist[end - HIST:end, :]


def _inproj_prompt(x, mod3, g_attn, w_in, gqk, bd, w_pool, pool_scale, tl=512):
    B, L, D = x.shape
    nt = L // tl
    full = lambda shape: pl.BlockSpec(shape, lambda b, j: (0,) * len(shape))
    return pl.pallas_call(
        functools.partial(_inproj_prompt_body, tl=tl, nt=nt),
        grid=(B, nt),
        in_specs=[
            pl.BlockSpec((1, tl, D), lambda b, j: (b, j, 0)),
            pl.BlockSpec((1, 1, D), lambda b, j: (b, 0, 0)),
            pl.BlockSpec((1, 1, D), lambda b, j: (b, 0, 1)),
            full((1, D)),
            full((D, QKV_W)),
            full((1, QK_W)),
            full((QK_W, QK_W)),
            full((4, POOL_GC, POOL_GC)),
            full((1, POOL_W)),
        ],
        out_specs=[
            pl.BlockSpec((1, tl, ATTN_W), lambda b, j: (b, j, 0)),
            pl.BlockSpec((1, tl, KV_W), lambda b, j: (b, j, 0)),
            pl.BlockSpec((1, tl, KV_W), lambda b, j: (b, j, 0)),
            pl.BlockSpec((1, tl, POOL_W), lambda b, j: (b, j, 0)),
            pl.BlockSpec((1, POOL_BUF, POOL_W), lambda b, j: (b, 0, 0)),
            pl.BlockSpec((1, WINDOW, KV_W), lambda b, j: (b, 0, 0)),
            pl.BlockSpec((1, WINDOW, KV_W), lambda b, j: (b, 0, 0)),
        ],
        out_shape=[
            jax.ShapeDtypeStruct((B, L, ATTN_W), bf16),
            jax.ShapeDtypeStruct((B, L, KV_W), bf16),
            jax.ShapeDtypeStruct((B, L, KV_W), bf16),
            jax.ShapeDtypeStruct((B, L, POOL_W), bf16),
            jax.ShapeDtypeStruct((B, POOL_BUF, POOL_W), f32),
            jax.ShapeDtypeStruct((B, WINDOW, KV_W), f32),
            jax.ShapeDtypeStruct((B, WINDOW, KV_W), f32),
        ],
        scratch_shapes=[pltpu.VMEM((2 * HIST + tl, POOL_W), f32), pltpu.VMEM((2 * HIST + tl, POOL_W), f32)],
        compiler_params=pltpu.CompilerParams(dimension_semantics=("arbitrary", "arbitrary")),
        name="inproj_prompt",
    )(x, mod3, mod3, g_attn, w_in, gqk, bd, w_pool, pool_scale)


def _inproj_sample_body(x_ref, sh_ref, sc_ref, g_ref, w_ref, gqk_ref, bd_ref, wp_ref, ps_ref, st_ref,
                        q_ref, k_ref, v_ref, po_ref, np_ref, ext, *, bt, ls, pos0):
    n = bt * ls
    h3 = _mod_norm(x_ref[...], g_ref[...][None], sc_ref[...], sh_ref[...])
    (up, q, k, v), = _qkv_from_h([h3.reshape(n, D_MODEL)], w_ref, gqk_ref, bd_ref)
    q_ref[...] = q.astype(bf16)
    k_ref[...] = k
    v_ref[...] = v

    ext[:, 1:HIST, :] = st_ref[...]
    ext[:, HIST:HIST + ls, :] = up.reshape(bt, ls, POOL_W)
    pos = pos0 + lax.broadcasted_iota(i32, (1, ls, 1), 1)
    for g, w in enumerate(POOL_WINDOWS):
        lanes = slice(g * POOL_GC, (g + 1) * POOL_GC)
        cur = ext[:, HIST:HIST + ls, lanes]
        acc = cur
        for s in range(1, w):
            acc = acc + ext[:, HIST - s:HIST - s + ls, lanes]
        cnt = jnp.minimum(w, pos + 1).astype(f32)
        d = (acc / cnt - cur).reshape(n, POOL_GC)
        yg = _dot(d.astype(bf16), wp_ref[g]) * ps_ref[:, lanes]
        po_ref[:, lanes] = yg.astype(bf16)
    np_ref[...] = ext[:, ls + 1:ls + HIST, :]


def _inproj_sample(x, mod3, g_attn, w_in, gqk, bd, w_pool, pool_scale, state, pos0, bt=64):
    B, ls, D = x.shape
    n = bt * ls
    full = lambda shape: pl.BlockSpec(shape, lambda i: (0,) * len(shape))
    return pl.pallas_call(
        functools.partial(_inproj_sample_body, bt=bt, ls=ls, pos0=pos0),
        grid=(B // bt,),
        in_specs=[
            pl.BlockSpec((bt, ls, D), lambda i: (i, 0, 0)),
            pl.BlockSpec((bt, 1, D), lambda i: (i, 0, 0)),
            pl.BlockSpec((bt, 1, D), lambda i: (i, 0, 1)),
            full((1, D)),
            full((D, QKV_W)),
            full((1, QK_W)),
            full((QK_W, QK_W)),
            full((4, POOL_GC, POOL_GC)),
            full((1, POOL_W)),
            pl.BlockSpec((bt, POOL_BUF, POOL_W), lambda i: (i, 0, 0)),
        ],
        out_specs=[
            pl.BlockSpec((n, ATTN_W), lambda i: (i, 0)),
            pl.BlockSpec((n, KV_W), lambda i: (i, 0)),
            pl.BlockSpec((n, KV_W), lambda i: (i, 0)),
            pl.BlockSpec((n, POOL_W), lambda i: (i, 0)),
            pl.BlockSpec((bt, POOL_BUF, POOL_W), lambda i: (i, 0, 0)),
        ],
        out_shape=[
            jax.ShapeDtypeStruct((B * ls, ATTN_W), bf16),
            jax.ShapeDtypeStruct((B * ls, KV_W), f32),
            jax.ShapeDtypeStruct((B * ls, KV_W), f32),
            jax.ShapeDtypeStruct((B * ls, POOL_W), bf16),
            jax.ShapeDtypeStruct((B, POOL_BUF, POOL_W), f32),
        ],
        scratch_shapes=[pltpu.VMEM((bt, HIST + ls, POOL_W), f32)],
        name="inproj_sample",
    )(x, mod3, mod3, g_attn, w_in, gqk, bd, w_pool, pool_scale, state)


def _softmax_sink(parts, sink):
    m = sink
    for s in parts:
        m = jnp.maximum(m, jnp.max(s, axis=-1, keepdims=True))
    ps = [jnp.exp(s - m) for s in parts]
    denom = jnp.exp(sink - m)
    for p in ps:
        denom = denom + jnp.sum(p, axis=-1, keepdims=True)
    inv = 1.0 / denom
    return [(p * inv).astype(bf16) for p in ps]


def _attn_prompt_body(sinks_ref, q_ref, kp_ref, kc_ref, vp_ref, vc_ref, bias_ref, mask_ref, o_ref):
    j = pl.program_id(1)
    nk = 2 * WINDOW
    qw = GQA * HEAD_DIM
    nq = q_ref.shape[1] // WINDOW
    kall = jnp.concatenate([kp_ref[0], kc_ref[0]], axis=0)
    vall = jnp.concatenate([vp_ref[0], vc_ref[0]], axis=0)
    lane_group = lax.broadcasted_iota(i32, (nk, 2 * KV_W), 1) // HEAD_DIM
    first_has_prev = jnp.minimum(j, 1)

    def spread(t):
        t0 = jnp.concatenate([t, t], axis=1)
        return t0, pltpu.roll(t0, HEAD_DIM, 1)

    def blockdiag(t01, kv):
        t0, t1 = t01
        return jnp.concatenate(
            [jnp.where(lane_group == g, t0 if g % 2 == kv else t1, jnp.zeros_like(t0)) for g in range(GQA)],
            axis=0)

    chains = [(qb, kv) for qb in range(nq) for kv in range(N_KV_HEADS)]
    ksp = [spread(kall[qb * WINDOW:qb * WINDOW + nk]) for qb in range(nq)]
    vsp = [spread(vall[qb * WINDOW:qb * WINDOW + nk]) for qb in range(nq)]
    valid = [mask_ref[first_has_prev] > 0.5] + [mask_ref[1] > 0.5] * (nq - 1)
    s = [_dot_t(q_ref[0, qb * WINDOW:(qb + 1) * WINDOW, kv * qw:(kv + 1) * qw], blockdiag(ksp[qb], kv))
         for qb, kv in chains]
    s = [jnp.where(valid[qb], s[c] + bias_ref[kv], NEG_INF) for c, (qb, kv) in enumerate(chains)]
    p = [jnp.concatenate([_softmax_sink([s[c][:, g * nk:(g + 1) * nk]], sinks_ref[kv * GQA + g])[0]
                          for g in range(GQA)], axis=1) for c, (qb, kv) in enumerate(chains)]
    o = [_dot(p[c], blockdiag(vsp[qb], kv)) for c, (qb, kv) in enumerate(chains)]
    for c, (qb, kv) in enumerate(chains):
        o_ref[0, qb * WINDOW:(qb + 1) * WINDOW, kv * qw:(kv + 1) * qw] = o[c].astype(bf16)


def _attn_prompt(q, k, v, bias, sinks):
    B, L, _ = q.shape
    nq = 4
    nb = L // (nq * WINDOW)
    cur = lambda b, j: (b, j, 0)
    prev = lambda b, j: (b, jnp.maximum(nq * j - 1, 0), 0)
    qi = np.arange(WINDOW)[:, None]
    kc = np.arange(2 * WINDOW)[None, :]
    own = (kc >= WINDOW) & (kc - WINDOW <= qi)
    prv = (kc < WINDOW) & (kc > qi)
    mask = np.stack([np.tile(own, (1, GQA)), np.tile(own | prv, (1, GQA))]).astype(np.float32)
    return pl.pallas_call(
        _attn_prompt_body,
        grid=(B, nb),
        in_specs=[
            pl.BlockSpec(memory_space=pltpu.SMEM),
            pl.BlockSpec((1, nq * WINDOW, ATTN_W), cur),
            pl.BlockSpec((1, WINDOW, KV_W), prev),
            pl.BlockSpec((1, nq * WINDOW, KV_W), cur),
            pl.BlockSpec((1, WINDOW, KV_W), prev),
            pl.BlockSpec((1, nq * WINDOW, KV_W), cur),
            pl.BlockSpec((N_KV_HEADS, WINDOW, GQA * 2 * WINDOW), lambda b, j: (0, 0, 0)),
            pl.BlockSpec((2, WINDOW, GQA * 2 * WINDOW), lambda b, j: (0, 0, 0)),
        ],
        out_specs=pl.BlockSpec((1, nq * WINDOW, ATTN_W), cur),
        out_shape=jax.ShapeDtypeStruct((B, L, ATTN_W), bf16),
        name="attn_prompt",
    )(sinks, q, k, k, v, v, bias, jnp.asarray(mask))


def _attn_sample_body(q_ref, kb_ref, vb_ref, kn_ref, vn_ref, bb_ref, bn_ref, sink_ref,
                      o_ref, nk_ref, nv_ref, *, bb, ls):
    W = kb_ref.shape[1]
    rows = GQA * ls
    qi = lax.broadcasted_iota(i32, (rows, W), 0) % ls
    kj = lax.broadcasted_iota(i32, (rows, W), 1)
    valid_buf = kj > qi
    qi2 = lax.broadcasted_iota(i32, (rows, ls), 0) % ls
    kj2 = lax.broadcasted_iota(i32, (rows, ls), 1)
    valid_new = kj2 <= qi2

    nbat = 4
    ks = [slice(kv * HEAD_DIM, (kv + 1) * HEAD_DIM) for kv in range(N_KV_HEADS)]

    def group(i, carry):
        bs = [i * nbat + u for u in range(nbat)]
        chains = [(u, kv) for u in range(nbat) for kv in range(N_KV_HEADS)]
        qb = [q_ref[b] for b in bs]
        kbuf = [kb_ref[b] for b in bs]
        vbuf = [vb_ref[b] for b in bs]
        knew = [kn_ref[b] for b in bs]
        vnew = [vn_ref[b] for b in bs]
        qg = [jnp.concatenate([qb[u][:, (kv * GQA + g) * HEAD_DIM:(kv * GQA + g + 1) * HEAD_DIM]
                               for g in range(GQA)], axis=0) for u, kv in chains]
        s_buf = [_dot_t(qg[c], kbuf[u][:, ks[kv]].astype(bf16)) for c, (u, kv) in enumerate(chains)]
        s_new = [_dot_t(qg[c], knew[u][:, ks[kv]].astype(bf16)) for c, (u, kv) in enumerate(chains)]
        s_buf = [jnp.where(valid_buf, s_buf[c] + bb_ref[kv], NEG_INF) for c, (u, kv) in enumerate(chains)]
        s_new = [jnp.where(valid_new, s_new[c] + bn_ref[kv], NEG_INF) for c, (u, kv) in enumerate(chains)]
        probs = [_softmax_sink([s_buf[c], s_new[c]], sink_ref[kv]) for c, (u, kv) in enumerate(chains)]
        outs = [_dot(probs[c][0], vbuf[u][:, ks[kv]].astype(bf16)) + _dot(probs[c][1], vnew[u][:, ks[kv]].astype(bf16))
                for c, (u, kv) in enumerate(chains)]
        for u, b in enumerate(bs):
            heads = [outs[u * N_KV_HEADS + kv][g * ls:(g + 1) * ls] for kv in range(N_KV_HEADS) for g in range(GQA)]
            o_ref[b] = jnp.concatenate(heads, axis=-1).astype(bf16)
            nk_ref[b, 0:W - ls, :] = kbuf[u][ls:, :]
            nk_ref[b, W - ls:W, :] = knew[u]
            nv_ref[b, 0:W - ls, :] = vbuf[u][ls:, :]
            nv_ref[b, W - ls:W, :] = vnew[u]
        return carry

    lax.fori_loop(0, bb // nbat, group, 0)


def _attn_sample(q, k_buf, v_buf, k_new, v_new, bias_buf, bias_new, sink_col, bb=16):
    B, ls, _ = q.shape
    W = k_buf.shape[1]
    rows = GQA * ls
    blk = lambda shape: pl.BlockSpec(shape, lambda i: (i, 0, 0))
    full = lambda shape: pl.BlockSpec(shape, lambda i: (0, 0, 0))
    return pl.pallas_call(
        functools.partial(_attn_sample_body, bb=bb, ls=ls),
        grid=(B // bb,),
        in_specs=[
            blk((bb, ls, ATTN_W)),
            blk((bb, W, KV_W)),
            blk((bb, W, KV_W)),
            blk((bb, ls, KV_W)),
            blk((bb, ls, KV_W)),
            full((N_KV_HEADS, rows, W)),
            full((N_KV_HEADS, rows, ls)),
            full((N_KV_HEADS, rows, 1)),
        ],
        out_specs=[blk((bb, ls, ATTN_W)), blk((bb, W, KV_W)), blk((bb, W, KV_W))],
        out_shape=[
            jax.ShapeDtypeStruct((B, ls, ATTN_W), bf16),
            jax.ShapeDtypeStruct((B, W, KV_W), f32),
            jax.ShapeDtypeStruct((B, W, KV_W), f32),
        ],
        name="attn_sample",
    )(q, k_buf, v_buf, k_new, v_new, bias_buf, bias_new, sink_col)


def _outproj_core(po, at, x, gt, sc, sh, g_ref, wo_ref, wrh_ref, wrl_ref):
    mixo = _dot(po, wo_ref[0:POOL_W, :]) + _dot(at, wo_ref[POOL_W:, :])
    x1 = x + gt * mixo.reshape(x.shape)
    h2 = _mod_norm(x1, g_ref[...].reshape((1,) * (x.ndim - 1) + (D_MODEL,)), sc, sh).reshape(-1, D_MODEL)
    h_hi, h_lo = _split_bf16(h2)
    wh = wrh_ref[...]
    logits = _dot_t(wh, h_hi) + (_dot_t(wh, h_lo) + _dot_t(wrl_ref[...], h_hi))
    return x1, _pack_pairs(h2), logits


def _outproj_prompt_body(po_ref, at_ref, x_ref, gt_ref, sc_ref, sh_ref, g_ref, wo_ref, wrh_ref, wrl_ref,
                         x1_ref, h2_ref, lg_ref):
    x1, h2p, logits = _outproj_core(po_ref[0], at_ref[0], x_ref[0], gt_ref[0], sc_ref[0], sh_ref[0],
                                    g_ref, wo_ref, wrh_ref, wrl_ref)
    x1_ref[...] = x1
    h2_ref[...] = h2p
    lg_ref[...] = logits


def _outproj_sample_body(po_ref, at_ref, x_ref, gt_ref, sc_ref, sh_ref, g_ref, wo_ref, wrh_ref, wrl_ref,
                         x1_ref, h2_ref, lg_ref):
    x1, h2p, logits = _outproj_core(po_ref[...], at_ref[...], x_ref[...], gt_ref[...], sc_ref[...], sh_ref[...],
                                    g_ref, wo_ref, wrh_ref, wrl_ref)
    x1_ref[...] = x1.reshape(-1, D_MODEL)
    h2_ref[...] = h2p
    lg_ref[...] = logits


def _outproj_prompt(po, at, x, mod3, g_ffn, w_out, wr_hi, wr_lo, tm=512):
    B, L, D = x.shape
    nt = L // tm
    n_tok = B * L
    full = lambda shape: pl.BlockSpec(shape, lambda b, j: (0,) * len(shape))
    modspec = lambda c: pl.BlockSpec((1, 1, D), lambda b, j: (b, 0, c))
    return pl.pallas_call(
        _outproj_prompt_body,
        grid=(B, nt),
        in_specs=[
            pl.BlockSpec((1, tm, POOL_W), lambda b, j: (b, j, 0)),
            pl.BlockSpec((1, tm, ATTN_W), lambda b, j: (b, j, 0)),
            pl.BlockSpec((1, tm, D), lambda b, j: (b, j, 0)),
            modspec(2), modspec(4), modspec(3),
            full((1, D)), full((D, D)), full((N_EXPERTS, D)), full((N_EXPERTS, D)),
        ],
        out_specs=[
            pl.BlockSpec((tm, D), lambda b, j: (b * nt + j, 0)),
            pl.BlockSpec((tm, D // 2), lambda b, j: (b * nt + j, 0)),
            pl.BlockSpec((N_EXPERTS, tm), lambda b, j: (0, b * nt + j)),
        ],
        out_shape=[
            jax.ShapeDtypeStruct((n_tok, D), f32),
            jax.ShapeDtypeStruct((n_tok, D // 2), i32),
            jax.ShapeDtypeStruct((N_EXPERTS, n_tok), f32),
        ],
        name="outproj_prompt",
    )(po, at, x, mod3, mod3, mod3, g_ffn, w_out, wr_hi, wr_lo)


def _outproj_sample(po, at, x, mod3, g_ffn, w_out, wr_hi, wr_lo, bt=64):
    B, ls, D = x.shape
    n = bt * ls
    full = lambda shape: pl.BlockSpec(shape, lambda i: (0,) * len(shape))
    modspec = lambda c: pl.BlockSpec((bt, 1, D), lambda i: (i, 0, c))
    return pl.pallas_call(
        _outproj_sample_body,
        grid=(B // bt,),
        in_specs=[
            pl.BlockSpec((n, POOL_W), lambda i: (i, 0)),
            pl.BlockSpec((n, ATTN_W), lambda i: (i, 0)),
            pl.BlockSpec((bt, ls, D), lambda i: (i, 0, 0)),
            modspec(2), modspec(4), modspec(3),
            full((1, D)), full((D, D)), full((N_EXPERTS, D)), full((N_EXPERTS, D)),
        ],
        out_specs=[
            pl.BlockSpec((n, D), lambda i: (i, 0)),
            pl.BlockSpec((n, D // 2), lambda i: (i, 0)),
            pl.BlockSpec((N_EXPERTS, n), lambda i: (0, i)),
        ],
        out_shape=[
            jax.ShapeDtypeStruct((B * ls, D), f32),
            jax.ShapeDtypeStruct((B * ls, D // 2), i32),
            jax.ShapeDtypeStruct((N_EXPERTS, B * ls), f32),
        ],
        name="outproj_sample",
    )(po, at, x, mod3, mod3, mod3, g_ffn, w_out, wr_hi, wr_lo)


def _route_body(lga_ref, lgb_ref, rb_ref, tri_ref, idx_ref, rank_ref, gate_ref, cnt_ref, carry, *, tr, nsteps, na):
    step = pl.program_id(0)

    @pl.when(step == 0)
    def _():
        carry[...] = jnp.zeros(carry.shape, f32)

    logits = jnp.where(step < na, lga_ref[...], lgb_ref[...])
    s = jax.nn.sigmoid(logits)
    sb = s + rb_ref[...]
    e_iota = lax.broadcasted_iota(i32, (N_EXPERTS, tr), 0)
    g_iota = lax.broadcasted_iota(i32, (GROUP_SIZE, tr), 0)

    gscore = []
    for g in range(N_EXPERT_GROUPS):
        v = sb[g * GROUP_SIZE:(g + 1) * GROUP_SIZE]
        m1 = jnp.max(v, axis=0, keepdims=True)
        i1 = jnp.min(jnp.where(v == m1, g_iota, GROUP_SIZE), axis=0, keepdims=True)
        m2 = jnp.max(jnp.where(g_iota == i1, -jnp.inf, v), axis=0, keepdims=True)
        gscore.append(m1 + m2)
    parts = []
    for g in range(N_EXPERT_GROUPS):
        beaten = jnp.zeros((1, tr), i32)
        for g2 in range(N_EXPERT_GROUPS):
            if g2 == g:
                continue
            ahead = gscore[g2] > gscore[g]
            if g2 < g:
                ahead = ahead | (gscore[g2] == gscore[g])
            beaten = beaten + ahead.astype(i32)
        keep = beaten < TOPK_GROUPS
        parts.append(jnp.where(keep, sb[g * GROUP_SIZE:(g + 1) * GROUP_SIZE], NEG_INF))
    cur = jnp.concatenate(parts, axis=0)

    idxs, svals = [], []
    for _ in range(TOP_K):
        m = jnp.max(cur, axis=0, keepdims=True)
        ik = jnp.min(jnp.where(cur == m, e_iota, N_EXPERTS), axis=0, keepdims=True)
        hit = e_iota == ik
        svals.append(jnp.sum(jnp.where(hit, s, 0.0), axis=0, keepdims=True))
        cur = jnp.where(hit, -jnp.inf, cur)
        idxs.append(ik)
    sel = (cur == -jnp.inf).astype(f32)
    ssum = svals[0]
    for sv in svals[1:]:
        ssum = ssum + sv
    gate_ref[...] = jnp.concatenate([sv / ssum * ROUTED_SCALE for sv in svals], axis=0)
    idx_ref[...] = jnp.concatenate(idxs, axis=0)

    before = carry[...] + _dot(sel.astype(bf16), tri_ref[...])
    ranks = [jnp.sum(jnp.where(e_iota == ik, before, 0.0), axis=0, keepdims=True) for ik in idxs]
    rank_ref[...] = jnp.concatenate(ranks, axis=0).astype(i32)
    carry[...] = carry[...] + jnp.sum(sel, axis=1, keepdims=True)

    @pl.when(step == nsteps - 1)
    def _():
        cnt_ref[...] = carry[...]


def _route(logits_a, logits_b, router_bias, tr=512):
    E, Ta = logits_a.shape
    T = Ta + logits_b.shape[1]
    na = Ta // tr
    nsteps = T // tr
    tri = jnp.asarray(np.triu(np.ones((tr, tr), np.float32), 1), bf16)
    return pl.pallas_call(
        functools.partial(_route_body, tr=tr, nsteps=nsteps, na=na),
        grid=(nsteps,),
        in_specs=[
            pl.BlockSpec((E, tr), lambda i: (0, jnp.minimum(i, na - 1))),
            pl.BlockSpec((E, tr), lambda i: (0, jnp.maximum(i - na, 0))),
            pl.BlockSpec((E, 1), lambda i: (0, 0)),
            pl.BlockSpec((tr, tr), lambda i: (0, 0)),
        ],
        out_specs=[
            pl.BlockSpec((TOP_K, tr), lambda i: (0, i)),
            pl.BlockSpec((TOP_K, tr), lambda i: (0, i)),
            pl.BlockSpec((TOP_K, tr), lambda i: (0, i)),
            pl.BlockSpec((E, 1), lambda i: (0, 0)),
        ],
        out_shape=[
            jax.ShapeDtypeStruct((TOP_K, T), i32),
            jax.ShapeDtypeStruct((TOP_K, T), i32),
            jax.ShapeDtypeStruct((TOP_K, T), f32),
            jax.ShapeDtypeStruct((E, 1), f32),
        ],
        scratch_shapes=[pltpu.VMEM((E, 1), f32)],
        compiler_params=pltpu.CompilerParams(dimension_semantics=("arbitrary",)),
        name="route",
    )(logits_a, logits_b, router_bias.reshape(E, 1), tri)


def _dest_body(idx_ref, rank_ref, ps_ref, dest_ref, *, tr):
    e_iota = lax.broadcasted_iota(i32, (N_EXPERTS, tr), 0)
    start = ps_ref[...]
    rows = []
    for k in range(TOP_K):
        hit = e_iota == idx_ref[k:k + 1, :]
        rows.append(jnp.sum(jnp.where(hit, start, 0.0), axis=0, keepdims=True))
    dest_ref[...] = jnp.concatenate(rows, axis=0).astype(i32) + rank_ref[...]


def _dest_rows(idx, rank, pad_start, tr=512):
    K, T = idx.shape
    blk = pl.BlockSpec((K, tr), lambda i: (0, i))
    return pl.pallas_call(
        functools.partial(_dest_body, tr=tr),
        grid=(T // tr,),
        in_specs=[blk, blk, pl.BlockSpec((N_EXPERTS, 1), lambda i: (0, 0))],
        out_specs=blk,
        out_shape=jax.ShapeDtypeStruct((K, T), i32),
        name="dest_rows",
    )(idx, rank, pad_start.astype(f32).reshape(N_EXPERTS, 1))


def _sc_mesh():
    return plsc.VectorSubcoreMesh(core_axis_name="c", subcore_axis_name="s")


def _sc_worker_id():
    return lax.axis_index("s") * 2 + lax.axis_index("c")


def _dispatch(h2_a, h2_b, dest, gate, n_rows, chunk=32):
    ta, Dw = h2_a.shape
    T = ta + h2_b.shape[0]
    per_worker = T // SC_WORKERS
    nchunk = per_worker // chunk
    assert per_worker * SC_WORKERS == T and nchunk * chunk == per_worker and ta % chunk == 0

    nrow_idx = nchunk * TOP_K

    @functools.partial(
        pl.kernel, mesh=_sc_mesh(),
        out_type=[jax.ShapeDtypeStruct((n_rows, Dw), i32), jax.ShapeDtypeStruct((n_rows, GATE_ROW), f32)],
        scratch_types=[pltpu.VMEM((nrow_idx, chunk), i32), pltpu.VMEM((nrow_idx, chunk), f32),
                       pltpu.VMEM((chunk, Dw), i32), pltpu.VMEM((TOP_K, chunk, GATE_ROW), f32),
                       pltpu.SemaphoreType.DMA],
        compiler_params=pltpu.CompilerParams(needs_layout_passes=False),
        name="moe_dispatch",
    )
    def body(ha_hbm, hb_hbm, dest_hbm, gate_hbm, xs_hbm, gs_hbm, idx_v, gate_v, rows_v, grow_v, sem):
        wid = _sc_worker_id()
        base = wid * per_worker
        pltpu.sync_copy(dest_hbm.at[wid], idx_v)
        pltpu.sync_copy(gate_hbm.at[wid], gate_v)
        zero = jnp.zeros((SC_LANES,), f32)
        for k in range(TOP_K):
            @pl.loop(0, chunk)
            def _(t):
                for j in range(GATE_ROW // SC_LANES):
                    grow_v[k, t, pl.ds(j * SC_LANES, SC_LANES)] = zero

        @pl.loop(0, nchunk)
        def _(ci):
            t0 = base + ci * chunk

            @pl.when(t0 < ta)
            def _():
                pltpu.sync_copy(ha_hbm.at[pl.ds(t0, chunk)], rows_v)

            @pl.when(t0 >= ta)
            def _():
                pltpu.sync_copy(hb_hbm.at[pl.ds(t0 - ta, chunk)], rows_v)

            for k in range(TOP_K):
                @pl.loop(0, chunk)
                def _(t):
                    row = jnp.zeros((SC_LANES,), i32) + (ci * TOP_K + k)
                    grow_v[k, t, pl.ds(0, SC_LANES)] = plsc.load_gather(
                        gate_v, [row, jnp.zeros((SC_LANES,), i32) + t])

            cps = []
            for k in range(TOP_K):
                idx = idx_v.at[ci * TOP_K + k]
                cps.append(pltpu.make_async_copy(rows_v, xs_hbm.at[idx], sem))
                cps.append(pltpu.make_async_copy(grow_v.at[k], gs_hbm.at[idx], sem))
            for cp in cps:
                cp.start()
            for cp in cps:
                cp.wait()

    def per_worker_rows(a):
        return a.reshape(TOP_K, SC_WORKERS, nchunk, chunk).transpose(1, 2, 0, 3).reshape(
            SC_WORKERS, nrow_idx, chunk)

    return body(h2_a, h2_b, per_worker_rows(dest), per_worker_rows(gate))


def _combine(ys, dest, chunk=8):
    T = dest.shape[1]
    Dw = ys.shape[1]
    per_worker = T // SC_WORKERS
    nchunk = per_worker // chunk
    assert per_worker * SC_WORKERS == T and nchunk * chunk == per_worker and nchunk % 2 == 0

    @functools.partial(
        pl.kernel, mesh=_sc_mesh(),
        out_type=jax.ShapeDtypeStruct((T, 2 * Dw), f32),
        scratch_types=[
            pltpu.VMEM((TOP_K * per_worker,), i32),
            pltpu.VMEM((2, TOP_K, chunk, Dw), i32),
            pltpu.VMEM((chunk, 2 * Dw), f32),
            pltpu.SemaphoreType.DMA((2,)),
        ],
        compiler_params=pltpu.CompilerParams(needs_layout_passes=False),
        name="moe_combine",
    )
    def body(ys_hbm, dest_hbm, out_hbm, idx_v, buf, out_v, sems):
        base = _sc_worker_id() * per_worker
        pltpu.sync_copy(dest_hbm.at[pl.ds(_sc_worker_id() * (TOP_K * per_worker), TOP_K * per_worker)], idx_v)

        def gather(ci, slot):
            return [pltpu.make_async_copy(ys_hbm.at[idx_v.at[pl.ds(k * per_worker + ci * chunk, chunk)]],
                                          buf.at[slot, k], sems.at[slot]) for k in range(TOP_K)]

        for cp in gather(0, 0):
            cp.start()

        @pl.loop(0, nchunk, step=2)
        def _(c0):
            for slot in range(2):
                ci = c0 + slot

                @pl.when(ci + 1 < nchunk)
                def _():
                    for cp in gather(ci + 1, 1 - slot):
                        cp.start()

                for cp in gather(ci, slot):
                    cp.wait()

                @pl.loop(0, chunk)
                def _(t):
                    @plsc.parallel_loop(0, Dw // SC_LANES, unroll=4)
                    def _(j):
                        sl = pl.ds(j * SC_LANES, SC_LANES)
                        w = buf[slot, 0, t, sl]
                        hi = plsc.bitcast(w & HI_MASK, f32)
                        lo = plsc.bitcast(lax.shift_left(w, 16), f32)
                        for k in range(1, TOP_K):
                            w = buf[slot, k, t, sl]
                            hi = hi + plsc.bitcast(w & HI_MASK, f32)
                            lo = lo + plsc.bitcast(lax.shift_left(w, 16), f32)
                        out_v[t, sl] = hi
                        out_v[t, pl.ds(Dw + j * SC_LANES, SC_LANES)] = lo

                pltpu.sync_copy(out_v, out_hbm.at[pl.ds(base + ci * chunk, chunk)])

    dest_w = dest.reshape(TOP_K, SC_WORKERS, per_worker).transpose(1, 0, 2).reshape(-1)
    return body(ys, dest_w)


def _gmm_body(blk_e_ref, blk_rows_ref, blk_next_ref, nv_ref, xs_hbm, gs_hbm, w1_hbm, w3_hbm, w2_hbm, ys_hbm,
              xbuf, gbuf, ybuf, w1f, w3f, w2f, xsem, gsem, ysem, wsem):
    nv = nv_ref[0]
    nb = blk_e_ref.shape[0]
    half = D_MODEL // 2
    RB = EXPERT_BLOCK
    ahead_w = W_RING - 2

    def expert_of(blk):
        return blk_e_ref[jnp.minimum(blk, nb - 1)]

    def next_expert_block(blk):
        return jnp.where(blk < nv, blk_next_ref[jnp.minimum(blk, nb - 1)], blk)

    def start_weights(blk, ordinal):
        @pl.when(blk < nv)
        def _():
            for cp in weight_copies(expert_of(blk), lax.rem(ordinal, W_RING)):
                cp.start()

    def row_copies(b, slot):
        r0 = pl.multiple_of(b * RB, RB)
        return (pltpu.make_async_copy(xs_hbm.at[pl.ds(r0, RB)], xbuf.at[slot], xsem.at[slot]),
                pltpu.make_async_copy(gs_hbm.at[pl.ds(r0, RB)], gbuf.at[slot], gsem.at[slot]))

    def out_copy(b, slot):
        r0 = pl.multiple_of(b * RB, RB)
        return pltpu.make_async_copy(ybuf.at[slot], ys_hbm.at[pl.ds(r0, RB)], ysem.at[slot])

    def weight_copies(e, ws):
        return (pltpu.make_async_copy(w1_hbm.at[e], w1f.at[ws], wsem.at[ws, 0]),
                pltpu.make_async_copy(w3_hbm.at[e], w3f.at[ws], wsem.at[ws, 1]),
                pltpu.make_async_copy(w2_hbm.at[e], w2f.at[ws], wsem.at[ws, 2]))

    blk = jnp.int32(0)
    for n in range(ahead_w):
        start_weights(blk, n)
        blk = next_expert_block(blk)
    for i in range(ROW_RING - 2):
        @pl.when(i < nv)
        def _():
            for cp in row_copies(i, i):
                cp.start()

    def enter(b, live, ordinal_prev):
        e = expert_of(b)
        first = live & ((b == 0) | (e != expert_of(jnp.maximum(b - 1, 0))))
        ordinal = jnp.where(first & (b > 0), ordinal_prev + 1, ordinal_prev)

        @pl.when(first)
        def _():
            for cp in weight_copies(e, lax.rem(ordinal, W_RING)):
                cp.wait()
            nxt = b
            for _ in range(ahead_w):
                nxt = next_expert_block(nxt)
            start_weights(nxt, ordinal + ahead_w)

        return ordinal

    def load_block(b, slot, ws):
        valid = lax.broadcasted_iota(i32, (RB, 1), 0) < blk_rows_ref[jnp.minimum(b, nb - 1)]
        x_hi, x_lo = _unpack_pairs(jnp.where(valid, xbuf[slot], 0))
        g = jnp.where(valid, gbuf[slot][:, 0:1], 0.0)
        return x_hi, x_lo, g, ws

    def pair(p, ordinal_prev):
        b0 = 2 * p
        b1 = b0 + 1
        live1 = b1 < nv
        slot0 = lax.rem(b0, ROW_RING)
        slots = (slot0, jnp.where(live1, lax.rem(b1, ROW_RING), slot0))

        for b in (b0 + ROW_RING - 2, b1 + ROW_RING - 2):
            @pl.when(b < nv)
            def _():
                for cp in row_copies(b, lax.rem(b, ROW_RING)):
                    cp.start()

        ord0 = enter(b0, b0 < nv, ordinal_prev)
        ord1 = enter(b1, live1, ord0)
        for cp in row_copies(b0, slots[0]):
            cp.wait()

        @pl.when(live1)
        def _():
            for cp in row_copies(b1, slots[1]):
                cp.wait()

        blocks = [load_block(b0, slots[0], lax.rem(ord0, W_RING)),
                  load_block(jnp.where(live1, b1, b0), slots[1], lax.rem(ord1, W_RING))]
        a = [_dot(xh, w1f[ws, 0:half, :].astype(bf16)) + _dot(xl, w1f[ws, half:, :].astype(bf16))
             for xh, xl, _, ws in blocks]
        c = [_dot(xh, w3f[ws, 0:half, :].astype(bf16)) + _dot(xl, w3f[ws, half:, :].astype(bf16))
             for xh, xl, _, ws in blocks]
        hmid = [((a[i] * jax.nn.sigmoid(a[i])) * c[i]).astype(bf16) for i in range(2)]
        y = [_pack_pairs(_dot(hmid[i], w2f[blocks[i][3]].astype(bf16)) * blocks[i][2]) for i in range(2)]

        for i, (b, live) in enumerate(((b0, b0 < nv), (b1, live1))):
            @pl.when(live & (b >= ROW_RING))
            def _():
                out_copy(b - ROW_RING, slots[i]).wait()

            @pl.when(live)
            def _():
                ybuf[slots[i]] = y[i]
                out_copy(b, slots[i]).start()

        return ord1

    lax.fori_loop(0, (nv + 1) // 2, pair, 0)

    for i in range(1, ROW_RING + 1):
        @pl.when(nv >= i)
        def _():
            out_copy(nv - i, lax.rem(nv - i, ROW_RING)).wait()


def _gmm(xs, gs, w1, w3, w2, blk_e, blk_rows, blk_next, n_valid):
    n_rows, Dw = xs.shape
    D = 2 * Dw
    RB = EXPERT_BLOCK
    hbm = pl.BlockSpec(memory_space=pl.ANY)
    return pl.pallas_call(
        _gmm_body,
        grid_spec=pltpu.PrefetchScalarGridSpec(
            num_scalar_prefetch=4,
            grid=(1,),
            in_specs=[hbm, hbm, hbm, hbm, hbm],
            out_specs=hbm,
            scratch_shapes=[
                pltpu.VMEM((ROW_RING, RB, Dw), i32), pltpu.VMEM((ROW_RING, RB, GATE_ROW), f32),
                pltpu.VMEM((ROW_RING, RB, Dw), i32),
                pltpu.VMEM((W_RING, D, EXPERT_FF), f32), pltpu.VMEM((W_RING, D, EXPERT_FF), f32),
                pltpu.VMEM((W_RING, EXPERT_FF, D), f32),
                pltpu.SemaphoreType.DMA((ROW_RING,)), pltpu.SemaphoreType.DMA((ROW_RING,)),
                pltpu.SemaphoreType.DMA((ROW_RING,)), pltpu.SemaphoreType.DMA((W_RING, 3)),
            ],
        ),
        out_shape=jax.ShapeDtypeStruct((n_rows, Dw), i32),
        compiler_params=pltpu.CompilerParams(dimension_semantics=("arbitrary",)),
        name="moe_gmm",
    )(blk_e, blk_rows, blk_next, n_valid, xs, gs, w1, w3, w2)


def _final_core(x1, h2p, comb, gt, ws1_ref, ws3_ref, ws2_ref):
    half = D_MODEL // 2
    h_hi, h_lo = _unpack_pairs(h2p)
    a = _dot(h_hi, ws1_ref[0:half, :]) + _dot(h_lo, ws1_ref[half:, :])
    c = _dot(h_hi, ws3_ref[0:half, :]) + _dot(h_lo, ws3_ref[half:, :])
    shared = _dot(((a * jax.nn.sigmoid(a)) * c).astype(bf16), ws2_ref[...])
    return x1, comb + shared, gt


def _final_prompt_body(x1_ref, h2_ref, cb_ref, gt_ref, ws1_ref, ws3_ref, ws2_ref, *rest):
    y_ref = rest[-1]
    x1, ffn, gt = _final_core(x1_ref[...], h2_ref[...], cb_ref[...], gt_ref[0], ws1_ref, ws3_ref, ws2_ref)
    y_ref[0] = x1 + gt * ffn


def _final_sample_body(x1_ref, h2_ref, cb_ref, gt_ref, ws1_ref, ws3_ref, ws2_ref, y_ref):
    x1, ffn, gt = _final_core(x1_ref[...], h2_ref[...], cb_ref[...], gt_ref[...], ws1_ref, ws3_ref, ws2_ref)
    shp = y_ref.shape
    y_ref[...] = x1.reshape(shp) + gt * ffn.reshape(shp)


def _final_prompt(x1, h2, comb, mod3, ws1, ws3, ws2, B, L, b0, nbat, y_prev=None, tm=512):
    D = D_MODEL
    nt = L // tm
    full = lambda shape: pl.BlockSpec(shape, lambda b, j: (0,) * len(shape))
    rows = pl.BlockSpec((tm, D), lambda b, j: ((b0 + b) * nt + j, 0))
    words = pl.BlockSpec((tm, D // 2), lambda b, j: ((b0 + b) * nt + j, 0))
    comb_rows = pl.BlockSpec((tm, D), lambda b, j: (b * nt + j, 0))
    extra_specs, extra_args, aliases = [], [], {}
    if y_prev is not None:
        extra_specs, extra_args, aliases = [pl.BlockSpec(memory_space=pl.ANY)], [y_prev], {7: 0}
    return pl.pallas_call(
        _final_prompt_body,
        grid=(nbat, nt),
        in_specs=[rows, words, comb_rows, pl.BlockSpec((1, 1, D), lambda b, j: (b0 + b, 0, 5)),
                  full((D, EXPERT_FF)), full((D, EXPERT_FF)), full((EXPERT_FF, D))] + extra_specs,
        out_specs=pl.BlockSpec((1, tm, D), lambda b, j: (b0 + b, j, 0)),
        out_shape=jax.ShapeDtypeStruct((B, L, D), f32),
        input_output_aliases=aliases,
        name="final_prompt",
    )(x1, h2, comb, mod3, ws1, ws3, ws2, *extra_args)


def _final_sample(x1, h2, comb, mod3, ws1, ws3, ws2, B, ls, row0, bt=64):
    D = D_MODEL
    n = bt * ls
    blk0 = row0 // n
    full = lambda shape: pl.BlockSpec(shape, lambda i: (0,) * len(shape))
    rows = pl.BlockSpec((n, D), lambda i: (i, 0))
    words = pl.BlockSpec((n, D // 2), lambda i: (i, 0))
    comb_rows = pl.BlockSpec((n, D), lambda i: (blk0 + i, 0))
    return pl.pallas_call(
        _final_sample_body,
        grid=(B // bt,),
        in_specs=[rows, words, comb_rows, pl.BlockSpec((bt, 1, D), lambda i: (i, 0, 5)),
                  full((D, EXPERT_FF)), full((D, EXPERT_FF)), full((EXPERT_FF, D))],
        out_specs=pl.BlockSpec((bt, ls, D), lambda i: (i, 0, 0)),
        out_shape=jax.ShapeDtypeStruct((B, ls, D), f32),
        name="final_sample",
    )(x1, h2, comb, mod3, ws1, ws3, ws2)


def kernel(x_prompt, x_sample, state_pool, cache_swa_k, cache_swa_v, c_prompt, c_sample, w_ada, b_ada,
           g_attn_norm, w_in, g_q, g_k, w_pool, pool_scale, w_out, attn_sinks, rel_bias, g_ffn_norm,
           w_router, router_bias, w1, w3, w2, ws1, ws3, ws2):
    B, L, D = x_prompt.shape
    BS, LS, _ = x_sample.shape
    depth = w_ada.shape[0]
    assert depth == 1
    W = cache_swa_k.shape[2]
    tp, ts = B * L, BS * LS
    T = tp + ts
    n_rows = (T * TOP_K // EXPERT_BLOCK + N_EXPERTS) * EXPERT_BLOCK
    nb = n_rows // EXPERT_BLOCK

    g_attn = g_attn_norm[0].reshape(1, D)
    g_ffn = g_ffn_norm[0].reshape(1, D)
    w_in_b = w_in[0].astype(bf16)
    w_out_b = w_out[0].astype(bf16)
    w_pool_b = w_pool[0].astype(bf16)
    ps = pool_scale[0].reshape(1, POOL_W)
    gqk = jnp.concatenate([jnp.tile(g_q[0], N_HEADS), jnp.tile(g_k[0], N_KV_HEADS)]).reshape(1, QK_W)
    head_of = np.arange(QK_W) // HEAD_DIM
    bd = jnp.asarray((head_of[:, None] == head_of[None, :]).astype(np.float32), bf16)
    wr_t = w_router[0].T
    wr_hi = wr_t.astype(bf16)
    wr_lo = (wr_t - wr_hi.astype(f32)).astype(bf16)
    ws1_b, ws3_b, ws2_b = ws1[0].astype(bf16), ws3[0].astype(bf16), ws2[0].astype(bf16)
    sinks = attn_sinks[0]

    mod = _ada(jnp.concatenate([c_prompt, c_sample], axis=0), w_ada[0], b_ada[0])
    mod_p = mod[:B].reshape(B, 1, 6 * D)
    mod_s = mod[B:].reshape(BS, 1, 6 * D)

    dist_p = np.arange(WINDOW)[:, None] + WINDOW - np.arange(2 * WINDOW)[None, :]
    bias_p = _relbias(rel_bias, dist_p)
    bias_p = bias_p.reshape(N_KV_HEADS, GQA, WINDOW, 2 * WINDOW).transpose(0, 2, 1, 3).reshape(
        N_KV_HEADS, WINDOW, GQA * 2 * WINDOW)
    dist_s = np.arange(LS)[:, None] + W - np.arange(W + LS)[None, :]
    bias_s = _relbias(rel_bias, dist_s)
    bias_s_buf = bias_s[:, :, :W].reshape(N_KV_HEADS, GQA * LS, W)
    bias_s_new = bias_s[:, :, W:].reshape(N_KV_HEADS, GQA * LS, LS)
    sink_col = jnp.repeat(sinks, LS).reshape(N_KV_HEADS, GQA * LS, 1)

    q_p, k_p, v_p, po_p, new_pool_p, kc_p, vc_p = _inproj_prompt(
        x_prompt, mod_p, g_attn, w_in_b, gqk, bd, w_pool_b, ps)
    q_s, k_s, v_s, po_s, new_pool_s = _inproj_sample(
        x_sample, mod_s, g_attn, w_in_b, gqk, bd, w_pool_b, ps, state_pool[0], PAST_LEN)
    at_p = _attn_prompt(q_p, k_p, v_p, bias_p, sinks)
    at_s, nk_s, nv_s = _attn_sample(
        q_s.reshape(BS, LS, ATTN_W), cache_swa_k[0].reshape(BS, W, KV_W), cache_swa_v[0].reshape(BS, W, KV_W),
        k_s.reshape(BS, LS, KV_W), v_s.reshape(BS, LS, KV_W), bias_s_buf, bias_s_new, sink_col)

    x1_p, h2_p, lg_p = _outproj_prompt(po_p, at_p, x_prompt, mod_p, g_ffn, w_out_b, wr_hi, wr_lo)
    x1_s, h2_s, lg_s = _outproj_sample(po_s, at_s.reshape(ts, ATTN_W), x_sample, mod_s, g_ffn, w_out_b,
                                       wr_hi, wr_lo)

    idx, rank, gate, counts = _route(lg_p, lg_s, router_bias[0])
    counts = counts.reshape(N_EXPERTS).astype(i32)
    padded = (counts + EXPERT_BLOCK - 1) // EXPERT_BLOCK * EXPERT_BLOCK
    pad_end = jnp.cumsum(padded)
    pad_start = pad_end - padded
    dest = _dest_rows(idx, rank, pad_start)
    n_valid = (pad_end[-1] // EXPERT_BLOCK).astype(i32).reshape(1)
    blk_row0 = jnp.arange(nb, dtype=i32) * EXPERT_BLOCK
    blk_e = jnp.minimum(jnp.sum(blk_row0[:, None] >= pad_end[None, :], axis=1), N_EXPERTS - 1).astype(i32)
    own = jnp.arange(N_EXPERTS, dtype=i32)[None, :] == blk_e[:, None]
    blk_cnt = jnp.sum(jnp.where(own, counts[None, :], 0), axis=1)
    blk_start = jnp.sum(jnp.where(own, pad_start[None, :], 0), axis=1)
    blk_rows = jnp.clip(blk_cnt - (blk_row0 - blk_start), 0, EXPERT_BLOCK).astype(i32)
    blk_next = (jnp.sum(jnp.where(own, pad_end[None, :], 0), axis=1) // EXPERT_BLOCK).astype(i32)

    xs, gs = _dispatch(h2_p, h2_s, dest, gate, n_rows)
    ys = _gmm(xs, gs, w1[0], w3[0], w2[0], blk_e, blk_rows, blk_next, n_valid)
    bh = B // 2
    th = bh * L
    comb_a = _combine(ys, dest[:, :th])
    comb_b = _combine(ys, dest[:, th:])
    y_p = _final_prompt(x1_p, h2_p, comb_a, mod_p, ws1_b, ws3_b, ws2_b, B, L, 0, bh)
    y_p = _final_prompt(x1_p, h2_p, comb_b, mod_p, ws1_b, ws3_b, ws2_b, B, L, bh, B - bh, y_prev=y_p)
    y_s = _final_sample(x1_s, h2_s, comb_b, mod_s, ws1_b, ws3_b, ws2_b, BS, LS, tp - th)

    return (y_p, y_s, new_pool_p[None], kc_p.reshape(1, B, WINDOW, N_KV_HEADS, HEAD_DIM),
            vc_p.reshape(1, B, WINDOW, N_KV_HEADS, HEAD_DIM), new_pool_s[None],
            nk_s.reshape(1, BS, W, N_KV_HEADS, HEAD_DIM), nv_s.reshape(1, BS, W, N_KV_HEADS, HEAD_DIM))
```

```python
import functools
import math

import numpy as np
import jax
import jax.numpy as jnp
from jax import lax
from jax.experimental import pallas as pl
from jax.experimental.pallas import tpu as pltpu
from jax.experimental.pallas import tpu_sc as plsc

f32 = jnp.float32
bf16 = jnp.bfloat16
i32 = jnp.int32

D_MODEL = 1024
PAST_LEN = 8192
POOL_W = 512
POOL_WINDOWS = (2, 4, 8, 16)
POOL_GC = 128
POOL_BUF = 15
ATTN_W = 512
HEAD_DIM = 64
N_HEADS = 8
N_KV_HEADS = 2
GQA = 4
KV_W = 128
WINDOW = 128
NUM_BUCKETS = 32
MAX_EXACT = 16
REL_MAX_DIST = 128
N_EXPERTS = 256
N_EXPERT_GROUPS = 8
GROUP_SIZE = 32
TOPK_GROUPS = 4
TOP_K = 8
EXPERT_FF = 256
ROUTED_SCALE = 2.5
EXPERT_BLOCK = 128
EPS = 1e-6
NEG_INF = -1e30
QKV_W = POOL_W + ATTN_W + 2 * KV_W
QK_W = ATTN_W + KV_W
HIST = 16

SC_WORKERS = 32
SC_LANES = 16
GATE_ROW = 128


def _dot(a, b):
    return jnp.dot(a, b, preferred_element_type=f32)


def _dot_t(a, b):
    return lax.dot_general(a, b, (((1,), (1,)), ((), ())), preferred_element_type=f32)


def _split_bf16(a):
    hi = a.astype(bf16)
    lo = (a - hi.astype(f32)).astype(bf16)
    return hi, lo


ROW_RING = 8
W_RING = 5
HI_MASK = -65536


def _pack_pairs(a):
    h = a.shape[1] // 2
    hi = lax.bitcast_convert_type(a[:, :h].astype(bf16).astype(f32), i32)
    lo = lax.bitcast_convert_type(a[:, h:].astype(bf16).astype(f32), i32)
    return hi | lax.shift_right_logical(lo, 16)


def _unpack_pairs(w):
    hi = lax.bitcast_convert_type(w & HI_MASK, f32).astype(bf16)
    lo = lax.bitcast_convert_type(lax.shift_left(w, 16), f32).astype(bf16)
    return hi, lo


def _mod_norm(x, g, sc, sh):
    ms = jnp.mean(x * x, axis=-1, keepdims=True)
    y = x * lax.rsqrt(ms + EPS)
    return (y * g) * (1.0 + sc) + sh


def _ada_body(c_ref, w_ref, b_ref, o_ref):
    c = c_ref[...]
    a = (c * jax.nn.sigmoid(c)).astype(bf16)
    o_ref[...] = _dot(a, w_ref[...].astype(bf16)) + b_ref[...]


def _ada(c, w_ada, b_ada):
    n = c.shape[0]
    tn = 1024
    return pl.pallas_call(
        _ada_body,
        grid=(6 * D_MODEL // tn,),
        in_specs=[
            pl.BlockSpec((n, D_MODEL), lambda j: (0, 0)),
            pl.BlockSpec((D_MODEL, tn), lambda j: (0, j)),
            pl.BlockSpec((1, tn), lambda j: (0, j)),
        ],
        out_specs=pl.BlockSpec((n, tn), lambda j: (0, j)),
        out_shape=jax.ShapeDtypeStruct((n, 6 * D_MODEL), f32),
        name="ada_mod",
    )(c, w_ada, b_ada.reshape(1, -1))


def _relbias_body(table_ref, bucket_ref, o_ref):
    bucket = bucket_ref[...]
    for h in range(N_HEADS):
        acc = jnp.zeros(bucket.shape, f32)
        for b in range(NUM_BUCKETS):
            acc = jnp.where(bucket == b, table_ref[b, h], acc)
        o_ref[h] = acc


def _rel_buckets(dist):
    n = np.maximum(dist, 0)
    nf = np.maximum(n, 1).astype(np.float64)
    large = MAX_EXACT + (np.log(nf / MAX_EXACT) / math.log(REL_MAX_DIST / MAX_EXACT)
                         * (NUM_BUCKETS - MAX_EXACT)).astype(np.int32)
    return np.where(n < MAX_EXACT, n, np.minimum(large, NUM_BUCKETS - 1)).astype(np.int32)


def _relbias(table, dist):
    lq, lk = dist.shape
    return pl.pallas_call(
        _relbias_body,
        in_specs=[
            pl.BlockSpec(memory_space=pltpu.SMEM),
            pl.BlockSpec((lq, lk), lambda: (0, 0)),
        ],
        out_specs=pl.BlockSpec((N_HEADS, lq, lk), lambda: (0, 0, 0)),
        out_shape=jax.ShapeDtypeStruct((N_HEADS, lq, lk), f32),
        name="rel_bias",
    )(table, jnp.asarray(_rel_buckets(dist)))


def _qkv_from_h(hs, w_ref, gqk_ref, bd_ref):
    us = [_dot(h.astype(bf16), w_ref[...]) for h in hs]
    qks = [u[:, POOL_W:POOL_W + QK_W] for u in us]
    sq = [_split_bf16(qk * qk) for qk in qks]
    bd = bd_ref[...]
    ss = [_dot(y_hi, bd) + _dot(y_lo, bd) for y_hi, y_lo in sq]
    qkn = [(qk * lax.rsqrt(s * (1.0 / HEAD_DIM) + EPS)) * gqk_ref[...] for qk, s in zip(qks, ss)]
    return [(u[:, :POOL_W], n[:, :ATTN_W] * (HEAD_DIM ** -0.5), n[:, ATTN_W:], u[:, POOL_W + QK_W:])
            for u, n in zip(us, qkn)]


def _inproj_prompt_body(x_ref, sh_ref, sc_ref, g_ref, w_ref, gqk_ref, bd_ref, wp_ref, ps_ref,
                        q_ref, k_ref, v_ref, po_ref, np_ref, kc_ref, vc_ref, hist, wsum, *, tl, nt):
    j = pl.program_id(1)
    nsub = 4
    ts = tl // nsub
    hs = [_mod_norm(x_ref[0, i * ts:(i + 1) * ts, :], g_ref[...], sc_ref[0], sh_ref[0]) for i in range(nsub)]
    parts = _qkv_from_h(hs, w_ref, gqk_ref, bd_ref)

    base = 2 * HIST
    end = base + tl

    @pl.when(j == 0)
    def _():
        hist[0:base, :] = jnp.zeros((base, POOL_W), f32)
        wsum[0:HIST, :] = jnp.zeros((HIST, POOL_W), f32)

    for i, (up, q, k, v) in enumerate(parts):
        rows = slice(i * ts, (i + 1) * ts)
        q_ref[0, rows, :] = q.astype(bf16)
        k_ref[0, rows, :] = k.astype(bf16)
        v_ref[0, rows, :] = v.astype(bf16)
        hist[base + i * ts:base + (i + 1) * ts, :] = up

    @pl.when(j == nt - 1)
    def _():
        kc_ref[0] = parts[-1][2][ts - WINDOW:, :]
        vc_ref[0] = parts[-1][3][ts - WINDOW:, :]

    pos = j * tl + lax.broadcasted_iota(i32, (tl, 1), 0)
    src = hist
    for p, w in enumerate(POOL_WINDOWS):
        sh = w // 2
        live = slice(p * POOL_GC, POOL_W)
        summed = src[HIST:end, live] + src[HIST - sh:end - sh, live]
        wsum[HIST:end, live] = summed
        src = wsum
        lanes = slice(p * POOL_GC, (p + 1) * POOL_GC)
        cur = hist[base:end, lanes]
        cnt = jnp.minimum(w, pos + 1).astype(f32)
        d = summed[HIST:, 0:POOL_GC] / cnt - cur
        yg = _dot(d.astype(bf16), wp_ref[p]) * ps_ref[:, lanes]
        po_ref[0, :, lanes] = yg.astype(bf16)

    @pl.when(j == nt - 1)
    def _():
        np_ref[0] = hist[end - POOL_BUF:end, :]

    hist[HIST:base, :] = hist[end - HIST:end, :]


def _inproj_prompt(x, mod3, g_attn, w_in, gqk, bd, w_pool, pool_scale, tl=512):
    B, L, D = x.shape
    nt = L // tl
    full = lambda shape: pl.BlockSpec(shape, lambda b, j: (0,) * len(shape))
    return pl.pallas_call(
        functools.partial(_inproj_prompt_body, tl=tl, nt=nt),
        grid=(B, nt),
        in_specs=[
            pl.BlockSpec((1, tl, D), lambda b, j: (b, j, 0)),
            pl.BlockSpec((1, 1, D), lambda b, j: (b, 0, 0)),
            pl.BlockSpec((1, 1, D), lambda b, j: (b, 0, 1)),
            full((1, D)),
            full((D, QKV_W)),
            full((1, QK_W)),
            full((QK_W, QK_W)),
            full((4, POOL_GC, POOL_GC)),
            full((1, POOL_W)),
        ],
        out_specs=[
            pl.BlockSpec((1, tl, ATTN_W), lambda b, j: (b, j, 0)),
            pl.BlockSpec((1, tl, KV_W), lambda b, j: (b, j, 0)),
            pl.BlockSpec((1, tl, KV_W), lambda b, j: (b, j, 0)),
            pl.BlockSpec((1, tl, POOL_W), lambda b, j: (b, j, 0)),
            pl.BlockSpec((1, POOL_BUF, POOL_W), lambda b, j: (b, 0, 0)),
            pl.BlockSpec((1, WINDOW, KV_W), lambda b, j: (b, 0, 0)),
            pl.BlockSpec((1, WINDOW, KV_W), lambda b, j: (b, 0, 0)),
        ],
        out_shape=[
            jax.ShapeDtypeStruct((B, L, ATTN_W), bf16),
            jax.ShapeDtypeStruct((B, L, KV_W), bf16),
            jax.ShapeDtypeStruct((B, L, KV_W), bf16),
            jax.ShapeDtypeStruct((B, L, POOL_W), bf16),
            jax.ShapeDtypeStruct((B, POOL_BUF, POOL_W), f32),
            jax.ShapeDtypeStruct((B, WINDOW, KV_W), f32),
            jax.ShapeDtypeStruct((B, WINDOW, KV_W), f32),
        ],
        scratch_shapes=[pltpu.VMEM((2 * HIST + tl, POOL_W), f32), pltpu.VMEM((2 * HIST + tl, POOL_W), f32)],
        compiler_params=pltpu.CompilerParams(dimension_semantics=("arbitrary", "arbitrary")),
        name="inproj_prompt",
    )(x, mod3, mod3, g_attn, w_in, gqk, bd, w_pool, pool_scale)


def _inproj_sample_body(x_ref, sh_ref, sc_ref, g_ref, w_ref, gqk_ref, bd_ref, wp_ref, ps_ref, st_ref,
                        q_ref, k_ref, v_ref, po_ref, np_ref, ext, *, bt, ls, pos0):
    n = bt * ls
    h3 = _mod_norm(x_ref[...], g_ref[...][None], sc_ref[...], sh_ref[...])
    (up, q, k, v), = _qkv_from_h([h3.reshape(n, D_MODEL)], w_ref, gqk_ref, bd_ref)
    q_ref[...] = q.astype(bf16)
    k_ref[...] = k
    v_ref[...] = v

    ext[:, 1:HIST, :] = st_ref[...]
    ext[:, HIST:HIST + ls, :] = up.reshape(bt, ls, POOL_W)
    pos = pos0 + lax.broadcasted_iota(i32, (1, ls, 1), 1)
    for g, w in enumerate(POOL_WINDOWS):
        lanes = slice(g * POOL_GC, (g + 1) * POOL_GC)
        cur = ext[:, HIST:HIST + ls, lanes]
        acc = cur
        for s in range(1, w):
            acc = acc + ext[:, HIST - s:HIST - s + ls, lanes]
        cnt = jnp.minimum(w, pos + 1).astype(f32)
        d = (acc / cnt - cur).reshape(n, POOL_GC)
        yg = _dot(d.astype(bf16), wp_ref[g]) * ps_ref[:, lanes]
        po_ref[:, lanes] = yg.astype(bf16)
    np_ref[...] = ext[:, ls + 1:ls + HIST, :]


def _inproj_sample(x, mod3, g_attn, w_in, gqk, bd, w_pool, pool_scale, state, pos0, bt=64):
    B, ls, D = x.shape
    n = bt * ls
    full = lambda shape: pl.BlockSpec(shape, lambda i: (0,) * len(shape))
    return pl.pallas_call(
        functools.partial(_inproj_sample_body, bt=bt, ls=ls, pos0=pos0),
        grid=(B // bt,),
        in_specs=[
            pl.BlockSpec((bt, ls, D), lambda i: (i, 0, 0)),
            pl.BlockSpec((bt, 1, D), lambda i: (i, 0, 0)),
            pl.BlockSpec((bt, 1, D), lambda i: (i, 0, 1)),
            full((1, D)),
            full((D, QKV_W)),
            full((1, QK_W)),
            full((QK_W, QK_W)),
            full((4, POOL_GC, POOL_GC)),
            full((1, POOL_W)),
            pl.BlockSpec((bt, POOL_BUF, POOL_W), lambda i: (i, 0, 0)),
        ],
        out_specs=[
            pl.BlockSpec((n, ATTN_W), lambda i: (i, 0)),
            pl.BlockSpec((n, KV_W), lambda i: (i, 0)),
            pl.BlockSpec((n, KV_W), lambda i: (i, 0)),
            pl.BlockSpec((n, POOL_W), lambda i: (i, 0)),
            pl.BlockSpec((bt, POOL_BUF, POOL_W), lambda i: (i, 0, 0)),
        ],
        out_shape=[
            jax.ShapeDtypeStruct((B * ls, ATTN_W), bf16),
            jax.ShapeDtypeStruct((B * ls, KV_W), f32),
            jax.ShapeDtypeStruct((B * ls, KV_W), f32),
            jax.ShapeDtypeStruct((B * ls, POOL_W), bf16),
            jax.ShapeDtypeStruct((B, POOL_BUF, POOL_W), f32),
        ],
        scratch_shapes=[pltpu.VMEM((bt, HIST + ls, POOL_W), f32)],
        name="inproj_sample",
    )(x, mod3, mod3, g_attn, w_in, gqk, bd, w_pool, pool_scale, state)


def _softmax_sink(parts, sink):
    m = sink
    for s in parts:
        m = jnp.maximum(m, jnp.max(s, axis=-1, keepdims=True))
    ps = [jnp.exp(s - m) for s in parts]
    denom = jnp.exp(sink - m)
    for p in ps:
        denom = denom + jnp.sum(p, axis=-1, keepdims=True)
    inv = 1.0 / denom
    return [(p * inv).astype(bf16) for p in ps]


def _attn_prompt_body(sinks_ref, q_ref, kp_ref, kc_ref, vp_ref, vc_ref, bias_ref, mask_ref, o_ref):
    j = pl.program_id(1)
    nk = 2 * WINDOW
    qw = GQA * HEAD_DIM
    nq = q_ref.shape[1] // WINDOW
    kall = jnp.concatenate([kp_ref[0], kc_ref[0]], axis=0)
    vall = jnp.concatenate([vp_ref[0], vc_ref[0]], axis=0)
    lane_group = lax.broadcasted_iota(i32, (nk, 2 * KV_W), 1) // HEAD_DIM
    first_has_prev = jnp.minimum(j, 1)

    def spread(t):
        t0 = jnp.concatenate([t, t], axis=1)
        return t0, pltpu.roll(t0, HEAD_DIM, 1)

    def blockdiag(t01, kv):
        t0, t1 = t01
        return jnp.concatenate(
            [jnp.where(lane_group == g, t0 if g % 2 == kv else t1, jnp.zeros_like(t0)) for g in range(GQA)],
            axis=0)

    chains = [(qb, kv) for qb in range(nq) for kv in range(N_KV_HEADS)]
    ksp = [spread(kall[qb * WINDOW:qb * WINDOW + nk]) for qb in range(nq)]
    vsp = [spread(vall[qb * WINDOW:qb * WINDOW + nk]) for qb in range(nq)]
    valid = [mask_ref[first_has_prev] > 0.5] + [mask_ref[1] > 0.5] * (nq - 1)
    s = [_dot_t(q_ref[0, qb * WINDOW:(qb + 1) * WINDOW, kv * qw:(kv + 1) * qw], blockdiag(ksp[qb], kv))
         for qb, kv in chains]
    s = [jnp.where(valid[qb], s[c] + bias_ref[kv], NEG_INF) for c, (qb, kv) in enumerate(chains)]
    p = [jnp.concatenate([_softmax_sink([s[c][:, g * nk:(g + 1) * nk]], sinks_ref[kv * GQA + g])[0]
                          for g in range(GQA)], axis=1) for c, (qb, kv) in enumerate(chains)]
    o = [_dot(p[c], blockdiag(vsp[qb], kv)) for c, (qb, kv) in enumerate(chains)]
    for c, (qb, kv) in enumerate(chains):
        o_ref[0, qb * WINDOW:(qb + 1) * WINDOW, kv * qw:(kv + 1) * qw] = o[c].astype(bf16)


def _attn_prompt(q, k, v, bias, sinks):
    B, L, _ = q.shape
    nq = 4
    nb = L // (nq * WINDOW)
    cur = lambda b, j: (b, j, 0)
    prev = lambda b, j: (b, jnp.maximum(nq * j - 1, 0), 0)
    qi = np.arange(WINDOW)[:, None]
    kc = np.arange(2 * WINDOW)[None, :]
    own = (kc >= WINDOW) & (kc - WINDOW <= qi)
    prv = (kc < WINDOW) & (kc > qi)
    mask = np.stack([np.tile(own, (1, GQA)), np.tile(own | prv, (1, GQA))]).astype(np.float32)
    return pl.pallas_call(
        _attn_prompt_body,
        grid=(B, nb),
        in_specs=[
            pl.BlockSpec(memory_space=pltpu.SMEM),
            pl.BlockSpec((1, nq * WINDOW, ATTN_W), cur),
            pl.BlockSpec((1, WINDOW, KV_W), prev),
            pl.BlockSpec((1, nq * WINDOW, KV_W), cur),
            pl.BlockSpec((1, WINDOW, KV_W), prev),
            pl.BlockSpec((1, nq * WINDOW, KV_W), cur),
            pl.BlockSpec((N_KV_HEADS, WINDOW, GQA * 2 * WINDOW), lambda b, j: (0, 0, 0)),
            pl.BlockSpec((2, WINDOW, GQA * 2 * WINDOW), lambda b, j: (0, 0, 0)),
        ],
        out_specs=pl.BlockSpec((1, nq * WINDOW, ATTN_W), cur),
        out_shape=jax.ShapeDtypeStruct((B, L, ATTN_W), bf16),
        name="attn_prompt",
    )(sinks, q, k, k, v, v, bias, jnp.asarray(mask))


def _attn_sample_body(q_ref, kb_ref, vb_ref, kn_ref, vn_ref, bb_ref, bn_ref, sink_ref,
                      o_ref, nk_ref, nv_ref, *, bb, ls):
    W = kb_ref.shape[1]
    rows = GQA * ls
    qi = lax.broadcasted_iota(i32, (rows, W), 0) % ls
    kj = lax.broadcasted_iota(i32, (rows, W), 1)
    valid_buf = kj > qi
    qi2 = lax.broadcasted_iota(i32, (rows, ls), 0) % ls
    kj2 = lax.broadcasted_iota(i32, (rows, ls), 1)
    valid_new = kj2 <= qi2

    nbat = 4
    ks = [slice(kv * HEAD_DIM, (kv + 1) * HEAD_DIM) for kv in range(N_KV_HEADS)]

    def group(i, carry):
        bs = [i * nbat + u for u in range(nbat)]
        chains = [(u, kv) for u in range(nbat) for kv in range(N_KV_HEADS)]
        qb = [q_ref[b] for b in bs]
        kbuf = [kb_ref[b] for b in bs]
        vbuf = [vb_ref[b] for b in bs]
        knew = [kn_ref[b] for b in bs]
        vnew = [vn_ref[b] for b in bs]
        qg = [jnp.concatenate([qb[u][:, (kv * GQA + g) * HEAD_DIM:(kv * GQA + g + 1) * HEAD_DIM]
                               for g in range(GQA)], axis=0) for u, kv in chains]
        s_buf = [_dot_t(qg[c], kbuf[u][:, ks[kv]].astype(bf16)) for c, (u, kv) in enumerate(chains)]
        s_new = [_dot_t(qg[c], knew[u][:, ks[kv]].astype(bf16)) for c, (u, kv) in enumerate(chains)]
        s_buf = [jnp.where(valid_buf, s_buf[c] + bb_ref[kv], NEG_INF) for c, (u, kv) in enumerate(chains)]
        s_new = [jnp.where(valid_new, s_new[c] + bn_ref[kv], NEG_INF) for c, (u, kv) in enumerate(chains)]
        probs = [_softmax_sink([s_buf[c], s_new[c]], sink_ref[kv]) for c, (u, kv) in enumerate(chains)]
        outs = [_dot(probs[c][0], vbuf[u][:, ks[kv]].astype(bf16)) + _dot(probs[c][1], vnew[u][:, ks[kv]].astype(bf16))
                for c, (u, kv) in enumerate(chains)]
        for u, b in enumerate(bs):
            heads = [outs[u * N_KV_HEADS + kv][g * ls:(g + 1) * ls] for kv in range(N_KV_HEADS) for g in range(GQA)]
            o_ref[b] = jnp.concatenate(heads, axis=-1).astype(bf16)
            nk_ref[b, 0:W - ls, :] = kbuf[u][ls:, :]
            nk_ref[b, W - ls:W, :] = knew[u]
            nv_ref[b, 0:W - ls, :] = vbuf[u][ls:, :]
            nv_ref[b, W - ls:W, :] = vnew[u]
        return carry

    lax.fori_loop(0, bb // nbat, group, 0)


def _attn_sample(q, k_buf, v_buf, k_new, v_new, bias_buf, bias_new, sink_col, bb=16):
    B, ls, _ = q.shape
    W = k_buf.shape[1]
    rows = GQA * ls
    blk = lambda shape: pl.BlockSpec(shape, lambda i: (i, 0, 0))
    full = lambda shape: pl.BlockSpec(shape, lambda i: (0, 0, 0))
    return pl.pallas_call(
        functools.partial(_attn_sample_body, bb=bb, ls=ls),
        grid=(B // bb,),
        in_specs=[
            blk((bb, ls, ATTN_W)),
            blk((bb, W, KV_W)),
            blk((bb, W, KV_W)),
            blk((bb, ls, KV_W)),
            blk((bb, ls, KV_W)),
            full((N_KV_HEADS, rows, W)),
            full((N_KV_HEADS, rows, ls)),
            full((N_KV_HEADS, rows, 1)),
        ],
        out_specs=[blk((bb, ls, ATTN_W)), blk((bb, W, KV_W)), blk((bb, W, KV_W))],
        out_shape=[
            jax.ShapeDtypeStruct((B, ls, ATTN_W), bf16),
            jax.ShapeDtypeStruct((B, W, KV_W), f32),
            jax.ShapeDtypeStruct((B, W, KV_W), f32),
        ],
        name="attn_sample",
    )(q, k_buf, v_buf, k_new, v_new, bias_buf, bias_new, sink_col)


def _outproj_core(po, at, x, gt, sc, sh, g_ref, wo_ref, wrh_ref, wrl_ref):
    mixo = _dot(po, wo_ref[0:POOL_W, :]) + _dot(at, wo_ref[POOL_W:, :])
    x1 = x + gt * mixo.reshape(x.shape)
    h2 = _mod_norm(x1, g_ref[...].reshape((1,) * (x.ndim - 1) + (D_MODEL,)), sc, sh).reshape(-1, D_MODEL)
    h_hi, h_lo = _split_bf16(h2)
    wh = wrh_ref[...]
    logits = _dot_t(wh, h_hi) + (_dot_t(wh, h_lo) + _dot_t(wrl_ref[...], h_hi))
    return x1, _pack_pairs(h2), logits


def _outproj_prompt_body(po_ref, at_ref, x_ref, gt_ref, sc_ref, sh_ref, g_ref, wo_ref, wrh_ref, wrl_ref,
                         x1_ref, h2_ref, lg_ref):
    x1, h2p, logits = _outproj_core(po_ref[0], at_ref[0], x_ref[0], gt_ref[0], sc_ref[0], sh_ref[0],
                                    g_ref, wo_ref, wrh_ref, wrl_ref)
    x1_ref[...] = x1
    h2_ref[...] = h2p
    lg_ref[...] = logits


def _outproj_sample_body(po_ref, at_ref, x_ref, gt_ref, sc_ref, sh_ref, g_ref, wo_ref, wrh_ref, wrl_ref,
                         x1_ref, h2_ref, lg_ref):
    x1, h2p, logits = _outproj_core(po_ref[...], at_ref[...], x_ref[...], gt_ref[...], sc_ref[...], sh_ref[...],
                                    g_ref, wo_ref, wrh_ref, wrl_ref)
    x1_ref[...] = x1.reshape(-1, D_MODEL)
    h2_ref[...] = h2p
    lg_ref[...] = logits


def _outproj_prompt(po, at, x, mod3, g_ffn, w_out, wr_hi, wr_lo, tm=512):
    B, L, D = x.shape
    nt = L // tm
    n_tok = B * L
    full = lambda shape: pl.BlockSpec(shape, lambda b, j: (0,) * len(shape))
    modspec = lambda c: pl.BlockSpec((1, 1, D), lambda b, j: (b, 0, c))
    return pl.pallas_call(
        _outproj_prompt_body,
        grid=(B, nt),
        in_specs=[
            pl.BlockSpec((1, tm, POOL_W), lambda b, j: (b, j, 0)),
            pl.BlockSpec((1, tm, ATTN_W), lambda b, j: (b, j, 0)),
            pl.BlockSpec((1, tm, D), lambda b, j: (b, j, 0)),
            modspec(2), modspec(4), modspec(3),
            full((1, D)), full((D, D)), full((N_EXPERTS, D)), full((N_EXPERTS, D)),
        ],
        out_specs=[
            pl.BlockSpec((tm, D), lambda b, j: (b * nt + j, 0)),
            pl.BlockSpec((tm, D // 2), lambda b, j: (b * nt + j, 0)),
            pl.BlockSpec((N_EXPERTS, tm), lambda b, j: (0, b * nt + j)),
        ],
        out_shape=[
            jax.ShapeDtypeStruct((n_tok, D), f32),
            jax.ShapeDtypeStruct((n_tok, D // 2), i32),
            jax.ShapeDtypeStruct((N_EXPERTS, n_tok), f32),
        ],
        name="outproj_prompt",
    )(po, at, x, mod3, mod3, mod3, g_ffn, w_out, wr_hi, wr_lo)


def _outproj_sample(po, at, x, mod3, g_ffn, w_out, wr_hi, wr_lo, bt=64):
    B, ls, D = x.shape
    n = bt * ls
    full = lambda shape: pl.BlockSpec(shape, lambda i: (0,) * len(shape))
    modspec = lambda c: pl.BlockSpec((bt, 1, D), lambda i: (i, 0, c))
    return pl.pallas_call(
        _outproj_sample_body,
        grid=(B // bt,),
        in_specs=[
            pl.BlockSpec((n, POOL_W), lambda i: (i, 0)),
            pl.BlockSpec((n, ATTN_W), lambda i: (i, 0)),
            pl.BlockSpec((bt, ls, D), lambda i: (i, 0, 0)),
            modspec(2), modspec(4), modspec(3),
            full((1, D)), full((D, D)), full((N_EXPERTS, D)), full((N_EXPERTS, D)),
        ],
        out_specs=[
            pl.BlockSpec((n, D), lambda i: (i, 0)),
            pl.BlockSpec((n, D // 2), lambda i: (i, 0)),
            pl.BlockSpec((N_EXPERTS, n), lambda i: (0, i)),
        ],
        out_shape=[
            jax.ShapeDtypeStruct((B * ls, D), f32),
            jax.ShapeDtypeStruct((B * ls, D // 2), i32),
            jax.ShapeDtypeStruct((N_EXPERTS, B * ls), f32),
        ],
        name="outproj_sample",
    )(po, at, x, mod3, mod3, mod3, g_ffn, w_out, wr_hi, wr_lo)


def _route_body(lga_ref, lgb_ref, rb_ref, tri_ref, idx_ref, rank_ref, gate_ref, cnt_ref, carry, *, tr, nsteps, na):
    step = pl.program_id(0)

    @pl.when(step == 0)
    def _():
        carry[...] = jnp.zeros(carry.shape, f32)

    logits = jnp.where(step < na, lga_ref[...], lgb_ref[...])
    s = jax.nn.sigmoid(logits)
    sb = s + rb_ref[...]
    e_iota = lax.broadcasted_iota(i32, (N_EXPERTS, tr), 0)
    g_iota = lax.broadcasted_iota(i32, (GROUP_SIZE, tr), 0)

    gscore = []
    for g in range(N_EXPERT_GROUPS):
        v = sb[g * GROUP_SIZE:(g + 1) * GROUP_SIZE]
        m1 = jnp.max(v, axis=0, keepdims=True)
        i1 = jnp.min(jnp.where(v == m1, g_iota, GROUP_SIZE), axis=0, keepdims=True)
        m2 = jnp.max(jnp.where(g_iota == i1, -jnp.inf, v), axis=0, keepdims=True)
        gscore.append(m1 + m2)
    parts = []
    for g in range(N_EXPERT_GROUPS):
        beaten = jnp.zeros((1, tr), i32)
        for g2 in range(N_EXPERT_GROUPS):
            if g2 == g:
                continue
            ahead = gscore[g2] > gscore[g]
            if g2 < g:
                ahead = ahead | (gscore[g2] == gscore[g])
            beaten = beaten + ahead.astype(i32)
        keep = beaten < TOPK_GROUPS
        parts.append(jnp.where(keep, sb[g * GROUP_SIZE:(g + 1) * GROUP_SIZE], NEG_INF))
    cur = jnp.concatenate(parts, axis=0)

    idxs, svals = [], []
    for _ in range(TOP_K):
        m = jnp.max(cur, axis=0, keepdims=True)
        ik = jnp.min(jnp.where(cur == m, e_iota, N_EXPERTS), axis=0, keepdims=True)
        hit = e_iota == ik
        svals.append(jnp.sum(jnp.where(hit, s, 0.0), axis=0, keepdims=True))
        cur = jnp.where(hit, -jnp.inf, cur)
        idxs.append(ik)
    sel = (cur == -jnp.inf).astype(f32)
    ssum = svals[0]
    for sv in svals[1:]:
        ssum = ssum + sv
    gate_ref[...] = jnp.concatenate([sv / ssum * ROUTED_SCALE for sv in svals], axis=0)
    idx_ref[...] = jnp.concatenate(idxs, axis=0)

    before = carry[...] + _dot(sel.astype(bf16), tri_ref[...])
    ranks = [jnp.sum(jnp.where(e_iota == ik, before, 0.0), axis=0, keepdims=True) for ik in idxs]
    rank_ref[...] = jnp.concatenate(ranks, axis=0).astype(i32)
    carry[...] = carry[...] + jnp.sum(sel, axis=1, keepdims=True)

    @pl.when(step == nsteps - 1)
    def _():
        cnt_ref[...] = carry[...]


def _route(logits_a, logits_b, router_bias, tr=512):
    E, Ta = logits_a.shape
    T = Ta + logits_b.shape[1]
    na = Ta // tr
    nsteps = T // tr
    tri = jnp.asarray(np.triu(np.ones((tr, tr), np.float32), 1), bf16)
    return pl.pallas_call(
        functools.partial(_route_body, tr=tr, nsteps=nsteps, na=na),
        grid=(nsteps,),
        in_specs=[
            pl.BlockSpec((E, tr), lambda i: (0, jnp.minimum(i, na - 1))),
            pl.BlockSpec((E, tr), lambda i: (0, jnp.maximum(i - na, 0))),
            pl.BlockSpec((E, 1), lambda i: (0, 0)),
            pl.BlockSpec((tr, tr), lambda i: (0, 0)),
        ],
        out_specs=[
            pl.BlockSpec((TOP_K, tr), lambda i: (0, i)),
            pl.BlockSpec((TOP_K, tr), lambda i: (0, i)),
            pl.BlockSpec((TOP_K, tr), lambda i: (0, i)),
            pl.BlockSpec((E, 1), lambda i: (0, 0)),
        ],
        out_shape=[
            jax.ShapeDtypeStruct((TOP_K, T), i32),
            jax.ShapeDtypeStruct((TOP_K, T), i32),
            jax.ShapeDtypeStruct((TOP_K, T), f32),
            jax.ShapeDtypeStruct((E, 1), f32),
        ],
        scratch_shapes=[pltpu.VMEM((E, 1), f32)],
        compiler_params=pltpu.CompilerParams(dimension_semantics=("arbitrary",)),
        name="route",
    )(logits_a, logits_b, router_bias.reshape(E, 1), tri)


def _dest_body(idx_ref, rank_ref, ps_ref, dest_ref, *, tr):
    e_iota = lax.broadcasted_iota(i32, (N_EXPERTS, tr), 0)
    start = ps_ref[...]
    rows = []
    for k in range(TOP_K):
        hit = e_iota == idx_ref[k:k + 1, :]
        rows.append(jnp.sum(jnp.where(hit, start, 0.0), axis=0, keepdims=True))
    dest_ref[...] = jnp.concatenate(rows, axis=0).astype(i32) + rank_ref[...]


def _dest_rows(idx, rank, pad_start, tr=512):
    K, T = idx.shape
    blk = pl.BlockSpec((K, tr), lambda i: (0, i))
    return pl.pallas_call(
        functools.partial(_dest_body, tr=tr),
        grid=(T // tr,),
        in_specs=[blk, blk, pl.BlockSpec((N_EXPERTS, 1), lambda i: (0, 0))],
        out_specs=blk,
        out_shape=jax.ShapeDtypeStruct((K, T), i32),
        name="dest_rows",
    )(idx, rank, pad_start.astype(f32).reshape(N_EXPERTS, 1))


def _sc_mesh():
    return plsc.VectorSubcoreMesh(core_axis_name="c", subcore_axis_name="s")


def _sc_worker_id():
    return lax.axis_index("s") * 2 + lax.axis_index("c")


def _dispatch(h2_a, h2_b, dest, gate, n_rows, chunk=32):
    ta, Dw = h2_a.shape
    T = ta + h2_b.shape[0]
    per_worker = T // SC_WORKERS
    nchunk = per_worker // chunk
    assert per_worker * SC_WORKERS == T and nchunk * chunk == per_worker and ta % chunk == 0

    nrow_idx = nchunk * TOP_K

    @functools.partial(
        pl.kernel, mesh=_sc_mesh(),
        out_type=[jax.ShapeDtypeStruct((n_rows, Dw), i32), jax.ShapeDtypeStruct((n_rows, GATE_ROW), f32)],
        scratch_types=[pltpu.VMEM((nrow_idx, chunk), i32), pltpu.VMEM((nrow_idx, chunk), f32),
                       pltpu.VMEM((chunk, Dw), i32), pltpu.VMEM((TOP_K, chunk, GATE_ROW), f32),
                       pltpu.SemaphoreType.DMA],
        compiler_params=pltpu.CompilerParams(needs_layout_passes=False),
        name="moe_dispatch",
    )
    def body(ha_hbm, hb_hbm, dest_hbm, gate_hbm, xs_hbm, gs_hbm, idx_v, gate_v, rows_v, grow_v, sem):
        wid = _sc_worker_id()
        base = wid * per_worker
        pltpu.sync_copy(dest_hbm.at[wid], idx_v)
        pltpu.sync_copy(gate_hbm.at[wid], gate_v)
        zero = jnp.zeros((SC_LANES,), f32)
        for k in range(TOP_K):
            @pl.loop(0, chunk)
            def _(t):
                for j in range(GATE_ROW // SC_LANES):
                    grow_v[k, t, pl.ds(j * SC_LANES, SC_LANES)] = zero

        @pl.loop(0, nchunk)
        def _(ci):
            t0 = base + ci * chunk

            @pl.when(t0 < ta)
            def _():
                pltpu.sync_copy(ha_hbm.at[pl.ds(t0, chunk)], rows_v)

            @pl.when(t0 >= ta)
            def _():
                pltpu.sync_copy(hb_hbm.at[pl.ds(t0 - ta, chunk)], rows_v)

            for k in range(TOP_K):
                @pl.loop(0, chunk)
                def _(t):
                    row = jnp.zeros((SC_LANES,), i32) + (ci * TOP_K + k)
                    grow_v[k, t, pl.ds(0, SC_LANES)] = plsc.load_gather(
                        gate_v, [row, jnp.zeros((SC_LANES,), i32) + t])

            cps = []
            for k in range(TOP_K):
                idx = idx_v.at[ci * TOP_K + k]
                cps.append(pltpu.make_async_copy(rows_v, xs_hbm.at[idx], sem))
                cps.append(pltpu.make_async_copy(grow_v.at[k], gs_hbm.at[idx], sem))
            for cp in cps:
                cp.start()
            for cp in cps:
                cp.wait()

    def per_worker_rows(a):
        return a.reshape(TOP_K, SC_WORKERS, nchunk, chunk).transpose(1, 2, 0, 3).reshape(
            SC_WORKERS, nrow_idx, chunk)

    return body(h2_a, h2_b, per_worker_rows(dest), per_worker_rows(gate))


def _combine(ys, dest, chunk=8):
    T = dest.shape[1]
    Dw = ys.shape[1]
    per_worker = T // SC_WORKERS
    nchunk = per_worker // chunk
    assert per_worker * SC_WORKERS == T and nchunk * chunk == per_worker and nchunk % 2 == 0

    @functools.partial(
        pl.kernel, mesh=_sc_mesh(),
        out_type=jax.ShapeDtypeStruct((T, 2 * Dw), f32),
        scratch_types=[
            pltpu.VMEM((TOP_K * per_worker,), i32),
            pltpu.VMEM((2, TOP_K, chunk, Dw), i32),
            pltpu.VMEM((chunk, 2 * Dw), f32),
            pltpu.SemaphoreType.DMA((2,)),
        ],
        compiler_params=pltpu.CompilerParams(needs_layout_passes=False),
        name="moe_combine",
    )
    def body(ys_hbm, dest_hbm, out_hbm, idx_v, buf, out_v, sems):
        base = _sc_worker_id() * per_worker
        pltpu.sync_copy(dest_hbm.at[pl.ds(_sc_worker_id() * (TOP_K * per_worker), TOP_K * per_worker)], idx_v)

        def gather(ci, slot):
            return [pltpu.make_async_copy(ys_hbm.at[idx_v.at[pl.ds(k * per_worker + ci * chunk, chunk)]],
                                          buf.at[slot, k], sems.at[slot]) for k in range(TOP_K)]

        for cp in gather(0, 0):
            cp.start()

        @pl.loop(0, nchunk, step=2)
        def _(c0):
            for slot in range(2):
                ci = c0 + slot

                @pl.when(ci + 1 < nchunk)
                def _():
                    for cp in gather(ci + 1, 1 - slot):
                        cp.start()

                for cp in gather(ci, slot):
                    cp.wait()

                @pl.loop(0, chunk)
                def _(t):
                    @plsc.parallel_loop(0, Dw // SC_LANES, unroll=4)
                    def _(j):
                        sl = pl.ds(j * SC_LANES, SC_LANES)
                        w = buf[slot, 0, t, sl]
                        hi = plsc.bitcast(w & HI_MASK, f32)
                        lo = plsc.bitcast(lax.shift_left(w, 16), f32)
                        for k in range(1, TOP_K):
                            w = buf[slot, k, t, sl]
                            hi = hi + plsc.bitcast(w & HI_MASK, f32)
                            lo = lo + plsc.bitcast(lax.shift_left(w, 16), f32)
                        out_v[t, sl] = hi
                        out_v[t, pl.ds(Dw + j * SC_LANES, SC_LANES)] = lo

                pltpu.sync_copy(out_v, out_hbm.at[pl.ds(base + ci * chunk, chunk)])

    dest_w = dest.reshape(TOP_K, SC_WORKERS, per_worker).transpose(1, 0, 2).reshape(-1)
    return body(ys, dest_w)


def _gmm_body(blk_e_ref, blk_rows_ref, blk_next_ref, nv_ref, xs_hbm, gs_hbm, w1_hbm, w3_hbm, w2_hbm, ys_hbm,
              xbuf, gbuf, ybuf, w1f, w3f, w2f, xsem, gsem, ysem, wsem):
    nv = nv_ref[0]
    nb = blk_e_ref.shape[0]
    half = D_MODEL // 2
    RB = EXPERT_BLOCK
    ahead_w = W_RING - 2

    def expert_of(blk):
        return blk_e_ref[jnp.minimum(blk, nb - 1)]

    def next_expert_block(blk):
        return jnp.where(blk < nv, blk_next_ref[jnp.minimum(blk, nb - 1)], blk)

    def start_weights(blk, ordinal):
        @pl.when(blk < nv)
        def _():
            for cp in weight_copies(expert_of(blk), lax.rem(ordinal, W_RING)):
                cp.start()

    def row_copies(b, slot):
        r0 = pl.multiple_of(b * RB, RB)
        return (pltpu.make_async_copy(xs_hbm.at[pl.ds(r0, RB)], xbuf.at[slot], xsem.at[slot]),
                pltpu.make_async_copy(gs_hbm.at[pl.ds(r0, RB)], gbuf.at[slot], gsem.at[slot]))

    def out_copy(b, slot):
        r0 = pl.multiple_of(b * RB, RB)
        return pltpu.make_async_copy(ybuf.at[slot], ys_hbm.at[pl.ds(r0, RB)], ysem.at[slot])

    def weight_copies(e, ws):
        return (pltpu.make_async_copy(w1_hbm.at[e], w1f.at[ws], wsem.at[ws, 0]),
                pltpu.make_async_copy(w3_hbm.at[e], w3f.at[ws], wsem.at[ws, 1]),
                pltpu.make_async_copy(w2_hbm.at[e], w2f.at[ws], wsem.at[ws, 2]))

    blk = jnp.int32(0)
    for n in range(ahead_w):
        start_weights(blk, n)
        blk = next_expert_block(blk)
    for i in range(ROW_RING - 2):
        @pl.when(i < nv)
        def _():
            for cp in row_copies(i, i):
                cp.start()

    def enter(b, live, ordinal_prev):
        e = expert_of(b)
        first = live & ((b == 0) | (e != expert_of(jnp.maximum(b - 1, 0))))
        ordinal = jnp.where(first & (b > 0), ordinal_prev + 1, ordinal_prev)

        @pl.when(first)
        def _():
            for cp in weight_copies(e, lax.rem(ordinal, W_RING)):
                cp.wait()
            nxt = b
            for _ in range(ahead_w):
                nxt = next_expert_block(nxt)
            start_weights(nxt, ordinal + ahead_w)

        return ordinal

    def load_block(b, slot, ws):
        valid = lax.broadcasted_iota(i32, (RB, 1), 0) < blk_rows_ref[jnp.minimum(b, nb - 1)]
        x_hi, x_lo = _unpack_pairs(jnp.where(valid, xbuf[slot], 0))
        g = jnp.where(valid, gbuf[slot][:, 0:1], 0.0)
        return x_hi, x_lo, g, ws

    def pair(p, ordinal_prev):
        b0 = 2 * p
        b1 = b0 + 1
        live1 = b1 < nv
        slot0 = lax.rem(b0, ROW_RING)
        slots = (slot0, jnp.where(live1, lax.rem(b1, ROW_RING), slot0))

        for b in (b0 + ROW_RING - 2, b1 + ROW_RING - 2):
            @pl.when(b < nv)
            def _():
                for cp in row_copies(b, lax.rem(b, ROW_RING)):
                    cp.start()

        ord0 = enter(b0, b0 < nv, ordinal_prev)
        ord1 = enter(b1, live1, ord0)
        for cp in row_copies(b0, slots[0]):
            cp.wait()

        @pl.when(live1)
        def _():
            for cp in row_copies(b1, slots[1]):
                cp.wait()

        blocks = [load_block(b0, slots[0], lax.rem(ord0, W_RING)),
                  load_block(jnp.where(live1, b1, b0), slots[1], lax.rem(ord1, W_RING))]
        a = [_dot(xh, w1f[ws, 0:half, :].astype(bf16)) + _dot(xl, w1f[ws, half:, :].astype(bf16))
             for xh, xl, _, ws in blocks]
        c = [_dot(xh, w3f[ws, 0:half, :].astype(bf16)) + _dot(xl, w3f[ws, half:, :].astype(bf16))
             for xh, xl, _, ws in blocks]
        hmid = [((a[i] * jax.nn.sigmoid(a[i])) * c[i]).astype(bf16) for i in range(2)]
        y = [_pack_pairs(_dot(hmid[i], w2f[blocks[i][3]].astype(bf16)) * blocks[i][2]) for i in range(2)]

        for i, (b, live) in enumerate(((b0, b0 < nv), (b1, live1))):
            @pl.when(live & (b >= ROW_RING))
            def _():
                out_copy(b - ROW_RING, slots[i]).wait()

            @pl.when(live)
            def _():
                ybuf[slots[i]] = y[i]
                out_copy(b, slots[i]).start()

        return ord1

    lax.fori_loop(0, (nv + 1) // 2, pair, 0)

    for i in range(1, ROW_RING + 1):
        @pl.when(nv >= i)
        def _():
            out_copy(nv - i, lax.rem(nv - i, ROW_RING)).wait()


def _gmm(xs, gs, w1, w3, w2, blk_e, blk_rows, blk_next, n_valid):
    n_rows, Dw = xs.shape
    D = 2 * Dw
    RB = EXPERT_BLOCK
    hbm = pl.BlockSpec(memory_space=pl.ANY)
    return pl.pallas_call(
        _gmm_body,
        grid_spec=pltpu.PrefetchScalarGridSpec(
            num_scalar_prefetch=4,
            grid=(1,),
            in_specs=[hbm, hbm, hbm, hbm, hbm],
            out_specs=hbm,
            scratch_shapes=[
                pltpu.VMEM((ROW_RING, RB, Dw), i32), pltpu.VMEM((ROW_RING, RB, GATE_ROW), f32),
                pltpu.VMEM((ROW_RING, RB, Dw), i32),
                pltpu.VMEM((W_RING, D, EXPERT_FF), f32), pltpu.VMEM((W_RING, D, EXPERT_FF), f32),
                pltpu.VMEM((W_RING, EXPERT_FF, D), f32),
                pltpu.SemaphoreType.DMA((ROW_RING,)), pltpu.SemaphoreType.DMA((ROW_RING,)),
                pltpu.SemaphoreType.DMA((ROW_RING,)), pltpu.SemaphoreType.DMA((W_RING, 3)),
            ],
        ),
        out_shape=jax.ShapeDtypeStruct((n_rows, Dw), i32),
        compiler_params=pltpu.CompilerParams(dimension_semantics=("arbitrary",)),
        name="moe_gmm",
    )(blk_e, blk_rows, blk_next, n_valid, xs, gs, w1, w3, w2)


def _final_core(x1, h2p, comb, gt, ws1_ref, ws3_ref, ws2_ref):
    half = D_MODEL // 2
    h_hi, h_lo = _unpack_pairs(h2p)
    a = _dot(h_hi, ws1_ref[0:half, :]) + _dot(h_lo, ws1_ref[half:, :])
    c = _dot(h_hi, ws3_ref[0:half, :]) + _dot(h_lo, ws3_ref[half:, :])
    shared = _dot(((a * jax.nn.sigmoid(a)) * c).astype(bf16), ws2_ref[...])
    return x1, comb + shared, gt


def _final_prompt_body(x1_ref, h2_ref, cb_ref, gt_ref, ws1_ref, ws3_ref, ws2_ref, y_ref):
    x1, ffn, gt = _final_core(x1_ref[...], h2_ref[...], cb_ref[...], gt_ref[0], ws1_ref, ws3_ref, ws2_ref)
    y_ref[0] = x1 + gt * ffn


def _final_sample_body(x1_ref, h2_ref, cb_ref, gt_ref, ws1_ref, ws3_ref, ws2_ref, y_ref):
    x1, ffn, gt = _final_core(x1_ref[...], h2_ref[...], cb_ref[...], gt_ref[...], ws1_ref, ws3_ref, ws2_ref)
    shp = y_ref.shape
    y_ref[...] = x1.reshape(shp) + gt * ffn.reshape(shp)


def _final_prompt(x1, h2, comb, mod3, ws1, ws3, ws2, B, L, tm=512):
    D = D_MODEL
    nt = L // tm
    full = lambda shape: pl.BlockSpec(shape, lambda b, j: (0,) * len(shape))
    rows = pl.BlockSpec((tm, D), lambda b, j: (b * nt + j, 0))
    words = pl.BlockSpec((tm, D // 2), lambda b, j: (b * nt + j, 0))
    return pl.pallas_call(
        _final_prompt_body,
        grid=(B, nt),
        in_specs=[rows, words, rows, pl.BlockSpec((1, 1, D), lambda b, j: (b, 0, 5)),
                  full((D, EXPERT_FF)), full((D, EXPERT_FF)), full((EXPERT_FF, D))],
        out_specs=pl.BlockSpec((1, tm, D), lambda b, j: (b, j, 0)),
        out_shape=jax.ShapeDtypeStruct((B, L, D), f32),
        name="final_prompt",
    )(x1, h2, comb, mod3, ws1, ws3, ws2)


def _final_sample(x1, h2, comb, mod3, ws1, ws3, ws2, B, ls, row0, bt=64):
    D = D_MODEL
    n = bt * ls
    blk0 = row0 // n
    full = lambda shape: pl.BlockSpec(shape, lambda i: (0,) * len(shape))
    rows = pl.BlockSpec((n, D), lambda i: (i, 0))
    words = pl.BlockSpec((n, D // 2), lambda i: (i, 0))
    comb_rows = pl.BlockSpec((n, D), lambda i: (blk0 + i, 0))
    return pl.pallas_call(
        _final_sample_body,
        grid=(B // bt,),
        in_specs=[rows, words, comb_rows, pl.BlockSpec((bt, 1, D), lambda i: (i, 0, 5)),
                  full((D, EXPERT_FF)), full((D, EXPERT_FF)), full((EXPERT_FF, D))],
        out_specs=pl.BlockSpec((bt, ls, D), lambda i: (i, 0, 0)),
        out_shape=jax.ShapeDtypeStruct((B, ls, D), f32),
        name="final_sample",
    )(x1, h2, comb, mod3, ws1, ws3, ws2)


def kernel(x_prompt, x_sample, state_pool, cache_swa_k, cache_swa_v, c_prompt, c_sample, w_ada, b_ada,
           g_attn_norm, w_in, g_q, g_k, w_pool, pool_scale, w_out, attn_sinks, rel_bias, g_ffn_norm,
           w_router, router_bias, w1, w3, w2, ws1, ws3, ws2):
    B, L, D = x_prompt.shape
    BS, LS, _ = x_sample.shape
    depth = w_ada.shape[0]
    assert depth == 1
    W = cache_swa_k.shape[2]
    tp, ts = B * L, BS * LS
    T = tp + ts
    n_rows = (T * TOP_K // EXPERT_BLOCK + N_EXPERTS) * EXPERT_BLOCK
    nb = n_rows // EXPERT_BLOCK

    g_attn = g_attn_norm[0].reshape(1, D)
    g_ffn = g_ffn_norm[0].reshape(1, D)
    w_in_b = w_in[0].astype(bf16)
    w_out_b = w_out[0].astype(bf16)
    w_pool_b = w_pool[0].astype(bf16)
    ps = pool_scale[0].reshape(1, POOL_W)
    gqk = jnp.concatenate([jnp.tile(g_q[0], N_HEADS), jnp.tile(g_k[0], N_KV_HEADS)]).reshape(1, QK_W)
    head_of = np.arange(QK_W) // HEAD_DIM
    bd = jnp.asarray((head_of[:, None] == head_of[None, :]).astype(np.float32), bf16)
    wr_t = w_router[0].T
    wr_hi = wr_t.astype(bf16)
    wr_lo = (wr_t - wr_hi.astype(f32)).astype(bf16)
    ws1_b, ws3_b, ws2_b = ws1[0].astype(bf16), ws3[0].astype(bf16), ws2[0].astype(bf16)
    sinks = attn_sinks[0]

    mod = _ada(jnp.concatenate([c_prompt, c_sample], axis=0), w_ada[0], b_ada[0])
    mod_p = mod[:B].reshape(B, 1, 6 * D)
    mod_s = mod[B:].reshape(BS, 1, 6 * D)

    dist_p = np.arange(WINDOW)[:, None] + WINDOW - np.arange(2 * WINDOW)[None, :]
    bias_p = _relbias(rel_bias, dist_p)
    bias_p = bias_p.reshape(N_KV_HEADS, GQA, WINDOW, 2 * WINDOW).transpose(0, 2, 1, 3).reshape(
        N_KV_HEADS, WINDOW, GQA * 2 * WINDOW)
    dist_s = np.arange(LS)[:, None] + W - np.arange(W + LS)[None, :]
    bias_s = _relbias(rel_bias, dist_s)
    bias_s_buf = bias_s[:, :, :W].reshape(N_KV_HEADS, GQA * LS, W)
    bias_s_new = bias_s[:, :, W:].reshape(N_KV_HEADS, GQA * LS, LS)
    sink_col = jnp.repeat(sinks, LS).reshape(N_KV_HEADS, GQA * LS, 1)

    q_p, k_p, v_p, po_p, new_pool_p, kc_p, vc_p = _inproj_prompt(
        x_prompt, mod_p, g_attn, w_in_b, gqk, bd, w_pool_b, ps)
    q_s, k_s, v_s, po_s, new_pool_s = _inproj_sample(
        x_sample, mod_s, g_attn, w_in_b, gqk, bd, w_pool_b, ps, state_pool[0], PAST_LEN)
    at_p = _attn_prompt(q_p, k_p, v_p, bias_p, sinks)
    at_s, nk_s, nv_s = _attn_sample(
        q_s.reshape(BS, LS, ATTN_W), cache_swa_k[0].reshape(BS, W, KV_W), cache_swa_v[0].reshape(BS, W, KV_W),
        k_s.reshape(BS, LS, KV_W), v_s.reshape(BS, LS, KV_W), bias_s_buf, bias_s_new, sink_col)

    x1_p, h2_p, lg_p = _outproj_prompt(po_p, at_p, x_prompt, mod_p, g_ffn, w_out_b, wr_hi, wr_lo)
    x1_s, h2_s, lg_s = _outproj_sample(po_s, at_s.reshape(ts, ATTN_W), x_sample, mod_s, g_ffn, w_out_b,
                                       wr_hi, wr_lo)

    idx, rank, gate, counts = _route(lg_p, lg_s, router_bias[0])
    counts = counts.reshape(N_EXPERTS).astype(i32)
    padded = (counts + EXPERT_BLOCK - 1) // EXPERT_BLOCK * EXPERT_BLOCK
    pad_end = jnp.cumsum(padded)
    pad_start = pad_end - padded
    dest = _dest_rows(idx, rank, pad_start)
    n_valid = (pad_end[-1] // EXPERT_BLOCK).astype(i32).reshape(1)
    blk_row0 = jnp.arange(nb, dtype=i32) * EXPERT_BLOCK
    blk_e = jnp.minimum(jnp.sum(blk_row0[:, None] >= pad_end[None, :], axis=1), N_EXPERTS - 1).astype(i32)
    own = jnp.arange(N_EXPERTS, dtype=i32)[None, :] == blk_e[:, None]
    blk_cnt = jnp.sum(jnp.where(own, counts[None, :], 0), axis=1)
    blk_start = jnp.sum(jnp.where(own, pad_start[None, :], 0), axis=1)
    blk_rows = jnp.clip(blk_cnt - (blk_row0 - blk_start), 0, EXPERT_BLOCK).astype(i32)
    blk_next = (jnp.sum(jnp.where(own, pad_end[None, :], 0), axis=1) // EXPERT_BLOCK).astype(i32)

    xs, gs = _dispatch(h2_p, h2_s, dest, gate, n_rows)
    ys = _gmm(xs, gs, w1[0], w3[0], w2[0], blk_e, blk_rows, blk_next, n_valid)
    comb = _combine(ys, dest)
    y_p = _final_prompt(x1_p, h2_p, comb, mod_p, ws1_b, ws3_b, ws2_b, B, L)
    y_s = _final_sample(x1_s, h2_s, comb, mod_s, ws1_b, ws3_b, ws2_b, BS, LS, tp)

    return (y_p, y_s, new_pool_p[None], kc_p.reshape(1, B, WINDOW, N_KV_HEADS, HEAD_DIM),
            vc_p.reshape(1, B, WINDOW, N_KV_HEADS, HEAD_DIM), new_pool_s[None],
            nk_s.reshape(1, BS, W, N_KV_HEADS, HEAD_DIM), nv_s.reshape(1, BS, W, N_KV_HEADS, HEAD_DIM))
```

```python
import functools
import math

import numpy as np
import jax
import jax.numpy as jnp
from jax import lax
from jax.experimental import pallas as pl
from jax.experimental.pallas import tpu as pltpu
from jax.experimental.pallas import tpu_sc as plsc

f32 = jnp.float32
bf16 = jnp.bfloat16
i32 = jnp.int32

D_MODEL = 1024
PAST_LEN = 8192
POOL_W = 512
POOL_WINDOWS = (2, 4, 8, 16)
POOL_GC = 128
POOL_BUF = 15
ATTN_W = 512
HEAD_DIM = 64
N_HEADS = 8
N_KV_HEADS = 2
GQA = 4
KV_W = 128
WINDOW = 128
NUM_BUCKETS = 32
MAX_EXACT = 16
REL_MAX_DIST = 128
N_EXPERTS = 256
N_EXPERT_GROUPS = 8
GROUP_SIZE = 32
TOPK_GROUPS = 4
TOP_K = 8
EXPERT_FF = 256
ROUTED_SCALE = 2.5
EXPERT_BLOCK = 128
EPS = 1e-6
NEG_INF = -1e30
QKV_W = POOL_W + ATTN_W + 2 * KV_W
QK_W = ATTN_W + KV_W
HIST = 16

SC_WORKERS = 32
SC_LANES = 16
GATE_ROW = 128


def _dot(a, b):
    return jnp.dot(a, b, preferred_element_type=f32)


def _dot_t(a, b):
    return lax.dot_general(a, b, (((1,), (1,)), ((), ())), preferred_element_type=f32)


def _split_bf16(a):
    hi = a.astype(bf16)
    lo = (a - hi.astype(f32)).astype(bf16)
    return hi, lo


ROW_RING = 8
W_RING = 5
HI_MASK = -65536


def _pack_pairs(a):
    h = a.shape[1] // 2
    hi = lax.bitcast_convert_type(a[:, :h].astype(bf16).astype(f32), i32)
    lo = lax.bitcast_convert_type(a[:, h:].astype(bf16).astype(f32), i32)
    return hi | lax.shift_right_logical(lo, 16)


def _unpack_pairs(w):
    hi = lax.bitcast_convert_type(w & HI_MASK, f32).astype(bf16)
    lo = lax.bitcast_convert_type(lax.shift_left(w, 16), f32).astype(bf16)
    return hi, lo


def _mod_norm(x, g, sc, sh):
    ms = jnp.mean(x * x, axis=-1, keepdims=True)
    y = x * lax.rsqrt(ms + EPS)
    return (y * g) * (1.0 + sc) + sh


def _ada_body(c_ref, w_ref, b_ref, o_ref):
    c = c_ref[...]
    a = (c * jax.nn.sigmoid(c)).astype(bf16)
    o_ref[...] = _dot(a, w_ref[...].astype(bf16)) + b_ref[...]


def _ada(c, w_ada, b_ada):
    n = c.shape[0]
    tn = 1024
    return pl.pallas_call(
        _ada_body,
        grid=(6 * D_MODEL // tn,),
        in_specs=[
            pl.BlockSpec((n, D_MODEL), lambda j: (0, 0)),
            pl.BlockSpec((D_MODEL, tn), lambda j: (0, j)),
            pl.BlockSpec((1, tn), lambda j: (0, j)),
        ],
        out_specs=pl.BlockSpec((n, tn), lambda j: (0, j)),
        out_shape=jax.ShapeDtypeStruct((n, 6 * D_MODEL), f32),
        name="ada_mod",
    )(c, w_ada, b_ada.reshape(1, -1))


def _relbias_body(table_ref, bucket_ref, o_ref):
    bucket = bucket_ref[...]
    for h in range(N_HEADS):
        acc = jnp.zeros(bucket.shape, f32)
        for b in range(NUM_BUCKETS):
            acc = jnp.where(bucket == b, table_ref[b, h], acc)
        o_ref[h] = acc


def _rel_buckets(dist):
    n = np.maximum(dist, 0)
    nf = np.maximum(n, 1).astype(np.float64)
    large = MAX_EXACT + (np.log(nf / MAX_EXACT) / math.log(REL_MAX_DIST / MAX_EXACT)
                         * (NUM_BUCKETS - MAX_EXACT)).astype(np.int32)
    return np.where(n < MAX_EXACT, n, np.minimum(large, NUM_BUCKETS - 1)).astype(np.int32)


def _relbias(table, dist):
    lq, lk = dist.shape
    return pl.pallas_call(
        _relbias_body,
        in_specs=[
            pl.BlockSpec(memory_space=pltpu.SMEM),
            pl.BlockSpec((lq, lk), lambda: (0, 0)),
        ],
        out_specs=pl.BlockSpec((N_HEADS, lq, lk), lambda: (0, 0, 0)),
        out_shape=jax.ShapeDtypeStruct((N_HEADS, lq, lk), f32),
        name="rel_bias",
    )(table, jnp.asarray(_rel_buckets(dist)))


def _qkv_from_h(hs, w_ref, gqk_ref, bd_ref):
    us = [_dot(h.astype(bf16), w_ref[...]) for h in hs]
    qks = [u[:, POOL_W:POOL_W + QK_W] for u in us]
    sq = [_split_bf16(qk * qk) for qk in qks]
    bd = bd_ref[...]
    ss = [_dot(y_hi, bd) + _dot(y_lo, bd) for y_hi, y_lo in sq]
    qkn = [(qk * lax.rsqrt(s * (1.0 / HEAD_DIM) + EPS)) * gqk_ref[...] for qk, s in zip(qks, ss)]
    return [(u[:, :POOL_W], n[:, :ATTN_W] * (HEAD_DIM ** -0.5), n[:, ATTN_W:], u[:, POOL_W + QK_W:])
            for u, n in zip(us, qkn)]


def _inproj_prompt_body(x_ref, sh_ref, sc_ref, g_ref, w_ref, gqk_ref, bd_ref, wp_ref, ps_ref,
                        q_ref, k_ref, v_ref, po_ref, np_ref, kc_ref, vc_ref, hist, wsum, *, tl, nt):
    j = pl.program_id(1)
    nsub = 2
    ts = tl // nsub
    hs = [_mod_norm(x_ref[0, i * ts:(i + 1) * ts, :], g_ref[...], sc_ref[0], sh_ref[0]) for i in range(nsub)]
    parts = _qkv_from_h(hs, w_ref, gqk_ref, bd_ref)

    base = 2 * HIST
    end = base + tl

    @pl.when(j == 0)
    def _():
        hist[0:base, :] = jnp.zeros((base, POOL_W), f32)
        wsum[0:HIST, :] = jnp.zeros((HIST, POOL_W), f32)

    for i, (up, q, k, v) in enumerate(parts):
        rows = slice(i * ts, (i + 1) * ts)
        q_ref[0, rows, :] = q.astype(bf16)
        k_ref[0, rows, :] = k.astype(bf16)
        v_ref[0, rows, :] = v.astype(bf16)
        hist[base + i * ts:base + (i + 1) * ts, :] = up

    @pl.when(j == nt - 1)
    def _():
        kc_ref[0] = parts[-1][2][ts - WINDOW:, :]
        vc_ref[0] = parts[-1][3][ts - WINDOW:, :]

    pos = j * tl + lax.broadcasted_iota(i32, (tl, 1), 0)
    src = hist
    for p, w in enumerate(POOL_WINDOWS):
        sh = w // 2
        live = slice(p * POOL_GC, POOL_W)
        summed = src[HIST:end, live] + src[HIST - sh:end - sh, live]
        wsum[HIST:end, live] = summed
        src = wsum
        lanes = slice(p * POOL_GC, (p + 1) * POOL_GC)
        cur = hist[base:end, lanes]
        cnt = jnp.minimum(w, pos + 1).astype(f32)
        d = summed[HIST:, 0:POOL_GC] / cnt - cur
        yg = _dot(d.astype(bf16), wp_ref[p]) * ps_ref[:, lanes]
        po_ref[0, :, lanes] = yg.astype(bf16)

    @pl.when(j == nt - 1)
    def _():
        np_ref[0] = hist[end - POOL_BUF:end, :]

    hist[HIST:base, :] = hist[end - HIST:end, :]


def _inproj_prompt(x, mod3, g_attn, w_in, gqk, bd, w_pool, pool_scale, tl=512):
    B, L, D = x.shape
    nt = L // tl
    full = lambda shape: pl.BlockSpec(shape, lambda b, j: (0,) * len(shape))
    return pl.pallas_call(
        functools.partial(_inproj_prompt_body, tl=tl, nt=nt),
        grid=(B, nt),
        in_specs=[
            pl.BlockSpec((1, tl, D), lambda b, j: (b, j, 0)),
            pl.BlockSpec((1, 1, D), lambda b, j: (b, 0, 0)),
            pl.BlockSpec((1, 1, D), lambda b, j: (b, 0, 1)),
            full((1, D)),
            full((D, QKV_W)),
            full((1, QK_W)),
            full((QK_W, QK_W)),
            full((4, POOL_GC, POOL_GC)),
            full((1, POOL_W)),
        ],
        out_specs=[
            pl.BlockSpec((1, tl, ATTN_W), lambda b, j: (b, j, 0)),
            pl.BlockSpec((1, tl, KV_W), lambda b, j: (b, j, 0)),
            pl.BlockSpec((1, tl, KV_W), lambda b, j: (b, j, 0)),
            pl.BlockSpec((1, tl, POOL_W), lambda b, j: (b, j, 0)),
            pl.BlockSpec((1, POOL_BUF, POOL_W), lambda b, j: (b, 0, 0)),
            pl.BlockSpec((1, WINDOW, KV_W), lambda b, j: (b, 0, 0)),
            pl.BlockSpec((1, WINDOW, KV_W), lambda b, j: (b, 0, 0)),
        ],
        out_shape=[
            jax.ShapeDtypeStruct((B, L, ATTN_W), bf16),
            jax.ShapeDtypeStruct((B, L, KV_W), bf16),
            jax.ShapeDtypeStruct((B, L, KV_W), bf16),
            jax.ShapeDtypeStruct((B, L, POOL_W), bf16),
            jax.ShapeDtypeStruct((B, POOL_BUF, POOL_W), f32),
            jax.ShapeDtypeStruct((B, WINDOW, KV_W), f32),
            jax.ShapeDtypeStruct((B, WINDOW, KV_W), f32),
        ],
        scratch_shapes=[pltpu.VMEM((2 * HIST + tl, POOL_W), f32), pltpu.VMEM((2 * HIST + tl, POOL_W), f32)],
        compiler_params=pltpu.CompilerParams(dimension_semantics=("arbitrary", "arbitrary")),
        name="inproj_prompt",
    )(x, mod3, mod3, g_attn, w_in, gqk, bd, w_pool, pool_scale)


def _inproj_sample_body(x_ref, sh_ref, sc_ref, g_ref, w_ref, gqk_ref, bd_ref, wp_ref, ps_ref, st_ref,
                        q_ref, k_ref, v_ref, po_ref, np_ref, ext, *, bt, ls, pos0):
    n = bt * ls
    h3 = _mod_norm(x_ref[...], g_ref[...][None], sc_ref[...], sh_ref[...])
    (up, q, k, v), = _qkv_from_h([h3.reshape(n, D_MODEL)], w_ref, gqk_ref, bd_ref)
    q_ref[...] = q.astype(bf16)
    k_ref[...] = k
    v_ref[...] = v

    ext[:, 1:HIST, :] = st_ref[...]
    ext[:, HIST:HIST + ls, :] = up.reshape(bt, ls, POOL_W)
    pos = pos0 + lax.broadcasted_iota(i32, (1, ls, 1), 1)
    for g, w in enumerate(POOL_WINDOWS):
        lanes = slice(g * POOL_GC, (g + 1) * POOL_GC)
        cur = ext[:, HIST:HIST + ls, lanes]
        acc = cur
        for s in range(1, w):
            acc = acc + ext[:, HIST - s:HIST - s + ls, lanes]
        cnt = jnp.minimum(w, pos + 1).astype(f32)
        d = (acc / cnt - cur).reshape(n, POOL_GC)
        yg = _dot(d.astype(bf16), wp_ref[g]) * ps_ref[:, lanes]
        po_ref[:, lanes] = yg.astype(bf16)
    np_ref[...] = ext[:, ls + 1:ls + HIST, :]


def _inproj_sample(x, mod3, g_attn, w_in, gqk, bd, w_pool, pool_scale, state, pos0, bt=64):
    B, ls, D = x.shape
    n = bt * ls
    full = lambda shape: pl.BlockSpec(shape, lambda i: (0,) * len(shape))
    return pl.pallas_call(
        functools.partial(_inproj_sample_body, bt=bt, ls=ls, pos0=pos0),
        grid=(B // bt,),
        in_specs=[
            pl.BlockSpec((bt, ls, D), lambda i: (i, 0, 0)),
            pl.BlockSpec((bt, 1, D), lambda i: (i, 0, 0)),
            pl.BlockSpec((bt, 1, D), lambda i: (i, 0, 1)),
            full((1, D)),
            full((D, QKV_W)),
            full((1, QK_W)),
            full((QK_W, QK_W)),
            full((4, POOL_GC, POOL_GC)),
            full((1, POOL_W)),
            pl.BlockSpec((bt, POOL_BUF, POOL_W), lambda i: (i, 0, 0)),
        ],
        out_specs=[
            pl.BlockSpec((n, ATTN_W), lambda i: (i, 0)),
            pl.BlockSpec((n, KV_W), lambda i: (i, 0)),
            pl.BlockSpec((n, KV_W), lambda i: (i, 0)),
            pl.BlockSpec((n, POOL_W), lambda i: (i, 0)),
            pl.BlockSpec((bt, POOL_BUF, POOL_W), lambda i: (i, 0, 0)),
        ],
        out_shape=[
            jax.ShapeDtypeStruct((B * ls, ATTN_W), bf16),
            jax.ShapeDtypeStruct((B * ls, KV_W), f32),
            jax.ShapeDtypeStruct((B * ls, KV_W), f32),
            jax.ShapeDtypeStruct((B * ls, POOL_W), bf16),
            jax.ShapeDtypeStruct((B, POOL_BUF, POOL_W), f32),
        ],
        scratch_shapes=[pltpu.VMEM((bt, HIST + ls, POOL_W), f32)],
        name="inproj_sample",
    )(x, mod3, mod3, g_attn, w_in, gqk, bd, w_pool, pool_scale, state)


def _softmax_sink(parts, sink):
    m = sink
    for s in parts:
        m = jnp.maximum(m, jnp.max(s, axis=-1, keepdims=True))
    ps = [jnp.exp(s - m) for s in parts]
    denom = jnp.exp(sink - m)
    for p in ps:
        denom = denom + jnp.sum(p, axis=-1, keepdims=True)
    inv = 1.0 / denom
    return [(p * inv).astype(bf16) for p in ps]


def _attn_prompt_body(sinks_ref, q_ref, kp_ref, kc_ref, vp_ref, vc_ref, bias_ref, mask_ref, o_ref):
    j = pl.program_id(1)
    nk = 2 * WINDOW
    qw = GQA * HEAD_DIM
    nq = q_ref.shape[1] // WINDOW
    kall = jnp.concatenate([kp_ref[0], kc_ref[0]], axis=0)
    vall = jnp.concatenate([vp_ref[0], vc_ref[0]], axis=0)
    lane_group = lax.broadcasted_iota(i32, (nk, 2 * KV_W), 1) // HEAD_DIM
    first_has_prev = jnp.minimum(j, 1)

    def spread(t):
        t0 = jnp.concatenate([t, t], axis=1)
        return t0, pltpu.roll(t0, HEAD_DIM, 1)

    def blockdiag(t01, kv):
        t0, t1 = t01
        return jnp.concatenate(
            [jnp.where(lane_group == g, t0 if g % 2 == kv else t1, jnp.zeros_like(t0)) for g in range(GQA)],
            axis=0)

    chains = [(qb, kv) for qb in range(nq) for kv in range(N_KV_HEADS)]
    ksp = [spread(kall[qb * WINDOW:qb * WINDOW + nk]) for qb in range(nq)]
    vsp = [spread(vall[qb * WINDOW:qb * WINDOW + nk]) for qb in range(nq)]
    valid = [mask_ref[first_has_prev] > 0.5] + [mask_ref[1] > 0.5] * (nq - 1)
    s = [_dot_t(q_ref[0, qb * WINDOW:(qb + 1) * WINDOW, kv * qw:(kv + 1) * qw], blockdiag(ksp[qb], kv))
         for qb, kv in chains]
    s = [jnp.where(valid[qb], s[c] + bias_ref[kv], NEG_INF) for c, (qb, kv) in enumerate(chains)]
    p = [jnp.concatenate([_softmax_sink([s[c][:, g * nk:(g + 1) * nk]], sinks_ref[kv * GQA + g])[0]
                          for g in range(GQA)], axis=1) for c, (qb, kv) in enumerate(chains)]
    o = [_dot(p[c], blockdiag(vsp[qb], kv)) for c, (qb, kv) in enumerate(chains)]
    for c, (qb, kv) in enumerate(chains):
        o_ref[0, qb * WINDOW:(qb + 1) * WINDOW, kv * qw:(kv + 1) * qw] = o[c].astype(bf16)


def _attn_prompt(q, k, v, bias, sinks):
    B, L, _ = q.shape
    nq = 4
    nb = L // (nq * WINDOW)
    cur = lambda b, j: (b, j, 0)
    prev = lambda b, j: (b, jnp.maximum(nq * j - 1, 0), 0)
    qi = np.arange(WINDOW)[:, None]
    kc = np.arange(2 * WINDOW)[None, :]
    own = (kc >= WINDOW) & (kc - WINDOW <= qi)
    prv = (kc < WINDOW) & (kc > qi)
    mask = np.stack([np.tile(own, (1, GQA)), np.tile(own | prv, (1, GQA))]).astype(np.float32)
    return pl.pallas_call(
        _attn_prompt_body,
        grid=(B, nb),
        in_specs=[
            pl.BlockSpec(memory_space=pltpu.SMEM),
            pl.BlockSpec((1, nq * WINDOW, ATTN_W), cur),
            pl.BlockSpec((1, WINDOW, KV_W), prev),
            pl.BlockSpec((1, nq * WINDOW, KV_W), cur),
            pl.BlockSpec((1, WINDOW, KV_W), prev),
            pl.BlockSpec((1, nq * WINDOW, KV_W), cur),
            pl.BlockSpec((N_KV_HEADS, WINDOW, GQA * 2 * WINDOW), lambda b, j: (0, 0, 0)),
            pl.BlockSpec((2, WINDOW, GQA * 2 * WINDOW), lambda b, j: (0, 0, 0)),
        ],
        out_specs=pl.BlockSpec((1, nq * WINDOW, ATTN_W), cur),
        out_shape=jax.ShapeDtypeStruct((B, L, ATTN_W), bf16),
        name="attn_prompt",
    )(sinks, q, k, k, v, v, bias, jnp.asarray(mask))


def _attn_sample_body(q_ref, kb_ref, vb_ref, kn_ref, vn_ref, bb_ref, bn_ref, sink_ref,
                      o_ref, nk_ref, nv_ref, *, bb, ls):
    W = kb_ref.shape[1]
    rows = GQA * ls
    qi = lax.broadcasted_iota(i32, (rows, W), 0) % ls
    kj = lax.broadcasted_iota(i32, (rows, W), 1)
    valid_buf = kj > qi
    qi2 = lax.broadcasted_iota(i32, (rows, ls), 0) % ls
    kj2 = lax.broadcasted_iota(i32, (rows, ls), 1)
    valid_new = kj2 <= qi2

    nbat = 4
    ks = [slice(kv * HEAD_DIM, (kv + 1) * HEAD_DIM) for kv in range(N_KV_HEADS)]

    def group(i, carry):
        bs = [i * nbat + u for u in range(nbat)]
        chains = [(u, kv) for u in range(nbat) for kv in range(N_KV_HEADS)]
        qb = [q_ref[b] for b in bs]
        kbuf = [kb_ref[b] for b in bs]
        vbuf = [vb_ref[b] for b in bs]
        knew = [kn_ref[b] for b in bs]
        vnew = [vn_ref[b] for b in bs]
        qg = [jnp.concatenate([qb[u][:, (kv * GQA + g) * HEAD_DIM:(kv * GQA + g + 1) * HEAD_DIM]
                               for g in range(GQA)], axis=0) for u, kv in chains]
        s_buf = [_dot_t(qg[c], kbuf[u][:, ks[kv]].astype(bf16)) for c, (u, kv) in enumerate(chains)]
        s_new = [_dot_t(qg[c], knew[u][:, ks[kv]].astype(bf16)) for c, (u, kv) in enumerate(chains)]
        s_buf = [jnp.where(valid_buf, s_buf[c] + bb_ref[kv], NEG_INF) for c, (u, kv) in enumerate(chains)]
        s_new = [jnp.where(valid_new, s_new[c] + bn_ref[kv], NEG_INF) for c, (u, kv) in enumerate(chains)]
        probs = [_softmax_sink([s_buf[c], s_new[c]], sink_ref[kv]) for c, (u, kv) in enumerate(chains)]
        outs = [_dot(probs[c][0], vbuf[u][:, ks[kv]].astype(bf16)) + _dot(probs[c][1], vnew[u][:, ks[kv]].astype(bf16))
                for c, (u, kv) in enumerate(chains)]
        for u, b in enumerate(bs):
            heads = [outs[u * N_KV_HEADS + kv][g * ls:(g + 1) * ls] for kv in range(N_KV_HEADS) for g in range(GQA)]
            o_ref[b] = jnp.concatenate(heads, axis=-1).astype(bf16)
            nk_ref[b, 0:W - ls, :] = kbuf[u][ls:, :]
            nk_ref[b, W - ls:W, :] = knew[u]
            nv_ref[b, 0:W - ls, :] = vbuf[u][ls:, :]
            nv_ref[b, W - ls:W, :] = vnew[u]
        return carry

    lax.fori_loop(0, bb // nbat, group, 0)


def _attn_sample(q, k_buf, v_buf, k_new, v_new, bias_buf, bias_new, sink_col, bb=16):
    B, ls, _ = q.shape
    W = k_buf.shape[1]
    rows = GQA * ls
    blk = lambda shape: pl.BlockSpec(shape, lambda i: (i, 0, 0))
    full = lambda shape: pl.BlockSpec(shape, lambda i: (0, 0, 0))
    return pl.pallas_call(
        functools.partial(_attn_sample_body, bb=bb, ls=ls),
        grid=(B // bb,),
        in_specs=[
            blk((bb, ls, ATTN_W)),
            blk((bb, W, KV_W)),
            blk((bb, W, KV_W)),
            blk((bb, ls, KV_W)),
            blk((bb, ls, KV_W)),
            full((N_KV_HEADS, rows, W)),
            full((N_KV_HEADS, rows, ls)),
            full((N_KV_HEADS, rows, 1)),
        ],
        out_specs=[blk((bb, ls, ATTN_W)), blk((bb, W, KV_W)), blk((bb, W, KV_W))],
        out_shape=[
            jax.ShapeDtypeStruct((B, ls, ATTN_W), bf16),
            jax.ShapeDtypeStruct((B, W, KV_W), f32),
            jax.ShapeDtypeStruct((B, W, KV_W), f32),
        ],
        name="attn_sample",
    )(q, k_buf, v_buf, k_new, v_new, bias_buf, bias_new, sink_col)


def _outproj_core(po, at, x, gt, sc, sh, g_ref, wo_ref, wrh_ref, wrl_ref):
    mixo = _dot(po, wo_ref[0:POOL_W, :]) + _dot(at, wo_ref[POOL_W:, :])
    x1 = x + gt * mixo.reshape(x.shape)
    h2 = _mod_norm(x1, g_ref[...].reshape((1,) * (x.ndim - 1) + (D_MODEL,)), sc, sh).reshape(-1, D_MODEL)
    h_hi, h_lo = _split_bf16(h2)
    wh = wrh_ref[...]
    logits = _dot_t(wh, h_hi) + (_dot_t(wh, h_lo) + _dot_t(wrl_ref[...], h_hi))
    return x1, _pack_pairs(h2), logits


def _outproj_prompt_body(po_ref, at_ref, x_ref, gt_ref, sc_ref, sh_ref, g_ref, wo_ref, wrh_ref, wrl_ref,
                         x1_ref, h2_ref, lg_ref):
    x1, h2p, logits = _outproj_core(po_ref[0], at_ref[0], x_ref[0], gt_ref[0], sc_ref[0], sh_ref[0],
                                    g_ref, wo_ref, wrh_ref, wrl_ref)
    x1_ref[...] = x1
    h2_ref[...] = h2p
    lg_ref[...] = logits


def _outproj_sample_body(po_ref, at_ref, x_ref, gt_ref, sc_ref, sh_ref, g_ref, wo_ref, wrh_ref, wrl_ref,
                         x1_ref, h2_ref, lg_ref):
    x1, h2p, logits = _outproj_core(po_ref[...], at_ref[...], x_ref[...], gt_ref[...], sc_ref[...], sh_ref[...],
                                    g_ref, wo_ref, wrh_ref, wrl_ref)
    x1_ref[...] = x1.reshape(-1, D_MODEL)
    h2_ref[...] = h2p
    lg_ref[...] = logits


def _outproj_prompt(po, at, x, mod3, g_ffn, w_out, wr_hi, wr_lo, tm=512):
    B, L, D = x.shape
    nt = L // tm
    n_tok = B * L
    full = lambda shape: pl.BlockSpec(shape, lambda b, j: (0,) * len(shape))
    modspec = lambda c: pl.BlockSpec((1, 1, D), lambda b, j: (b, 0, c))
    return pl.pallas_call(
        _outproj_prompt_body,
        grid=(B, nt),
        in_specs=[
            pl.BlockSpec((1, tm, POOL_W), lambda b, j: (b, j, 0)),
            pl.BlockSpec((1, tm, ATTN_W), lambda b, j: (b, j, 0)),
            pl.BlockSpec((1, tm, D), lambda b, j: (b, j, 0)),
            modspec(2), modspec(4), modspec(3),
            full((1, D)), full((D, D)), full((N_EXPERTS, D)), full((N_EXPERTS, D)),
        ],
        out_specs=[
            pl.BlockSpec((tm, D), lambda b, j: (b * nt + j, 0)),
            pl.BlockSpec((tm, D // 2), lambda b, j: (b * nt + j, 0)),
            pl.BlockSpec((N_EXPERTS, tm), lambda b, j: (0, b * nt + j)),
        ],
        out_shape=[
            jax.ShapeDtypeStruct((n_tok, D), f32),
            jax.ShapeDtypeStruct((n_tok, D // 2), i32),
            jax.ShapeDtypeStruct((N_EXPERTS, n_tok), f32),
        ],
        name="outproj_prompt",
    )(po, at, x, mod3, mod3, mod3, g_ffn, w_out, wr_hi, wr_lo)


def _outproj_sample(po, at, x, mod3, g_ffn, w_out, wr_hi, wr_lo, bt=64):
    B, ls, D = x.shape
    n = bt * ls
    full = lambda shape: pl.BlockSpec(shape, lambda i: (0,) * len(shape))
    modspec = lambda c: pl.BlockSpec((bt, 1, D), lambda i: (i, 0, c))
    return pl.pallas_call(
        _outproj_sample_body,
        grid=(B // bt,),
        in_specs=[
            pl.BlockSpec((n, POOL_W), lambda i: (i, 0)),
            pl.BlockSpec((n, ATTN_W), lambda i: (i, 0)),
            pl.BlockSpec((bt, ls, D), lambda i: (i, 0, 0)),
            modspec(2), modspec(4), modspec(3),
            full((1, D)), full((D, D)), full((N_EXPERTS, D)), full((N_EXPERTS, D)),
        ],
        out_specs=[
            pl.BlockSpec((n, D), lambda i: (i, 0)),
            pl.BlockSpec((n, D // 2), lambda i: (i, 0)),
            pl.BlockSpec((N_EXPERTS, n), lambda i: (0, i)),
        ],
        out_shape=[
            jax.ShapeDtypeStruct((B * ls, D), f32),
            jax.ShapeDtypeStruct((B * ls, D // 2), i32),
            jax.ShapeDtypeStruct((N_EXPERTS, B * ls), f32),
        ],
        name="outproj_sample",
    )(po, at, x, mod3, mod3, mod3, g_ffn, w_out, wr_hi, wr_lo)


def _route_body(lga_ref, lgb_ref, rb_ref, tri_ref, idx_ref, rank_ref, gate_ref, cnt_ref, carry, *, tr, nsteps, na):
    step = pl.program_id(0)

    @pl.when(step == 0)
    def _():
        carry[...] = jnp.zeros(carry.shape, f32)

    logits = jnp.where(step < na, lga_ref[...], lgb_ref[...])
    s = jax.nn.sigmoid(logits)
    sb = s + rb_ref[...]
    e_iota = lax.broadcasted_iota(i32, (N_EXPERTS, tr), 0)
    g_iota = lax.broadcasted_iota(i32, (GROUP_SIZE, tr), 0)

    gscore = []
    for g in range(N_EXPERT_GROUPS):
        v = sb[g * GROUP_SIZE:(g + 1) * GROUP_SIZE]
        m1 = jnp.max(v, axis=0, keepdims=True)
        i1 = jnp.min(jnp.where(v == m1, g_iota, GROUP_SIZE), axis=0, keepdims=True)
        m2 = jnp.max(jnp.where(g_iota == i1, -jnp.inf, v), axis=0, keepdims=True)
        gscore.append(m1 + m2)
    parts = []
    for g in range(N_EXPERT_GROUPS):
        beaten = jnp.zeros((1, tr), i32)
        for g2 in range(N_EXPERT_GROUPS):
            if g2 == g:
                continue
            ahead = gscore[g2] > gscore[g]
            if g2 < g:
                ahead = ahead | (gscore[g2] == gscore[g])
            beaten = beaten + ahead.astype(i32)
        keep = beaten < TOPK_GROUPS
        parts.append(jnp.where(keep, sb[g * GROUP_SIZE:(g + 1) * GROUP_SIZE], NEG_INF))
    cur = jnp.concatenate(parts, axis=0)

    idxs, svals = [], []
    for _ in range(TOP_K):
        m = jnp.max(cur, axis=0, keepdims=True)
        ik = jnp.min(jnp.where(cur == m, e_iota, N_EXPERTS), axis=0, keepdims=True)
        hit = e_iota == ik
        svals.append(jnp.sum(jnp.where(hit, s, 0.0), axis=0, keepdims=True))
        cur = jnp.where(hit, -jnp.inf, cur)
        idxs.append(ik)
    sel = (cur == -jnp.inf).astype(f32)
    ssum = svals[0]
    for sv in svals[1:]:
        ssum = ssum + sv
    gate_ref[...] = jnp.concatenate([sv / ssum * ROUTED_SCALE for sv in svals], axis=0)
    idx_ref[...] = jnp.concatenate(idxs, axis=0)

    before = carry[...] + _dot(sel.astype(bf16), tri_ref[...])
    ranks = [jnp.sum(jnp.where(e_iota == ik, before, 0.0), axis=0, keepdims=True) for ik in idxs]
    rank_ref[...] = jnp.concatenate(ranks, axis=0).astype(i32)
    carry[...] = carry[...] + jnp.sum(sel, axis=1, keepdims=True)

    @pl.when(step == nsteps - 1)
    def _():
        cnt_ref[...] = carry[...]


def _route(logits_a, logits_b, router_bias, tr=512):
    E, Ta = logits_a.shape
    T = Ta + logits_b.shape[1]
    na = Ta // tr
    nsteps = T // tr
    tri = jnp.asarray(np.triu(np.ones((tr, tr), np.float32), 1), bf16)
    return pl.pallas_call(
        functools.partial(_route_body, tr=tr, nsteps=nsteps, na=na),
        grid=(nsteps,),
        in_specs=[
            pl.BlockSpec((E, tr), lambda i: (0, jnp.minimum(i, na - 1))),
            pl.BlockSpec((E, tr), lambda i: (0, jnp.maximum(i - na, 0))),
            pl.BlockSpec((E, 1), lambda i: (0, 0)),
            pl.BlockSpec((tr, tr), lambda i: (0, 0)),
        ],
        out_specs=[
            pl.BlockSpec((TOP_K, tr), lambda i: (0, i)),
            pl.BlockSpec((TOP_K, tr), lambda i: (0, i)),
            pl.BlockSpec((TOP_K, tr), lambda i: (0, i)),
            pl.BlockSpec((E, 1), lambda i: (0, 0)),
        ],
        out_shape=[
            jax.ShapeDtypeStruct((TOP_K, T), i32),
            jax.ShapeDtypeStruct((TOP_K, T), i32),
            jax.ShapeDtypeStruct((TOP_K, T), f32),
            jax.ShapeDtypeStruct((E, 1), f32),
        ],
        scratch_shapes=[pltpu.VMEM((E, 1), f32)],
        compiler_params=pltpu.CompilerParams(dimension_semantics=("arbitrary",)),
        name="route",
    )(logits_a, logits_b, router_bias.reshape(E, 1), tri)


def _dest_body(idx_ref, rank_ref, ps_ref, dest_ref, *, tr):
    e_iota = lax.broadcasted_iota(i32, (N_EXPERTS, tr), 0)
    start = ps_ref[...]
    rows = []
    for k in range(TOP_K):
        hit = e_iota == idx_ref[k:k + 1, :]
        rows.append(jnp.sum(jnp.where(hit, start, 0.0), axis=0, keepdims=True))
    dest_ref[...] = jnp.concatenate(rows, axis=0).astype(i32) + rank_ref[...]


def _dest_rows(idx, rank, pad_start, tr=512):
    K, T = idx.shape
    blk = pl.BlockSpec((K, tr), lambda i: (0, i))
    return pl.pallas_call(
        functools.partial(_dest_body, tr=tr),
        grid=(T // tr,),
        in_specs=[blk, blk, pl.BlockSpec((N_EXPERTS, 1), lambda i: (0, 0))],
        out_specs=blk,
        out_shape=jax.ShapeDtypeStruct((K, T), i32),
        name="dest_rows",
    )(idx, rank, pad_start.astype(f32).reshape(N_EXPERTS, 1))


def _sc_mesh():
    return plsc.VectorSubcoreMesh(core_axis_name="c", subcore_axis_name="s")


def _sc_worker_id():
    return lax.axis_index("s") * 2 + lax.axis_index("c")


def _dispatch(h2_a, h2_b, dest, gate, n_rows, chunk=32):
    ta, Dw = h2_a.shape
    T = ta + h2_b.shape[0]
    per_worker = T // SC_WORKERS
    nchunk = per_worker // chunk
    assert per_worker * SC_WORKERS == T and nchunk * chunk == per_worker and ta % chunk == 0

    nrow_idx = nchunk * TOP_K

    @functools.partial(
        pl.kernel, mesh=_sc_mesh(),
        out_type=[jax.ShapeDtypeStruct((n_rows, Dw), i32), jax.ShapeDtypeStruct((n_rows, GATE_ROW), f32)],
        scratch_types=[pltpu.VMEM((nrow_idx, chunk), i32), pltpu.VMEM((nrow_idx, chunk), f32),
                       pltpu.VMEM((chunk, Dw), i32), pltpu.VMEM((TOP_K, chunk, GATE_ROW), f32),
                       pltpu.SemaphoreType.DMA],
        compiler_params=pltpu.CompilerParams(needs_layout_passes=False),
        name="moe_dispatch",
    )
    def body(ha_hbm, hb_hbm, dest_hbm, gate_hbm, xs_hbm, gs_hbm, idx_v, gate_v, rows_v, grow_v, sem):
        wid = _sc_worker_id()
        base = wid * per_worker
        pltpu.sync_copy(dest_hbm.at[wid], idx_v)
        pltpu.sync_copy(gate_hbm.at[wid], gate_v)
        zero = jnp.zeros((SC_LANES,), f32)
        for k in range(TOP_K):
            @pl.loop(0, chunk)
            def _(t):
                for j in range(GATE_ROW // SC_LANES):
                    grow_v[k, t, pl.ds(j * SC_LANES, SC_LANES)] = zero

        @pl.loop(0, nchunk)
        def _(ci):
            t0 = base + ci * chunk

            @pl.when(t0 < ta)
            def _():
                pltpu.sync_copy(ha_hbm.at[pl.ds(t0, chunk)], rows_v)

            @pl.when(t0 >= ta)
            def _():
                pltpu.sync_copy(hb_hbm.at[pl.ds(t0 - ta, chunk)], rows_v)

            for k in range(TOP_K):
                @plsc.parallel_loop(0, chunk, unroll=4)
                def _(t):
                    row = jnp.zeros((SC_LANES,), i32) + (ci * TOP_K + k)
                    grow_v[k, t, pl.ds(0, SC_LANES)] = plsc.load_gather(
                        gate_v, [row, jnp.zeros((SC_LANES,), i32) + t])

            cps = []
            for k in range(TOP_K):
                idx = idx_v.at[ci * TOP_K + k]
                cps.append(pltpu.make_async_copy(rows_v, xs_hbm.at[idx], sem))
                cps.append(pltpu.make_async_copy(grow_v.at[k], gs_hbm.at[idx], sem))
            for cp in cps:
                cp.start()
            for cp in cps:
                cp.wait()

    def per_worker_rows(a):
        return a.reshape(TOP_K, SC_WORKERS, nchunk, chunk).transpose(1, 2, 0, 3).reshape(
            SC_WORKERS, nrow_idx, chunk)

    return body(h2_a, h2_b, per_worker_rows(dest), per_worker_rows(gate))


def _combine(ys, dest, chunk=8):
    T = dest.shape[1]
    Dw = ys.shape[1]
    per_worker = T // SC_WORKERS
    nchunk = per_worker // chunk
    assert per_worker * SC_WORKERS == T and nchunk * chunk == per_worker and nchunk % 2 == 0

    @functools.partial(
        pl.kernel, mesh=_sc_mesh(),
        out_type=jax.ShapeDtypeStruct((T, 2 * Dw), f32),
        scratch_types=[
            pltpu.VMEM((TOP_K * per_worker,), i32),
            pltpu.VMEM((2, TOP_K, chunk, Dw), i32),
            pltpu.VMEM((chunk, 2 * Dw), f32),
            pltpu.SemaphoreType.DMA((2,)),
        ],
        compiler_params=pltpu.CompilerParams(needs_layout_passes=False),
        name="moe_combine",
    )
    def body(ys_hbm, dest_hbm, out_hbm, idx_v, buf, out_v, sems):
        base = _sc_worker_id() * per_worker
        pltpu.sync_copy(dest_hbm.at[pl.ds(_sc_worker_id() * (TOP_K * per_worker), TOP_K * per_worker)], idx_v)

        def gather(ci, slot):
            return [pltpu.make_async_copy(ys_hbm.at[idx_v.at[pl.ds(k * per_worker + ci * chunk, chunk)]],
                                          buf.at[slot, k], sems.at[slot]) for k in range(TOP_K)]

        for cp in gather(0, 0):
            cp.start()

        @pl.loop(0, nchunk, step=2)
        def _(c0):
            for slot in range(2):
                ci = c0 + slot

                @pl.when(ci + 1 < nchunk)
                def _():
                    for cp in gather(ci + 1, 1 - slot):
                        cp.start()

                for cp in gather(ci, slot):
                    cp.wait()

                @pl.loop(0, chunk)
                def _(t):
                    @plsc.parallel_loop(0, Dw // SC_LANES, unroll=8)
                    def _(j):
                        sl = pl.ds(j * SC_LANES, SC_LANES)
                        w = buf[slot, 0, t, sl]
                        hi = plsc.bitcast(w & HI_MASK, f32)
                        lo = plsc.bitcast(lax.shift_left(w, 16), f32)
                        for k in range(1, TOP_K):
                            w = buf[slot, k, t, sl]
                            hi = hi + plsc.bitcast(w & HI_MASK, f32)
                            lo = lo + plsc.bitcast(lax.shift_left(w, 16), f32)
                        out_v[t, sl] = hi
                        out_v[t, pl.ds(Dw + j * SC_LANES, SC_LANES)] = lo

                pltpu.sync_copy(out_v, out_hbm.at[pl.ds(base + ci * chunk, chunk)])

    dest_w = dest.reshape(TOP_K, SC_WORKERS, per_worker).transpose(1, 0, 2).reshape(-1)
    return body(ys, dest_w)


def _gmm_body(blk_e_ref, blk_rows_ref, blk_next_ref, nv_ref, xs_hbm, gs_hbm, w1_hbm, w3_hbm, w2_hbm, ys_hbm,
              xbuf, gbuf, ybuf, w1f, w3f, w2f, xsem, gsem, ysem, wsem):
    nv = nv_ref[0]
    nb = blk_e_ref.shape[0]
    half = D_MODEL // 2
    RB = EXPERT_BLOCK
    ahead_w = W_RING - 2

    def expert_of(blk):
        return blk_e_ref[jnp.minimum(blk, nb - 1)]

    def next_expert_block(blk):
        return jnp.where(blk < nv, blk_next_ref[jnp.minimum(blk, nb - 1)], blk)

    def start_weights(blk, ordinal):
        @pl.when(blk < nv)
        def _():
            for cp in weight_copies(expert_of(blk), lax.rem(ordinal, W_RING)):
                cp.start()

    def row_copies(b, slot):
        r0 = pl.multiple_of(b * RB, RB)
        return (pltpu.make_async_copy(xs_hbm.at[pl.ds(r0, RB)], xbuf.at[slot], xsem.at[slot]),
                pltpu.make_async_copy(gs_hbm.at[pl.ds(r0, RB)], gbuf.at[slot], gsem.at[slot]))

    def out_copy(b, slot):
        r0 = pl.multiple_of(b * RB, RB)
        return pltpu.make_async_copy(ybuf.at[slot], ys_hbm.at[pl.ds(r0, RB)], ysem.at[slot])

    def weight_copies(e, ws):
        return (pltpu.make_async_copy(w1_hbm.at[e], w1f.at[ws], wsem.at[ws, 0]),
                pltpu.make_async_copy(w3_hbm.at[e], w3f.at[ws], wsem.at[ws, 1]),
                pltpu.make_async_copy(w2_hbm.at[e], w2f.at[ws], wsem.at[ws, 2]))

    blk = jnp.int32(0)
    for n in range(ahead_w):
        start_weights(blk, n)
        blk = next_expert_block(blk)
    for i in range(ROW_RING - 2):
        @pl.when(i < nv)
        def _():
            for cp in row_copies(i, i):
                cp.start()

    def enter(b, live, ordinal_prev):
        e = expert_of(b)
        first = live & ((b == 0) | (e != expert_of(jnp.maximum(b - 1, 0))))
        ordinal = jnp.where(first & (b > 0), ordinal_prev + 1, ordinal_prev)

        @pl.when(first)
        def _():
            for cp in weight_copies(e, lax.rem(ordinal, W_RING)):
                cp.wait()
            nxt = b
            for _ in range(ahead_w):
                nxt = next_expert_block(nxt)
            start_weights(nxt, ordinal + ahead_w)

        return ordinal

    def load_block(b, slot, ws):
        valid = lax.broadcasted_iota(i32, (RB, 1), 0) < blk_rows_ref[jnp.minimum(b, nb - 1)]
        x_hi, x_lo = _unpack_pairs(jnp.where(valid, xbuf[slot], 0))
        g = jnp.where(valid, gbuf[slot][:, 0:1], 0.0)
        return x_hi, x_lo, g, ws

    def pair(p, ordinal_prev):
        b0 = 2 * p
        b1 = b0 + 1
        live1 = b1 < nv
        slot0 = lax.rem(b0, ROW_RING)
        slots = (slot0, jnp.where(live1, lax.rem(b1, ROW_RING), slot0))

        for b in (b0 + ROW_RING - 2, b1 + ROW_RING - 2):
            @pl.when(b < nv)
            def _():
                for cp in row_copies(b, lax.rem(b, ROW_RING)):
                    cp.start()

        ord0 = enter(b0, b0 < nv, ordinal_prev)
        ord1 = enter(b1, live1, ord0)
        for cp in row_copies(b0, slots[0]):
            cp.wait()

        @pl.when(live1)
        def _():
            for cp in row_copies(b1, slots[1]):
                cp.wait()

        blocks = [load_block(b0, slots[0], lax.rem(ord0, W_RING)),
                  load_block(jnp.where(live1, b1, b0), slots[1], lax.rem(ord1, W_RING))]
        a = [_dot(xh, w1f[ws, 0:half, :].astype(bf16)) + _dot(xl, w1f[ws, half:, :].astype(bf16))
             for xh, xl, _, ws in blocks]
        c = [_dot(xh, w3f[ws, 0:half, :].astype(bf16)) + _dot(xl, w3f[ws, half:, :].astype(bf16))
             for xh, xl, _, ws in blocks]
        hmid = [((a[i] * jax.nn.sigmoid(a[i])) * c[i]).astype(bf16) for i in range(2)]
        y = [_pack_pairs(_dot(hmid[i], w2f[blocks[i][3]].astype(bf16)) * blocks[i][2]) for i in range(2)]

        for i, (b, live) in enumerate(((b0, b0 < nv), (b1, live1))):
            @pl.when(live & (b >= ROW_RING))
            def _():
                out_copy(b - ROW_RING, slots[i]).wait()

            @pl.when(live)
            def _():
                ybuf[slots[i]] = y[i]
                out_copy(b, slots[i]).start()

        return ord1

    lax.fori_loop(0, (nv + 1) // 2, pair, 0)

    for i in range(1, ROW_RING + 1):
        @pl.when(nv >= i)
        def _():
            out_copy(nv - i, lax.rem(nv - i, ROW_RING)).wait()


def _gmm(xs, gs, w1, w3, w2, blk_e, blk_rows, blk_next, n_valid):
    n_rows, Dw = xs.shape
    D = 2 * Dw
    RB = EXPERT_BLOCK
    hbm = pl.BlockSpec(memory_space=pl.ANY)
    return pl.pallas_call(
        _gmm_body,
        grid_spec=pltpu.PrefetchScalarGridSpec(
            num_scalar_prefetch=4,
            grid=(1,),
            in_specs=[hbm, hbm, hbm, hbm, hbm],
            out_specs=hbm,
            scratch_shapes=[
                pltpu.VMEM((ROW_RING, RB, Dw), i32), pltpu.VMEM((ROW_RING, RB, GATE_ROW), f32),
                pltpu.VMEM((ROW_RING, RB, Dw), i32),
                pltpu.VMEM((W_RING, D, EXPERT_FF), f32), pltpu.VMEM((W_RING, D, EXPERT_FF), f32),
                pltpu.VMEM((W_RING, EXPERT_FF, D), f32),
                pltpu.SemaphoreType.DMA((ROW_RING,)), pltpu.SemaphoreType.DMA((ROW_RING,)),
                pltpu.SemaphoreType.DMA((ROW_RING,)), pltpu.SemaphoreType.DMA((W_RING, 3)),
            ],
        ),
        out_shape=jax.ShapeDtypeStruct((n_rows, Dw), i32),
        compiler_params=pltpu.CompilerParams(dimension_semantics=("arbitrary",)),
        name="moe_gmm",
    )(blk_e, blk_rows, blk_next, n_valid, xs, gs, w1, w3, w2)


def _final_core(x1, h2p, comb, gt, ws1_ref, ws3_ref, ws2_ref):
    half = D_MODEL // 2
    h_hi, h_lo = _unpack_pairs(h2p)
    a = _dot(h_hi, ws1_ref[0:half, :]) + _dot(h_lo, ws1_ref[half:, :])
    c = _dot(h_hi, ws3_ref[0:half, :]) + _dot(h_lo, ws3_ref[half:, :])
    shared = _dot(((a * jax.nn.sigmoid(a)) * c).astype(bf16), ws2_ref[...])
    return x1, comb + shared, gt


def _final_prompt_body(x1_ref, h2_ref, cb_ref, gt_ref, ws1_ref, ws3_ref, ws2_ref, y_ref):
    x1, ffn, gt = _final_core(x1_ref[...], h2_ref[...], cb_ref[...], gt_ref[0], ws1_ref, ws3_ref, ws2_ref)
    y_ref[0] = x1 + gt * ffn


def _final_sample_body(x1_ref, h2_ref, cb_ref, gt_ref, ws1_ref, ws3_ref, ws2_ref, y_ref):
    x1, ffn, gt = _final_core(x1_ref[...], h2_ref[...], cb_ref[...], gt_ref[...], ws1_ref, ws3_ref, ws2_ref)
    shp = y_ref.shape
    y_ref[...] = x1.reshape(shp) + gt * ffn.reshape(shp)


def _final_prompt(x1, h2, comb, mod3, ws1, ws3, ws2, B, L, tm=512):
    D = D_MODEL
    nt = L // tm
    full = lambda shape: pl.BlockSpec(shape, lambda b, j: (0,) * len(shape))
    rows = pl.BlockSpec((tm, D), lambda b, j: (b * nt + j, 0))
    words = pl.BlockSpec((tm, D // 2), lambda b, j: (b * nt + j, 0))
    return pl.pallas_call(
        _final_prompt_body,
        grid=(B, nt),
        in_specs=[rows, words, rows, pl.BlockSpec((1, 1, D), lambda b, j: (b, 0, 5)),
                  full((D, EXPERT_FF)), full((D, EXPERT_FF)), full((EXPERT_FF, D))],
        out_specs=pl.BlockSpec((1, tm, D), lambda b, j: (b, j, 0)),
        out_shape=jax.ShapeDtypeStruct((B, L, D), f32),
        name="final_prompt",
    )(x1, h2, comb, mod3, ws1, ws3, ws2)


def _final_sample(x1, h2, comb, mod3, ws1, ws3, ws2, B, ls, row0, bt=64):
    D = D_MODEL
    n = bt * ls
    blk0 = row0 // n
    full = lambda shape: pl.BlockSpec(shape, lambda i: (0,) * len(shape))
    rows = pl.BlockSpec((n, D), lambda i: (i, 0))
    words = pl.BlockSpec((n, D // 2), lambda i: (i, 0))
    comb_rows = pl.BlockSpec((n, D), lambda i: (blk0 + i, 0))
    return pl.pallas_call(
        _final_sample_body,
        grid=(B // bt,),
        in_specs=[rows, words, comb_rows, pl.BlockSpec((bt, 1, D), lambda i: (i, 0, 5)),
                  full((D, EXPERT_FF)), full((D, EXPERT_FF)), full((EXPERT_FF, D))],
        out_specs=pl.BlockSpec((bt, ls, D), lambda i: (i, 0, 0)),
        out_shape=jax.ShapeDtypeStruct((B, ls, D), f32),
        name="final_sample",
    )(x1, h2, comb, mod3, ws1, ws3, ws2)


def kernel(x_prompt, x_sample, state_pool, cache_swa_k, cache_swa_v, c_prompt, c_sample, w_ada, b_ada,
           g_attn_norm, w_in, g_q, g_k, w_pool, pool_scale, w_out, attn_sinks, rel_bias, g_ffn_norm,
           w_router, router_bias, w1, w3, w2, ws1, ws3, ws2):
    B, L, D = x_prompt.shape
    BS, LS, _ = x_sample.shape
    depth = w_ada.shape[0]
    assert depth == 1
    W = cache_swa_k.shape[2]
    tp, ts = B * L, BS * LS
    T = tp + ts
    n_rows = (T * TOP_K // EXPERT_BLOCK + N_EXPERTS) * EXPERT_BLOCK
    nb = n_rows // EXPERT_BLOCK

    g_attn = g_attn_norm[0].reshape(1, D)
    g_ffn = g_ffn_norm[0].reshape(1, D)
    w_in_b = w_in[0].astype(bf16)
    w_out_b = w_out[0].astype(bf16)
    w_pool_b = w_pool[0].astype(bf16)
    ps = pool_scale[0].reshape(1, POOL_W)
    gqk = jnp.concatenate([jnp.tile(g_q[0], N_HEADS), jnp.tile(g_k[0], N_KV_HEADS)]).reshape(1, QK_W)
    head_of = np.arange(QK_W) // HEAD_DIM
    bd = jnp.asarray((head_of[:, None] == head_of[None, :]).astype(np.float32), bf16)
    wr_t = w_router[0].T
    wr_hi = wr_t.astype(bf16)
    wr_lo = (wr_t - wr_hi.astype(f32)).astype(bf16)
    ws1_b, ws3_b, ws2_b = ws1[0].astype(bf16), ws3[0].astype(bf16), ws2[0].astype(bf16)
    sinks = attn_sinks[0]

    mod = _ada(jnp.concatenate([c_prompt, c_sample], axis=0), w_ada[0], b_ada[0])
    mod_p = mod[:B].reshape(B, 1, 6 * D)
    mod_s = mod[B:].reshape(BS, 1, 6 * D)

    dist_p = np.arange(WINDOW)[:, None] + WINDOW - np.arange(2 * WINDOW)[None, :]
    bias_p = _relbias(rel_bias, dist_p)
    bias_p = bias_p.reshape(N_KV_HEADS, GQA, WINDOW, 2 * WINDOW).transpose(0, 2, 1, 3).reshape(
        N_KV_HEADS, WINDOW, GQA * 2 * WINDOW)
    dist_s = np.arange(LS)[:, None] + W - np.arange(W + LS)[None, :]
    bias_s = _relbias(rel_bias, dist_s)
    bias_s_buf = bias_s[:, :, :W].reshape(N_KV_HEADS, GQA * LS, W)
    bias_s_new = bias_s[:, :, W:].reshape(N_KV_HEADS, GQA * LS, LS)
    sink_col = jnp.repeat(sinks, LS).reshape(N_KV_HEADS, GQA * LS, 1)

    q_p, k_p, v_p, po_p, new_pool_p, kc_p, vc_p = _inproj_prompt(
        x_prompt, mod_p, g_attn, w_in_b, gqk, bd, w_pool_b, ps)
    q_s, k_s, v_s, po_s, new_pool_s = _inproj_sample(
        x_sample, mod_s, g_attn, w_in_b, gqk, bd, w_pool_b, ps, state_pool[0], PAST_LEN)
    at_p = _attn_prompt(q_p, k_p, v_p, bias_p, sinks)
    at_s, nk_s, nv_s = _attn_sample(
        q_s.reshape(BS, LS, ATTN_W), cache_swa_k[0].reshape(BS, W, KV_W), cache_swa_v[0].reshape(BS, W, KV_W),
        k_s.reshape(BS, LS, KV_W), v_s.reshape(BS, LS, KV_W), bias_s_buf, bias_s_new, sink_col)

    x1_p, h2_p, lg_p = _outproj_prompt(po_p, at_p, x_prompt, mod_p, g_ffn, w_out_b, wr_hi, wr_lo)
    x1_s, h2_s, lg_s = _outproj_sample(po_s, at_s.reshape(ts, ATTN_W), x_sample, mod_s, g_ffn, w_out_b,
                                       wr_hi, wr_lo)

    idx, rank, gate, counts = _route(lg_p, lg_s, router_bias[0])
    counts = counts.reshape(N_EXPERTS).astype(i32)
    padded = (counts + EXPERT_BLOCK - 1) // EXPERT_BLOCK * EXPERT_BLOCK
    pad_end = jnp.cumsum(padded)
    pad_start = pad_end - padded
    dest = _dest_rows(idx, rank, pad_start)
    n_valid = (pad_end[-1] // EXPERT_BLOCK).astype(i32).reshape(1)
    blk_row0 = jnp.arange(nb, dtype=i32) * EXPERT_BLOCK
    blk_e = jnp.minimum(jnp.sum(blk_row0[:, None] >= pad_end[None, :], axis=1), N_EXPERTS - 1).astype(i32)
    own = jnp.arange(N_EXPERTS, dtype=i32)[None, :] == blk_e[:, None]
    blk_cnt = jnp.sum(jnp.where(own, counts[None, :], 0), axis=1)
    blk_start = jnp.sum(jnp.where(own, pad_start[None, :], 0), axis=1)
    blk_rows = jnp.clip(blk_cnt - (blk_row0 - blk_start), 0, EXPERT_BLOCK).astype(i32)
    blk_next = (jnp.sum(jnp.where(own, pad_end[None, :], 0), axis=1) // EXPERT_BLOCK).astype(i32)

    xs, gs = _dispatch(h2_p, h2_s, dest, gate, n_rows)
    ys = _gmm(xs, gs, w1[0], w3[0], w2[0], blk_e, blk_rows, blk_next, n_valid)
    comb = _combine(ys, dest)
    y_p = _final_prompt(x1_p, h2_p, comb, mod_p, ws1_b, ws3_b, ws2_b, B, L)
    y_s = _final_sample(x1_s, h2_s, comb, mod_s, ws1_b, ws3_b, ws2_b, BS, LS, tp)

    return (y_p, y_s, new_pool_p[None], kc_p.reshape(1, B, WINDOW, N_KV_HEADS, HEAD_DIM),
            vc_p.reshape(1, B, WINDOW, N_KV_HEADS, HEAD_DIM), new_pool_s[None],
            nk_s.reshape(1, BS, W, N_KV_HEADS, HEAD_DIM), nv_s.reshape(1, BS, W, N_KV_HEADS, HEAD_DIM))
```

```python
import functools
import math

import numpy as np
import jax
import jax.numpy as jnp
from jax import lax
from jax.experimental import pallas as pl
from jax.experimental.pallas import tpu as pltpu
from jax.experimental.pallas import tpu_sc as plsc

f32 = jnp.float32
bf16 = jnp.bfloat16
i32 = jnp.int32

D_MODEL = 1024
PAST_LEN = 8192
POOL_W = 512
POOL_WINDOWS = (2, 4, 8, 16)
POOL_GC = 128
POOL_BUF = 15
ATTN_W = 512
HEAD_DIM = 64
N_HEADS = 8
N_KV_HEADS = 2
GQA = 4
KV_W = 128
WINDOW = 128
NUM_BUCKETS = 32
MAX_EXACT = 16
REL_MAX_DIST = 128
N_EXPERTS = 256
N_EXPERT_GROUPS = 8
GROUP_SIZE = 32
TOPK_GROUPS = 4
TOP_K = 8
EXPERT_FF = 256
ROUTED_SCALE = 2.5
EXPERT_BLOCK = 128
EPS = 1e-6
NEG_INF = -1e30
QKV_W = POOL_W + ATTN_W + 2 * KV_W
QK_W = ATTN_W + KV_W
HIST = 16

SC_WORKERS = 32
SC_LANES = 16
GATE_ROW = 128


def _dot(a, b):
    return jnp.dot(a, b, preferred_element_type=f32)


def _dot_t(a, b):
    return lax.dot_general(a, b, (((1,), (1,)), ((), ())), preferred_element_type=f32)


def _split_bf16(a):
    hi = a.astype(bf16)
    lo = (a - hi.astype(f32)).astype(bf16)
    return hi, lo


ROW_RING = 8
W_RING = 5
HI_MASK = -65536


def _pack_pairs(a):
    h = a.shape[1] // 2
    hi = lax.bitcast_convert_type(a[:, :h].astype(bf16).astype(f32), i32)
    lo = lax.bitcast_convert_type(a[:, h:].astype(bf16).astype(f32), i32)
    return hi | lax.shift_right_logical(lo, 16)


def _unpack_pairs(w):
    hi = lax.bitcast_convert_type(w & HI_MASK, f32).astype(bf16)
    lo = lax.bitcast_convert_type(lax.shift_left(w, 16), f32).astype(bf16)
    return hi, lo


def _mod_norm(x, g, sc, sh):
    ms = jnp.mean(x * x, axis=-1, keepdims=True)
    y = x * lax.rsqrt(ms + EPS)
    return (y * g) * (1.0 + sc) + sh


def _ada_body(c_ref, w_ref, b_ref, o_ref):
    c = c_ref[...]
    a = (c * jax.nn.sigmoid(c)).astype(bf16)
    o_ref[...] = _dot(a, w_ref[...].astype(bf16)) + b_ref[...]


def _ada(c, w_ada, b_ada):
    n = c.shape[0]
    tn = 1024
    return pl.pallas_call(
        _ada_body,
        grid=(6 * D_MODEL // tn,),
        in_specs=[
            pl.BlockSpec((n, D_MODEL), lambda j: (0, 0)),
            pl.BlockSpec((D_MODEL, tn), lambda j: (0, j)),
            pl.BlockSpec((1, tn), lambda j: (0, j)),
        ],
        out_specs=pl.BlockSpec((n, tn), lambda j: (0, j)),
        out_shape=jax.ShapeDtypeStruct((n, 6 * D_MODEL), f32),
        name="ada_mod",
    )(c, w_ada, b_ada.reshape(1, -1))


def _relbias_body(table_ref, bucket_ref, o_ref):
    bucket = bucket_ref[...]
    for h in range(N_HEADS):
        acc = jnp.zeros(bucket.shape, f32)
        for b in range(NUM_BUCKETS):
            acc = jnp.where(bucket == b, table_ref[b, h], acc)
        o_ref[h] = acc


def _rel_buckets(dist):
    n = np.maximum(dist, 0)
    nf = np.maximum(n, 1).astype(np.float64)
    large = MAX_EXACT + (np.log(nf / MAX_EXACT) / math.log(REL_MAX_DIST / MAX_EXACT)
                         * (NUM_BUCKETS - MAX_EXACT)).astype(np.int32)
    return np.where(n < MAX_EXACT, n, np.minimum(large, NUM_BUCKETS - 1)).astype(np.int32)


def _relbias(table, dist):
    lq, lk = dist.shape
    return pl.pallas_call(
        _relbias_body,
        in_specs=[
            pl.BlockSpec(memory_space=pltpu.SMEM),
            pl.BlockSpec((lq, lk), lambda: (0, 0)),
        ],
        out_specs=pl.BlockSpec((N_HEADS, lq, lk), lambda: (0, 0, 0)),
        out_shape=jax.ShapeDtypeStruct((N_HEADS, lq, lk), f32),
        name="rel_bias",
    )(table, jnp.asarray(_rel_buckets(dist)))


def _qkv_from_h(hs, w_ref, gqk_ref, bd_ref):
    us = [_dot(h.astype(bf16), w_ref[...]) for h in hs]
    qks = [u[:, POOL_W:POOL_W + QK_W] for u in us]
    sq = [_split_bf16(qk * qk) for qk in qks]
    bd = bd_ref[...]
    ss = [_dot(y_hi, bd) + _dot(y_lo, bd) for y_hi, y_lo in sq]
    qkn = [(qk * lax.rsqrt(s * (1.0 / HEAD_DIM) + EPS)) * gqk_ref[...] for qk, s in zip(qks, ss)]
    return [(u[:, :POOL_W], n[:, :ATTN_W] * (HEAD_DIM ** -0.5), n[:, ATTN_W:], u[:, POOL_W + QK_W:])
            for u, n in zip(us, qkn)]


def _inproj_prompt_body(x_ref, sh_ref, sc_ref, g_ref, w_ref, gqk_ref, bd_ref, wp_ref, ps_ref,
                        q_ref, k_ref, v_ref, po_ref, np_ref, kc_ref, vc_ref, hist, wsum, *, tl, nt):
    j = pl.program_id(1)
    nsub = 2
    ts = tl // nsub
    hs = [_mod_norm(x_ref[0, i * ts:(i + 1) * ts, :], g_ref[...], sc_ref[0], sh_ref[0]) for i in range(nsub)]
    parts = _qkv_from_h(hs, w_ref, gqk_ref, bd_ref)

    base = 2 * HIST
    end = base + tl

    @pl.when(j == 0)
    def _():
        hist[0:base, :] = jnp.zeros((base, POOL_W), f32)
        wsum[0:HIST, :] = jnp.zeros((HIST, POOL_W), f32)

    for i, (up, q, k, v) in enumerate(parts):
        rows = slice(i * ts, (i + 1) * ts)
        q_ref[0, rows, :] = q.astype(bf16)
        k_ref[0, rows, :] = k.astype(bf16)
        v_ref[0, rows, :] = v.astype(bf16)
        hist[base + i * ts:base + (i + 1) * ts, :] = up

    @pl.when(j == nt - 1)
    def _():
        kc_ref[0] = parts[-1][2][ts - WINDOW:, :]
        vc_ref[0] = parts[-1][3][ts - WINDOW:, :]

    pos = j * tl + lax.broadcasted_iota(i32, (tl, 1), 0)
    src = hist
    for p, w in enumerate(POOL_WINDOWS):
        sh = w // 2
        live = slice(p * POOL_GC, POOL_W)
        summed = src[HIST:end, live] + src[HIST - sh:end - sh, live]
        wsum[HIST:end, live] = summed
        src = wsum
        lanes = slice(p * POOL_GC, (p + 1) * POOL_GC)
        cur = hist[base:end, lanes]
        cnt = jnp.minimum(w, pos + 1).astype(f32)
        d = summed[HIST:, 0:POOL_GC] / cnt - cur
        yg = _dot(d.astype(bf16), wp_ref[p]) * ps_ref[:, lanes]
        po_ref[0, :, lanes] = yg.astype(bf16)

    @pl.when(j == nt - 1)
    def _():
        np_ref[0] = hist[end - POOL_BUF:end, :]

    hist[HIST:base, :] = hist[end - HIST:end, :]


def _inproj_prompt(x, mod3, g_attn, w_in, gqk, bd, w_pool, pool_scale, tl=512):
    B, L, D = x.shape
    nt = L // tl
    full = lambda shape: pl.BlockSpec(shape, lambda b, j: (0,) * len(shape))
    return pl.pallas_call(
        functools.partial(_inproj_prompt_body, tl=tl, nt=nt),
        grid=(B, nt),
        in_specs=[
            pl.BlockSpec((1, tl, D), lambda b, j: (b, j, 0)),
            pl.BlockSpec((1, 1, D), lambda b, j: (b, 0, 0)),
            pl.BlockSpec((1, 1, D), lambda b, j: (b, 0, 1)),
            full((1, D)),
            full((D, QKV_W)),
            full((1, QK_W)),
            full((QK_W, QK_W)),
            full((4, POOL_GC, POOL_GC)),
            full((1, POOL_W)),
        ],
        out_specs=[
            pl.BlockSpec((1, tl, ATTN_W), lambda b, j: (b, j, 0)),
            pl.BlockSpec((1, tl, KV_W), lambda b, j: (b, j, 0)),
            pl.BlockSpec((1, tl, KV_W), lambda b, j: (b, j, 0)),
            pl.BlockSpec((1, tl, POOL_W), lambda b, j: (b, j, 0)),
            pl.BlockSpec((1, POOL_BUF, POOL_W), lambda b, j: (b, 0, 0)),
            pl.BlockSpec((1, WINDOW, KV_W), lambda b, j: (b, 0, 0)),
            pl.BlockSpec((1, WINDOW, KV_W), lambda b, j: (b, 0, 0)),
        ],
        out_shape=[
            jax.ShapeDtypeStruct((B, L, ATTN_W), bf16),
            jax.ShapeDtypeStruct((B, L, KV_W), bf16),
            jax.ShapeDtypeStruct((B, L, KV_W), bf16),
            jax.ShapeDtypeStruct((B, L, POOL_W), bf16),
            jax.ShapeDtypeStruct((B, POOL_BUF, POOL_W), f32),
            jax.ShapeDtypeStruct((B, WINDOW, KV_W), f32),
            jax.ShapeDtypeStruct((B, WINDOW, KV_W), f32),
        ],
        scratch_shapes=[pltpu.VMEM((2 * HIST + tl, POOL_W), f32), pltpu.VMEM((2 * HIST + tl, POOL_W), f32)],
        compiler_params=pltpu.CompilerParams(dimension_semantics=("arbitrary", "arbitrary")),
        name="inproj_prompt",
    )(x, mod3, mod3, g_attn, w_in, gqk, bd, w_pool, pool_scale)


def _inproj_sample_body(x_ref, sh_ref, sc_ref, g_ref, w_ref, gqk_ref, bd_ref, wp_ref, ps_ref, st_ref,
                        q_ref, k_ref, v_ref, po_ref, np_ref, ext, *, bt, ls, pos0):
    n = bt * ls
    h3 = _mod_norm(x_ref[...], g_ref[...][None], sc_ref[...], sh_ref[...])
    (up, q, k, v), = _qkv_from_h([h3.reshape(n, D_MODEL)], w_ref, gqk_ref, bd_ref)
    q_ref[...] = q.astype(bf16)
    k_ref[...] = k
    v_ref[...] = v

    ext[:, 1:HIST, :] = st_ref[...]
    ext[:, HIST:HIST + ls, :] = up.reshape(bt, ls, POOL_W)
    pos = pos0 + lax.broadcasted_iota(i32, (1, ls, 1), 1)
    for g, w in enumerate(POOL_WINDOWS):
        lanes = slice(g * POOL_GC, (g + 1) * POOL_GC)
        cur = ext[:, HIST:HIST + ls, lanes]
        acc = cur
        for s in range(1, w):
            acc = acc + ext[:, HIST - s:HIST - s + ls, lanes]
        cnt = jnp.minimum(w, pos + 1).astype(f32)
        d = (acc / cnt - cur).reshape(n, POOL_GC)
        yg = _dot(d.astype(bf16), wp_ref[g]) * ps_ref[:, lanes]
        po_ref[:, lanes] = yg.astype(bf16)
    np_ref[...] = ext[:, ls + 1:ls + HIST, :]


def _inproj_sample(x, mod3, g_attn, w_in, gqk, bd, w_pool, pool_scale, state, pos0, bt=64):
    B, ls, D = x.shape
    n = bt * ls
    full = lambda shape: pl.BlockSpec(shape, lambda i: (0,) * len(shape))
    return pl.pallas_call(
        functools.partial(_inproj_sample_body, bt=bt, ls=ls, pos0=pos0),
        grid=(B // bt,),
        in_specs=[
            pl.BlockSpec((bt, ls, D), lambda i: (i, 0, 0)),
            pl.BlockSpec((bt, 1, D), lambda i: (i, 0, 0)),
            pl.BlockSpec((bt, 1, D), lambda i: (i, 0, 1)),
            full((1, D)),
            full((D, QKV_W)),
            full((1, QK_W)),
            full((QK_W, QK_W)),
            full((4, POOL_GC, POOL_GC)),
            full((1, POOL_W)),
            pl.BlockSpec((bt, POOL_BUF, POOL_W), lambda i: (i, 0, 0)),
        ],
        out_specs=[
            pl.BlockSpec((n, ATTN_W), lambda i: (i, 0)),
            pl.BlockSpec((n, KV_W), lambda i: (i, 0)),
            pl.BlockSpec((n, KV_W), lambda i: (i, 0)),
            pl.BlockSpec((n, POOL_W), lambda i: (i, 0)),
            pl.BlockSpec((bt, POOL_BUF, POOL_W), lambda i: (i, 0, 0)),
        ],
        out_shape=[
            jax.ShapeDtypeStruct((B * ls, ATTN_W), bf16),
            jax.ShapeDtypeStruct((B * ls, KV_W), f32),
            jax.ShapeDtypeStruct((B * ls, KV_W), f32),
            jax.ShapeDtypeStruct((B * ls, POOL_W), bf16),
            jax.ShapeDtypeStruct((B, POOL_BUF, POOL_W), f32),
        ],
        scratch_shapes=[pltpu.VMEM((bt, HIST + ls, POOL_W), f32)],
        name="inproj_sample",
    )(x, mod3, mod3, g_attn, w_in, gqk, bd, w_pool, pool_scale, state)


def _softmax_sink(parts, sink):
    m = sink
    for s in parts:
        m = jnp.maximum(m, jnp.max(s, axis=-1, keepdims=True))
    ps = [jnp.exp(s - m) for s in parts]
    denom = jnp.exp(sink - m)
    for p in ps:
        denom = denom + jnp.sum(p, axis=-1, keepdims=True)
    inv = 1.0 / denom
    return [(p * inv).astype(bf16) for p in ps]


def _attn_prompt_body(sinks_ref, q_ref, kp_ref, kc_ref, vp_ref, vc_ref, bias_ref, mask_ref, o_ref):
    j = pl.program_id(1)
    nk = 2 * WINDOW
    qw = GQA * HEAD_DIM
    nq = q_ref.shape[1] // WINDOW
    kall = jnp.concatenate([kp_ref[0], kc_ref[0]], axis=0)
    vall = jnp.concatenate([vp_ref[0], vc_ref[0]], axis=0)
    lane_group = lax.broadcasted_iota(i32, (nk, 2 * KV_W), 1) // HEAD_DIM
    first_has_prev = jnp.minimum(j, 1)

    def spread(t):
        t0 = jnp.concatenate([t, t], axis=1)
        return t0, pltpu.roll(t0, HEAD_DIM, 1)

    def blockdiag(t01, kv):
        t0, t1 = t01
        return jnp.concatenate(
            [jnp.where(lane_group == g, t0 if g % 2 == kv else t1, jnp.zeros_like(t0)) for g in range(GQA)],
            axis=0)

    chains = [(qb, kv) for qb in range(nq) for kv in range(N_KV_HEADS)]
    ksp = [spread(kall[qb * WINDOW:qb * WINDOW + nk]) for qb in range(nq)]
    vsp = [spread(vall[qb * WINDOW:qb * WINDOW + nk]) for qb in range(nq)]
    valid = [mask_ref[first_has_prev] > 0.5] + [mask_ref[1] > 0.5] * (nq - 1)
    s = [_dot_t(q_ref[0, qb * WINDOW:(qb + 1) * WINDOW, kv * qw:(kv + 1) * qw], blockdiag(ksp[qb], kv))
         for qb, kv in chains]
    s = [jnp.where(valid[qb], s[c] + bias_ref[kv], NEG_INF) for c, (qb, kv) in enumerate(chains)]
    p = [jnp.concatenate([_softmax_sink([s[c][:, g * nk:(g + 1) * nk]], sinks_ref[kv * GQA + g])[0]
                          for g in range(GQA)], axis=1) for c, (qb, kv) in enumerate(chains)]
    o = [_dot(p[c], blockdiag(vsp[qb], kv)) for c, (qb, kv) in enumerate(chains)]
    for c, (qb, kv) in enumerate(chains):
        o_ref[0, qb * WINDOW:(qb + 1) * WINDOW, kv * qw:(kv + 1) * qw] = o[c].astype(bf16)


def _attn_prompt(q, k, v, bias, sinks):
    B, L, _ = q.shape
    nq = 4
    nb = L // (nq * WINDOW)
    cur = lambda b, j: (b, j, 0)
    prev = lambda b, j: (b, jnp.maximum(nq * j - 1, 0), 0)
    qi = np.arange(WINDOW)[:, None]
    kc = np.arange(2 * WINDOW)[None, :]
    own = (kc >= WINDOW) & (kc - WINDOW <= qi)
    prv = (kc < WINDOW) & (kc > qi)
    mask = np.stack([np.tile(own, (1, GQA)), np.tile(own | prv, (1, GQA))]).astype(np.float32)
    return pl.pallas_call(
        _attn_prompt_body,
        grid=(B, nb),
        in_specs=[
            pl.BlockSpec(memory_space=pltpu.SMEM),
            pl.BlockSpec((1, nq * WINDOW, ATTN_W), cur),
            pl.BlockSpec((1, WINDOW, KV_W), prev),
            pl.BlockSpec((1, nq * WINDOW, KV_W), cur),
            pl.BlockSpec((1, WINDOW, KV_W), prev),
            pl.BlockSpec((1, nq * WINDOW, KV_W), cur),
            pl.BlockSpec((N_KV_HEADS, WINDOW, GQA * 2 * WINDOW), lambda b, j: (0, 0, 0)),
            pl.BlockSpec((2, WINDOW, GQA * 2 * WINDOW), lambda b, j: (0, 0, 0)),
        ],
        out_specs=pl.BlockSpec((1, nq * WINDOW, ATTN_W), cur),
        out_shape=jax.ShapeDtypeStruct((B, L, ATTN_W), bf16),
        name="attn_prompt",
    )(sinks, q, k, k, v, v, bias, jnp.asarray(mask))


def _attn_sample_body(q_ref, kb_ref, vb_ref, kn_ref, vn_ref, bb_ref, bn_ref, sink_ref,
                      o_ref, nk_ref, nv_ref, *, bb, ls):
    W = kb_ref.shape[1]
    rows = GQA * ls
    qi = lax.broadcasted_iota(i32, (rows, W), 0) % ls
    kj = lax.broadcasted_iota(i32, (rows, W), 1)
    valid_buf = kj > qi
    qi2 = lax.broadcasted_iota(i32, (rows, ls), 0) % ls
    kj2 = lax.broadcasted_iota(i32, (rows, ls), 1)
    valid_new = kj2 <= qi2

    nbat = 4
    ks = [slice(kv * HEAD_DIM, (kv + 1) * HEAD_DIM) for kv in range(N_KV_HEADS)]

    def group(i, carry):
        bs = [i * nbat + u for u in range(nbat)]
        chains = [(u, kv) for u in range(nbat) for kv in range(N_KV_HEADS)]
        qb = [q_ref[b] for b in bs]
        kbuf = [kb_ref[b] for b in bs]
        vbuf = [vb_ref[b] for b in bs]
        knew = [kn_ref[b] for b in bs]
        vnew = [vn_ref[b] for b in bs]
        qg = [jnp.concatenate([qb[u][:, (kv * GQA + g) * HEAD_DIM:(kv * GQA + g + 1) * HEAD_DIM]
                               for g in range(GQA)], axis=0) for u, kv in chains]
        s_buf = [_dot_t(qg[c], kbuf[u][:, ks[kv]].astype(bf16)) for c, (u, kv) in enumerate(chains)]
        s_new = [_dot_t(qg[c], knew[u][:, ks[kv]].astype(bf16)) for c, (u, kv) in enumerate(chains)]
        s_buf = [jnp.where(valid_buf, s_buf[c] + bb_ref[kv], NEG_INF) for c, (u, kv) in enumerate(chains)]
        s_new = [jnp.where(valid_new, s_new[c] + bn_ref[kv], NEG_INF) for c, (u, kv) in enumerate(chains)]
        probs = [_softmax_sink([s_buf[c], s_new[c]], sink_ref[kv]) for c, (u, kv) in enumerate(chains)]
        outs = [_dot(probs[c][0], vbuf[u][:, ks[kv]].astype(bf16)) + _dot(probs[c][1], vnew[u][:, ks[kv]].astype(bf16))
                for c, (u, kv) in enumerate(chains)]
        for u, b in enumerate(bs):
            heads = [outs[u * N_KV_HEADS + kv][g * ls:(g + 1) * ls] for kv in range(N_KV_HEADS) for g in range(GQA)]
            o_ref[b] = jnp.concatenate(heads, axis=-1).astype(bf16)
            nk_ref[b, 0:W - ls, :] = kbuf[u][ls:, :]
            nk_ref[b, W - ls:W, :] = knew[u]
            nv_ref[b, 0:W - ls, :] = vbuf[u][ls:, :]
            nv_ref[b, W - ls:W, :] = vnew[u]
        return carry

    lax.fori_loop(0, bb // nbat, group, 0)


def _attn_sample(q, k_buf, v_buf, k_new, v_new, bias_buf, bias_new, sink_col, bb=16):
    B, ls, _ = q.shape
    W = k_buf.shape[1]
    rows = GQA * ls
    blk = lambda shape: pl.BlockSpec(shape, lambda i: (i, 0, 0))
    full = lambda shape: pl.BlockSpec(shape, lambda i: (0, 0, 0))
    return pl.pallas_call(
        functools.partial(_attn_sample_body, bb=bb, ls=ls),
        grid=(B // bb,),
        in_specs=[
            blk((bb, ls, ATTN_W)),
            blk((bb, W, KV_W)),
            blk((bb, W, KV_W)),
            blk((bb, ls, KV_W)),
            blk((bb, ls, KV_W)),
            full((N_KV_HEADS, rows, W)),
            full((N_KV_HEADS, rows, ls)),
            full((N_KV_HEADS, rows, 1)),
        ],
        out_specs=[blk((bb, ls, ATTN_W)), blk((bb, W, KV_W)), blk((bb, W, KV_W))],
        out_shape=[
            jax.ShapeDtypeStruct((B, ls, ATTN_W), bf16),
            jax.ShapeDtypeStruct((B, W, KV_W), f32),
            jax.ShapeDtypeStruct((B, W, KV_W), f32),
        ],
        name="attn_sample",
    )(q, k_buf, v_buf, k_new, v_new, bias_buf, bias_new, sink_col)


def _outproj_core(po, at, x, gt, sc, sh, g_ref, wo_ref, wrh_ref, wrl_ref):
    mixo = _dot(po, wo_ref[0:POOL_W, :]) + _dot(at, wo_ref[POOL_W:, :])
    x1 = x + gt * mixo.reshape(x.shape)
    h2 = _mod_norm(x1, g_ref[...].reshape((1,) * (x.ndim - 1) + (D_MODEL,)), sc, sh).reshape(-1, D_MODEL)
    h_hi, h_lo = _split_bf16(h2)
    wh = wrh_ref[...]
    logits = _dot_t(wh, h_hi) + (_dot_t(wh, h_lo) + _dot_t(wrl_ref[...], h_hi))
    return x1, _pack_pairs(h2), logits


def _outproj_prompt_body(po_ref, at_ref, x_ref, gt_ref, sc_ref, sh_ref, g_ref, wo_ref, wrh_ref, wrl_ref,
                         x1_ref, h2_ref, lg_ref):
    x1, h2p, logits = _outproj_core(po_ref[0], at_ref[0], x_ref[0], gt_ref[0], sc_ref[0], sh_ref[0],
                                    g_ref, wo_ref, wrh_ref, wrl_ref)
    x1_ref[...] = x1
    h2_ref[...] = h2p
    lg_ref[...] = logits


def _outproj_sample_body(po_ref, at_ref, x_ref, gt_ref, sc_ref, sh_ref, g_ref, wo_ref, wrh_ref, wrl_ref,
                         x1_ref, h2_ref, lg_ref):
    x1, h2p, logits = _outproj_core(po_ref[...], at_ref[...], x_ref[...], gt_ref[...], sc_ref[...], sh_ref[...],
                                    g_ref, wo_ref, wrh_ref, wrl_ref)
    x1_ref[...] = x1.reshape(-1, D_MODEL)
    h2_ref[...] = h2p
    lg_ref[...] = logits


def _outproj_prompt(po, at, x, mod3, g_ffn, w_out, wr_hi, wr_lo, tm=1024):
    B, L, D = x.shape
    nt = L // tm
    n_tok = B * L
    full = lambda shape: pl.BlockSpec(shape, lambda b, j: (0,) * len(shape))
    modspec = lambda c: pl.BlockSpec((1, 1, D), lambda b, j: (b, 0, c))
    return pl.pallas_call(
        _outproj_prompt_body,
        grid=(B, nt),
        in_specs=[
            pl.BlockSpec((1, tm, POOL_W), lambda b, j: (b, j, 0)),
            pl.BlockSpec((1, tm, ATTN_W), lambda b, j: (b, j, 0)),
            pl.BlockSpec((1, tm, D), lambda b, j: (b, j, 0)),
            modspec(2), modspec(4), modspec(3),
            full((1, D)), full((D, D)), full((N_EXPERTS, D)), full((N_EXPERTS, D)),
        ],
        out_specs=[
            pl.BlockSpec((tm, D), lambda b, j: (b * nt + j, 0)),
            pl.BlockSpec((tm, D // 2), lambda b, j: (b * nt + j, 0)),
            pl.BlockSpec((N_EXPERTS, tm), lambda b, j: (0, b * nt + j)),
        ],
        out_shape=[
            jax.ShapeDtypeStruct((n_tok, D), f32),
            jax.ShapeDtypeStruct((n_tok, D // 2), i32),
            jax.ShapeDtypeStruct((N_EXPERTS, n_tok), f32),
        ],
        name="outproj_prompt",
    )(po, at, x, mod3, mod3, mod3, g_ffn, w_out, wr_hi, wr_lo)


def _outproj_sample(po, at, x, mod3, g_ffn, w_out, wr_hi, wr_lo, bt=64):
    B, ls, D = x.shape
    n = bt * ls
    full = lambda shape: pl.BlockSpec(shape, lambda i: (0,) * len(shape))
    modspec = lambda c: pl.BlockSpec((bt, 1, D), lambda i: (i, 0, c))
    return pl.pallas_call(
        _outproj_sample_body,
        grid=(B // bt,),
        in_specs=[
            pl.BlockSpec((n, POOL_W), lambda i: (i, 0)),
            pl.BlockSpec((n, ATTN_W), lambda i: (i, 0)),
            pl.BlockSpec((bt, ls, D), lambda i: (i, 0, 0)),
            modspec(2), modspec(4), modspec(3),
            full((1, D)), full((D, D)), full((N_EXPERTS, D)), full((N_EXPERTS, D)),
        ],
        out_specs=[
            pl.BlockSpec((n, D), lambda i: (i, 0)),
            pl.BlockSpec((n, D // 2), lambda i: (i, 0)),
            pl.BlockSpec((N_EXPERTS, n), lambda i: (0, i)),
        ],
        out_shape=[
            jax.ShapeDtypeStruct((B * ls, D), f32),
            jax.ShapeDtypeStruct((B * ls, D // 2), i32),
            jax.ShapeDtypeStruct((N_EXPERTS, B * ls), f32),
        ],
        name="outproj_sample",
    )(po, at, x, mod3, mod3, mod3, g_ffn, w_out, wr_hi, wr_lo)


def _route_body(lga_ref, lgb_ref, rb_ref, tri_ref, idx_ref, rank_ref, gate_ref, cnt_ref, carry, *, tr, nsteps, na):
    step = pl.program_id(0)

    @pl.when(step == 0)
    def _():
        carry[...] = jnp.zeros(carry.shape, f32)

    logits = jnp.where(step < na, lga_ref[...], lgb_ref[...])
    s = jax.nn.sigmoid(logits)
    sb = s + rb_ref[...]
    e_iota = lax.broadcasted_iota(i32, (N_EXPERTS, tr), 0)
    g_iota = lax.broadcasted_iota(i32, (GROUP_SIZE, tr), 0)

    gscore = []
    for g in range(N_EXPERT_GROUPS):
        v = sb[g * GROUP_SIZE:(g + 1) * GROUP_SIZE]
        m1 = jnp.max(v, axis=0, keepdims=True)
        i1 = jnp.min(jnp.where(v == m1, g_iota, GROUP_SIZE), axis=0, keepdims=True)
        m2 = jnp.max(jnp.where(g_iota == i1, -jnp.inf, v), axis=0, keepdims=True)
        gscore.append(m1 + m2)
    parts = []
    for g in range(N_EXPERT_GROUPS):
        beaten = jnp.zeros((1, tr), i32)
        for g2 in range(N_EXPERT_GROUPS):
            if g2 == g:
                continue
            ahead = gscore[g2] > gscore[g]
            if g2 < g:
                ahead = ahead | (gscore[g2] == gscore[g])
            beaten = beaten + ahead.astype(i32)
        keep = beaten < TOPK_GROUPS
        parts.append(jnp.where(keep, sb[g * GROUP_SIZE:(g + 1) * GROUP_SIZE], NEG_INF))
    cur = jnp.concatenate(parts, axis=0)

    idxs, svals = [], []
    for _ in range(TOP_K):
        m = jnp.max(cur, axis=0, keepdims=True)
        ik = jnp.min(jnp.where(cur == m, e_iota, N_EXPERTS), axis=0, keepdims=True)
        hit = e_iota == ik
        svals.append(jnp.sum(jnp.where(hit, s, 0.0), axis=0, keepdims=True))
        cur = jnp.where(hit, -jnp.inf, cur)
        idxs.append(ik)
    sel = (cur == -jnp.inf).astype(f32)
    ssum = svals[0]
    for sv in svals[1:]:
        ssum = ssum + sv
    gate_ref[...] = jnp.concatenate([sv / ssum * ROUTED_SCALE for sv in svals], axis=0)
    idx_ref[...] = jnp.concatenate(idxs, axis=0)

    before = carry[...] + _dot(sel.astype(bf16), tri_ref[...])
    ranks = [jnp.sum(jnp.where(e_iota == ik, before, 0.0), axis=0, keepdims=True) for ik in idxs]
    rank_ref[...] = jnp.concatenate(ranks, axis=0).astype(i32)
    carry[...] = carry[...] + jnp.sum(sel, axis=1, keepdims=True)

    @pl.when(step == nsteps - 1)
    def _():
        cnt_ref[...] = carry[...]


def _route(logits_a, logits_b, router_bias, tr=512):
    E, Ta = logits_a.shape
    T = Ta + logits_b.shape[1]
    na = Ta // tr
    nsteps = T // tr
    tri = jnp.asarray(np.triu(np.ones((tr, tr), np.float32), 1), bf16)
    return pl.pallas_call(
        functools.partial(_route_body, tr=tr, nsteps=nsteps, na=na),
        grid=(nsteps,),
        in_specs=[
            pl.BlockSpec((E, tr), lambda i: (0, jnp.minimum(i, na - 1))),
            pl.BlockSpec((E, tr), lambda i: (0, jnp.maximum(i - na, 0))),
            pl.BlockSpec((E, 1), lambda i: (0, 0)),
            pl.BlockSpec((tr, tr), lambda i: (0, 0)),
        ],
        out_specs=[
            pl.BlockSpec((TOP_K, tr), lambda i: (0, i)),
            pl.BlockSpec((TOP_K, tr), lambda i: (0, i)),
            pl.BlockSpec((TOP_K, tr), lambda i: (0, i)),
            pl.BlockSpec((E, 1), lambda i: (0, 0)),
        ],
        out_shape=[
            jax.ShapeDtypeStruct((TOP_K, T), i32),
            jax.ShapeDtypeStruct((TOP_K, T), i32),
            jax.ShapeDtypeStruct((TOP_K, T), f32),
            jax.ShapeDtypeStruct((E, 1), f32),
        ],
        scratch_shapes=[pltpu.VMEM((E, 1), f32)],
        compiler_params=pltpu.CompilerParams(dimension_semantics=("arbitrary",)),
        name="route",
    )(logits_a, logits_b, router_bias.reshape(E, 1), tri)


def _dest_body(idx_ref, rank_ref, ps_ref, dest_ref, *, tr):
    e_iota = lax.broadcasted_iota(i32, (N_EXPERTS, tr), 0)
    start = ps_ref[...]
    rows = []
    for k in range(TOP_K):
        hit = e_iota == idx_ref[k:k + 1, :]
        rows.append(jnp.sum(jnp.where(hit, start, 0.0), axis=0, keepdims=True))
    dest_ref[...] = jnp.concatenate(rows, axis=0).astype(i32) + rank_ref[...]


def _dest_rows(idx, rank, pad_start, tr=512):
    K, T = idx.shape
    blk = pl.BlockSpec((K, tr), lambda i: (0, i))
    return pl.pallas_call(
        functools.partial(_dest_body, tr=tr),
        grid=(T // tr,),
        in_specs=[blk, blk, pl.BlockSpec((N_EXPERTS, 1), lambda i: (0, 0))],
        out_specs=blk,
        out_shape=jax.ShapeDtypeStruct((K, T), i32),
        name="dest_rows",
    )(idx, rank, pad_start.astype(f32).reshape(N_EXPERTS, 1))


def _sc_mesh():
    return plsc.VectorSubcoreMesh(core_axis_name="c", subcore_axis_name="s")


def _sc_worker_id():
    return lax.axis_index("s") * 2 + lax.axis_index("c")


def _dispatch(h2_a, h2_b, dest, gate, n_rows, chunk=32):
    ta, Dw = h2_a.shape
    T = ta + h2_b.shape[0]
    per_worker = T // SC_WORKERS
    nchunk = per_worker // chunk
    assert per_worker * SC_WORKERS == T and nchunk * chunk == per_worker and ta % chunk == 0

    nrow_idx = nchunk * TOP_K

    @functools.partial(
        pl.kernel, mesh=_sc_mesh(),
        out_type=[jax.ShapeDtypeStruct((n_rows, Dw), i32), jax.ShapeDtypeStruct((n_rows, GATE_ROW), f32)],
        scratch_types=[pltpu.VMEM((nrow_idx, chunk), i32), pltpu.VMEM((nrow_idx, chunk), f32),
                       pltpu.VMEM((chunk, Dw), i32), pltpu.VMEM((TOP_K, chunk, GATE_ROW), f32),
                       pltpu.SemaphoreType.DMA],
        compiler_params=pltpu.CompilerParams(needs_layout_passes=False),
        name="moe_dispatch",
    )
    def body(ha_hbm, hb_hbm, dest_hbm, gate_hbm, xs_hbm, gs_hbm, idx_v, gate_v, rows_v, grow_v, sem):
        wid = _sc_worker_id()
        base = wid * per_worker
        pltpu.sync_copy(dest_hbm.at[wid], idx_v)
        pltpu.sync_copy(gate_hbm.at[wid], gate_v)
        zero = jnp.zeros((SC_LANES,), f32)
        for k in range(TOP_K):
            @pl.loop(0, chunk)
            def _(t):
                for j in range(GATE_ROW // SC_LANES):
                    grow_v[k, t, pl.ds(j * SC_LANES, SC_LANES)] = zero

        @pl.loop(0, nchunk)
        def _(ci):
            t0 = base + ci * chunk

            @pl.when(t0 < ta)
            def _():
                pltpu.sync_copy(ha_hbm.at[pl.ds(t0, chunk)], rows_v)

            @pl.when(t0 >= ta)
            def _():
                pltpu.sync_copy(hb_hbm.at[pl.ds(t0 - ta, chunk)], rows_v)

            for k in range(TOP_K):
                @plsc.parallel_loop(0, chunk, unroll=4)
                def _(t):
                    row = jnp.zeros((SC_LANES,), i32) + (ci * TOP_K + k)
                    grow_v[k, t, pl.ds(0, SC_LANES)] = plsc.load_gather(
                        gate_v, [row, jnp.zeros((SC_LANES,), i32) + t])

            cps = []
            for k in range(TOP_K):
                idx = idx_v.at[ci * TOP_K + k]
                cps.append(pltpu.make_async_copy(rows_v, xs_hbm.at[idx], sem))
                cps.append(pltpu.make_async_copy(grow_v.at[k], gs_hbm.at[idx], sem))
            for cp in cps:
                cp.start()
            for cp in cps:
                cp.wait()

    def per_worker_rows(a):
        return a.reshape(TOP_K, SC_WORKERS, nchunk, chunk).transpose(1, 2, 0, 3).reshape(
            SC_WORKERS, nrow_idx, chunk)

    return body(h2_a, h2_b, per_worker_rows(dest), per_worker_rows(gate))


def _combine(ys, dest, chunk=8):
    T = dest.shape[1]
    Dw = ys.shape[1]
    per_worker = T // SC_WORKERS
    nchunk = per_worker // chunk
    assert per_worker * SC_WORKERS == T and nchunk * chunk == per_worker and nchunk % 2 == 0

    @functools.partial(
        pl.kernel, mesh=_sc_mesh(),
        out_type=jax.ShapeDtypeStruct((T, 2 * Dw), f32),
        scratch_types=[
            pltpu.VMEM((TOP_K * per_worker,), i32),
            pltpu.VMEM((2, TOP_K, chunk, Dw), i32),
            pltpu.VMEM((chunk, 2 * Dw), f32),
            pltpu.SemaphoreType.DMA((2,)),
        ],
        compiler_params=pltpu.CompilerParams(needs_layout_passes=False),
        name="moe_combine",
    )
    def body(ys_hbm, dest_hbm, out_hbm, idx_v, buf, out_v, sems):
        base = _sc_worker_id() * per_worker
        pltpu.sync_copy(dest_hbm.at[pl.ds(_sc_worker_id() * (TOP_K * per_worker), TOP_K * per_worker)], idx_v)

        def gather(ci, slot):
            return [pltpu.make_async_copy(ys_hbm.at[idx_v.at[pl.ds(k * per_worker + ci * chunk, chunk)]],
                                          buf.at[slot, k], sems.at[slot]) for k in range(TOP_K)]

        for cp in gather(0, 0):
            cp.start()

        @pl.loop(0, nchunk, step=2)
        def _(c0):
            for slot in range(2):
                ci = c0 + slot

                @pl.when(ci + 1 < nchunk)
                def _():
                    for cp in gather(ci + 1, 1 - slot):
                        cp.start()

                for cp in gather(ci, slot):
                    cp.wait()

                @pl.loop(0, chunk)
                def _(t):
                    @plsc.parallel_loop(0, Dw // SC_LANES, unroll=8)
                    def _(j):
                        sl = pl.ds(j * SC_LANES, SC_LANES)
                        w = buf[slot, 0, t, sl]
                        hi = plsc.bitcast(w & HI_MASK, f32)
                        lo = plsc.bitcast(lax.shift_left(w, 16), f32)
                        for k in range(1, TOP_K):
                            w = buf[slot, k, t, sl]
                            hi = hi + plsc.bitcast(w & HI_MASK, f32)
                            lo = lo + plsc.bitcast(lax.shift_left(w, 16), f32)
                        out_v[t, sl] = hi
                        out_v[t, pl.ds(Dw + j * SC_LANES, SC_LANES)] = lo

                pltpu.sync_copy(out_v, out_hbm.at[pl.ds(base + ci * chunk, chunk)])

    dest_w = dest.reshape(TOP_K, SC_WORKERS, per_worker).transpose(1, 0, 2).reshape(-1)
    return body(ys, dest_w)


def _gmm_body(blk_e_ref, blk_rows_ref, blk_next_ref, nv_ref, xs_hbm, gs_hbm, w1_hbm, w3_hbm, w2_hbm, ys_hbm,
              xbuf, gbuf, ybuf, w1f, w3f, w2f, xsem, gsem, ysem, wsem):
    nv = nv_ref[0]
    nb = blk_e_ref.shape[0]
    half = D_MODEL // 2
    RB = EXPERT_BLOCK
    ahead_w = W_RING - 2

    def expert_of(blk):
        return blk_e_ref[jnp.minimum(blk, nb - 1)]

    def next_expert_block(blk):
        return jnp.where(blk < nv, blk_next_ref[jnp.minimum(blk, nb - 1)], blk)

    def start_weights(blk, ordinal):
        @pl.when(blk < nv)
        def _():
            for cp in weight_copies(expert_of(blk), lax.rem(ordinal, W_RING)):
                cp.start()

    def row_copies(b, slot):
        r0 = pl.multiple_of(b * RB, RB)
        return (pltpu.make_async_copy(xs_hbm.at[pl.ds(r0, RB)], xbuf.at[slot], xsem.at[slot]),
                pltpu.make_async_copy(gs_hbm.at[pl.ds(r0, RB)], gbuf.at[slot], gsem.at[slot]))

    def out_copy(b, slot):
        r0 = pl.multiple_of(b * RB, RB)
        return pltpu.make_async_copy(ybuf.at[slot], ys_hbm.at[pl.ds(r0, RB)], ysem.at[slot])

    def weight_copies(e, ws):
        return (pltpu.make_async_copy(w1_hbm.at[e], w1f.at[ws], wsem.at[ws, 0]),
                pltpu.make_async_copy(w3_hbm.at[e], w3f.at[ws], wsem.at[ws, 1]),
                pltpu.make_async_copy(w2_hbm.at[e], w2f.at[ws], wsem.at[ws, 2]))

    blk = jnp.int32(0)
    for n in range(ahead_w):
        start_weights(blk, n)
        blk = next_expert_block(blk)
    for i in range(ROW_RING - 2):
        @pl.when(i < nv)
        def _():
            for cp in row_copies(i, i):
                cp.start()

    def enter(b, live, ordinal_prev):
        e = expert_of(b)
        first = live & ((b == 0) | (e != expert_of(jnp.maximum(b - 1, 0))))
        ordinal = jnp.where(first & (b > 0), ordinal_prev + 1, ordinal_prev)

        @pl.when(first)
        def _():
            for cp in weight_copies(e, lax.rem(ordinal, W_RING)):
                cp.wait()
            nxt = b
            for _ in range(ahead_w):
                nxt = next_expert_block(nxt)
            start_weights(nxt, ordinal + ahead_w)

        return ordinal

    def load_block(b, slot, ws):
        valid = lax.broadcasted_iota(i32, (RB, 1), 0) < blk_rows_ref[jnp.minimum(b, nb - 1)]
        x_hi, x_lo = _unpack_pairs(jnp.where(valid, xbuf[slot], 0))
        g = jnp.where(valid, gbuf[slot][:, 0:1], 0.0)
        return x_hi, x_lo, g, ws

    def pair(p, ordinal_prev):
        b0 = 2 * p
        b1 = b0 + 1
        live1 = b1 < nv
        slot0 = lax.rem(b0, ROW_RING)
        slots = (slot0, jnp.where(live1, lax.rem(b1, ROW_RING), slot0))

        for b in (b0 + ROW_RING - 2, b1 + ROW_RING - 2):
            @pl.when(b < nv)
            def _():
                for cp in row_copies(b, lax.rem(b, ROW_RING)):
                    cp.start()

        ord0 = enter(b0, b0 < nv, ordinal_prev)
        ord1 = enter(b1, live1, ord0)
        for cp in row_copies(b0, slots[0]):
            cp.wait()

        @pl.when(live1)
        def _():
            for cp in row_copies(b1, slots[1]):
                cp.wait()

        blocks = [load_block(b0, slots[0], lax.rem(ord0, W_RING)),
                  load_block(jnp.where(live1, b1, b0), slots[1], lax.rem(ord1, W_RING))]
        a = [_dot(xh, w1f[ws, 0:half, :].astype(bf16)) + _dot(xl, w1f[ws, half:, :].astype(bf16))
             for xh, xl, _, ws in blocks]
        c = [_dot(xh, w3f[ws, 0:half, :].astype(bf16)) + _dot(xl, w3f[ws, half:, :].astype(bf16))
             for xh, xl, _, ws in blocks]
        hmid = [((a[i] * jax.nn.sigmoid(a[i])) * c[i]).astype(bf16) for i in range(2)]
        y = [_pack_pairs(_dot(hmid[i], w2f[blocks[i][3]].astype(bf16)) * blocks[i][2]) for i in range(2)]

        for i, (b, live) in enumerate(((b0, b0 < nv), (b1, live1))):
            @pl.when(live & (b >= ROW_RING))
            def _():
                out_copy(b - ROW_RING, slots[i]).wait()

            @pl.when(live)
            def _():
                ybuf[slots[i]] = y[i]
                out_copy(b, slots[i]).start()

        return ord1

    lax.fori_loop(0, (nv + 1) // 2, pair, 0)

    for i in range(1, ROW_RING + 1):
        @pl.when(nv >= i)
        def _():
            out_copy(nv - i, lax.rem(nv - i, ROW_RING)).wait()


def _gmm(xs, gs, w1, w3, w2, blk_e, blk_rows, blk_next, n_valid):
    n_rows, Dw = xs.shape
    D = 2 * Dw
    RB = EXPERT_BLOCK
    hbm = pl.BlockSpec(memory_space=pl.ANY)
    return pl.pallas_call(
        _gmm_body,
        grid_spec=pltpu.PrefetchScalarGridSpec(
            num_scalar_prefetch=4,
            grid=(1,),
            in_specs=[hbm, hbm, hbm, hbm, hbm],
            out_specs=hbm,
            scratch_shapes=[
                pltpu.VMEM((ROW_RING, RB, Dw), i32), pltpu.VMEM((ROW_RING, RB, GATE_ROW), f32),
                pltpu.VMEM((ROW_RING, RB, Dw), i32),
                pltpu.VMEM((W_RING, D, EXPERT_FF), f32), pltpu.VMEM((W_RING, D, EXPERT_FF), f32),
                pltpu.VMEM((W_RING, EXPERT_FF, D), f32),
                pltpu.SemaphoreType.DMA((ROW_RING,)), pltpu.SemaphoreType.DMA((ROW_RING,)),
                pltpu.SemaphoreType.DMA((ROW_RING,)), pltpu.SemaphoreType.DMA((W_RING, 3)),
            ],
        ),
        out_shape=jax.ShapeDtypeStruct((n_rows, Dw), i32),
        compiler_params=pltpu.CompilerParams(dimension_semantics=("arbitrary",)),
        name="moe_gmm",
    )(blk_e, blk_rows, blk_next, n_valid, xs, gs, w1, w3, w2)


def _final_core(x1, h2p, comb, gt, ws1_ref, ws3_ref, ws2_ref):
    half = D_MODEL // 2
    h_hi, h_lo = _unpack_pairs(h2p)
    a = _dot(h_hi, ws1_ref[0:half, :]) + _dot(h_lo, ws1_ref[half:, :])
    c = _dot(h_hi, ws3_ref[0:half, :]) + _dot(h_lo, ws3_ref[half:, :])
    shared = _dot(((a * jax.nn.sigmoid(a)) * c).astype(bf16), ws2_ref[...])
    return x1, comb + shared, gt


def _final_prompt_body(x1_ref, h2_ref, cb_ref, gt_ref, ws1_ref, ws3_ref, ws2_ref, y_ref):
    x1, ffn, gt = _final_core(x1_ref[...], h2_ref[...], cb_ref[...], gt_ref[0], ws1_ref, ws3_ref, ws2_ref)
    y_ref[0] = x1 + gt * ffn


def _final_sample_body(x1_ref, h2_ref, cb_ref, gt_ref, ws1_ref, ws3_ref, ws2_ref, y_ref):
    x1, ffn, gt = _final_core(x1_ref[...], h2_ref[...], cb_ref[...], gt_ref[...], ws1_ref, ws3_ref, ws2_ref)
    shp = y_ref.shape
    y_ref[...] = x1.reshape(shp) + gt * ffn.reshape(shp)


def _final_prompt(x1, h2, comb, mod3, ws1, ws3, ws2, B, L, tm=1024):
    D = D_MODEL
    nt = L // tm
    full = lambda shape: pl.BlockSpec(shape, lambda b, j: (0,) * len(shape))
    rows = pl.BlockSpec((tm, D), lambda b, j: (b * nt + j, 0))
    words = pl.BlockSpec((tm, D // 2), lambda b, j: (b * nt + j, 0))
    return pl.pallas_call(
        _final_prompt_body,
        grid=(B, nt),
        in_specs=[rows, words, rows, pl.BlockSpec((1, 1, D), lambda b, j: (b, 0, 5)),
                  full((D, EXPERT_FF)), full((D, EXPERT_FF)), full((EXPERT_FF, D))],
        out_specs=pl.BlockSpec((1, tm, D), lambda b, j: (b, j, 0)),
        out_shape=jax.ShapeDtypeStruct((B, L, D), f32),
        name="final_prompt",
    )(x1, h2, comb, mod3, ws1, ws3, ws2)


def _final_sample(x1, h2, comb, mod3, ws1, ws3, ws2, B, ls, row0, bt=64):
    D = D_MODEL
    n = bt * ls
    blk0 = row0 // n
    full = lambda shape: pl.BlockSpec(shape, lambda i: (0,) * len(shape))
    rows = pl.BlockSpec((n, D), lambda i: (i, 0))
    words = pl.BlockSpec((n, D // 2), lambda i: (i, 0))
    comb_rows = pl.BlockSpec((n, D), lambda i: (blk0 + i, 0))
    return pl.pallas_call(
        _final_sample_body,
        grid=(B // bt,),
        in_specs=[rows, words, comb_rows, pl.BlockSpec((bt, 1, D), lambda i: (i, 0, 5)),
                  full((D, EXPERT_FF)), full((D, EXPERT_FF)), full((EXPERT_FF, D))],
        out_specs=pl.BlockSpec((bt, ls, D), lambda i: (i, 0, 0)),
        out_shape=jax.ShapeDtypeStruct((B, ls, D), f32),
        name="final_sample",
    )(x1, h2, comb, mod3, ws1, ws3, ws2)


def kernel(x_prompt, x_sample, state_pool, cache_swa_k, cache_swa_v, c_prompt, c_sample, w_ada, b_ada,
           g_attn_norm, w_in, g_q, g_k, w_pool, pool_scale, w_out, attn_sinks, rel_bias, g_ffn_norm,
           w_router, router_bias, w1, w3, w2, ws1, ws3, ws2):
    B, L, D = x_prompt.shape
    BS, LS, _ = x_sample.shape
    depth = w_ada.shape[0]
    assert depth == 1
    W = cache_swa_k.shape[2]
    tp, ts = B * L, BS * LS
    T = tp + ts
    n_rows = (T * TOP_K // EXPERT_BLOCK + N_EXPERTS) * EXPERT_BLOCK
    nb = n_rows // EXPERT_BLOCK

    g_attn = g_attn_norm[0].reshape(1, D)
    g_ffn = g_ffn_norm[0].reshape(1, D)
    w_in_b = w_in[0].astype(bf16)
    w_out_b = w_out[0].astype(bf16)
    w_pool_b = w_pool[0].astype(bf16)
    ps = pool_scale[0].reshape(1, POOL_W)
    gqk = jnp.concatenate([jnp.tile(g_q[0], N_HEADS), jnp.tile(g_k[0], N_KV_HEADS)]).reshape(1, QK_W)
    head_of = np.arange(QK_W) // HEAD_DIM
    bd = jnp.asarray((head_of[:, None] == head_of[None, :]).astype(np.float32), bf16)
    wr_t = w_router[0].T
    wr_hi = wr_t.astype(bf16)
    wr_lo = (wr_t - wr_hi.astype(f32)).astype(bf16)
    ws1_b, ws3_b, ws2_b = ws1[0].astype(bf16), ws3[0].astype(bf16), ws2[0].astype(bf16)
    sinks = attn_sinks[0]

    mod = _ada(jnp.concatenate([c_prompt, c_sample], axis=0), w_ada[0], b_ada[0])
    mod_p = mod[:B].reshape(B, 1, 6 * D)
    mod_s = mod[B:].reshape(BS, 1, 6 * D)

    dist_p = np.arange(WINDOW)[:, None] + WINDOW - np.arange(2 * WINDOW)[None, :]
    bias_p = _relbias(rel_bias, dist_p)
    bias_p = bias_p.reshape(N_KV_HEADS, GQA, WINDOW, 2 * WINDOW).transpose(0, 2, 1, 3).reshape(
        N_KV_HEADS, WINDOW, GQA * 2 * WINDOW)
    dist_s = np.arange(LS)[:, None] + W - np.arange(W + LS)[None, :]
    bias_s = _relbias(rel_bias, dist_s)
    bias_s_buf = bias_s[:, :, :W].reshape(N_KV_HEADS, GQA * LS, W)
    bias_s_new = bias_s[:, :, W:].reshape(N_KV_HEADS, GQA * LS, LS)
    sink_col = jnp.repeat(sinks, LS).reshape(N_KV_HEADS, GQA * LS, 1)

    q_p, k_p, v_p, po_p, new_pool_p, kc_p, vc_p = _inproj_prompt(
        x_prompt, mod_p, g_attn, w_in_b, gqk, bd, w_pool_b, ps)
    q_s, k_s, v_s, po_s, new_pool_s = _inproj_sample(
        x_sample, mod_s, g_attn, w_in_b, gqk, bd, w_pool_b, ps, state_pool[0], PAST_LEN)
    at_p = _attn_prompt(q_p, k_p, v_p, bias_p, sinks)
    at_s, nk_s, nv_s = _attn_sample(
        q_s.reshape(BS, LS, ATTN_W), cache_swa_k[0].reshape(BS, W, KV_W), cache_swa_v[0].reshape(BS, W, KV_W),
        k_s.reshape(BS, LS, KV_W), v_s.reshape(BS, LS, KV_W), bias_s_buf, bias_s_new, sink_col)

    x1_p, h2_p, lg_p = _outproj_prompt(po_p, at_p, x_prompt, mod_p, g_ffn, w_out_b, wr_hi, wr_lo)
    x1_s, h2_s, lg_s = _outproj_sample(po_s, at_s.reshape(ts, ATTN_W), x_sample, mod_s, g_ffn, w_out_b,
                                       wr_hi, wr_lo)

    idx, rank, gate, counts = _route(lg_p, lg_s, router_bias[0])
    counts = counts.reshape(N_EXPERTS).astype(i32)
    padded = (counts + EXPERT_BLOCK - 1) // EXPERT_BLOCK * EXPERT_BLOCK
    pad_end = jnp.cumsum(padded)
    pad_start = pad_end - padded
    dest = _dest_rows(idx, rank, pad_start)
    n_valid = (pad_end[-1] // EXPERT_BLOCK).astype(i32).reshape(1)
    blk_row0 = jnp.arange(nb, dtype=i32) * EXPERT_BLOCK
    blk_e = jnp.minimum(jnp.sum(blk_row0[:, None] >= pad_end[None, :], axis=1), N_EXPERTS - 1).astype(i32)
    own = jnp.arange(N_EXPERTS, dtype=i32)[None, :] == blk_e[:, None]
    blk_cnt = jnp.sum(jnp.where(own, counts[None, :], 0), axis=1)
    blk_start = jnp.sum(jnp.where(own, pad_start[None, :], 0), axis=1)
    blk_rows = jnp.clip(blk_cnt - (blk_row0 - blk_start), 0, EXPERT_BLOCK).astype(i32)
    blk_next = (jnp.sum(jnp.where(own, pad_end[None, :], 0), axis=1) // EXPERT_BLOCK).astype(i32)

    xs, gs = _dispatch(h2_p, h2_s, dest, gate, n_rows)
    ys = _gmm(xs, gs, w1[0], w3[0], w2[0], blk_e, blk_rows, blk_next, n_valid)
    comb = _combine(ys, dest)
    y_p = _final_prompt(x1_p, h2_p, comb, mod_p, ws1_b, ws3_b, ws2_b, B, L)
    y_s = _final_sample(x1_s, h2_s, comb, mod_s, ws1_b, ws3_b, ws2_b, BS, LS, tp)

    return (y_p, y_s, new_pool_p[None], kc_p.reshape(1, B, WINDOW, N_KV_HEADS, HEAD_DIM),
            vc_p.reshape(1, B, WINDOW, N_KV_HEADS, HEAD_DIM), new_pool_s[None],
            nk_s.reshape(1, BS, W, N_KV_HEADS, HEAD_DIM), nv_s.reshape(1, BS, W, N_KV_HEADS, HEAD_DIM))
```

```python
import functools
import math

import numpy as np
import jax
import jax.numpy as jnp
from jax import lax
from jax.experimental import pallas as pl
from jax.experimental.pallas import tpu as pltpu
from jax.experimental.pallas import tpu_sc as plsc

f32 = jnp.float32
bf16 = jnp.bfloat16
i32 = jnp.int32

D_MODEL = 1024
PAST_LEN = 8192
POOL_W = 512
POOL_WINDOWS = (2, 4, 8, 16)
POOL_GC = 128
POOL_BUF = 15
ATTN_W = 512
HEAD_DIM = 64
N_HEADS = 8
N_KV_HEADS = 2
GQA = 4
KV_W = 128
WINDOW = 128
NUM_BUCKETS = 32
MAX_EXACT = 16
REL_MAX_DIST = 128
N_EXPERTS = 256
N_EXPERT_GROUPS = 8
GROUP_SIZE = 32
TOPK_GROUPS = 4
TOP_K = 8
EXPERT_FF = 256
ROUTED_SCALE = 2.5
EXPERT_BLOCK = 128
EPS = 1e-6
NEG_INF = -1e30
QKV_W = POOL_W + ATTN_W + 2 * KV_W
QK_W = ATTN_W + KV_W
HIST = 16

SC_WORKERS = 32
SC_LANES = 16
GATE_ROW = 128


def _dot(a, b):
    return jnp.dot(a, b, preferred_element_type=f32)


def _dot_t(a, b):
    return lax.dot_general(a, b, (((1,), (1,)), ((), ())), preferred_element_type=f32)


def _split_bf16(a):
    hi = a.astype(bf16)
    lo = (a - hi.astype(f32)).astype(bf16)
    return hi, lo


ROW_RING = 8
W_RING = 5
HI_MASK = -65536


def _pack_pairs(a):
    h = a.shape[1] // 2
    hi = lax.bitcast_convert_type(a[:, :h].astype(bf16).astype(f32), i32)
    lo = lax.bitcast_convert_type(a[:, h:].astype(bf16).astype(f32), i32)
    return hi | lax.shift_right_logical(lo, 16)


def _unpack_pairs(w):
    hi = lax.bitcast_convert_type(w & HI_MASK, f32).astype(bf16)
    lo = lax.bitcast_convert_type(lax.shift_left(w, 16), f32).astype(bf16)
    return hi, lo


def _mod_norm(x, g, sc, sh):
    ms = jnp.mean(x * x, axis=-1, keepdims=True)
    y = x * lax.rsqrt(ms + EPS)
    return (y * g) * (1.0 + sc) + sh


def _ada_body(c_ref, w_ref, b_ref, o_ref):
    c = c_ref[...]
    a = (c * jax.nn.sigmoid(c)).astype(bf16)
    o_ref[...] = _dot(a, w_ref[...].astype(bf16)) + b_ref[...]


def _ada(c, w_ada, b_ada):
    n = c.shape[0]
    tn = 1024
    return pl.pallas_call(
        _ada_body,
        grid=(6 * D_MODEL // tn,),
        in_specs=[
            pl.BlockSpec((n, D_MODEL), lambda j: (0, 0)),
            pl.BlockSpec((D_MODEL, tn), lambda j: (0, j)),
            pl.BlockSpec((1, tn), lambda j: (0, j)),
        ],
        out_specs=pl.BlockSpec((n, tn), lambda j: (0, j)),
        out_shape=jax.ShapeDtypeStruct((n, 6 * D_MODEL), f32),
        name="ada_mod",
    )(c, w_ada, b_ada.reshape(1, -1))


def _relbias_body(table_ref, bucket_ref, o_ref):
    bucket = bucket_ref[...]
    for h in range(N_HEADS):
        acc = jnp.zeros(bucket.shape, f32)
        for b in range(NUM_BUCKETS):
            acc = jnp.where(bucket == b, table_ref[b, h], acc)
        o_ref[h] = acc


def _rel_buckets(dist):
    n = np.maximum(dist, 0)
    nf = np.maximum(n, 1).astype(np.float64)
    large = MAX_EXACT + (np.log(nf / MAX_EXACT) / math.log(REL_MAX_DIST / MAX_EXACT)
                         * (NUM_BUCKETS - MAX_EXACT)).astype(np.int32)
    return np.where(n < MAX_EXACT, n, np.minimum(large, NUM_BUCKETS - 1)).astype(np.int32)


def _relbias(table, dist):
    lq, lk = dist.shape
    return pl.pallas_call(
        _relbias_body,
        in_specs=[
            pl.BlockSpec(memory_space=pltpu.SMEM),
            pl.BlockSpec((lq, lk), lambda: (0, 0)),
        ],
        out_specs=pl.BlockSpec((N_HEADS, lq, lk), lambda: (0, 0, 0)),
        out_shape=jax.ShapeDtypeStruct((N_HEADS, lq, lk), f32),
        name="rel_bias",
    )(table, jnp.asarray(_rel_buckets(dist)))


def _qkv_from_h(hs, w_ref, gqk_ref, bd_ref):
    us = [_dot(h.astype(bf16), w_ref[...]) for h in hs]
    qks = [u[:, POOL_W:POOL_W + QK_W] for u in us]
    sq = [_split_bf16(qk * qk) for qk in qks]
    bd = bd_ref[...]
    ss = [_dot(y_hi, bd) + _dot(y_lo, bd) for y_hi, y_lo in sq]
    qkn = [(qk * lax.rsqrt(s * (1.0 / HEAD_DIM) + EPS)) * gqk_ref[...] for qk, s in zip(qks, ss)]
    return [(u[:, :POOL_W], n[:, :ATTN_W] * (HEAD_DIM ** -0.5), n[:, ATTN_W:], u[:, POOL_W + QK_W:])
            for u, n in zip(us, qkn)]


def _inproj_prompt_body(x_ref, sh_ref, sc_ref, g_ref, w_ref, gqk_ref, bd_ref, wp_ref, ps_ref,
                        q_ref, k_ref, v_ref, po_ref, np_ref, kc_ref, vc_ref, hist, wsum, *, tl, nt):
    j = pl.program_id(1)
    nsub = 2
    ts = tl // nsub
    hs = [_mod_norm(x_ref[0, i * ts:(i + 1) * ts, :], g_ref[...], sc_ref[0], sh_ref[0]) for i in range(nsub)]
    parts = _qkv_from_h(hs, w_ref, gqk_ref, bd_ref)

    base = 2 * HIST
    end = base + tl

    @pl.when(j == 0)
    def _():
        hist[0:base, :] = jnp.zeros((base, POOL_W), f32)
        wsum[0:HIST, :] = jnp.zeros((HIST, POOL_W), f32)

    for i, (up, q, k, v) in enumerate(parts):
        rows = slice(i * ts, (i + 1) * ts)
        q_ref[0, rows, :] = q.astype(bf16)
        k_ref[0, rows, :] = k.astype(bf16)
        v_ref[0, rows, :] = v.astype(bf16)
        hist[base + i * ts:base + (i + 1) * ts, :] = up

    @pl.when(j == nt - 1)
    def _():
        kc_ref[0] = parts[-1][2][ts - WINDOW:, :]
        vc_ref[0] = parts[-1][3][ts - WINDOW:, :]

    pos = j * tl + lax.broadcasted_iota(i32, (tl, 1), 0)
    src = hist
    for p, w in enumerate(POOL_WINDOWS):
        sh = w // 2
        live = slice(p * POOL_GC, POOL_W)
        summed = src[HIST:end, live] + src[HIST - sh:end - sh, live]
        wsum[HIST:end, live] = summed
        src = wsum
        lanes = slice(p * POOL_GC, (p + 1) * POOL_GC)
        cur = hist[base:end, lanes]
        cnt = jnp.minimum(w, pos + 1).astype(f32)
        d = summed[HIST:, 0:POOL_GC] / cnt - cur
        yg = _dot(d.astype(bf16), wp_ref[p]) * ps_ref[:, lanes]
        po_ref[0, :, lanes] = yg.astype(bf16)

    @pl.when(j == nt - 1)
    def _():
        np_ref[0] = hist[end - POOL_BUF:end, :]

    hist[HIST:base, :] = hist[end - HIST:end, :]


def _inproj_prompt(x, mod3, g_attn, w_in, gqk, bd, w_pool, pool_scale, tl=512):
    B, L, D = x.shape
    nt = L // tl
    full = lambda shape: pl.BlockSpec(shape, lambda b, j: (0,) * len(shape))
    return pl.pallas_call(
        functools.partial(_inproj_prompt_body, tl=tl, nt=nt),
        grid=(B, nt),
        in_specs=[
            pl.BlockSpec((1, tl, D), lambda b, j: (b, j, 0)),
            pl.BlockSpec((1, 1, D), lambda b, j: (b, 0, 0)),
            pl.BlockSpec((1, 1, D), lambda b, j: (b, 0, 1)),
            full((1, D)),
            full((D, QKV_W)),
            full((1, QK_W)),
            full((QK_W, QK_W)),
            full((4, POOL_GC, POOL_GC)),
            full((1, POOL_W)),
        ],
        out_specs=[
            pl.BlockSpec((1, tl, ATTN_W), lambda b, j: (b, j, 0)),
            pl.BlockSpec((1, tl, KV_W), lambda b, j: (b, j, 0)),
            pl.BlockSpec((1, tl, KV_W), lambda b, j: (b, j, 0)),
            pl.BlockSpec((1, tl, POOL_W), lambda b, j: (b, j, 0)),
            pl.BlockSpec((1, POOL_BUF, POOL_W), lambda b, j: (b, 0, 0)),
            pl.BlockSpec((1, WINDOW, KV_W), lambda b, j: (b, 0, 0)),
            pl.BlockSpec((1, WINDOW, KV_W), lambda b, j: (b, 0, 0)),
        ],
        out_shape=[
            jax.ShapeDtypeStruct((B, L, ATTN_W), bf16),
            jax.ShapeDtypeStruct((B, L, KV_W), bf16),
            jax.ShapeDtypeStruct((B, L, KV_W), bf16),
            jax.ShapeDtypeStruct((B, L, POOL_W), bf16),
            jax.ShapeDtypeStruct((B, POOL_BUF, POOL_W), f32),
            jax.ShapeDtypeStruct((B, WINDOW, KV_W), f32),
            jax.ShapeDtypeStruct((B, WINDOW, KV_W), f32),
        ],
        scratch_shapes=[pltpu.VMEM((2 * HIST + tl, POOL_W), f32), pltpu.VMEM((2 * HIST + tl, POOL_W), f32)],
        compiler_params=pltpu.CompilerParams(dimension_semantics=("arbitrary", "arbitrary")),
        name="inproj_prompt",
    )(x, mod3, mod3, g_attn, w_in, gqk, bd, w_pool, pool_scale)


def _inproj_sample_body(x_ref, sh_ref, sc_ref, g_ref, w_ref, gqk_ref, bd_ref, wp_ref, ps_ref, st_ref,
                        q_ref, k_ref, v_ref, po_ref, np_ref, ext, *, bt, ls, pos0):
    n = bt * ls
    h3 = _mod_norm(x_ref[...], g_ref[...][None], sc_ref[...], sh_ref[...])
    (up, q, k, v), = _qkv_from_h([h3.reshape(n, D_MODEL)], w_ref, gqk_ref, bd_ref)
    q_ref[...] = q.astype(bf16)
    k_ref[...] = k
    v_ref[...] = v

    ext[:, 1:HIST, :] = st_ref[...]
    ext[:, HIST:HIST + ls, :] = up.reshape(bt, ls, POOL_W)
    pos = pos0 + lax.broadcasted_iota(i32, (1, ls, 1), 1)
    for g, w in enumerate(POOL_WINDOWS):
        lanes = slice(g * POOL_GC, (g + 1) * POOL_GC)
        cur = ext[:, HIST:HIST + ls, lanes]
        acc = cur
        for s in range(1, w):
            acc = acc + ext[:, HIST - s:HIST - s + ls, lanes]
        cnt = jnp.minimum(w, pos + 1).astype(f32)
        d = (acc / cnt - cur).reshape(n, POOL_GC)
        yg = _dot(d.astype(bf16), wp_ref[g]) * ps_ref[:, lanes]
        po_ref[:, lanes] = yg.astype(bf16)
    np_ref[...] = ext[:, ls + 1:ls + HIST, :]


def _inproj_sample(x, mod3, g_attn, w_in, gqk, bd, w_pool, pool_scale, state, pos0, bt=64):
    B, ls, D = x.shape
    n = bt * ls
    full = lambda shape: pl.BlockSpec(shape, lambda i: (0,) * len(shape))
    return pl.pallas_call(
        functools.partial(_inproj_sample_body, bt=bt, ls=ls, pos0=pos0),
        grid=(B // bt,),
        in_specs=[
            pl.BlockSpec((bt, ls, D), lambda i: (i, 0, 0)),
            pl.BlockSpec((bt, 1, D), lambda i: (i, 0, 0)),
            pl.BlockSpec((bt, 1, D), lambda i: (i, 0, 1)),
            full((1, D)),
            full((D, QKV_W)),
            full((1, QK_W)),
            full((QK_W, QK_W)),
            full((4, POOL_GC, POOL_GC)),
            full((1, POOL_W)),
            pl.BlockSpec((bt, POOL_BUF, POOL_W), lambda i: (i, 0, 0)),
        ],
        out_specs=[
            pl.BlockSpec((n, ATTN_W), lambda i: (i, 0)),
            pl.BlockSpec((n, KV_W), lambda i: (i, 0)),
            pl.BlockSpec((n, KV_W), lambda i: (i, 0)),
            pl.BlockSpec((n, POOL_W), lambda i: (i, 0)),
            pl.BlockSpec((bt, POOL_BUF, POOL_W), lambda i: (i, 0, 0)),
        ],
        out_shape=[
            jax.ShapeDtypeStruct((B * ls, ATTN_W), bf16),
            jax.ShapeDtypeStruct((B * ls, KV_W), f32),
            jax.ShapeDtypeStruct((B * ls, KV_W), f32),
            jax.ShapeDtypeStruct((B * ls, POOL_W), bf16),
            jax.ShapeDtypeStruct((B, POOL_BUF, POOL_W), f32),
        ],
        scratch_shapes=[pltpu.VMEM((bt, HIST + ls, POOL_W), f32)],
        name="inproj_sample",
    )(x, mod3, mod3, g_attn, w_in, gqk, bd, w_pool, pool_scale, state)


def _softmax_sink(parts, sink):
    m = sink
    for s in parts:
        m = jnp.maximum(m, jnp.max(s, axis=-1, keepdims=True))
    ps = [jnp.exp(s - m) for s in parts]
    denom = jnp.exp(sink - m)
    for p in ps:
        denom = denom + jnp.sum(p, axis=-1, keepdims=True)
    inv = 1.0 / denom
    return [(p * inv).astype(bf16) for p in ps]


def _attn_prompt_body(sinks_ref, q_ref, kp_ref, kc_ref, vp_ref, vc_ref, bias_ref, mask_ref, o_ref):
    j = pl.program_id(1)
    nk = 2 * WINDOW
    qw = GQA * HEAD_DIM
    nq = q_ref.shape[1] // WINDOW
    kall = jnp.concatenate([kp_ref[0], kc_ref[0]], axis=0)
    vall = jnp.concatenate([vp_ref[0], vc_ref[0]], axis=0)
    lane_group = lax.broadcasted_iota(i32, (nk, 2 * KV_W), 1) // HEAD_DIM
    first_has_prev = jnp.minimum(j, 1)

    def spread(t):
        t0 = jnp.concatenate([t, t], axis=1)
        return t0, pltpu.roll(t0, HEAD_DIM, 1)

    def blockdiag(t01, kv):
        t0, t1 = t01
        return jnp.concatenate(
            [jnp.where(lane_group == g, t0 if g % 2 == kv else t1, jnp.zeros_like(t0)) for g in range(GQA)],
            axis=0)

    chains = [(qb, kv) for qb in range(nq) for kv in range(N_KV_HEADS)]
    ksp = [spread(kall[qb * WINDOW:qb * WINDOW + nk]) for qb in range(nq)]
    vsp = [spread(vall[qb * WINDOW:qb * WINDOW + nk]) for qb in range(nq)]
    valid = [mask_ref[first_has_prev] > 0.5] + [mask_ref[1] > 0.5] * (nq - 1)
    s = [_dot_t(q_ref[0, qb * WINDOW:(qb + 1) * WINDOW, kv * qw:(kv + 1) * qw], blockdiag(ksp[qb], kv))
         for qb, kv in chains]
    s = [jnp.where(valid[qb], s[c] + bias_ref[kv], NEG_INF) for c, (qb, kv) in enumerate(chains)]
    p = [jnp.concatenate([_softmax_sink([s[c][:, g * nk:(g + 1) * nk]], sinks_ref[kv * GQA + g])[0]
                          for g in range(GQA)], axis=1) for c, (qb, kv) in enumerate(chains)]
    o = [_dot(p[c], blockdiag(vsp[qb], kv)) for c, (qb, kv) in enumerate(chains)]
    for c, (qb, kv) in enumerate(chains):
        o_ref[0, qb * WINDOW:(qb + 1) * WINDOW, kv * qw:(kv + 1) * qw] = o[c].astype(bf16)


def _attn_prompt(q, k, v, bias, sinks):
    B, L, _ = q.shape
    nq = 4
    nb = L // (nq * WINDOW)
    cur = lambda b, j: (b, j, 0)
    prev = lambda b, j: (b, jnp.maximum(nq * j - 1, 0), 0)
    qi = np.arange(WINDOW)[:, None]
    kc = np.arange(2 * WINDOW)[None, :]
    own = (kc >= WINDOW) & (kc - WINDOW <= qi)
    prv = (kc < WINDOW) & (kc > qi)
    mask = np.stack([np.tile(own, (1, GQA)), np.tile(own | prv, (1, GQA))]).astype(np.float32)
    return pl.pallas_call(
        _attn_prompt_body,
        grid=(B, nb),
        in_specs=[
            pl.BlockSpec(memory_space=pltpu.SMEM),
            pl.BlockSpec((1, nq * WINDOW, ATTN_W), cur),
            pl.BlockSpec((1, WINDOW, KV_W), prev),
            pl.BlockSpec((1, nq * WINDOW, KV_W), cur),
            pl.BlockSpec((1, WINDOW, KV_W), prev),
            pl.BlockSpec((1, nq * WINDOW, KV_W), cur),
            pl.BlockSpec((N_KV_HEADS, WINDOW, GQA * 2 * WINDOW), lambda b, j: (0, 0, 0)),
            pl.BlockSpec((2, WINDOW, GQA * 2 * WINDOW), lambda b, j: (0, 0, 0)),
        ],
        out_specs=pl.BlockSpec((1, nq * WINDOW, ATTN_W), cur),
        out_shape=jax.ShapeDtypeStruct((B, L, ATTN_W), bf16),
        name="attn_prompt",
    )(sinks, q, k, k, v, v, bias, jnp.asarray(mask))


def _attn_sample_body(q_ref, kb_ref, vb_ref, kn_ref, vn_ref, bb_ref, bn_ref, sink_ref,
                      o_ref, nk_ref, nv_ref, *, bb, ls):
    W = kb_ref.shape[1]
    rows = GQA * ls
    qi = lax.broadcasted_iota(i32, (rows, W), 0) % ls
    kj = lax.broadcasted_iota(i32, (rows, W), 1)
    valid_buf = kj > qi
    qi2 = lax.broadcasted_iota(i32, (rows, ls), 0) % ls
    kj2 = lax.broadcasted_iota(i32, (rows, ls), 1)
    valid_new = kj2 <= qi2

    nbat = 4
    ks = [slice(kv * HEAD_DIM, (kv + 1) * HEAD_DIM) for kv in range(N_KV_HEADS)]

    def group(i, carry):
        bs = [i * nbat + u for u in range(nbat)]
        chains = [(u, kv) for u in range(nbat) for kv in range(N_KV_HEADS)]
        qb = [q_ref[b] for b in bs]
        kbuf = [kb_ref[b] for b in bs]
        vbuf = [vb_ref[b] for b in bs]
        knew = [kn_ref[b] for b in bs]
        vnew = [vn_ref[b] for b in bs]
        qg = [jnp.concatenate([qb[u][:, (kv * GQA + g) * HEAD_DIM:(kv * GQA + g + 1) * HEAD_DIM]
                               for g in range(GQA)], axis=0) for u, kv in chains]
        s_buf = [_dot_t(qg[c], kbuf[u][:, ks[kv]].astype(bf16)) for c, (u, kv) in enumerate(chains)]
        s_new = [_dot_t(qg[c], knew[u][:, ks[kv]].astype(bf16)) for c, (u, kv) in enumerate(chains)]
        s_buf = [jnp.where(valid_buf, s_buf[c] + bb_ref[kv], NEG_INF) for c, (u, kv) in enumerate(chains)]
        s_new = [jnp.where(valid_new, s_new[c] + bn_ref[kv], NEG_INF) for c, (u, kv) in enumerate(chains)]
        probs = [_softmax_sink([s_buf[c], s_new[c]], sink_ref[kv]) for c, (u, kv) in enumerate(chains)]
        outs = [_dot(probs[c][0], vbuf[u][:, ks[kv]].astype(bf16)) + _dot(probs[c][1], vnew[u][:, ks[kv]].astype(bf16))
                for c, (u, kv) in enumerate(chains)]
        for u, b in enumerate(bs):
            heads = [outs[u * N_KV_HEADS + kv][g * ls:(g + 1) * ls] for kv in range(N_KV_HEADS) for g in range(GQA)]
            o_ref[b] = jnp.concatenate(heads, axis=-1).astype(bf16)
            nk_ref[b, 0:W - ls, :] = kbuf[u][ls:, :]
            nk_ref[b, W - ls:W, :] = knew[u]
            nv_ref[b, 0:W - ls, :] = vbuf[u][ls:, :]
            nv_ref[b, W - ls:W, :] = vnew[u]
        return carry

    lax.fori_loop(0, bb // nbat, group, 0)


def _attn_sample(q, k_buf, v_buf, k_new, v_new, bias_buf, bias_new, sink_col, bb=16):
    B, ls, _ = q.shape
    W = k_buf.shape[1]
    rows = GQA * ls
    blk = lambda shape: pl.BlockSpec(shape, lambda i: (i, 0, 0))
    full = lambda shape: pl.BlockSpec(shape, lambda i: (0, 0, 0))
    return pl.pallas_call(
        functools.partial(_attn_sample_body, bb=bb, ls=ls),
        grid=(B // bb,),
        in_specs=[
            blk((bb, ls, ATTN_W)),
            blk((bb, W, KV_W)),
            blk((bb, W, KV_W)),
            blk((bb, ls, KV_W)),
            blk((bb, ls, KV_W)),
            full((N_KV_HEADS, rows, W)),
            full((N_KV_HEADS, rows, ls)),
            full((N_KV_HEADS, rows, 1)),
        ],
        out_specs=[blk((bb, ls, ATTN_W)), blk((bb, W, KV_W)), blk((bb, W, KV_W))],
        out_shape=[
            jax.ShapeDtypeStruct((B, ls, ATTN_W), bf16),
            jax.ShapeDtypeStruct((B, W, KV_W), f32),
            jax.ShapeDtypeStruct((B, W, KV_W), f32),
        ],
        name="attn_sample",
    )(q, k_buf, v_buf, k_new, v_new, bias_buf, bias_new, sink_col)


def _outproj_core(po, at, x, gt, sc, sh, g_ref, wo_ref, wrh_ref, wrl_ref):
    mixo = _dot(po, wo_ref[0:POOL_W, :]) + _dot(at, wo_ref[POOL_W:, :])
    x1 = x + gt * mixo.reshape(x.shape)
    h2 = _mod_norm(x1, g_ref[...].reshape((1,) * (x.ndim - 1) + (D_MODEL,)), sc, sh).reshape(-1, D_MODEL)
    h_hi, h_lo = _split_bf16(h2)
    wh = wrh_ref[...]
    logits = _dot_t(wh, h_hi) + (_dot_t(wh, h_lo) + _dot_t(wrl_ref[...], h_hi))
    return x1, _pack_pairs(h2), logits


def _outproj_prompt_body(po_ref, at_ref, x_ref, gt_ref, sc_ref, sh_ref, g_ref, wo_ref, wrh_ref, wrl_ref,
                         x1_ref, h2_ref, lg_ref):
    x1, h2p, logits = _outproj_core(po_ref[0], at_ref[0], x_ref[0], gt_ref[0], sc_ref[0], sh_ref[0],
                                    g_ref, wo_ref, wrh_ref, wrl_ref)
    x1_ref[...] = x1
    h2_ref[...] = h2p
    lg_ref[...] = logits


def _outproj_sample_body(po_ref, at_ref, x_ref, gt_ref, sc_ref, sh_ref, g_ref, wo_ref, wrh_ref, wrl_ref,
                         x1_ref, h2_ref, lg_ref):
    x1, h2p, logits = _outproj_core(po_ref[...], at_ref[...], x_ref[...], gt_ref[...], sc_ref[...], sh_ref[...],
                                    g_ref, wo_ref, wrh_ref, wrl_ref)
    x1_ref[...] = x1.reshape(-1, D_MODEL)
    h2_ref[...] = h2p
    lg_ref[...] = logits


def _outproj_prompt(po, at, x, mod3, g_ffn, w_out, wr_hi, wr_lo, tm=1024):
    B, L, D = x.shape
    nt = L // tm
    n_tok = B * L
    full = lambda shape: pl.BlockSpec(shape, lambda b, j: (0,) * len(shape))
    modspec = lambda c: pl.BlockSpec((1, 1, D), lambda b, j: (b, 0, c))
    return pl.pallas_call(
        _outproj_prompt_body,
        grid=(B, nt),
        in_specs=[
            pl.BlockSpec((1, tm, POOL_W), lambda b, j: (b, j, 0)),
            pl.BlockSpec((1, tm, ATTN_W), lambda b, j: (b, j, 0)),
            pl.BlockSpec((1, tm, D), lambda b, j: (b, j, 0)),
            modspec(2), modspec(4), modspec(3),
            full((1, D)), full((D, D)), full((N_EXPERTS, D)), full((N_EXPERTS, D)),
        ],
        out_specs=[
            pl.BlockSpec((tm, D), lambda b, j: (b * nt + j, 0)),
            pl.BlockSpec((tm, D // 2), lambda b, j: (b * nt + j, 0)),
            pl.BlockSpec((N_EXPERTS, tm), lambda b, j: (0, b * nt + j)),
        ],
        out_shape=[
            jax.ShapeDtypeStruct((n_tok, D), f32),
            jax.ShapeDtypeStruct((n_tok, D // 2), i32),
            jax.ShapeDtypeStruct((N_EXPERTS, n_tok), f32),
        ],
        name="outproj_prompt",
    )(po, at, x, mod3, mod3, mod3, g_ffn, w_out, wr_hi, wr_lo)


def _outproj_sample(po, at, x, mod3, g_ffn, w_out, wr_hi, wr_lo, bt=64):
    B, ls, D = x.shape
    n = bt * ls
    full = lambda shape: pl.BlockSpec(shape, lambda i: (0,) * len(shape))
    modspec = lambda c: pl.BlockSpec((bt, 1, D), lambda i: (i, 0, c))
    return pl.pallas_call(
        _outproj_sample_body,
        grid=(B // bt,),
        in_specs=[
            pl.BlockSpec((n, POOL_W), lambda i: (i, 0)),
            pl.BlockSpec((n, ATTN_W), lambda i: (i, 0)),
            pl.BlockSpec((bt, ls, D), lambda i: (i, 0, 0)),
            modspec(2), modspec(4), modspec(3),
            full((1, D)), full((D, D)), full((N_EXPERTS, D)), full((N_EXPERTS, D)),
        ],
        out_specs=[
            pl.BlockSpec((n, D), lambda i: (i, 0)),
            pl.BlockSpec((n, D // 2), lambda i: (i, 0)),
            pl.BlockSpec((N_EXPERTS, n), lambda i: (0, i)),
        ],
        out_shape=[
            jax.ShapeDtypeStruct((B * ls, D), f32),
            jax.ShapeDtypeStruct((B * ls, D // 2), i32),
            jax.ShapeDtypeStruct((N_EXPERTS, B * ls), f32),
        ],
        name="outproj_sample",
    )(po, at, x, mod3, mod3, mod3, g_ffn, w_out, wr_hi, wr_lo)


def _route_body(lga_ref, lgb_ref, rb_ref, tri_ref, idx_ref, rank_ref, gate_ref, cnt_ref, carry, *, tr, nsteps, na):
    step = pl.program_id(0)

    @pl.when(step == 0)
    def _():
        carry[...] = jnp.zeros(carry.shape, f32)

    logits = jnp.where(step < na, lga_ref[...], lgb_ref[...])
    s = jax.nn.sigmoid(logits)
    sb = s + rb_ref[...]
    e_iota = lax.broadcasted_iota(i32, (N_EXPERTS, tr), 0)
    g_iota = lax.broadcasted_iota(i32, (GROUP_SIZE, tr), 0)

    gscore = []
    for g in range(N_EXPERT_GROUPS):
        v = sb[g * GROUP_SIZE:(g + 1) * GROUP_SIZE]
        m1 = jnp.max(v, axis=0, keepdims=True)
        i1 = jnp.min(jnp.where(v == m1, g_iota, GROUP_SIZE), axis=0, keepdims=True)
        m2 = jnp.max(jnp.where(g_iota == i1, -jnp.inf, v), axis=0, keepdims=True)
        gscore.append(m1 + m2)
    parts = []
    for g in range(N_EXPERT_GROUPS):
        beaten = jnp.zeros((1, tr), i32)
        for g2 in range(N_EXPERT_GROUPS):
            if g2 == g:
                continue
            ahead = gscore[g2] > gscore[g]
            if g2 < g:
                ahead = ahead | (gscore[g2] == gscore[g])
            beaten = beaten + ahead.astype(i32)
        keep = beaten < TOPK_GROUPS
        parts.append(jnp.where(keep, sb[g * GROUP_SIZE:(g + 1) * GROUP_SIZE], NEG_INF))
    cur = jnp.concatenate(parts, axis=0)

    idxs, svals = [], []
    for _ in range(TOP_K):
        m = jnp.max(cur, axis=0, keepdims=True)
        ik = jnp.min(jnp.where(cur == m, e_iota, N_EXPERTS), axis=0, keepdims=True)
        hit = e_iota == ik
        svals.append(jnp.sum(jnp.where(hit, s, 0.0), axis=0, keepdims=True))
        cur = jnp.where(hit, -jnp.inf, cur)
        idxs.append(ik)
    sel = (cur == -jnp.inf).astype(f32)
    ssum = svals[0]
    for sv in svals[1:]:
        ssum = ssum + sv
    gate_ref[...] = jnp.concatenate([sv / ssum * ROUTED_SCALE for sv in svals], axis=0)
    idx_ref[...] = jnp.concatenate(idxs, axis=0)

    before = carry[...] + _dot(sel.astype(bf16), tri_ref[...])
    ranks = [jnp.sum(jnp.where(e_iota == ik, before, 0.0), axis=0, keepdims=True) for ik in idxs]
    rank_ref[...] = jnp.concatenate(ranks, axis=0).astype(i32)
    carry[...] = carry[...] + jnp.sum(sel, axis=1, keepdims=True)

    @pl.when(step == nsteps - 1)
    def _():
        cnt_ref[...] = carry[...]


def _route(logits_a, logits_b, router_bias, tr=512):
    E, Ta = logits_a.shape
    T = Ta + logits_b.shape[1]
    na = Ta // tr
    nsteps = T // tr
    tri = jnp.asarray(np.triu(np.ones((tr, tr), np.float32), 1), bf16)
    return pl.pallas_call(
        functools.partial(_route_body, tr=tr, nsteps=nsteps, na=na),
        grid=(nsteps,),
        in_specs=[
            pl.BlockSpec((E, tr), lambda i: (0, jnp.minimum(i, na - 1))),
            pl.BlockSpec((E, tr), lambda i: (0, jnp.maximum(i - na, 0))),
            pl.BlockSpec((E, 1), lambda i: (0, 0)),
            pl.BlockSpec((tr, tr), lambda i: (0, 0)),
        ],
        out_specs=[
            pl.BlockSpec((TOP_K, tr), lambda i: (0, i)),
            pl.BlockSpec((TOP_K, tr), lambda i: (0, i)),
            pl.BlockSpec((TOP_K, tr), lambda i: (0, i)),
            pl.BlockSpec((E, 1), lambda i: (0, 0)),
        ],
        out_shape=[
            jax.ShapeDtypeStruct((TOP_K, T), i32),
            jax.ShapeDtypeStruct((TOP_K, T), i32),
            jax.ShapeDtypeStruct((TOP_K, T), f32),
            jax.ShapeDtypeStruct((E, 1), f32),
        ],
        scratch_shapes=[pltpu.VMEM((E, 1), f32)],
        compiler_params=pltpu.CompilerParams(dimension_semantics=("arbitrary",)),
        name="route",
    )(logits_a, logits_b, router_bias.reshape(E, 1), tri)


def _dest_body(idx_ref, rank_ref, ps_ref, dest_ref, *, tr):
    e_iota = lax.broadcasted_iota(i32, (N_EXPERTS, tr), 0)
    start = ps_ref[...]
    rows = []
    for k in range(TOP_K):
        hit = e_iota == idx_ref[k:k + 1, :]
        rows.append(jnp.sum(jnp.where(hit, start, 0.0), axis=0, keepdims=True))
    dest_ref[...] = jnp.concatenate(rows, axis=0).astype(i32) + rank_ref[...]


def _dest_rows(idx, rank, pad_start, tr=512):
    K, T = idx.shape
    blk = pl.BlockSpec((K, tr), lambda i: (0, i))
    return pl.pallas_call(
        functools.partial(_dest_body, tr=tr),
        grid=(T // tr,),
        in_specs=[blk, blk, pl.BlockSpec((N_EXPERTS, 1), lambda i: (0, 0))],
        out_specs=blk,
        out_shape=jax.ShapeDtypeStruct((K, T), i32),
        name="dest_rows",
    )(idx, rank, pad_start.astype(f32).reshape(N_EXPERTS, 1))


def _sc_mesh():
    return plsc.VectorSubcoreMesh(core_axis_name="c", subcore_axis_name="s")


def _sc_worker_id():
    return lax.axis_index("s") * 2 + lax.axis_index("c")


def _dispatch(h2_a, h2_b, dest, gate, n_rows, chunk=32):
    ta, Dw = h2_a.shape
    T = ta + h2_b.shape[0]
    per_worker = T // SC_WORKERS
    nchunk = per_worker // chunk
    assert per_worker * SC_WORKERS == T and nchunk * chunk == per_worker and ta % chunk == 0

    nrow_idx = nchunk * TOP_K

    @functools.partial(
        pl.kernel, mesh=_sc_mesh(),
        out_type=[jax.ShapeDtypeStruct((n_rows, Dw), i32), jax.ShapeDtypeStruct((n_rows, GATE_ROW), f32)],
        scratch_types=[pltpu.VMEM((nrow_idx, chunk), i32), pltpu.VMEM((nrow_idx, chunk), f32),
                       pltpu.VMEM((chunk, Dw), i32), pltpu.VMEM((TOP_K, chunk, GATE_ROW), f32),
                       pltpu.SemaphoreType.DMA],
        compiler_params=pltpu.CompilerParams(needs_layout_passes=False),
        name="moe_dispatch",
    )
    def body(ha_hbm, hb_hbm, dest_hbm, gate_hbm, xs_hbm, gs_hbm, idx_v, gate_v, rows_v, grow_v, sem):
        wid = _sc_worker_id()
        base = wid * per_worker
        pltpu.sync_copy(dest_hbm.at[wid], idx_v)
        pltpu.sync_copy(gate_hbm.at[wid], gate_v)
        zero = jnp.zeros((SC_LANES,), f32)
        for k in range(TOP_K):
            @pl.loop(0, chunk)
            def _(t):
                for j in range(GATE_ROW // SC_LANES):
                    grow_v[k, t, pl.ds(j * SC_LANES, SC_LANES)] = zero

        @pl.loop(0, nchunk)
        def _(ci):
            t0 = base + ci * chunk

            @pl.when(t0 < ta)
            def _():
                pltpu.sync_copy(ha_hbm.at[pl.ds(t0, chunk)], rows_v)

            @pl.when(t0 >= ta)
            def _():
                pltpu.sync_copy(hb_hbm.at[pl.ds(t0 - ta, chunk)], rows_v)

            for k in range(TOP_K):
                @plsc.parallel_loop(0, chunk, unroll=4)
                def _(t):
                    row = jnp.zeros((SC_LANES,), i32) + (ci * TOP_K + k)
                    grow_v[k, t, pl.ds(0, SC_LANES)] = plsc.load_gather(
                        gate_v, [row, jnp.zeros((SC_LANES,), i32) + t])

            cps = []
            for k in range(TOP_K):
                idx = idx_v.at[ci * TOP_K + k]
                cps.append(pltpu.make_async_copy(rows_v, xs_hbm.at[idx], sem))
                cps.append(pltpu.make_async_copy(grow_v.at[k], gs_hbm.at[idx], sem))
            for cp in cps:
                cp.start()
            for cp in cps:
                cp.wait()

    def per_worker_rows(a):
        return a.reshape(TOP_K, SC_WORKERS, nchunk, chunk).transpose(1, 2, 0, 3).reshape(
            SC_WORKERS, nrow_idx, chunk)

    return body(h2_a, h2_b, per_worker_rows(dest), per_worker_rows(gate))


def _combine(ys, dest, chunk=8):
    T = dest.shape[1]
    Dw = ys.shape[1]
    per_worker = T // SC_WORKERS
    nchunk = per_worker // chunk
    assert per_worker * SC_WORKERS == T and nchunk * chunk == per_worker and nchunk % 2 == 0

    @functools.partial(
        pl.kernel, mesh=_sc_mesh(),
        out_type=jax.ShapeDtypeStruct((T, 2 * Dw), f32),
        scratch_types=[
            pltpu.VMEM((TOP_K * per_worker,), i32),
            pltpu.VMEM((2, TOP_K, chunk, Dw), i32),
            pltpu.VMEM((chunk, 2 * Dw), f32),
            pltpu.SemaphoreType.DMA((2,)),
        ],
        compiler_params=pltpu.CompilerParams(needs_layout_passes=False),
        name="moe_combine",
    )
    def body(ys_hbm, dest_hbm, out_hbm, idx_v, buf, out_v, sems):
        base = _sc_worker_id() * per_worker
        pltpu.sync_copy(dest_hbm.at[pl.ds(_sc_worker_id() * (TOP_K * per_worker), TOP_K * per_worker)], idx_v)

        def gather(ci, slot):
            return [pltpu.make_async_copy(ys_hbm.at[idx_v.at[pl.ds(k * per_worker + ci * chunk, chunk)]],
                                          buf.at[slot, k], sems.at[slot]) for k in range(TOP_K)]

        for cp in gather(0, 0):
            cp.start()

        @pl.loop(0, nchunk, step=2)
        def _(c0):
            for slot in range(2):
                ci = c0 + slot

                @pl.when(ci + 1 < nchunk)
                def _():
                    for cp in gather(ci + 1, 1 - slot):
                        cp.start()

                for cp in gather(ci, slot):
                    cp.wait()

                @pl.loop(0, chunk)
                def _(t):
                    @plsc.parallel_loop(0, Dw // SC_LANES, unroll=8)
                    def _(j):
                        sl = pl.ds(j * SC_LANES, SC_LANES)
                        w = buf[slot, 0, t, sl]
                        hi = plsc.bitcast(w & HI_MASK, f32)
                        lo = plsc.bitcast(lax.shift_left(w, 16), f32)
                        for k in range(1, TOP_K):
                            w = buf[slot, k, t, sl]
                            hi = hi + plsc.bitcast(w & HI_MASK, f32)
                            lo = lo + plsc.bitcast(lax.shift_left(w, 16), f32)
                        out_v[t, sl] = hi
                        out_v[t, pl.ds(Dw + j * SC_LANES, SC_LANES)] = lo

                pltpu.sync_copy(out_v, out_hbm.at[pl.ds(base + ci * chunk, chunk)])

    dest_w = dest.reshape(TOP_K, SC_WORKERS, per_worker).transpose(1, 0, 2).reshape(-1)
    return body(ys, dest_w)


def _gmm_body(blk_e_ref, blk_rows_ref, blk_next_ref, nv_ref, xs_hbm, gs_hbm, w1_hbm, w3_hbm, w2_hbm, ys_hbm,
              xbuf, gbuf, ybuf, w1f, w3f, w2f, xsem, gsem, ysem, wsem):
    nv = nv_ref[0]
    nb = blk_e_ref.shape[0]
    half = D_MODEL // 2
    RB = EXPERT_BLOCK
    ahead_w = W_RING - 2

    def expert_of(blk):
        return blk_e_ref[jnp.minimum(blk, nb - 1)]

    def next_expert_block(blk):
        return jnp.where(blk < nv, blk_next_ref[jnp.minimum(blk, nb - 1)], blk)

    def start_weights(blk, ordinal):
        @pl.when(blk < nv)
        def _():
            for cp in weight_copies(expert_of(blk), lax.rem(ordinal, W_RING)):
                cp.start(priority=1)

    def row_copies(b, slot):
        r0 = pl.multiple_of(b * RB, RB)
        return (pltpu.make_async_copy(xs_hbm.at[pl.ds(r0, RB)], xbuf.at[slot], xsem.at[slot]),
                pltpu.make_async_copy(gs_hbm.at[pl.ds(r0, RB)], gbuf.at[slot], gsem.at[slot]))

    def out_copy(b, slot):
        r0 = pl.multiple_of(b * RB, RB)
        return pltpu.make_async_copy(ybuf.at[slot], ys_hbm.at[pl.ds(r0, RB)], ysem.at[slot])

    def weight_copies(e, ws):
        return (pltpu.make_async_copy(w1_hbm.at[e], w1f.at[ws], wsem.at[ws, 0]),
                pltpu.make_async_copy(w3_hbm.at[e], w3f.at[ws], wsem.at[ws, 1]),
                pltpu.make_async_copy(w2_hbm.at[e], w2f.at[ws], wsem.at[ws, 2]))

    blk = jnp.int32(0)
    for n in range(ahead_w):
        start_weights(blk, n)
        blk = next_expert_block(blk)
    for i in range(ROW_RING - 2):
        @pl.when(i < nv)
        def _():
            for cp in row_copies(i, i):
                cp.start()

    def enter(b, live, ordinal_prev):
        e = expert_of(b)
        first = live & ((b == 0) | (e != expert_of(jnp.maximum(b - 1, 0))))
        ordinal = jnp.where(first & (b > 0), ordinal_prev + 1, ordinal_prev)

        @pl.when(first)
        def _():
            for cp in weight_copies(e, lax.rem(ordinal, W_RING)):
                cp.wait()
            nxt = b
            for _ in range(ahead_w):
                nxt = next_expert_block(nxt)
            start_weights(nxt, ordinal + ahead_w)

        return ordinal

    def load_block(b, slot, ws):
        valid = lax.broadcasted_iota(i32, (RB, 1), 0) < blk_rows_ref[jnp.minimum(b, nb - 1)]
        x_hi, x_lo = _unpack_pairs(jnp.where(valid, xbuf[slot], 0))
        g = jnp.where(valid, gbuf[slot][:, 0:1], 0.0)
        return x_hi, x_lo, g, ws

    def pair(p, ordinal_prev):
        b0 = 2 * p
        b1 = b0 + 1
        live1 = b1 < nv
        slot0 = lax.rem(b0, ROW_RING)
        slots = (slot0, jnp.where(live1, lax.rem(b1, ROW_RING), slot0))

        for b in (b0 + ROW_RING - 2, b1 + ROW_RING - 2):
            @pl.when(b < nv)
            def _():
                for cp in row_copies(b, lax.rem(b, ROW_RING)):
                    cp.start()

        ord0 = enter(b0, b0 < nv, ordinal_prev)
        ord1 = enter(b1, live1, ord0)
        for cp in row_copies(b0, slots[0]):
            cp.wait()

        @pl.when(live1)
        def _():
            for cp in row_copies(b1, slots[1]):
                cp.wait()

        blocks = [load_block(b0, slots[0], lax.rem(ord0, W_RING)),
                  load_block(jnp.where(live1, b1, b0), slots[1], lax.rem(ord1, W_RING))]
        a = [_dot(xh, w1f[ws, 0:half, :].astype(bf16)) + _dot(xl, w1f[ws, half:, :].astype(bf16))
             for xh, xl, _, ws in blocks]
        c = [_dot(xh, w3f[ws, 0:half, :].astype(bf16)) + _dot(xl, w3f[ws, half:, :].astype(bf16))
             for xh, xl, _, ws in blocks]
        hmid = [((a[i] * jax.nn.sigmoid(a[i])) * c[i]).astype(bf16) for i in range(2)]
        y = [_pack_pairs(_dot(hmid[i], w2f[blocks[i][3]].astype(bf16)) * blocks[i][2]) for i in range(2)]

        for i, (b, live) in enumerate(((b0, b0 < nv), (b1, live1))):
            @pl.when(live & (b >= ROW_RING))
            def _():
                out_copy(b - ROW_RING, slots[i]).wait()

            @pl.when(live)
            def _():
                ybuf[slots[i]] = y[i]
                out_copy(b, slots[i]).start()

        return ord1

    lax.fori_loop(0, (nv + 1) // 2, pair, 0)

    for i in range(1, ROW_RING + 1):
        @pl.when(nv >= i)
        def _():
            out_copy(nv - i, lax.rem(nv - i, ROW_RING)).wait()


def _gmm(xs, gs, w1, w3, w2, blk_e, blk_rows, blk_next, n_valid):
    n_rows, Dw = xs.shape
    D = 2 * Dw
    RB = EXPERT_BLOCK
    hbm = pl.BlockSpec(memory_space=pl.ANY)
    return pl.pallas_call(
        _gmm_body,
        grid_spec=pltpu.PrefetchScalarGridSpec(
            num_scalar_prefetch=4,
            grid=(1,),
            in_specs=[hbm, hbm, hbm, hbm, hbm],
            out_specs=hbm,
            scratch_shapes=[
                pltpu.VMEM((ROW_RING, RB, Dw), i32), pltpu.VMEM((ROW_RING, RB, GATE_ROW), f32),
                pltpu.VMEM((ROW_RING, RB, Dw), i32),
                pltpu.VMEM((W_RING, D, EXPERT_FF), f32), pltpu.VMEM((W_RING, D, EXPERT_FF), f32),
                pltpu.VMEM((W_RING, EXPERT_FF, D), f32),
                pltpu.SemaphoreType.DMA((ROW_RING,)), pltpu.SemaphoreType.DMA((ROW_RING,)),
                pltpu.SemaphoreType.DMA((ROW_RING,)), pltpu.SemaphoreType.DMA((W_RING, 3)),
            ],
        ),
        out_shape=jax.ShapeDtypeStruct((n_rows, Dw), i32),
        compiler_params=pltpu.CompilerParams(dimension_semantics=("arbitrary",)),
        name="moe_gmm",
    )(blk_e, blk_rows, blk_next, n_valid, xs, gs, w1, w3, w2)


def _final_core(x1, h2p, comb, gt, ws1_ref, ws3_ref, ws2_ref):
    half = D_MODEL // 2
    h_hi, h_lo = _unpack_pairs(h2p)
    a = _dot(h_hi, ws1_ref[0:half, :]) + _dot(h_lo, ws1_ref[half:, :])
    c = _dot(h_hi, ws3_ref[0:half, :]) + _dot(h_lo, ws3_ref[half:, :])
    shared = _dot(((a * jax.nn.sigmoid(a)) * c).astype(bf16), ws2_ref[...])
    return x1, comb + shared, gt


def _final_prompt_body(x1_ref, h2_ref, cb_ref, gt_ref, ws1_ref, ws3_ref, ws2_ref, y_ref):
    x1, ffn, gt = _final_core(x1_ref[...], h2_ref[...], cb_ref[...], gt_ref[0], ws1_ref, ws3_ref, ws2_ref)
    y_ref[0] = x1 + gt * ffn


def _final_sample_body(x1_ref, h2_ref, cb_ref, gt_ref, ws1_ref, ws3_ref, ws2_ref, y_ref):
    x1, ffn, gt = _final_core(x1_ref[...], h2_ref[...], cb_ref[...], gt_ref[...], ws1_ref, ws3_ref, ws2_ref)
    shp = y_ref.shape
    y_ref[...] = x1.reshape(shp) + gt * ffn.reshape(shp)


def _final_prompt(x1, h2, comb, mod3, ws1, ws3, ws2, B, L, tm=1024):
    D = D_MODEL
    nt = L // tm
    full = lambda shape: pl.BlockSpec(shape, lambda b, j: (0,) * len(shape))
    rows = pl.BlockSpec((tm, D), lambda b, j: (b * nt + j, 0))
    words = pl.BlockSpec((tm, D // 2), lambda b, j: (b * nt + j, 0))
    return pl.pallas_call(
        _final_prompt_body,
        grid=(B, nt),
        in_specs=[rows, words, rows, pl.BlockSpec((1, 1, D), lambda b, j: (b, 0, 5)),
                  full((D, EXPERT_FF)), full((D, EXPERT_FF)), full((EXPERT_FF, D))],
        out_specs=pl.BlockSpec((1, tm, D), lambda b, j: (b, j, 0)),
        out_shape=jax.ShapeDtypeStruct((B, L, D), f32),
        name="final_prompt",
    )(x1, h2, comb, mod3, ws1, ws3, ws2)


def _final_sample(x1, h2, comb, mod3, ws1, ws3, ws2, B, ls, row0, bt=64):
    D = D_MODEL
    n = bt * ls
    blk0 = row0 // n
    full = lambda shape: pl.BlockSpec(shape, lambda i: (0,) * len(shape))
    rows = pl.BlockSpec((n, D), lambda i: (i, 0))
    words = pl.BlockSpec((n, D // 2), lambda i: (i, 0))
    comb_rows = pl.BlockSpec((n, D), lambda i: (blk0 + i, 0))
    return pl.pallas_call(
        _final_sample_body,
        grid=(B // bt,),
        in_specs=[rows, words, comb_rows, pl.BlockSpec((bt, 1, D), lambda i: (i, 0, 5)),
                  full((D, EXPERT_FF)), full((D, EXPERT_FF)), full((EXPERT_FF, D))],
        out_specs=pl.BlockSpec((bt, ls, D), lambda i: (i, 0, 0)),
        out_shape=jax.ShapeDtypeStruct((B, ls, D), f32),
        name="final_sample",
    )(x1, h2, comb, mod3, ws1, ws3, ws2)


def kernel(x_prompt, x_sample, state_pool, cache_swa_k, cache_swa_v, c_prompt, c_sample, w_ada, b_ada,
           g_attn_norm, w_in, g_q, g_k, w_pool, pool_scale, w_out, attn_sinks, rel_bias, g_ffn_norm,
           w_router, router_bias, w1, w3, w2, ws1, ws3, ws2):
    B, L, D = x_prompt.shape
    BS, LS, _ = x_sample.shape
    depth = w_ada.shape[0]
    assert depth == 1
    W = cache_swa_k.shape[2]
    tp, ts = B * L, BS * LS
    T = tp + ts
    n_rows = (T * TOP_K // EXPERT_BLOCK + N_EXPERTS) * EXPERT_BLOCK
    nb = n_rows // EXPERT_BLOCK

    g_attn = g_attn_norm[0].reshape(1, D)
    g_ffn = g_ffn_norm[0].reshape(1, D)
    w_in_b = w_in[0].astype(bf16)
    w_out_b = w_out[0].astype(bf16)
    w_pool_b = w_pool[0].astype(bf16)
    ps = pool_scale[0].reshape(1, POOL_W)
    gqk = jnp.concatenate([jnp.tile(g_q[0], N_HEADS), jnp.tile(g_k[0], N_KV_HEADS)]).reshape(1, QK_W)
    head_of = np.arange(QK_W) // HEAD_DIM
    bd = jnp.asarray((head_of[:, None] == head_of[None, :]).astype(np.float32), bf16)
    wr_t = w_router[0].T
    wr_hi = wr_t.astype(bf16)
    wr_lo = (wr_t - wr_hi.astype(f32)).astype(bf16)
    ws1_b, ws3_b, ws2_b = ws1[0].astype(bf16), ws3[0].astype(bf16), ws2[0].astype(bf16)
    sinks = attn_sinks[0]

    mod = _ada(jnp.concatenate([c_prompt, c_sample], axis=0), w_ada[0], b_ada[0])
    mod_p = mod[:B].reshape(B, 1, 6 * D)
    mod_s = mod[B:].reshape(BS, 1, 6 * D)

    dist_p = np.arange(WINDOW)[:, None] + WINDOW - np.arange(2 * WINDOW)[None, :]
    bias_p = _relbias(rel_bias, dist_p)
    bias_p = bias_p.reshape(N_KV_HEADS, GQA, WINDOW, 2 * WINDOW).transpose(0, 2, 1, 3).reshape(
        N_KV_HEADS, WINDOW, GQA * 2 * WINDOW)
    dist_s = np.arange(LS)[:, None] + W - np.arange(W + LS)[None, :]
    bias_s = _relbias(rel_bias, dist_s)
    bias_s_buf = bias_s[:, :, :W].reshape(N_KV_HEADS, GQA * LS, W)
    bias_s_new = bias_s[:, :, W:].reshape(N_KV_HEADS, GQA * LS, LS)
    sink_col = jnp.repeat(sinks, LS).reshape(N_KV_HEADS, GQA * LS, 1)

    q_p, k_p, v_p, po_p, new_pool_p, kc_p, vc_p = _inproj_prompt(
        x_prompt, mod_p, g_attn, w_in_b, gqk, bd, w_pool_b, ps)
    q_s, k_s, v_s, po_s, new_pool_s = _inproj_sample(
        x_sample, mod_s, g_attn, w_in_b, gqk, bd, w_pool_b, ps, state_pool[0], PAST_LEN)
    at_p = _attn_prompt(q_p, k_p, v_p, bias_p, sinks)
    at_s, nk_s, nv_s = _attn_sample(
        q_s.reshape(BS, LS, ATTN_W), cache_swa_k[0].reshape(BS, W, KV_W), cache_swa_v[0].reshape(BS, W, KV_W),
        k_s.reshape(BS, LS, KV_W), v_s.reshape(BS, LS, KV_W), bias_s_buf, bias_s_new, sink_col)

    x1_p, h2_p, lg_p = _outproj_prompt(po_p, at_p, x_prompt, mod_p, g_ffn, w_out_b, wr_hi, wr_lo)
    x1_s, h2_s, lg_s = _outproj_sample(po_s, at_s.reshape(ts, ATTN_W), x_sample, mod_s, g_ffn, w_out_b,
                                       wr_hi, wr_lo)

    idx, rank, gate, counts = _route(lg_p, lg_s, router_bias[0])
    counts = counts.reshape(N_EXPERTS).astype(i32)
    padded = (counts + EXPERT_BLOCK - 1) // EXPERT_BLOCK * EXPERT_BLOCK
    pad_end = jnp.cumsum(padded)
    pad_start = pad_end - padded
    dest = _dest_rows(idx, rank, pad_start)
    n_valid = (pad_end[-1] // EXPERT_BLOCK).astype(i32).reshape(1)
    blk_row0 = jnp.arange(nb, dtype=i32) * EXPERT_BLOCK
    blk_e = jnp.minimum(jnp.sum(blk_row0[:, None] >= pad_end[None, :], axis=1), N_EXPERTS - 1).astype(i32)
    own = jnp.arange(N_EXPERTS, dtype=i32)[None, :] == blk_e[:, None]
    blk_cnt = jnp.sum(jnp.where(own, counts[None, :], 0), axis=1)
    blk_start = jnp.sum(jnp.where(own, pad_start[None, :], 0), axis=1)
    blk_rows = jnp.clip(blk_cnt - (blk_row0 - blk_start), 0, EXPERT_BLOCK).astype(i32)
    blk_next = (jnp.sum(jnp.where(own, pad_end[None, :], 0), axis=1) // EXPERT_BLOCK).astype(i32)

    xs, gs = _dispatch(h2_p, h2_s, dest, gate, n_rows)
    ys = _gmm(xs, gs, w1[0], w3[0], w2[0], blk_e, blk_rows, blk_next, n_valid)
    comb = _combine(ys, dest)
    y_p = _final_prompt(x1_p, h2_p, comb, mod_p, ws1_b, ws3_b, ws2_b, B, L)
    y_s = _final_sample(x1_s, h2_s, comb, mod_s, ws1_b, ws3_b, ws2_b, BS, LS, tp)

    return (y_p, y_s, new_pool_p[None], kc_p.reshape(1, B, WINDOW, N_KV_HEADS, HEAD_DIM),
            vc_p.reshape(1, B, WINDOW, N_KV_HEADS, HEAD_DIM), new_pool_s[None],
            nk_s.reshape(1, BS, W, N_KV_HEADS, HEAD_DIM), nv_s.reshape(1, BS, W, N_KV_HEADS, HEAD_DIM))
```
